```python
import math
import jax, jax.numpy as jnp
from jax import lax
import numpy as np

D_MODEL = 1024
BATCH = 8
SEQ = 16384
DEPTH = 4

SSM_WIDTH = D_MODEL // 2
SSM_GROUP = 16
SSM_GROUPS = SSM_WIDTH // SSM_GROUP
SSM_STATE = 64
DT_MIN = 1e-3
DT_MAX = 1e-1
N_HEADS = 8
QK_NOPE = 64
QK_ROPE = 32
QK_DIM = QK_NOPE + QK_ROPE
V_DIM = 64
Q_LORA = 256
KV_LORA = 128
ROPE_BASE = 10000.0
Q_BLOCK = 128
D_FF = -(-8 * D_MODEL // (3 * 256)) * 256
N_MOD = 6
EPS = 1e-6
IN_SIZES = (SSM_WIDTH, Q_LORA, KV_LORA, QK_ROPE, D_MODEL, D_MODEL)
IN_COLS = SSM_WIDTH + Q_LORA + KV_LORA + QK_ROPE + 2 * D_MODEL

kernel_name = "hybrid_s5_mla_adaln_trunk"


def rms_norm(x, g):
    xf = x.astype(jnp.float32)
    y = xf * lax.rsqrt(jnp.mean(xf * xf, axis=-1, keepdims=True) + EPS)
    return (y * g.astype(jnp.float32)).astype(x.dtype)


def rope(x, cos, sin):
    x1, x2 = jnp.split(x, 2, axis=-1)
    return jnp.concatenate([x1 * cos - x2 * sin, x2 * cos + x1 * sin], axis=-1)


def _ssm_combine(left, right):
    a1r, a1i, b1r, b1i = left
    a2r, a2i, b2r, b2i = right
    return (a2r * a1r - a2i * a1i,
            a2r * a1i + a2i * a1r,
            a2r * b1r - a2i * b1i + b2r,
            a2r * b1i + a2i * b1r + b2i)


def s5_mixer(u, a_re, a_im, log_dt, b_re, b_im, c_re, c_im, d_skip, w_glu, b_glu):
    f32 = jnp.float32
    bsz, L, _ = u.shape
    dt = jnp.exp(log_dt.astype(f32))[:, None]
    ar = a_re.astype(f32)
    ai = a_im.astype(f32)
    mag = jnp.exp(ar * dt)
    abar_re = mag * jnp.cos(ai * dt)
    abar_im = mag * jnp.sin(ai * dt)
    den = ar * ar + ai * ai
    nr = abar_re - 1.0
    ni = abar_im
    coef_re = ((nr * ar + ni * ai) / den)[..., None]
    coef_im = ((ni * ar - nr * ai) / den)[..., None]
    br = b_re.astype(f32)
    bi = b_im.astype(f32)
    bbar_re = coef_re * br - coef_im * bi
    bbar_im = coef_re * bi + coef_im * br
    ug = u.reshape(bsz, L, SSM_GROUPS, SSM_GROUP).astype(f32)
    bu_re = jnp.einsum('blgm,gpm->blgp', ug, bbar_re)
    bu_im = jnp.einsum('blgm,gpm->blgp', ug, bbar_im)
    a_seq_re = jnp.broadcast_to(abar_re[None, None], (1, L, SSM_GROUPS, SSM_STATE))
    a_seq_im = jnp.broadcast_to(abar_im[None, None], (1, L, SSM_GROUPS, SSM_STATE))
    _, _, h_re, h_im = lax.associative_scan(
        _ssm_combine, (a_seq_re, a_seq_im, bu_re, bu_im), axis=1)
    y = (jnp.einsum('blgp,gmp->blgm', h_re, c_re.astype(f32))
         - jnp.einsum('blgp,gmp->blgm', h_im, c_im.astype(f32))
         + d_skip.astype(f32).reshape(SSM_GROUPS, SSM_GROUP) * ug)
    y = jax.nn.gelu(y.reshape(bsz, L, SSM_WIDTH)).astype(u.dtype)
    return y * jax.nn.sigmoid(y @ w_glu + b_glu)


def mla_mixer(cq_in, ckv_in, kr_in, positions, cos, sin, q_norm_g, w_uq, kv_norm_g, w_uk, w_uv):
    bsz, L, _ = cq_in.shape
    cq = rms_norm(cq_in, q_norm_g)
    q = (cq @ w_uq).reshape(bsz, L, N_HEADS, QK_DIM)
    q = jnp.concatenate([q[..., :QK_NOPE],
                         rope(q[..., QK_NOPE:], cos[:, :, None], sin[:, :, None])], axis=-1)
    ckv = rms_norm(ckv_in, kv_norm_g)
    k_nope = (ckv @ w_uk).reshape(bsz, L, N_HEADS, QK_NOPE)
    v = (ckv @ w_uv).reshape(bsz, L, N_HEADS, V_DIM)
    k_pe = rope(kr_in[:, :, None, :], cos[:, :, None], sin[:, :, None])
    k = jnp.concatenate([k_nope, jnp.broadcast_to(k_pe, (bsz, L, N_HEADS, QK_ROPE))], axis=-1)
    scale = QK_DIM ** -0.5
    nb = L // Q_BLOCK
    qb = q.reshape(bsz, nb, Q_BLOCK, N_HEADS, QK_DIM).transpose(1, 0, 2, 3, 4)
    pb = positions.reshape(bsz, nb, Q_BLOCK).transpose(1, 0, 2)
    neg = jnp.finfo(jnp.float32).min

    def attend_block(args):
        qi, pi = args
        s = jnp.einsum('bqhd,bkhd->bhqk', qi, k).astype(jnp.float32) * scale
        mask = positions[:, None, None, :] <= pi[:, None, :, None]
        p = jax.nn.softmax(jnp.where(mask, s, neg), axis=-1).astype(v.dtype)
        return jnp.einsum('bhqk,bkhd->bqhd', p, v)

    o = lax.map(attend_block, (qb, pb))
    return o.transpose(1, 0, 2, 3, 4).reshape(bsz, L, N_HEADS * V_DIM)


def _fwd_setup_inputs(seed: int = 0) -> dict:
    key = jax.random.key(seed)
    ks = jax.random.split(key, 32)
    nrm = jax.random.normal
    Lr = DEPTH
    G, P, M = SSM_GROUPS, SSM_STATE, SSM_GROUP
    x = nrm(ks[0], (BATCH, SEQ, D_MODEL), jnp.float32)
    c = nrm(ks[1], (BATCH, D_MODEL), jnp.float32)
    offset = jax.random.randint(ks[2], (BATCH, 1), 0, 4096, dtype=jnp.int32)
    positions = offset + jnp.arange(SEQ, dtype=jnp.int32)[None, :]
    n_idx = jnp.arange(P, dtype=jnp.float32)
    inv2 = 0.5 ** 0.5
    return {
        "x": x,
        "c": c,
        "positions": positions,
        "w_ada": nrm(ks[3], (Lr, D_MODEL, N_MOD * D_MODEL)) * (0.5 * D_MODEL ** -0.5),
        "b_ada": nrm(ks[4], (Lr, N_MOD * D_MODEL)) * 0.01,
        "norm1_g": 1.0 + 0.01 * nrm(ks[5], (Lr, D_MODEL)),
        "w_in": nrm(ks[6], (Lr, D_MODEL, IN_COLS)) * D_MODEL ** -0.5,
        "ssm_a_re": -0.5 * jnp.exp(0.02 * nrm(ks[7], (Lr, G, P))),
        "ssm_a_im": math.pi * n_idx[None, None, :] + 0.01 * nrm(ks[8], (Lr, G, P)),
        "ssm_log_dt": jax.random.uniform(ks[9], (Lr, G), jnp.float32, math.log(DT_MIN), math.log(DT_MAX)),
        "ssm_b_re": nrm(ks[10], (Lr, G, P, M)) * (inv2 * M ** -0.5),
        "ssm_b_im": nrm(ks[11], (Lr, G, P, M)) * (inv2 * M ** -0.5),
        "ssm_c_re": nrm(ks[12], (Lr, G, M, P)) * (inv2 * P ** -0.5),
        "ssm_c_im": nrm(ks[13], (Lr, G, M, P)) * (inv2 * P ** -0.5),
        "ssm_d": nrm(ks[14], (Lr, SSM_WIDTH)),
        "w_glu": nrm(ks[15], (Lr, SSM_WIDTH, SSM_WIDTH)) * SSM_WIDTH ** -0.5,
        "b_glu": nrm(ks[16], (Lr, SSM_WIDTH)) * 0.01,
        "w_a_out": nrm(ks[17], (Lr, SSM_WIDTH, D_MODEL)) * SSM_WIDTH ** -0.5,
        "q_norm_g": 1.0 + 0.01 * nrm(ks[18], (Lr, Q_LORA)),
        "w_uq": nrm(ks[19], (Lr, Q_LORA, N_HEADS * QK_DIM)) * Q_LORA ** -0.5,
        "kv_norm_g": 1.0 + 0.01 * nrm(ks[20], (Lr, KV_LORA)),
        "w_uk": nrm(ks[21], (Lr, KV_LORA, N_HEADS * QK_NOPE)) * KV_LORA ** -0.5,
        "w_uv": nrm(ks[22], (Lr, KV_LORA, N_HEADS * V_DIM)) * KV_LORA ** -0.5,
        "w_b_out": nrm(ks[23], (Lr, N_HEADS * V_DIM, D_MODEL)) * (N_HEADS * V_DIM) ** -0.5,
        "w_out": nrm(ks[24], (Lr, D_MODEL, D_MODEL)) * D_MODEL ** -0.5,
        "norm2_g": 1.0 + 0.01 * nrm(ks[25], (Lr, D_MODEL)),
        "w_gate": nrm(ks[26], (Lr, D_MODEL, D_FF)) * D_MODEL ** -0.5,
        "w_up": nrm(ks[27], (Lr, D_MODEL, D_FF)) * D_MODEL ** -0.5,
        "w_down": nrm(ks[28], (Lr, D_FF, D_MODEL)) * D_FF ** -0.5,
        "final_g": 1.0 + 0.01 * nrm(ks[29], (D_MODEL,)),
    }


def _fwd_reference(x, c, positions, w_ada, b_ada, norm1_g, w_in, ssm_a_re, ssm_a_im, ssm_log_dt,
              ssm_b_re, ssm_b_im, ssm_c_re, ssm_c_im, ssm_d, w_glu, b_glu, w_a_out,
              q_norm_g, w_uq, kv_norm_g, w_uk, w_uv, w_b_out, w_out, norm2_g,
              w_gate, w_up, w_down, final_g):
    inv_freq = ROPE_BASE ** (-jnp.arange(0, QK_ROPE, 2, dtype=jnp.float32) / QK_ROPE)
    ang = positions.astype(jnp.float32)[..., None] * inv_freq
    cos = jnp.cos(ang).astype(x.dtype)
    sin = jnp.sin(ang).astype(x.dtype)
    split_pts = [int(s) for s in np.cumsum(IN_SIZES)[:-1]]
    c_act = jax.nn.silu(c)
    for l in range(DEPTH):
        mod = (c_act @ w_ada[l] + b_ada[l])[:, None, :]
        sh1, sc1, g1, sh2, sc2, g2 = jnp.split(mod, N_MOD, axis=-1)
        h = rms_norm(x, norm1_g[l]) * (1.0 + sc1) + sh1
        z = h @ w_in[l]
        u, cq_in, ckv_in, kr_in, gate_a, gate_b = jnp.split(z, split_pts, axis=-1)
        y_a = s5_mixer(u, ssm_a_re[l], ssm_a_im[l], ssm_log_dt[l], ssm_b_re[l], ssm_b_im[l],
                       ssm_c_re[l], ssm_c_im[l], ssm_d[l], w_glu[l], b_glu[l]) @ w_a_out[l]
        y_b = mla_mixer(cq_in, ckv_in, kr_in, positions, cos, sin, q_norm_g[l], w_uq[l],
                        kv_norm_g[l], w_uk[l], w_uv[l]) @ w_b_out[l]
        merged = jax.nn.sigmoid(gate_a) * y_a + jax.nn.sigmoid(gate_b) * y_b
        x = x + g1 * (merged @ w_out[l])
        h = rms_norm(x, norm2_g[l]) * (1.0 + sc2) + sh2
        x = x + g2 * ((jax.nn.silu(h @ w_gate[l]) * (h @ w_up[l])) @ w_down[l])
    return rms_norm(x, final_g)


import jax as _jax
import jax.numpy as _jnp

TWIN_FORMAT = 'train_step'
FWD_PARAMS = ['x', 'c', 'positions', 'w_ada', 'b_ada', 'norm1_g', 'w_in', 'ssm_a_re', 'ssm_a_im', 'ssm_log_dt', 'ssm_b_re', 'ssm_b_im', 'ssm_c_re', 'ssm_c_im', 'ssm_d', 'w_glu', 'b_glu', 'w_a_out', 'q_norm_g', 'w_uq', 'kv_norm_g', 'w_uk', 'w_uv', 'w_b_out', 'w_out', 'norm2_g', 'w_gate', 'w_up', 'w_down', 'final_g']
TWIN_WEIGHTS = ['w_ada', 'b_ada', 'norm1_g', 'w_in', 'ssm_a_re', 'ssm_a_im', 'ssm_log_dt', 'ssm_b_re', 'ssm_b_im', 'ssm_c_re', 'ssm_c_im', 'ssm_d', 'w_glu', 'b_glu', 'w_a_out', 'q_norm_g', 'w_uq', 'kv_norm_g', 'w_uk', 'w_uv', 'w_b_out', 'w_out', 'norm2_g', 'w_gate', 'w_up', 'w_down', 'final_g']
TWIN_DIFF_INPUT = 'x'
TWIN_INPUTS = ['x', 'c', 'positions', 'w_ada', 'b_ada', 'norm1_g', 'w_in', 'ssm_a_re', 'ssm_a_im', 'ssm_log_dt', 'ssm_b_re', 'ssm_b_im', 'ssm_c_re', 'ssm_c_im', 'ssm_d', 'w_glu', 'b_glu', 'w_a_out', 'q_norm_g', 'w_uq', 'kv_norm_g', 'w_uk', 'w_uv', 'w_b_out', 'w_out', 'norm2_g', 'w_gate', 'w_up', 'w_down', 'final_g', 'loss_target', 'm_w_ada', 'm_b_ada', 'm_norm1_g', 'm_w_in', 'm_ssm_a_re', 'm_ssm_a_im', 'm_ssm_log_dt', 'm_ssm_b_re', 'm_ssm_b_im', 'm_ssm_c_re', 'm_ssm_c_im', 'm_ssm_d', 'm_w_glu', 'm_b_glu', 'm_w_a_out', 'm_q_norm_g', 'm_w_uq', 'm_kv_norm_g', 'm_w_uk', 'm_w_uv', 'm_w_b_out', 'm_w_out', 'm_norm2_g', 'm_w_gate', 'm_w_up', 'm_w_down', 'm_final_g', 'v_w_ada', 'v_b_ada', 'v_norm1_g', 'v_w_in', 'v_ssm_a_re', 'v_ssm_a_im', 'v_ssm_log_dt', 'v_ssm_b_re', 'v_ssm_b_im', 'v_ssm_c_re', 'v_ssm_c_im', 'v_ssm_d', 'v_w_glu', 'v_b_glu', 'v_w_a_out', 'v_q_norm_g', 'v_w_uq', 'v_kv_norm_g', 'v_w_uk', 'v_w_uv', 'v_w_b_out', 'v_w_out', 'v_norm2_g', 'v_w_gate', 'v_w_up', 'v_w_down', 'v_final_g']
TWIN_OUTPUTS = ['loss', 'grad_x', 'grad_w_ada', 'grad_b_ada', 'grad_norm1_g', 'grad_w_in', 'grad_ssm_a_re', 'grad_ssm_a_im', 'grad_ssm_log_dt', 'grad_ssm_b_re', 'grad_ssm_b_im', 'grad_ssm_c_re', 'grad_ssm_c_im', 'grad_ssm_d', 'grad_w_glu', 'grad_b_glu', 'grad_w_a_out', 'grad_q_norm_g', 'grad_w_uq', 'grad_kv_norm_g', 'grad_w_uk', 'grad_w_uv', 'grad_w_b_out', 'grad_w_out', 'grad_norm2_g', 'grad_w_gate', 'grad_w_up', 'grad_w_down', 'grad_final_g', 'delta_w_ada', 'delta_b_ada', 'delta_norm1_g', 'delta_w_in', 'delta_ssm_a_re', 'delta_ssm_a_im', 'delta_ssm_log_dt', 'delta_ssm_b_re', 'delta_ssm_b_im', 'delta_ssm_c_re', 'delta_ssm_c_im', 'delta_ssm_d', 'delta_w_glu', 'delta_b_glu', 'delta_w_a_out', 'delta_q_norm_g', 'delta_w_uq', 'delta_kv_norm_g', 'delta_w_uk', 'delta_w_uv', 'delta_w_b_out', 'delta_w_out', 'delta_norm2_g', 'delta_w_gate', 'delta_w_up', 'delta_w_down', 'delta_final_g', 'new_m_w_ada', 'new_m_b_ada', 'new_m_norm1_g', 'new_m_w_in', 'new_m_ssm_a_re', 'new_m_ssm_a_im', 'new_m_ssm_log_dt', 'new_m_ssm_b_re', 'new_m_ssm_b_im', 'new_m_ssm_c_re', 'new_m_ssm_c_im', 'new_m_ssm_d', 'new_m_w_glu', 'new_m_b_glu', 'new_m_w_a_out', 'new_m_q_norm_g', 'new_m_w_uq', 'new_m_kv_norm_g', 'new_m_w_uk', 'new_m_w_uv', 'new_m_w_b_out', 'new_m_w_out', 'new_m_norm2_g', 'new_m_w_gate', 'new_m_w_up', 'new_m_w_down', 'new_m_final_g', 'new_v_w_ada', 'new_v_b_ada', 'new_v_norm1_g', 'new_v_w_in', 'new_v_ssm_a_re', 'new_v_ssm_a_im', 'new_v_ssm_log_dt', 'new_v_ssm_b_re', 'new_v_ssm_b_im', 'new_v_ssm_c_re', 'new_v_ssm_c_im', 'new_v_ssm_d', 'new_v_w_glu', 'new_v_b_glu', 'new_v_w_a_out', 'new_v_q_norm_g', 'new_v_w_uq', 'new_v_kv_norm_g', 'new_v_w_uk', 'new_v_w_uv', 'new_v_w_b_out', 'new_v_w_out', 'new_v_norm2_g', 'new_v_w_gate', 'new_v_w_up', 'new_v_w_down', 'new_v_final_g']
TWIN_LEAF_KINDS = {'loss': 'loss', 'grad_x': 'grad_x', 'grad_w_ada': 'grad_w', 'grad_b_ada': 'grad_w', 'grad_norm1_g': 'grad_w', 'grad_w_in': 'grad_w', 'grad_ssm_a_re': 'grad_w', 'grad_ssm_a_im': 'grad_w', 'grad_ssm_log_dt': 'grad_w', 'grad_ssm_b_re': 'grad_w', 'grad_ssm_b_im': 'grad_w', 'grad_ssm_c_re': 'grad_w', 'grad_ssm_c_im': 'grad_w', 'grad_ssm_d': 'grad_w', 'grad_w_glu': 'grad_w', 'grad_b_glu': 'grad_w', 'grad_w_a_out': 'grad_w', 'grad_q_norm_g': 'grad_w', 'grad_w_uq': 'grad_w', 'grad_kv_norm_g': 'grad_w', 'grad_w_uk': 'grad_w', 'grad_w_uv': 'grad_w', 'grad_w_b_out': 'grad_w', 'grad_w_out': 'grad_w', 'grad_norm2_g': 'grad_w', 'grad_w_gate': 'grad_w', 'grad_w_up': 'grad_w', 'grad_w_down': 'grad_w', 'grad_final_g': 'grad_w', 'delta_w_ada': 'delta_w', 'delta_b_ada': 'delta_w', 'delta_norm1_g': 'delta_w', 'delta_w_in': 'delta_w', 'delta_ssm_a_re': 'delta_w', 'delta_ssm_a_im': 'delta_w', 'delta_ssm_log_dt': 'delta_w', 'delta_ssm_b_re': 'delta_w', 'delta_ssm_b_im': 'delta_w', 'delta_ssm_c_re': 'delta_w', 'delta_ssm_c_im': 'delta_w', 'delta_ssm_d': 'delta_w', 'delta_w_glu': 'delta_w', 'delta_b_glu': 'delta_w', 'delta_w_a_out': 'delta_w', 'delta_q_norm_g': 'delta_w', 'delta_w_uq': 'delta_w', 'delta_kv_norm_g': 'delta_w', 'delta_w_uk': 'delta_w', 'delta_w_uv': 'delta_w', 'delta_w_b_out': 'delta_w', 'delta_w_out': 'delta_w', 'delta_norm2_g': 'delta_w', 'delta_w_gate': 'delta_w', 'delta_w_up': 'delta_w', 'delta_w_down': 'delta_w', 'delta_final_g': 'delta_w', 'new_m_w_ada': 'new_m', 'new_m_b_ada': 'new_m', 'new_m_norm1_g': 'new_m', 'new_m_w_in': 'new_m', 'new_m_ssm_a_re': 'new_m', 'new_m_ssm_a_im': 'new_m', 'new_m_ssm_log_dt': 'new_m', 'new_m_ssm_b_re': 'new_m', 'new_m_ssm_b_im': 'new_m', 'new_m_ssm_c_re': 'new_m', 'new_m_ssm_c_im': 'new_m', 'new_m_ssm_d': 'new_m', 'new_m_w_glu': 'new_m', 'new_m_b_glu': 'new_m', 'new_m_w_a_out': 'new_m', 'new_m_q_norm_g': 'new_m', 'new_m_w_uq': 'new_m', 'new_m_kv_norm_g': 'new_m', 'new_m_w_uk': 'new_m', 'new_m_w_uv': 'new_m', 'new_m_w_b_out': 'new_m', 'new_m_w_out': 'new_m', 'new_m_norm2_g': 'new_m', 'new_m_w_gate': 'new_m', 'new_m_w_up': 'new_m', 'new_m_w_down': 'new_m', 'new_m_final_g': 'new_m', 'new_v_w_ada': 'new_v', 'new_v_b_ada': 'new_v', 'new_v_norm1_g': 'new_v', 'new_v_w_in': 'new_v', 'new_v_ssm_a_re': 'new_v', 'new_v_ssm_a_im': 'new_v', 'new_v_ssm_log_dt': 'new_v', 'new_v_ssm_b_re': 'new_v', 'new_v_ssm_b_im': 'new_v', 'new_v_ssm_c_re': 'new_v', 'new_v_ssm_c_im': 'new_v', 'new_v_ssm_d': 'new_v', 'new_v_w_glu': 'new_v', 'new_v_b_glu': 'new_v', 'new_v_w_a_out': 'new_v', 'new_v_q_norm_g': 'new_v', 'new_v_w_uq': 'new_v', 'new_v_kv_norm_g': 'new_v', 'new_v_w_uk': 'new_v', 'new_v_w_uv': 'new_v', 'new_v_w_b_out': 'new_v', 'new_v_w_out': 'new_v', 'new_v_norm2_g': 'new_v', 'new_v_w_gate': 'new_v', 'new_v_w_up': 'new_v', 'new_v_w_down': 'new_v', 'new_v_final_g': 'new_v'}


def _forward(args):
    return _fwd_reference(*[args[k] for k in FWD_PARAMS])


def _output_shape():
    def fwd():
        inp = _fwd_setup_inputs(0)
        return _fwd_reference(*[inp[k] for k in FWD_PARAMS])
    out = _jax.eval_shape(fwd)
    return out.shape, out.dtype

N_MICROBATCH = 1
ADAM_LR = 0.001
ADAM_B1 = 0.9
ADAM_B2 = 0.999
ADAM_EPS = 1e-08
ADAM_WD = 0.01
ADAM_STEP = 10
PER_EXAMPLE_BATCH_AXIS = {'x': 0, 'c': 0, 'positions': 0, 'loss_target': 0}
SHARED_INPUTS = []
_WEIGHT_DTYPES = {'w_ada': _jnp.float32, 'b_ada': _jnp.float32, 'norm1_g': _jnp.float32, 'w_in': _jnp.float32, 'ssm_a_re': _jnp.float32, 'ssm_a_im': _jnp.float32, 'ssm_log_dt': _jnp.float32, 'ssm_b_re': _jnp.float32, 'ssm_b_im': _jnp.float32, 'ssm_c_re': _jnp.float32, 'ssm_c_im': _jnp.float32, 'ssm_d': _jnp.float32, 'w_glu': _jnp.float32, 'b_glu': _jnp.float32, 'w_a_out': _jnp.float32, 'q_norm_g': _jnp.float32, 'w_uq': _jnp.float32, 'kv_norm_g': _jnp.float32, 'w_uk': _jnp.float32, 'w_uv': _jnp.float32, 'w_b_out': _jnp.float32, 'w_out': _jnp.float32, 'norm2_g': _jnp.float32, 'w_gate': _jnp.float32, 'w_up': _jnp.float32, 'w_down': _jnp.float32, 'final_g': _jnp.float32}
MOMENT_SCALE = {'w_ada': 7.659531e-02, 'b_ada': 1.273580e-01, 'norm1_g': 3.297301e-02, 'w_in': 2.216209e-02, 'ssm_a_re': 3.299595e-03, 'ssm_a_im': 2.827914e-03, 'ssm_log_dt': 2.043024e+00, 'ssm_b_re': 1.532211e-03, 'ssm_b_im': 1.541811e-03, 'ssm_c_re': 2.958759e-03, 'ssm_c_im': 3.082392e-03, 'ssm_d': 3.509326e-02, 'w_glu': 1.050929e-02, 'b_glu': 1.399607e-02, 'w_a_out': 2.322771e-02, 'q_norm_g': 2.124055e-02, 'w_uq': 1.205714e-02, 'kv_norm_g': 6.463139e-02, 'w_uk': 1.216570e-02, 'w_uv': 2.981654e-02, 'w_b_out': 2.117013e-02, 'w_out': 3.137740e-02, 'norm2_g': 1.096681e-01, 'w_gate': 4.653516e-02, 'w_up': 4.507309e-02, 'w_down': 7.469402e-02, 'final_g': 1.280266e+02}


def _to_microbatches(a, axis):
    t = _jnp.moveaxis(a, axis, 0)
    t = t.reshape((N_MICROBATCH, t.shape[0] // N_MICROBATCH) + t.shape[1:])
    return _jnp.moveaxis(t, 1, axis + 1)


def setup_inputs(seed: int = 0) -> dict:
    inp = _fwd_setup_inputs(seed)
    key = _jax.random.fold_in(_jax.random.key(seed), 7919)
    shape, _ = _output_shape()
    out = dict(inp)
    out["loss_target"] = _jax.random.normal(_jax.random.fold_in(key, 0), shape, _jnp.float32)
    for i, name in enumerate(TWIN_WEIGHTS):
        w = inp[name].astype(_jnp.float32)
        if MOMENT_SCALE is None:
            s = _jnp.sqrt(_jnp.mean(_jnp.square(w)) + 1e-30)
        else:
            s = MOMENT_SCALE[name]
        km, kv = _jax.random.split(_jax.random.fold_in(key, i + 1))
        out[name] = w
        out["m_" + name] = s * _jax.random.normal(km, w.shape, _jnp.float32)
        out["v_" + name] = (s * s) * _jax.random.uniform(kv, w.shape, _jnp.float32, 0.5, 1.5)
    if N_MICROBATCH > 1:
        for name, axis in PER_EXAMPLE_BATCH_AXIS.items():
            out[name] = _to_microbatches(out[name], axis)
    return {'x': out['x'], 'c': out['c'], 'positions': out['positions'], 'w_ada': out['w_ada'], 'b_ada': out['b_ada'], 'norm1_g': out['norm1_g'], 'w_in': out['w_in'], 'ssm_a_re': out['ssm_a_re'], 'ssm_a_im': out['ssm_a_im'], 'ssm_log_dt': out['ssm_log_dt'], 'ssm_b_re': out['ssm_b_re'], 'ssm_b_im': out['ssm_b_im'], 'ssm_c_re': out['ssm_c_re'], 'ssm_c_im': out['ssm_c_im'], 'ssm_d': out['ssm_d'], 'w_glu': out['w_glu'], 'b_glu': out['b_glu'], 'w_a_out': out['w_a_out'], 'q_norm_g': out['q_norm_g'], 'w_uq': out['w_uq'], 'kv_norm_g': out['kv_norm_g'], 'w_uk': out['w_uk'], 'w_uv': out['w_uv'], 'w_b_out': out['w_b_out'], 'w_out': out['w_out'], 'norm2_g': out['norm2_g'], 'w_gate': out['w_gate'], 'w_up': out['w_up'], 'w_down': out['w_down'], 'final_g': out['final_g'], 'loss_target': out['loss_target'], 'm_w_ada': out['m_w_ada'], 'm_b_ada': out['m_b_ada'], 'm_norm1_g': out['m_norm1_g'], 'm_w_in': out['m_w_in'], 'm_ssm_a_re': out['m_ssm_a_re'], 'm_ssm_a_im': out['m_ssm_a_im'], 'm_ssm_log_dt': out['m_ssm_log_dt'], 'm_ssm_b_re': out['m_ssm_b_re'], 'm_ssm_b_im': out['m_ssm_b_im'], 'm_ssm_c_re': out['m_ssm_c_re'], 'm_ssm_c_im': out['m_ssm_c_im'], 'm_ssm_d': out['m_ssm_d'], 'm_w_glu': out['m_w_glu'], 'm_b_glu': out['m_b_glu'], 'm_w_a_out': out['m_w_a_out'], 'm_q_norm_g': out['m_q_norm_g'], 'm_w_uq': out['m_w_uq'], 'm_kv_norm_g': out['m_kv_norm_g'], 'm_w_uk': out['m_w_uk'], 'm_w_uv': out['m_w_uv'], 'm_w_b_out': out['m_w_b_out'], 'm_w_out': out['m_w_out'], 'm_norm2_g': out['m_norm2_g'], 'm_w_gate': out['m_w_gate'], 'm_w_up': out['m_w_up'], 'm_w_down': out['m_w_down'], 'm_final_g': out['m_final_g'], 'v_w_ada': out['v_w_ada'], 'v_b_ada': out['v_b_ada'], 'v_norm1_g': out['v_norm1_g'], 'v_w_in': out['v_w_in'], 'v_ssm_a_re': out['v_ssm_a_re'], 'v_ssm_a_im': out['v_ssm_a_im'], 'v_ssm_log_dt': out['v_ssm_log_dt'], 'v_ssm_b_re': out['v_ssm_b_re'], 'v_ssm_b_im': out['v_ssm_b_im'], 'v_ssm_c_re': out['v_ssm_c_re'], 'v_ssm_c_im': out['v_ssm_c_im'], 'v_ssm_d': out['v_ssm_d'], 'v_w_glu': out['v_w_glu'], 'v_b_glu': out['v_b_glu'], 'v_w_a_out': out['v_w_a_out'], 'v_q_norm_g': out['v_q_norm_g'], 'v_w_uq': out['v_w_uq'], 'v_kv_norm_g': out['v_kv_norm_g'], 'v_w_uk': out['v_w_uk'], 'v_w_uv': out['v_w_uv'], 'v_w_b_out': out['v_w_b_out'], 'v_w_out': out['v_w_out'], 'v_norm2_g': out['v_norm2_g'], 'v_w_gate': out['v_w_gate'], 'v_w_up': out['v_w_up'], 'v_w_down': out['v_w_down'], 'v_final_g': out['v_final_g']}


def _loss(weights, diff, rest, loss_target):
    with _jax.named_scope("forward"):
        args = {**rest, TWIN_DIFF_INPUT: diff, **{k: w.astype(_WEIGHT_DTYPES[k]) for k, w in weights.items()}}
        y = _forward(args)
    with _jax.named_scope("loss_head"):
        err = _jnp.square(y.astype(_jnp.float32) - loss_target)
        return 0.5 * _jnp.sum(_jnp.mean(err, axis=-1)) if err.ndim else 0.5 * err


def _adamw(w, g, m, v):
    m = ADAM_B1 * m + (1.0 - ADAM_B1) * g
    v = ADAM_B2 * v + (1.0 - ADAM_B2) * _jnp.square(g)
    m_hat = m / (1.0 - ADAM_B1 ** ADAM_STEP)
    v_hat = v / (1.0 - ADAM_B2 ** ADAM_STEP)
    delta = -ADAM_LR * (m_hat / (_jnp.sqrt(v_hat) + ADAM_EPS) + ADAM_WD * w)
    return delta, m, v


def reference(x, c, positions, w_ada, b_ada, norm1_g, w_in, ssm_a_re, ssm_a_im, ssm_log_dt, ssm_b_re, ssm_b_im, ssm_c_re, ssm_c_im, ssm_d, w_glu, b_glu, w_a_out, q_norm_g, w_uq, kv_norm_g, w_uk, w_uv, w_b_out, w_out, norm2_g, w_gate, w_up, w_down, final_g, loss_target, m_w_ada, m_b_ada, m_norm1_g, m_w_in, m_ssm_a_re, m_ssm_a_im, m_ssm_log_dt, m_ssm_b_re, m_ssm_b_im, m_ssm_c_re, m_ssm_c_im, m_ssm_d, m_w_glu, m_b_glu, m_w_a_out, m_q_norm_g, m_w_uq, m_kv_norm_g, m_w_uk, m_w_uv, m_w_b_out, m_w_out, m_norm2_g, m_w_gate, m_w_up, m_w_down, m_final_g, v_w_ada, v_b_ada, v_norm1_g, v_w_in, v_ssm_a_re, v_ssm_a_im, v_ssm_log_dt, v_ssm_b_re, v_ssm_b_im, v_ssm_c_re, v_ssm_c_im, v_ssm_d, v_w_glu, v_b_glu, v_w_a_out, v_q_norm_g, v_w_uq, v_kv_norm_g, v_w_uk, v_w_uv, v_w_b_out, v_w_out, v_norm2_g, v_w_gate, v_w_up, v_w_down, v_final_g):
    given = dict(x=x, c=c, positions=positions, w_ada=w_ada, b_ada=b_ada, norm1_g=norm1_g, w_in=w_in, ssm_a_re=ssm_a_re, ssm_a_im=ssm_a_im, ssm_log_dt=ssm_log_dt, ssm_b_re=ssm_b_re, ssm_b_im=ssm_b_im, ssm_c_re=ssm_c_re, ssm_c_im=ssm_c_im, ssm_d=ssm_d, w_glu=w_glu, b_glu=b_glu, w_a_out=w_a_out, q_norm_g=q_norm_g, w_uq=w_uq, kv_norm_g=kv_norm_g, w_uk=w_uk, w_uv=w_uv, w_b_out=w_b_out, w_out=w_out, norm2_g=norm2_g, w_gate=w_gate, w_up=w_up, w_down=w_down, final_g=final_g, loss_target=loss_target, m_w_ada=m_w_ada, m_b_ada=m_b_ada, m_norm1_g=m_norm1_g, m_w_in=m_w_in, m_ssm_a_re=m_ssm_a_re, m_ssm_a_im=m_ssm_a_im, m_ssm_log_dt=m_ssm_log_dt, m_ssm_b_re=m_ssm_b_re, m_ssm_b_im=m_ssm_b_im, m_ssm_c_re=m_ssm_c_re, m_ssm_c_im=m_ssm_c_im, m_ssm_d=m_ssm_d, m_w_glu=m_w_glu, m_b_glu=m_b_glu, m_w_a_out=m_w_a_out, m_q_norm_g=m_q_norm_g, m_w_uq=m_w_uq, m_kv_norm_g=m_kv_norm_g, m_w_uk=m_w_uk, m_w_uv=m_w_uv, m_w_b_out=m_w_b_out, m_w_out=m_w_out, m_norm2_g=m_norm2_g, m_w_gate=m_w_gate, m_w_up=m_w_up, m_w_down=m_w_down, m_final_g=m_final_g, v_w_ada=v_w_ada, v_b_ada=v_b_ada, v_norm1_g=v_norm1_g, v_w_in=v_w_in, v_ssm_a_re=v_ssm_a_re, v_ssm_a_im=v_ssm_a_im, v_ssm_log_dt=v_ssm_log_dt, v_ssm_b_re=v_ssm_b_re, v_ssm_b_im=v_ssm_b_im, v_ssm_c_re=v_ssm_c_re, v_ssm_c_im=v_ssm_c_im, v_ssm_d=v_ssm_d, v_w_glu=v_w_glu, v_b_glu=v_b_glu, v_w_a_out=v_w_a_out, v_q_norm_g=v_q_norm_g, v_w_uq=v_w_uq, v_kv_norm_g=v_kv_norm_g, v_w_uk=v_w_uk, v_w_uv=v_w_uv, v_w_b_out=v_w_b_out, v_w_out=v_w_out, v_norm2_g=v_norm2_g, v_w_gate=v_w_gate, v_w_up=v_w_up, v_w_down=v_w_down, v_final_g=v_final_g)
    weights = {n: given[n] for n in TWIN_WEIGHTS}
    shared = {n: given[n] for n in SHARED_INPUTS}
    per_example = {n: given[n] for n in ['x', 'c', 'positions']}
    grad_fn = _jax.value_and_grad(_loss, argnums=(0, 1))

    def one_microbatch(ex, loss_target):
        ex = dict(ex)
        diff = ex.pop(TWIN_DIFF_INPUT)
        return grad_fn(weights, diff, {**shared, **ex}, loss_target)

    if N_MICROBATCH == 1:
        loss, (grad_w, grad_x) = one_microbatch(per_example, given["loss_target"])
    else:
        def body(carry, xs):
            loss_sum, grad_sum = carry
            l_k, (gw_k, gx_k) = one_microbatch(xs[0], xs[1])
            with _jax.named_scope("update"):
                return (loss_sum + l_k, _jax.tree.map(_jnp.add, grad_sum, gw_k)), gx_k

        init = (_jnp.zeros((), _jnp.float32), _jax.tree.map(_jnp.zeros_like, weights))
        (loss, grad_w), grad_x = _jax.lax.scan(body, init, (per_example, given["loss_target"]))
    with _jax.named_scope("update"):
        delta_w, new_m, new_v = {}, {}, {}
        for n in TWIN_WEIGHTS:
            delta_w[n], new_m[n], new_v[n] = _adamw(weights[n], grad_w[n], given["m_" + n], given["v_" + n])
    return (loss, grad_x, *[grad_w[n] for n in TWIN_WEIGHTS], *[delta_w[n] for n in TWIN_WEIGHTS],
            *[new_m[n] for n in TWIN_WEIGHTS], *[new_v[n] for n in TWIN_WEIGHTS])
```

```python
import functools
import math

import numpy as np
import jax
import jax.numpy as jnp
from jax import lax
from jax.experimental import pallas as pl
from jax.experimental.pallas import tpu as pltpu

F32 = jnp.float32
_MM = jnp.bfloat16
_ACT = jnp.bfloat16

N_HEADS = 8
QK_ROPE = 32
HEAD_PAD = 128
QW = N_HEADS * HEAD_PAD
ROPE_BASE = 10000.0
EPS = 1e-6
DT_MIN = 1e-3
ADAM_LR = 0.001
ADAM_B1 = 0.9
ADAM_B2 = 0.999
ADAM_EPS = 1e-08
ADAM_WD = 0.01
ADAM_STEP = 10
NEG_INF = -1e30

V7X_VMEM_BYTES = 64 * 1024 * 1024
VMEM_RESERVE_BYTES = 6 * 1024 * 1024
MESH = pl.DeviceIdType.MESH
ANY = pl.BlockSpec(memory_space=pl.ANY)


def _vmem_limit(block_bytes, temp_bytes):
    want = 2 * block_bytes + temp_bytes
    return int(min(V7X_VMEM_BYTES - VMEM_RESERVE_BYTES, max(want, 32 * 1024 * 1024)))


def _nbytes(shape, dtype):
    return int(np.prod(shape)) * jnp.dtype(dtype).itemsize


def _tile(n, target, mult=8):
    t = min(n, target)
    while t >= mult:
        if n % t == 0 and t % mult == 0:
            return t
        t -= 1
    return n


def _dot(a, b):
    return jnp.dot(a.astype(_MM), b.astype(_MM), preferred_element_type=F32)


def _dot_nt(a, b):
    return lax.dot_general(a.astype(_MM), b.astype(_MM), (((1,), (1,)), ((), ())), preferred_element_type=F32)


def _dot_tn(a, b):
    return lax.dot_general(a.astype(_MM), b.astype(_MM), (((0,), (0,)), ((), ())), preferred_element_type=F32)


def _sigmoid(x):
    return jax.nn.sigmoid(x)


_GELU_K = math.sqrt(2.0 / math.pi)


def _gelu(x):
    return x * (0.5 * (1.0 + jnp.tanh(_GELU_K * (x + 0.044715 * (x * x * x)))))


def _gelu_grad(x):
    th = jnp.tanh(_GELU_K * (x + 0.044715 * (x * x * x)))
    return 0.5 * (1.0 + th) + 0.5 * x * (1.0 - th * th) * (_GELU_K * (1.0 + 3.0 * 0.044715 * (x * x)))


def _rows_sum(v):
    return jnp.sum(v, axis=0, keepdims=True)


def _rms_stats(x):
    rstd = lax.rsqrt(jnp.mean(x * x, axis=-1, keepdims=True) + EPS)
    return x * rstd, rstd


def _rms_bwd(dxh, xh, rstd):
    return rstd * (dxh - xh * jnp.mean(dxh * xh, axis=-1, keepdims=True))


def _rowcall(name, body, n_rows, tm, row_ins, full_ins, row_outs, acc_outs=(), temp_cols=0):
    grid = (n_rows // tm,)
    in_specs = [pl.BlockSpec((tm, a.shape[1]), lambda i: (i, 0)) for a in row_ins]
    in_specs += [pl.BlockSpec(a.shape, lambda i: (0, 0), pipeline_mode=pl.Buffered(1)) for a in full_ins]
    out_shape = [jax.ShapeDtypeStruct((n_rows, c), dt) for c, dt in row_outs]
    out_shape += [jax.ShapeDtypeStruct(s, dt) for s, dt in acc_outs]
    out_specs = [pl.BlockSpec((tm, c), lambda i: (i, 0)) for c, _ in row_outs]
    out_specs += [pl.BlockSpec(s, lambda i: (0, 0)) for s, _ in acc_outs]
    blocks = sum(_nbytes((tm, a.shape[1]), a.dtype) for a in row_ins)
    blocks += sum(_nbytes((tm, c), dt) for c, dt in row_outs) + sum(_nbytes(s, dt) for s, dt in acc_outs)
    resident = sum(_nbytes(a.shape, a.dtype) for a in full_ins)
    limit = _vmem_limit(blocks, resident + _nbytes((tm, temp_cols), F32))
    res = pl.pallas_call(
        body, name=name, grid=grid, in_specs=in_specs, out_specs=out_specs, out_shape=out_shape,
        compiler_params=pltpu.CompilerParams(
            dimension_semantics=("arbitrary" if acc_outs else "parallel",), vmem_limit_bytes=limit),
    )(*row_ins, *full_ins)
    return res


def _first_step():
    return pl.program_id(0) == 0


def _acc(ref, val):
    @pl.when(_first_step())
    def _():
        ref[...] = val

    @pl.when(jnp.logical_not(_first_step()))
    def _():
        ref[...] += val


def _mm_tn(name, a, g):
    n_rows, k = a.shape
    n = g.shape[1]
    tk = k if k <= 1024 else _tile(k, 1408, 128)
    tn = n if n <= 1024 else _tile(n, 1408, 128)
    tl = _tile(n_rows, 512, 16)

    def body(a_ref, g_ref, o_ref):
        @pl.when(pl.program_id(2) == 0)
        def _():
            o_ref[...] = jnp.zeros_like(o_ref)
        o_ref[...] += _dot_tn(a_ref[...], g_ref[...])

    blocks = _nbytes((tl, tk), a.dtype) + _nbytes((tl, tn), g.dtype) + _nbytes((tk, tn), F32)
    return pl.pallas_call(
        body, name=name, grid=(k // tk, n // tn, n_rows // tl),
        in_specs=[pl.BlockSpec((tl, tk), lambda i, j, l: (l, i)), pl.BlockSpec((tl, tn), lambda i, j, l: (l, j))],
        out_specs=pl.BlockSpec((tk, tn), lambda i, j, l: (i, j)),
        out_shape=jax.ShapeDtypeStruct((k, n), F32),
        compiler_params=pltpu.CompilerParams(
            dimension_semantics=("parallel", "parallel", "arbitrary"),
            vmem_limit_bytes=_vmem_limit(blocks, 2 * _nbytes((tl, max(tk, tn)), F32) + _nbytes((tk, tn), F32))),
    )(a, g)


def _place():
    return lax.axis_index("x"), lax.axis_index("y"), lax.axis_index("c")


def _flip(v, bit):
    return 1 - v if bit else v


def _exchange(name, mode, arrays):
    n = len(arrays)
    if mode == "gather8":
        rel = [((k >> 2) & 1, (k >> 1) & 1, k & 1) for k in range(1, 8)]
        out_shape = [jax.ShapeDtypeStruct((8,) + a.shape, a.dtype) for a in arrays]
    elif mode == "gather4":
        rel = [((k >> 1) & 1, k & 1, 0) for k in range(1, 4)]
        out_shape = [jax.ShapeDtypeStruct((4,) + a.shape, a.dtype) for a in arrays]
    elif mode == "scatter4":
        rel = [((k >> 1) & 1, k & 1, 0) for k in range(1, 4)]
        out_shape = [jax.ShapeDtypeStruct(a.shape, a.dtype) for a in arrays]
    else:
        rel = [(0, 0, 1)]
        out_shape = [jax.ShapeDtypeStruct(a.shape, a.dtype) for a in arrays]
    n_rel = len(rel)

    def body(*refs):
        ins, outs = refs[:n], refs[n:2 * n]
        send_sems, recv_sems, local_sems = refs[2 * n:]
        x, y, c = _place()

        def slot(px, py, pc):
            return 4 * px + 2 * py + pc if mode == "gather8" else 2 * px + py

        mine = slot(x, y, c)
        local = []
        if mode != "swap":
            for a in range(n):
                src = ins[a].at[mine] if mode == "scatter4" else ins[a]
                local.append(pltpu.make_async_copy(src, outs[a].at[mine], local_sems.at[a]))
            for cp in local:
                cp.start()

        def remote(r, a):
            px, py, pc = _flip(x, rel[r][0]), _flip(y, rel[r][1]), _flip(c, rel[r][2])
            theirs = slot(px, py, pc)
            if mode == "swap":
                src, dst_there, dst_here = ins[a], outs[a], outs[a]
            elif mode == "scatter4":
                src, dst_there, dst_here = ins[a].at[theirs], outs[a].at[mine], outs[a].at[theirs]
            else:
                src, dst_there, dst_here = ins[a], outs[a].at[mine], outs[a].at[theirs]
            k = r * n + a
            push = pltpu.make_async_remote_copy(src_ref=src, dst_ref=dst_there, send_sem=send_sems.at[k],
                                                recv_sem=recv_sems.at[k], device_id=(px, py, pc), device_id_type=MESH)
            land = pltpu.make_async_remote_copy(src_ref=src, dst_ref=dst_here, send_sem=send_sems.at[k],
                                                recv_sem=recv_sems.at[k], device_id=(px, py, pc), device_id_type=MESH)
            return push, land

        copies = [remote(r, a) for r in range(n_rel) for a in range(n)]
        for push, _ in copies:
            push.start()
        for _, land in copies:
            land.wait_recv()
        for push, _ in copies:
            push.wait_send()
        for cp in local:
            cp.wait()

    return pl.pallas_call(
        body, name=name, in_specs=[ANY] * n, out_specs=[ANY] * n, out_shape=out_shape,
        scratch_shapes=[pltpu.SemaphoreType.DMA((n_rel * n,)), pltpu.SemaphoreType.DMA((n_rel * n,)),
                        pltpu.SemaphoreType.DMA((max(n, 1),))],
    )(*arrays)


def _sum_slots(name, stacked):
    p, rows, cols = stacked.shape
    tr = _tile(rows, 256)

    def body(s_ref, o_ref):
        acc = s_ref[0]
        for j in range(1, p):
            acc = acc + s_ref[j]
        o_ref[...] = acc

    return pl.pallas_call(
        body, name=name, grid=(rows // tr,),
        in_specs=[pl.BlockSpec((p, tr, cols), lambda i: (0, i, 0))],
        out_specs=pl.BlockSpec((tr, cols), lambda i: (i, 0)),
        out_shape=jax.ShapeDtypeStruct((rows, cols), F32),
        compiler_params=pltpu.CompilerParams(dimension_semantics=("parallel",)),
    )(stacked)


def _adamw(name, parts, w, m, v):
    rows, cols = w.shape
    tr = _tile(rows, 256)
    n_parts = len(parts)

    def body(*refs):
        part_refs = refs[:n_parts]
        w_ref, m_ref, v_ref, g_out, d_out, m_out, v_out = refs[n_parts:]
        g = None
        for pr in part_refs:
            if len(pr.shape) == 3:
                for j in range(pr.shape[0]):
                    g = pr[j] if g is None else g + pr[j]
            else:
                g = pr[...] if g is None else g + pr[...]
        m_new = ADAM_B1 * m_ref[...] + (1.0 - ADAM_B1) * g
        v_new = ADAM_B2 * v_ref[...] + (1.0 - ADAM_B2) * jnp.square(g)
        m_hat = m_new / (1.0 - ADAM_B1 ** ADAM_STEP)
        v_hat = v_new / (1.0 - ADAM_B2 ** ADAM_STEP)
        g_out[...] = g
        d_out[...] = -ADAM_LR * (m_hat / (jnp.sqrt(v_hat) + ADAM_EPS) + ADAM_WD * w_ref[...])
        m_out[...] = m_new
        v_out[...] = v_new

    spec2 = pl.BlockSpec((tr, cols), lambda i: (i, 0))
    in_specs = [pl.BlockSpec((p.shape[0], tr, cols), lambda i: (0, i, 0)) if p.ndim == 3 else spec2 for p in parts]
    blocks = sum(_nbytes((p.shape[0] if p.ndim == 3 else 1, tr, cols), F32) for p in parts) + 7 * _nbytes((tr, cols), F32)
    return pl.pallas_call(
        body, name=name, grid=(rows // tr,),
        in_specs=in_specs + [spec2] * 3, out_specs=[spec2] * 4,
        out_shape=[jax.ShapeDtypeStruct((rows, cols), F32)] * 4,
        compiler_params=pltpu.CompilerParams(dimension_semantics=("parallel",),
                                             vmem_limit_bytes=_vmem_limit(blocks, 4 * _nbytes((tr, cols), F32))),
    )(*parts, w, m, v)


def _mod_fwd(c_all, w_ada2d, depth):
    nb, d = c_all.shape
    cols = w_ada2d.shape[1]
    tn = _tile(cols, 512, 128)

    def body(c_ref, w_ref, o_ref):
        cv = c_ref[...]
        o_ref[...] = _dot(cv * _sigmoid(cv), w_ref[...])

    return pl.pallas_call(
        body, name="mod_fwd", grid=(depth, cols // tn),
        in_specs=[pl.BlockSpec((nb, d), lambda l, j: (0, 0)), pl.BlockSpec((d, tn), lambda l, j: (l, j))],
        out_specs=pl.BlockSpec((nb, tn), lambda l, j: (l, j)),
        out_shape=jax.ShapeDtypeStruct((depth * nb, cols), F32),
        compiler_params=pltpu.CompilerParams(dimension_semantics=("parallel", "parallel")),
    )(c_all, w_ada2d)


def _wada_bwd(c_all, dmod2d, depth):
    nb, d = c_all.shape
    cols = dmod2d.shape[1]
    tn = _tile(cols, 512, 128)

    def body(c_ref, g_ref, o_ref):
        cv = c_ref[...]
        o_ref[...] = _dot_tn(cv * _sigmoid(cv), g_ref[...])

    return pl.pallas_call(
        body, name="wada_bwd", grid=(depth, cols // tn),
        in_specs=[pl.BlockSpec((nb, d), lambda l, j: (0, 0)), pl.BlockSpec((nb, tn), lambda l, j: (l, j))],
        out_specs=pl.BlockSpec((d, tn), lambda l, j: (l, j)),
        out_shape=jax.ShapeDtypeStruct((depth * d, cols), F32),
        compiler_params=pltpu.CompilerParams(dimension_semantics=("parallel", "parallel")),
    )(c_all, dmod2d)


def _rope_tables(pos_col, inv_freq_lane, nope):
    n_rows = pos_col.shape[0]
    tm = _tile(n_rows, 512)
    half = QK_ROPE // 2

    def body(p_ref, f_ref, c_ref, s1_ref, s2_ref):
        ang = p_ref[...] * f_ref[...]
        lane = lax.broadcasted_iota(jnp.int32, ang.shape, 1)
        first = (lane >= nope) & (lane < nope + half)
        second = (lane >= nope + half) & (lane < nope + 2 * half)
        cos, sin = jnp.cos(ang), jnp.sin(ang)
        c_ref[...] = jnp.where(first | second, cos, 1.0)
        s1_ref[...] = jnp.where(first, -sin, 0.0)
        s2_ref[...] = jnp.where(second, sin, 0.0)

    return _rowcall("rope_tables", body, n_rows, tm, [pos_col], [inv_freq_lane], [(HEAD_PAD, F32)] * 3)


def _rope(q, c, s1, s2):
    w = q.shape[1]
    return q * c + pltpu.roll(q, w - QK_ROPE // 2, axis=1) * s1 + pltpu.roll(q, QK_ROPE // 2, axis=1) * s2


def _rope_adjoint(dr, c, s1, s2):
    w = dr.shape[1]
    return dr * c + pltpu.roll(dr * s1, QK_ROPE // 2, axis=1) + pltpu.roll(dr * s2, w - QK_ROPE // 2, axis=1)


def _ssm_disc(ar, ai, log_dt, br, bi):
    dt = jnp.exp(log_dt)
    mag = jnp.exp(ar * dt)
    abr = mag * jnp.cos(ai * dt)
    abi = mag * jnp.sin(ai * dt)
    den = ar * ar + ai * ai
    nr = abr - 1.0
    ni = abi
    cr = (nr * ar + ni * ai) / den
    ci = (ni * ar - nr * ai) / den
    return abr, abi, cr * br - ci * bi, cr * bi + ci * br


def _ssm_disc_fwd(ar, ai, log_dt, br, bi):
    n_rows, m = br.shape
    tm = _tile(n_rows, 1024)

    def body(ar_ref, ai_ref, dt_ref, br_ref, bi_ref, o1, o2, o3, o4):
        o1[...], o2[...], o3[...], o4[...] = _ssm_disc(ar_ref[...], ai_ref[...], dt_ref[...], br_ref[...], bi_ref[...])

    return _rowcall("ssm_disc_fwd", body, n_rows, tm, [ar, ai, log_dt, br, bi], [],
                    [(1, F32), (1, F32), (m, F32), (m, F32)])


def _ssm_disc_bwd(ar, ai, log_dt, br, bi, g_abr, g_abi, g_bbr, g_bbi):
    n_rows, m = br.shape
    tm = _tile(n_rows, 1024)

    def body(ar_ref, ai_ref, dt_ref, br_ref, bi_ref, g1, g2, g3, g4, o1, o2, o3, o4, o5):
        _, vjp = jax.vjp(_ssm_disc, ar_ref[...], ai_ref[...], dt_ref[...], br_ref[...], bi_ref[...])
        o1[...], o2[...], o3[...], o4[...], o5[...] = vjp((g1[...], g2[...], g3[...], g4[...]))

    return _rowcall("ssm_disc_bwd", body, n_rows, tm, [ar, ai, log_dt, br, bi, g_abr, g_abi, g_bbr, g_bbi], [],
                    [(1, F32), (1, F32), (1, F32), (m, F32), (m, F32)])


def _lane_sum(v2d):
    def body(v_ref, o_ref):
        o_ref[...] = jnp.sum(v_ref[...], axis=1, keepdims=True)
    return pl.pallas_call(body, name="lane_sum", out_shape=jax.ShapeDtypeStruct((v2d.shape[0], 1), F32))(v2d)


def _in_fwd(x, n1g, sc1, sh1, w_parts, tm):
    n_rows = x.shape[0]
    widths = [w.shape[1] for w in w_parts]

    def body(x_ref, g_ref, sc_ref, sh_ref, *rest):
        w_refs, (hb_ref, *z_refs) = rest[:len(w_parts)], rest[len(w_parts):]
        xh, _ = _rms_stats(x_ref[...])
        h = (xh * g_ref[...]) * (1.0 + sc_ref[...]) + sh_ref[...]
        hb = h.astype(_MM)
        hb_ref[...] = hb.astype(_ACT)
        for w_ref, z_ref in zip(w_refs, z_refs):
            z_ref[...] = _dot(hb, w_ref[...])

    return _rowcall("in_fwd", body, n_rows, tm, [x], [n1g, sc1, sh1, *w_parts],
                    [(x.shape[1], _ACT)] + [(w, F32) for w in widths], temp_cols=4 * x.shape[1])


def _ssm_fwd(u, bre_blk, bim_blk, abr_row, abi_row, cre_blk, cimneg_blk, d_row):
    n_rows, sw = u.shape
    gp = abr_row.shape[1]
    t = _tile(n_rows, 256)

    def body(u_ref, bre_ref, bim_ref, ar_ref, ai_ref, cre_ref, cim_ref, d_ref, y_ref, hre_ref, him_ref, cr, ci):
        @pl.when(_first_step())
        def _():
            cr[...] = jnp.zeros_like(cr)
            ci[...] = jnp.zeros_like(ci)

        uv = u_ref[...]
        ub = uv.astype(_MM)
        hre_ref[...] = _dot(ub, bre_ref[...])
        him_ref[...] = _dot(ub, bim_ref[...])
        a_r, a_i = ar_ref[...], ai_ref[...]

        def step(k, carry):
            pr, pi = carry
            row = pl.ds(k, 1)
            hr = a_r * pr - a_i * pi + hre_ref[row, :]
            hi = a_r * pi + a_i * pr + him_ref[row, :]
            hre_ref[row, :] = hr
            him_ref[row, :] = hi
            return hr, hi

        pr, pi = lax.fori_loop(0, t, step, (cr[0:1, :], ci[0:1, :]), unroll=8)
        cr[0:1, :] = pr
        ci[0:1, :] = pi
        y_ref[...] = _dot(hre_ref[...], cre_ref[...]) + _dot(him_ref[...], cim_ref[...]) + d_ref[...] * uv

    row = lambda c: pl.BlockSpec((t, c), lambda i: (i, 0))
    full = lambda a: pl.BlockSpec(a.shape, lambda i: (0, 0), pipeline_mode=pl.Buffered(1))
    blocks = _nbytes((t, sw), F32) * 2 + 2 * _nbytes((t, gp), F32)
    resident = 4 * _nbytes((sw, gp), _MM)
    return pl.pallas_call(
        body, name="ssm_fwd", grid=(n_rows // t,),
        in_specs=[row(sw), full(bre_blk), full(bim_blk), full(abr_row), full(abi_row), full(cre_blk),
                  full(cimneg_blk), full(d_row)],
        out_specs=[row(sw), row(gp), row(gp)],
        out_shape=[jax.ShapeDtypeStruct((n_rows, sw), F32), jax.ShapeDtypeStruct((n_rows, gp), F32),
                   jax.ShapeDtypeStruct((n_rows, gp), F32)],
        scratch_shapes=[pltpu.VMEM((8, gp), F32), pltpu.VMEM((8, gp), F32)],
        compiler_params=pltpu.CompilerParams(dimension_semantics=("arbitrary",),
                                             vmem_limit_bytes=_vmem_limit(blocks, resident + 3 * _nbytes((t, gp), F32))),
    )(u, bre_blk, bim_blk, abr_row, abi_row, cre_blk, cimneg_blk, d_row)


def _mla_prep_fwd(cq, ckv, kr, rc, rs1, rs2, gq, gkv, wuq, wuk, wuv, tm):
    n_rows = cq.shape[0]

    def body(cq_ref, ckv_ref, kr_ref, c_ref, s1_ref, s2_ref, gq_ref, gkv_ref, wuq_ref, wuk_ref, wuv_ref,
             q_ref, k_ref, v_ref, cqn_ref, ckvn_ref):
        c, s1, s2 = c_ref[...], s1_ref[...], s2_ref[...]
        c8, s18, s28 = (jnp.tile(a, (1, N_HEADS)) for a in (c, s1, s2))
        xh, _ = _rms_stats(cq_ref[...])
        cqn = (xh * gq_ref[...]).astype(_MM)
        cqn_ref[...] = cqn.astype(_ACT)
        q_ref[...] = _rope(_dot(cqn, wuq_ref[...]), c8, s18, s28).astype(_ACT)
        xh, _ = _rms_stats(ckv_ref[...])
        ckvn = (xh * gkv_ref[...]).astype(_MM)
        ckvn_ref[...] = ckvn.astype(_ACT)
        kpe = _rope(kr_ref[...], c, s1, s2)
        k_ref[...] = (_dot(ckvn, wuk_ref[...]) + jnp.tile(kpe, (1, N_HEADS))).astype(_ACT)
        v_ref[...] = _dot(ckvn, wuv_ref[...]).astype(_ACT)

    return _rowcall("mla_prep_fwd", body, n_rows, tm, [cq, ckv, kr, rc, rs1, rs2], [gq, gkv, wuq, wuk, wuv],
                    [(QW, _ACT), (QW, _ACT), (QW, _ACT), (cq.shape[1], _ACT), (ckv.shape[1], _ACT)], temp_cols=6 * QW)


def _causal_steps(n_blocks, key_major):
    if key_major:
        pairs = [(qi, ki) for ki in range(n_blocks) for qi in range(ki, n_blocks)]
    else:
        pairs = [(qi, ki) for qi in range(n_blocks) for ki in range(qi + 1)]
    return (jnp.asarray(np.array([p[0] for p in pairs], np.int32)), jnp.asarray(np.array([p[1] for p in pairs], np.int32)))


def _attn_fwd(q, k, v, pos_col, pos_row, scale):
    n_rows = q.shape[0]
    ta = _tile(n_rows, 512, 128)
    nb = n_rows // ta
    qmap, kmap = _causal_steps(nb, key_major=False)

    def body(qm, km, q_ref, k_ref, v_ref, pq_ref, pk_ref, o_ref, lse_ref, m_sc, l_sc, acc_sc):
        s_id = pl.program_id(1)
        qi, ki = qm[s_id], km[s_id]

        @pl.when(ki == 0)
        def _():
            m_sc[...] = jnp.full_like(m_sc, NEG_INF)
            l_sc[...] = jnp.zeros_like(l_sc)
            acc_sc[...] = jnp.zeros_like(acc_sc)

        s = _dot_nt(q_ref[...], k_ref[...]) * scale
        s = jnp.where(pk_ref[...] <= pq_ref[...], s, NEG_INF)
        m_prev = m_sc[...]
        m_new = jnp.maximum(m_prev, jnp.max(s, axis=1, keepdims=True))
        alpha = jnp.exp(m_prev - m_new)
        p = jnp.exp(s - m_new[:, :1])
        l_sc[...] = alpha * l_sc[...] + jnp.sum(p, axis=1, keepdims=True)
        acc_sc[...] = alpha * acc_sc[...] + _dot(p, v_ref[...])
        m_sc[...] = m_new

        @pl.when(ki == qi)
        def _():
            o_ref[...] = acc_sc[...] / l_sc[...]
            lse_ref[...] = m_sc[...] + jnp.log(l_sc[...])

    qspec = pl.BlockSpec((ta, HEAD_PAD), lambda h, s, qm, km: (qm[s], h))
    kspec = pl.BlockSpec((ta, HEAD_PAD), lambda h, s, qm, km: (km[s], h))
    grid_spec = pltpu.PrefetchScalarGridSpec(
        num_scalar_prefetch=2, grid=(N_HEADS, int(qmap.shape[0])),
        in_specs=[qspec, kspec, kspec,
                  pl.BlockSpec((ta, 1), lambda h, s, qm, km: (qm[s], 0)),
                  pl.BlockSpec((1, ta), lambda h, s, qm, km: (0, km[s]))],
        out_specs=[qspec, qspec],
        scratch_shapes=[pltpu.VMEM((ta, HEAD_PAD), F32)] * 3)
    return pl.pallas_call(
        body, name="attn_fwd", grid_spec=grid_spec,
        out_shape=[jax.ShapeDtypeStruct((n_rows, QW), F32), jax.ShapeDtypeStruct((n_rows, QW), F32)],
        compiler_params=pltpu.CompilerParams(dimension_semantics=("parallel", "arbitrary"),
                                             vmem_limit_bytes=_vmem_limit(8 * _nbytes((ta, HEAD_PAD), F32), 6 * _nbytes((ta, ta), F32))),
    )(qmap, kmap, q, k, v, pos_col, pos_row)


def _mix_fwd(ypre, o, ga, gb, x, g1, bglu, wglu, wa, wb, wout, tm):
    n_rows, d = x.shape
    sw = ypre.shape[1]

    def body(y_ref, o_ref, ga_ref, gb_ref, x_ref, g1_ref, bglu_ref, wglu_ref, wa_ref, wb_ref, wout_ref,
             yg_ref, ya_ref, yb_ref, mg_ref, mo_ref, xo_ref):
        ys = _gelu(y_ref[...])
        yg = ys * _sigmoid(_dot(ys, wglu_ref[...]) + bglu_ref[...])
        yg_ref[...] = yg.astype(_ACT)
        ya = _dot(yg, wa_ref[...])
        yb = _dot(o_ref[...], wb_ref[...])
        ya_ref[...] = ya
        yb_ref[...] = yb
        merged = _sigmoid(ga_ref[...]) * ya + _sigmoid(gb_ref[...]) * yb
        mg_ref[...] = merged.astype(_ACT)
        mo = _dot(merged, wout_ref[...])
        mo_ref[...] = mo
        xo_ref[...] = x_ref[...] + g1_ref[...] * mo

    return _rowcall("mix_fwd", body, n_rows, tm, [ypre, o, ga, gb, x], [g1, bglu, wglu, wa, wb, wout],
                    [(sw, _ACT), (d, F32), (d, F32), (d, _ACT), (d, F32), (d, F32)], temp_cols=4 * d)


def _ffn_fwd(x, n2g, sc2, sh2, g2, wg, wu, wd, tm):
    n_rows, d = x.shape
    ff = wg.shape[1]

    def body(x_ref, g_ref, sc_ref, sh_ref, g2_ref, wg_ref, wu_ref, wd_ref, hb_ref, a_ref, b_ref, d_ref, xo_ref):
        xv = x_ref[...]
        xh, _ = _rms_stats(xv)
        hb = ((xh * g_ref[...]) * (1.0 + sc_ref[...]) + sh_ref[...]).astype(_MM)
        hb_ref[...] = hb.astype(_ACT)
        a = _dot(hb, wg_ref[...])
        b = _dot(hb, wu_ref[...])
        a_ref[...] = a
        b_ref[...] = b
        dn = _dot((a * _sigmoid(a)) * b, wd_ref[...])
        d_ref[...] = dn
        xo_ref[...] = xv + g2_ref[...] * dn

    return _rowcall("ffn_fwd", body, n_rows, tm, [x], [n2g, sc2, sh2, g2, wg, wu, wd],
                    [(d, _ACT), (ff, F32), (ff, F32), (d, F32), (d, F32)], temp_cols=3 * ff)


def _head(x, fg, target, tm):
    n_rows, d = x.shape

    def body(x_ref, t_ref, g_ref, dx_ref, loss_ref, dg_ref):
        xh, rstd = _rms_stats(x_ref[...])
        err = xh * g_ref[...] - t_ref[...]
        part = jnp.sum(jnp.mean(err * err, axis=-1, keepdims=True), axis=0, keepdims=True) * 0.5
        _acc(loss_ref, jnp.broadcast_to(part, loss_ref.shape))
        dy = err * (1.0 / d)
        _acc(dg_ref, _rows_sum(dy * xh))
        dx_ref[...] = _rms_bwd(dy * g_ref[...], xh, rstd)

    return _rowcall("head", body, n_rows, tm, [x, target], [fg], [(d, F32)], [((1, 128), F32), ((1, d), F32)],
                    temp_cols=4 * d)


def _ffn_bwd(dxo, xmid, a, b, dn, n2g, sc2, g2, wg, wu, wd, tm):
    n_rows, d = dxo.shape
    ff = a.shape[1]

    def body(dxo_ref, x_ref, a_ref, b_ref, dn_ref, g_ref, sc_ref, g2_ref, wg_ref, wu_ref, wd_ref,
             dx_ref, da_ref, db_ref, f_ref, dd_ref, dg2_ref, dsh_ref, dsc_ref, dn2_ref):
        dxo_v = dxo_ref[...]
        dd = dxo_v * g2_ref[...]
        dd_ref[...] = dd.astype(_ACT)
        _acc(dg2_ref, _rows_sum(dxo_v * dn_ref[...]))
        df = _dot_nt(dd, wd_ref[...])
        av, bv = a_ref[...], b_ref[...]
        sa = _sigmoid(av)
        si = av * sa
        f_ref[...] = (si * bv).astype(_ACT)
        da = df * bv * (sa * (1.0 + av * (1.0 - sa)))
        db = df * si
        da_ref[...] = da.astype(_ACT)
        db_ref[...] = db.astype(_ACT)
        dh = _dot_nt(da, wg_ref[...]) + _dot_nt(db, wu_ref[...])
        xh, rstd = _rms_stats(x_ref[...])
        yg = xh * g_ref[...]
        _acc(dsh_ref, _rows_sum(dh))
        _acc(dsc_ref, _rows_sum(dh * yg))
        dy = dh * (1.0 + sc_ref[...])
        _acc(dn2_ref, _rows_sum(dy * xh))
        dx_ref[...] = dxo_v + _rms_bwd(dy * g_ref[...], xh, rstd)

    return _rowcall("ffn_bwd", body, n_rows, tm, [dxo, xmid, a, b, dn], [n2g, sc2, g2, wg, wu, wd],
                    [(d, F32), (ff, _ACT), (ff, _ACT), (ff, _ACT), (d, _ACT)], [((1, d), F32)] * 4, temp_cols=5 * ff)


def _mix_bwd(dxm, mo, ya, yb, ga, gb, ypre, g1, bglu, wglu, wa, wb, wout, tm):
    n_rows, d = dxm.shape
    sw = ypre.shape[1]

    def body(dxm_ref, mo_ref, ya_ref, yb_ref, ga_ref, gb_ref, y_ref, g1_ref, bglu_ref, wglu_ref, wa_ref, wb_ref,
             wout_ref, dmo_ref, dya_ref, dyb_ref, dt_ref, ys_ref, dga_ref, dgb_ref, dy_ref, do_ref, dg1_ref, dbg_ref):
        dxm_v = dxm_ref[...]
        dmo = dxm_v * g1_ref[...]
        dmo_ref[...] = dmo.astype(_ACT)
        _acc(dg1_ref, _rows_sum(dxm_v * mo_ref[...]))
        dmg = _dot_nt(dmo, wout_ref[...])
        sa, sb = _sigmoid(ga_ref[...]), _sigmoid(gb_ref[...])
        dya, dyb = dmg * sa, dmg * sb
        dya_ref[...] = dya.astype(_ACT)
        dyb_ref[...] = dyb.astype(_ACT)
        dga_ref[...] = (dmg * ya_ref[...] * (sa * (1.0 - sa))).astype(_ACT)
        dgb_ref[...] = (dmg * yb_ref[...] * (sb * (1.0 - sb))).astype(_ACT)
        do_ref[...] = _dot_nt(dyb, wb_ref[...])
        dyg = _dot_nt(dya, wa_ref[...])
        yv = y_ref[...]
        ys = _gelu(yv)
        ys_ref[...] = ys.astype(_ACT)
        sg = _sigmoid(_dot(ys, wglu_ref[...]) + bglu_ref[...])
        dt = dyg * ys * (sg * (1.0 - sg))
        dt_ref[...] = dt.astype(_ACT)
        _acc(dbg_ref, _rows_sum(dt))
        dys = dyg * sg + _dot_nt(dt, wglu_ref[...])
        dy_ref[...] = dys * _gelu_grad(yv)

    return _rowcall("mix_bwd", body, n_rows, tm, [dxm, mo, ya, yb, ga, gb, ypre], [g1, bglu, wglu, wa, wb, wout],
                    [(d, _ACT), (d, _ACT), (d, _ACT), (sw, _ACT), (sw, _ACT), (d, _ACT), (d, _ACT), (sw, F32), (QW, F32)],
                    [((1, d), F32), ((1, sw), F32)], temp_cols=6 * d)


def _attn_bwd(q, k, v, do, o, lse, pos_col, pos_row, scale):
    n_rows = q.shape[0]
    ta = _tile(n_rows, 512, 128)
    nb = n_rows // ta
    qmap, kmap = _causal_steps(nb, key_major=True)

    def body(qm, km, q_ref, k_ref, v_ref, do_ref, o_ref, lse_ref, pq_ref, pk_ref, dq_ref, dk_ref, dv_ref,
             dk_acc, dv_acc):
        s_id = pl.program_id(1)
        qi, ki = qm[s_id], km[s_id]

        @pl.when(s_id == 0)
        def _():
            dq_ref[...] = jnp.zeros_like(dq_ref)

        @pl.when(qi == ki)
        def _():
            dk_acc[...] = jnp.zeros_like(dk_acc)
            dv_acc[...] = jnp.zeros_like(dv_acc)

        qv, kv = q_ref[...], k_ref[...]
        dov = do_ref[...]
        s = _dot_nt(qv, kv) * scale
        s = jnp.where(pk_ref[...] <= pq_ref[...], s, NEG_INF)
        p = jnp.exp(s - lse_ref[...][:, :1])
        dp = _dot_nt(dov, v_ref[...])
        delta = jnp.sum(dov * o_ref[...], axis=1, keepdims=True)
        ds = p * (dp - delta) * scale
        dv_acc[...] += _dot_tn(p, dov)
        dk_acc[...] += _dot_tn(ds, qv)
        rows = pl.ds(pl.multiple_of(qi * ta, ta), ta)
        dq_ref[rows, :] += _dot(ds, kv)

        @pl.when(qi == nb - 1)
        def _():
            dk_ref[...] = dk_acc[...]
            dv_ref[...] = dv_acc[...]

    qspec = pl.BlockSpec((ta, HEAD_PAD), lambda h, s, qm, km: (qm[s], h))
    kspec = pl.BlockSpec((ta, HEAD_PAD), lambda h, s, qm, km: (km[s], h))
    grid_spec = pltpu.PrefetchScalarGridSpec(
        num_scalar_prefetch=2, grid=(N_HEADS, int(qmap.shape[0])),
        in_specs=[qspec, kspec, kspec, qspec, qspec, qspec,
                  pl.BlockSpec((ta, 1), lambda h, s, qm, km: (qm[s], 0)),
                  pl.BlockSpec((1, ta), lambda h, s, qm, km: (0, km[s]))],
        out_specs=[pl.BlockSpec((n_rows, HEAD_PAD), lambda h, s, qm, km: (0, h)), kspec, kspec],
        scratch_shapes=[pltpu.VMEM((ta, HEAD_PAD), F32), pltpu.VMEM((ta, HEAD_PAD), F32)])
    return pl.pallas_call(
        body, name="attn_bwd", grid_spec=grid_spec,
        out_shape=[jax.ShapeDtypeStruct((n_rows, QW), F32)] * 3,
        compiler_params=pltpu.CompilerParams(
            dimension_semantics=("parallel", "arbitrary"),
            vmem_limit_bytes=_vmem_limit(12 * _nbytes((ta, HEAD_PAD), F32) + _nbytes((n_rows, HEAD_PAD), F32),
                                         8 * _nbytes((ta, ta), F32))),
    )(qmap, kmap, q, k, v, do, o, lse, pos_col, pos_row)


def _mla_prep_bwd(dq, dk, dv, cq, ckv, rc, rs1, rs2, gq, gkv, wuq, wuk, wuv, nope, tm):
    n_rows = cq.shape[0]
    ql, kl = cq.shape[1], ckv.shape[1]

    def body(dq_ref, dk_ref, dv_ref, cq_ref, ckv_ref, c_ref, s1_ref, s2_ref, gq_ref, gkv_ref, wuq_ref, wuk_ref,
             wuv_ref, dqp_ref, dcq_ref, dckv_ref, dkr_ref, dgq_ref, dgkv_ref):
        c, s1, s2 = c_ref[...], s1_ref[...], s2_ref[...]
        c8, s18, s28 = (jnp.tile(a, (1, N_HEADS)) for a in (c, s1, s2))
        dqp = _rope_adjoint(dq_ref[...], c8, s18, s28)
        dqp_ref[...] = dqp.astype(_ACT)
        dcqn = _dot_nt(dqp, wuq_ref[...])
        xh, rstd = _rms_stats(cq_ref[...])
        _acc(dgq_ref, _rows_sum(dcqn * xh))
        dcq_ref[...] = _rms_bwd(dcqn * gq_ref[...], xh, rstd).astype(_ACT)
        dkv = dk_ref[...]
        dkpe = dkv[:, 0:HEAD_PAD]
        for h in range(1, N_HEADS):
            dkpe = dkpe + dkv[:, h * HEAD_PAD:(h + 1) * HEAD_PAD]
        lane = lax.broadcasted_iota(jnp.int32, dkpe.shape, 1)
        dkpe = jnp.where((lane >= nope) & (lane < nope + QK_ROPE), dkpe, 0.0)
        dkr_ref[...] = _rope_adjoint(dkpe, c, s1, s2).astype(_ACT)
        dckvn = _dot_nt(dkv, wuk_ref[...]) + _dot_nt(dv_ref[...], wuv_ref[...])
        xh, rstd = _rms_stats(ckv_ref[...])
        _acc(dgkv_ref, _rows_sum(dckvn * xh))
        dckv_ref[...] = _rms_bwd(dckvn * gkv_ref[...], xh, rstd).astype(_ACT)

    return _rowcall("mla_prep_bwd", body, n_rows, tm, [dq, dk, dv, cq, ckv, rc, rs1, rs2], [gq, gkv, wuq, wuk, wuv],
                    [(QW, _ACT), (ql, _ACT), (kl, _ACT), (HEAD_PAD, _ACT)], [((1, ql), F32), ((1, kl), F32)],
                    temp_cols=6 * QW)


def _ssm_bwd(dy, u, hre, him, bre_blk, bim_blk, abr_row, abi_row, cre_blk, cimneg_blk, d_row):
    n_rows, sw = u.shape
    gp = abr_row.shape[1]
    t = _tile(n_rows, 256)
    n_chunks = n_rows // t

    def body(dy_ref, u_ref, hre_ref, him_ref, hbre_ref, hbim_ref, bre_ref, bim_ref, ar_ref, ai_ref, cre_ref, cim_ref,
             d_ref, du_ref, gre_ref, gim_ref, dar_ref, dai_ref, dd_ref, g_re, g_im, hs_re, hs_im, cr, ci):
        i = pl.program_id(0)

        @pl.when(i == 0)
        def _():
            cr[...] = jnp.zeros_like(cr)
            ci[...] = jnp.zeros_like(ci)

        dyv = dy_ref[...]
        dyb = dyv.astype(_MM)
        g_re[...] = _dot_nt(dyb, cre_ref[...])
        g_im[...] = _dot_nt(dyb, cim_ref[...])
        a_r, a_i = ar_ref[...], ai_ref[...]

        def step(k, carry):
            nr, ni = carry
            row = pl.ds(t - 1 - k, 1)
            gr = g_re[row, :] + a_r * nr + a_i * ni
            gi = g_im[row, :] + a_r * ni - a_i * nr
            g_re[row, :] = gr
            g_im[row, :] = gi
            return gr, gi

        nr, ni = lax.fori_loop(0, t, step, (cr[0:1, :], ci[0:1, :]), unroll=8)
        cr[0:1, :] = nr
        ci[0:1, :] = ni
        gr_all, gi_all = g_re[...], g_im[...]
        gre_ref[...] = gr_all.astype(_ACT)
        gim_ref[...] = gi_all.astype(_ACT)
        du_ref[...] = (_dot_nt(gr_all, bre_ref[...]) + _dot_nt(gi_all, bim_ref[...]) + d_ref[...] * dyv).astype(_ACT)
        _acc(dd_ref, _rows_sum(dyv * u_ref[...]))
        is_first_chunk = i == n_chunks - 1
        hs_re[0:8, :] = jnp.where(is_first_chunk, 0.0, hbre_ref[...])
        hs_im[0:8, :] = jnp.where(is_first_chunk, 0.0, hbim_ref[...])
        hs_re[8:t + 8, :] = hre_ref[...]
        hs_im[8:t + 8, :] = him_ref[...]
        hp_re, hp_im = hs_re[pl.ds(7, t), :], hs_im[pl.ds(7, t), :]
        _acc(dar_ref, _rows_sum(gr_all * hp_re + gi_all * hp_im))
        _acc(dai_ref, _rows_sum(gi_all * hp_re - gr_all * hp_im))

    rev = lambda c: pl.BlockSpec((t, c), lambda i: (n_chunks - 1 - i, 0))
    before = pl.BlockSpec((8, gp), lambda i: (jnp.maximum((n_chunks - 1 - i) * (t // 8) - 1, 0), 0))
    full = lambda a: pl.BlockSpec(a.shape, lambda i: (0, 0), pipeline_mode=pl.Buffered(1))
    acc = lambda c: pl.BlockSpec((1, c), lambda i: (0, 0))
    blocks = 2 * _nbytes((t, sw), F32) + 2 * _nbytes((t, gp), F32) + _nbytes((t, sw), _ACT) + 2 * _nbytes((t, gp), _ACT)
    resident = 4 * _nbytes((sw, gp), _MM) + 4 * _nbytes((t + 8, gp), F32)
    return pl.pallas_call(
        body, name="ssm_bwd", grid=(n_chunks,),
        in_specs=[rev(sw), rev(sw), rev(gp), rev(gp), before, before, full(bre_blk), full(bim_blk), full(abr_row),
                  full(abi_row), full(cre_blk), full(cimneg_blk), full(d_row)],
        out_specs=[rev(sw), rev(gp), rev(gp), acc(gp), acc(gp), acc(sw)],
        out_shape=[jax.ShapeDtypeStruct((n_rows, sw), _ACT), jax.ShapeDtypeStruct((n_rows, gp), _ACT),
                   jax.ShapeDtypeStruct((n_rows, gp), _ACT), jax.ShapeDtypeStruct((1, gp), F32),
                   jax.ShapeDtypeStruct((1, gp), F32), jax.ShapeDtypeStruct((1, sw), F32)],
        scratch_shapes=[pltpu.VMEM((t, gp), F32), pltpu.VMEM((t, gp), F32), pltpu.VMEM((t + 8, gp), F32),
                        pltpu.VMEM((t + 8, gp), F32), pltpu.VMEM((8, gp), F32), pltpu.VMEM((8, gp), F32)],
        compiler_params=pltpu.CompilerParams(dimension_semantics=("arbitrary",),
                                             vmem_limit_bytes=_vmem_limit(blocks, resident + 4 * _nbytes((t, gp), F32))),
    )(dy, u, hre, him, hre, him, bre_blk, bim_blk, abr_row, abi_row, cre_blk, cimneg_blk, d_row)


def _in_bwd(dxm, x, dz_parts, n1g, sc1, w_parts, tm):
    n_rows, d = x.shape
    n = len(dz_parts)

    def body(dxm_ref, x_ref, *rest):
        dz_refs = rest[:n]
        g_ref, sc_ref = rest[n], rest[n + 1]
        w_refs = rest[n + 2:2 * n + 2]
        dx_ref, dsh_ref, dsc_ref, dn1_ref = rest[2 * n + 2:]
        dh = None
        for dz_ref, w_ref in zip(dz_refs, w_refs):
            term = _dot_nt(dz_ref[...], w_ref[...])
            dh = term if dh is None else dh + term
        xh, rstd = _rms_stats(x_ref[...])
        yg = xh * g_ref[...]
        _acc(dsh_ref, _rows_sum(dh))
        _acc(dsc_ref, _rows_sum(dh * yg))
        dy = dh * (1.0 + sc_ref[...])
        _acc(dn1_ref, _rows_sum(dy * xh))
        dx_ref[...] = dxm_ref[...] + _rms_bwd(dy * g_ref[...], xh, rstd)

    return _rowcall("in_bwd", body, n_rows, tm, [dxm, x, *dz_parts], [n1g, sc1, *w_parts],
                    [(d, F32)], [((1, d), F32)] * 3, temp_cols=5 * d)


def _pad_heads(w, per_head):
    lead = w.shape[:-1]
    w = w.reshape(lead + (N_HEADS, per_head))
    w = jnp.pad(w, [(0, 0)] * len(lead) + [(0, 0), (0, HEAD_PAD - per_head)])
    return w.reshape(lead + (QW,))


def _unpad_heads(w, per_head):
    lead = w.shape[:-1]
    return w.reshape(lead + (N_HEADS, HEAD_PAD))[..., :per_head].reshape(lead + (N_HEADS * per_head,))


def _cols_from_chips(g):
    ch, dep, r, cs = g.shape
    return g.transpose(1, 2, 0, 3).reshape(dep, r, ch * cs)


def _rows_from_chips(g):
    ch, dep, rs, c = g.shape
    return g.transpose(1, 0, 2, 3).reshape(dep, ch * rs, c)


def _cols_to_chips(w):
    dep, r, c = w.shape
    return w.reshape(dep, r, 4, c // 4).transpose(2, 0, 1, 3)


def _rows_to_chips(w):
    dep, r, c = w.shape
    return w.reshape(dep, 4, r // 4, c).transpose(1, 0, 2, 3)


def _block_diag(b_gxy):
    g, xx, yy = b_gxy.shape
    eye = jnp.eye(g, dtype=b_gxy.dtype)
    return (b_gxy[:, :, None, :] * eye[:, None, :, None]).reshape(g * xx, g * yy)


def _block_diag_extract(full, g):
    xx, yy = full.shape[0] // g, full.shape[1] // g
    eye = jnp.eye(g, dtype=full.dtype)
    return jnp.sum(full.reshape(g, xx, g, yy) * eye[:, None, :, None], axis=2)


def _pack_rows(arrays):
    parts = []
    for a in arrays:
        flat = a.reshape(-1)
        flat = jnp.pad(flat, (0, (-flat.shape[0]) % 1024))
        parts.append(flat.reshape(-1, 128))
    return jnp.concatenate(parts, axis=0)


def _unpack_rows(packed, shapes):
    out, row = [], 0
    for s in shapes:
        n = int(np.prod(s))
        rows = -(-n // 1024) * 8
        out.append(packed[row:row + rows].reshape(-1)[:n].reshape(s))
        row += rows
    return out


def kernel(x, c, positions, w_ada, b_ada, norm1_g, w_in, ssm_a_re, ssm_a_im, ssm_log_dt, ssm_b_re, ssm_b_im, ssm_c_re, ssm_c_im, ssm_d, w_glu, b_glu, w_a_out, q_norm_g, w_uq, kv_norm_g, w_uk, w_uv, w_b_out, w_out, norm2_g, w_gate, w_up, w_down, final_g, loss_target, m_w_ada, m_b_ada, m_norm1_g, m_w_in, m_ssm_a_re, m_ssm_a_im, m_ssm_log_dt, m_ssm_b_re, m_ssm_b_im, m_ssm_c_re, m_ssm_c_im, m_ssm_d, m_w_glu, m_b_glu, m_w_a_out, m_q_norm_g, m_w_uq, m_kv_norm_g, m_w_uk, m_w_uv, m_w_b_out, m_w_out, m_norm2_g, m_w_gate, m_w_up, m_w_down, m_final_g, v_w_ada, v_b_ada, v_norm1_g, v_w_in, v_ssm_a_re, v_ssm_a_im, v_ssm_log_dt, v_ssm_b_re, v_ssm_b_im, v_ssm_c_re, v_ssm_c_im, v_ssm_d, v_w_glu, v_b_glu, v_w_a_out, v_q_norm_g, v_w_uq, v_kv_norm_g, v_w_uk, v_w_uv, v_w_b_out, v_w_out, v_norm2_g, v_w_gate, v_w_up, v_w_down, v_final_g):
    weights = dict(w_ada=w_ada, b_ada=b_ada, norm1_g=norm1_g, w_in=w_in, ssm_a_re=ssm_a_re, ssm_a_im=ssm_a_im, ssm_log_dt=ssm_log_dt, ssm_b_re=ssm_b_re, ssm_b_im=ssm_b_im, ssm_c_re=ssm_c_re, ssm_c_im=ssm_c_im, ssm_d=ssm_d, w_glu=w_glu, b_glu=b_glu, w_a_out=w_a_out, q_norm_g=q_norm_g, w_uq=w_uq, kv_norm_g=kv_norm_g, w_uk=w_uk, w_uv=w_uv, w_b_out=w_b_out, w_out=w_out, norm2_g=norm2_g, w_gate=w_gate, w_up=w_up, w_down=w_down, final_g=final_g)
    mom_m = dict(w_ada=m_w_ada, b_ada=m_b_ada, norm1_g=m_norm1_g, w_in=m_w_in, ssm_a_re=m_ssm_a_re, ssm_a_im=m_ssm_a_im, ssm_log_dt=m_ssm_log_dt, ssm_b_re=m_ssm_b_re, ssm_b_im=m_ssm_b_im, ssm_c_re=m_ssm_c_re, ssm_c_im=m_ssm_c_im, ssm_d=m_ssm_d, w_glu=m_w_glu, b_glu=m_b_glu, w_a_out=m_w_a_out, q_norm_g=m_q_norm_g, w_uq=m_w_uq, kv_norm_g=m_kv_norm_g, w_uk=m_w_uk, w_uv=m_w_uv, w_b_out=m_w_b_out, w_out=m_w_out, norm2_g=m_norm2_g, w_gate=m_w_gate, w_up=m_w_up, w_down=m_w_down, final_g=m_final_g)
    mom_v = dict(w_ada=v_w_ada, b_ada=v_b_ada, norm1_g=v_norm1_g, w_in=v_w_in, ssm_a_re=v_ssm_a_re, ssm_a_im=v_ssm_a_im, ssm_log_dt=v_ssm_log_dt, ssm_b_re=v_ssm_b_re, ssm_b_im=v_ssm_b_im, ssm_c_re=v_ssm_c_re, ssm_c_im=v_ssm_c_im, ssm_d=v_ssm_d, w_glu=v_w_glu, b_glu=v_b_glu, w_a_out=v_w_a_out, q_norm_g=v_q_norm_g, w_uq=v_w_uq, kv_norm_g=v_kv_norm_g, w_uk=v_w_uk, w_uv=v_w_uv, w_b_out=v_w_b_out, w_out=v_w_out, norm2_g=v_norm2_g, w_gate=v_w_gate, w_up=v_w_up, w_down=v_w_down, final_g=v_final_g)
    names = list(weights)

    depth = w_in.shape[0]
    seq, d = x.shape[1], x.shape[2]
    sw = ssm_d.shape[1]
    groups, n_state, n_chan = ssm_b_re.shape[1:]
    gp = groups * n_state
    ql, kl = q_norm_g.shape[1], kv_norm_g.shape[1]
    nope = w_uk.shape[2] * 4 // N_HEADS
    vdim = w_uv.shape[2] * 4 // N_HEADS
    qk_dim = nope + QK_ROPE
    scale = qk_dim ** -0.5
    tm = _tile(seq, 256, 16)
    tm_ffn = _tile(seq, 128, 16)
    me = 4 * lax.axis_index("x") + 2 * lax.axis_index("y") + lax.axis_index("c")
    chip = 2 * lax.axis_index("x") + lax.axis_index("y")

    xs = x.reshape(seq, d)
    target = loss_target.reshape(seq, d)
    pos_f = positions.astype(F32)
    pos_col = pos_f.reshape(seq, 1)
    pos_row = pos_f.reshape(1, seq)

    (c_all,) = _exchange("gather_c", "gather8", [c])
    c_all = c_all.reshape(8, d)
    ada_cols = w_ada.shape[2]
    mod_part = _mod_fwd(c_all, w_ada.reshape(depth * d, ada_cols), depth)
    (mod_all,) = _exchange("gather_mod", "gather8", [mod_part])
    mod_all = mod_all.reshape(4, 2, depth, 8, ada_cols)[:, 0]
    mod_me = lax.dynamic_index_in_dim(mod_all, me, axis=2, keepdims=False)
    mod = mod_me.transpose(1, 0, 2).reshape(depth, 4 * ada_cols) + b_ada
    mod = mod.reshape(depth, 6, 1, d)

    big = ["w_in", "w_glu", "w_a_out", "w_uq", "w_uk", "w_uv", "w_b_out", "w_out", "w_gate", "w_up", "w_down"]
    row_sharded = {"w_glu", "w_out", "w_down"}
    gathered = _exchange("gather_weights", "gather4", [weights[n].astype(_MM) for n in big])
    full = {n: (_rows_from_chips(g) if n in row_sharded else _cols_from_chips(g)) for n, g in zip(big, gathered)}
    o1, o2, o3, o4, o5 = sw, sw + ql, sw + ql + kl, sw + ql + kl + QK_ROPE, sw + ql + kl + QK_ROPE + d
    wi = full["w_in"]
    w_u, w_cq, w_ckv, w_ga, w_gb = wi[:, :, :o1], wi[:, :, o1:o2], wi[:, :, o2:o3], wi[:, :, o4:o5], wi[:, :, o5:]
    w_kr = jnp.pad(wi[:, :, o3:o4], ((0, 0), (0, 0), (nope, HEAD_PAD - nope - QK_ROPE)))
    wuq_p = _pad_heads(full["w_uq"], qk_dim)
    wuk_p = _pad_heads(full["w_uk"], nope)
    wuv_p = _pad_heads(full["w_uv"], vdim)
    wb_p = _pad_heads(full["w_b_out"].transpose(0, 2, 1), vdim).transpose(0, 2, 1)

    inv_freq = ROPE_BASE ** (-jnp.arange(0, QK_ROPE, 2, dtype=F32) / QK_ROPE)
    inv_lane = jnp.pad(jnp.concatenate([inv_freq, inv_freq]), (nope, HEAD_PAD - nope - QK_ROPE)).reshape(1, HEAD_PAD)
    rc, rs1, rs2 = _rope_tables(pos_col, inv_lane, nope)
    a_re_col = ssm_a_re.reshape(depth * gp, 1)
    a_im_col = ssm_a_im.reshape(depth * gp, 1)
    ldt_col = jnp.broadcast_to(ssm_log_dt[:, :, None], (depth, groups, n_state)).reshape(depth * gp, 1)
    b_re2, b_im2 = ssm_b_re.reshape(depth * gp, n_chan), ssm_b_im.reshape(depth * gp, n_chan)
    abr, abi, bbr, bbi = _ssm_disc_fwd(a_re_col, a_im_col, ldt_col, b_re2, b_im2)
    abr_rows, abi_rows = abr.reshape(depth, 1, gp), abi.reshape(depth, 1, gp)
    bbr, bbi = bbr.reshape(depth, groups, n_state, n_chan), bbi.reshape(depth, groups, n_state, n_chan)

    saved = []
    xl = xs
    for l in range(depth):
        sh1, sc1, g1, sh2, sc2, g2 = (mod[l, j] for j in range(6))
        n1g, n2g = norm1_g[l].reshape(1, d), norm2_g[l].reshape(1, d)
        w_parts = [w_u[l], w_cq[l], w_ckv[l], w_kr[l], w_ga[l], w_gb[l]]
        hb, u, cq, ckv, kr, ga, gb = _in_fwd(xl, n1g, sc1, sh1, w_parts, tm)
        bre_blk = _block_diag(bbr[l].transpose(0, 2, 1)).astype(_MM)
        bim_blk = _block_diag(bbi[l].transpose(0, 2, 1)).astype(_MM)
        cre_blk = _block_diag(ssm_c_re[l].transpose(0, 2, 1)).astype(_MM)
        cimneg_blk = _block_diag(-ssm_c_im[l].transpose(0, 2, 1)).astype(_MM)
        d_row = ssm_d[l].reshape(1, sw)
        ssm_w = (bre_blk, bim_blk, abr_rows[l], abi_rows[l], cre_blk, cimneg_blk, d_row)
        ypre, hre, him = _ssm_fwd(u, *ssm_w)
        gq, gkv = q_norm_g[l].reshape(1, ql), kv_norm_g[l].reshape(1, kl)
        q, k, v, cqn, ckvn = _mla_prep_fwd(cq, ckv, kr, rc, rs1, rs2, gq, gkv, wuq_p[l], wuk_p[l], wuv_p[l], tm)
        o, lse = _attn_fwd(q, k, v, pos_col, pos_row, scale)
        bglu = b_glu[l].reshape(1, sw)
        yg, ya, yb, merged, mo, xmid = _mix_fwd(ypre, o, ga, gb, xl, g1, bglu, full["w_glu"][l], full["w_a_out"][l],
                                                 wb_p[l], full["w_out"][l], tm)
        hb2, fa, fb, dn, xout = _ffn_fwd(xmid, n2g, sc2, sh2, g2, full["w_gate"][l], full["w_up"][l], full["w_down"][l], tm_ffn)
        saved.append(dict(x=xl, hb=hb, u=u, cq=cq, ckv=ckv, ga=ga, gb=gb, ssm_w=ssm_w, ypre=ypre, hre=hre, him=him,
                          q=q, k=k, v=v, cqn=cqn, ckvn=ckvn, o=o, lse=lse, yg=yg, ya=ya, yb=yb, merged=merged, mo=mo,
                          xmid=xmid, hb2=hb2, fa=fa, fb=fb, dn=dn, w_parts=w_parts))
        xl = xout

    dx, loss_acc, g_final = _head(xl, final_g.reshape(1, d), target, tm)
    loss = lax.psum(loss_acc[0, 0], ("x", "y", "c"))

    per_layer = ["w_gate", "w_up", "w_down", "norm2_g", "w_out", "w_a_out", "w_b_out", "w_glu", "b_glu", "w_uq", "w_uk",
                 "w_uv", "q_norm_g", "kv_norm_g", "ssm_d", "ssm_c_re", "ssm_c_im", "w_in", "norm1_g"]
    grads = {n: [None] * depth for n in per_layer}
    dmod = [None] * depth
    for l in reversed(range(depth)):
        s = saved[l]
        sh1, sc1, g1, sh2, sc2, g2 = (mod[l, j] for j in range(6))
        n1g, n2g = norm1_g[l].reshape(1, d), norm2_g[l].reshape(1, d)
        dxm, da, db, fb16, dd, dg2, dsh2, dsc2, dn2 = _ffn_bwd(
            dx, s["xmid"], s["fa"], s["fb"], s["dn"], n2g, sc2, g2, full["w_gate"][l], full["w_up"][l], full["w_down"][l], tm_ffn)
        grads["w_gate"][l] = _mm_tn("dw_gate", s["hb2"], da)
        grads["w_up"][l] = _mm_tn("dw_up", s["hb2"], db)
        grads["w_down"][l] = _mm_tn("dw_down", fb16, dd)
        grads["norm2_g"][l] = dn2.reshape(d)

        bglu = b_glu[l].reshape(1, sw)
        dmo, dya, dyb, dt, ys, dga, dgb, dypre, do, dg1, dbglu = _mix_bwd(
            dxm, s["mo"], s["ya"], s["yb"], s["ga"], s["gb"], s["ypre"], g1, bglu, full["w_glu"][l], full["w_a_out"][l],
            wb_p[l], full["w_out"][l], tm)
        grads["w_out"][l] = _mm_tn("dw_out", s["merged"], dmo)
        grads["w_a_out"][l] = _mm_tn("dw_a_out", s["yg"], dya)
        dwb_p = _mm_tn("dw_b_out", s["o"], dyb)
        grads["w_b_out"][l] = _unpad_heads(dwb_p.T, vdim).T
        grads["w_glu"][l] = _mm_tn("dw_glu", ys, dt)
        grads["b_glu"][l] = dbglu.reshape(sw)

        dq, dk, dv = _attn_bwd(s["q"], s["k"], s["v"], do, s["o"], s["lse"], pos_col, pos_row, scale)
        gq, gkv = q_norm_g[l].reshape(1, ql), kv_norm_g[l].reshape(1, kl)
        dqp, dcq, dckv, dkr, dgq, dgkv = _mla_prep_bwd(dq, dk, dv, s["cq"], s["ckv"], rc, rs1, rs2, gq, gkv,
                                                       wuq_p[l], wuk_p[l], wuv_p[l], nope, tm)
        grads["w_uq"][l] = _unpad_heads(_mm_tn("dw_uq", s["cqn"], dqp), qk_dim)
        grads["w_uk"][l] = _unpad_heads(_mm_tn("dw_uk", s["ckvn"], dk), nope)
        grads["w_uv"][l] = _unpad_heads(_mm_tn("dw_uv", s["ckvn"], dv), vdim)
        grads["q_norm_g"][l] = dgq.reshape(ql)
        grads["kv_norm_g"][l] = dgkv.reshape(kl)

        du, gre, gim, dar, dai, ddskip = _ssm_bwd(dypre, s["u"], s["hre"], s["him"], *s["ssm_w"])
        grads["ssm_d"][l] = ddskip.reshape(sw)
        d_bre = _block_diag_extract(_mm_tn("d_bre", s["u"], gre), groups).transpose(0, 2, 1)
        d_bim = _block_diag_extract(_mm_tn("d_bim", s["u"], gim), groups).transpose(0, 2, 1)
        grads["ssm_c_re"][l] = _block_diag_extract(_mm_tn("d_cre", s["hre"], dypre), groups).transpose(0, 2, 1)
        grads["ssm_c_im"][l] = -_block_diag_extract(_mm_tn("d_cim", s["him"], dypre), groups).transpose(0, 2, 1)
        s["disc_grads"] = (dar.reshape(gp, 1), dai.reshape(gp, 1), d_bre.reshape(gp, n_chan), d_bim.reshape(gp, n_chan))

        dz_parts = [du, dcq, dckv, dkr, dga, dgb]
        dx, dsh1, dsc1, dn1 = _in_bwd(dxm, s["x"], dz_parts, n1g, sc1, s["w_parts"], tm)
        dw_parts = [_mm_tn("dw_in_%d" % j, s["hb"], dz) for j, dz in enumerate(dz_parts)]
        dw_parts[3] = dw_parts[3][:, nope:nope + QK_ROPE]
        grads["w_in"][l] = jnp.concatenate(dw_parts, axis=1)
        grads["norm1_g"][l] = dn1.reshape(d)
        dmod[l] = jnp.concatenate([dsh1, dsc1, dg1, dsh2, dsc2, dg2], axis=1).reshape(6 * d)
    grad_x = dx.reshape(x.shape)

    disc = [jnp.concatenate([saved[l]["disc_grads"][j] for l in range(depth)], axis=0) for j in range(4)]
    da_re, da_im, dldt, db_re, db_im = _ssm_disc_bwd(a_re_col, a_im_col, ldt_col, b_re2, b_im2, *disc)
    stacked = {n: jnp.stack(v) for n, v in grads.items()}
    stacked["ssm_a_re"] = da_re.reshape(ssm_a_re.shape)
    stacked["ssm_a_im"] = da_im.reshape(ssm_a_im.shape)
    stacked["ssm_log_dt"] = _lane_sum(dldt.reshape(depth * groups, n_state)).reshape(ssm_log_dt.shape)
    stacked["ssm_b_re"] = db_re.reshape(ssm_b_re.shape)
    stacked["ssm_b_im"] = db_im.reshape(ssm_b_im.shape)
    stacked["final_g"] = g_final.reshape(d)
    stacked["b_ada"] = jnp.stack(dmod)

    small = [n for n in names if n not in big and n != "w_ada"]
    small_shapes = [weights[n].shape for n in small]
    (small_all,) = _exchange("gather_small", "gather8", [_pack_rows([stacked[n] for n in small])])
    sg, sd, sm, sv = _adamw("adamw_small", [small_all], _pack_rows([weights[n] for n in small]),
                            _pack_rows([mom_m[n] for n in small]), _pack_rows([mom_v[n] for n in small]))
    out_g = dict(zip(small, _unpack_rows(sg, small_shapes)))
    out_d = dict(zip(small, _unpack_rows(sd, small_shapes)))
    out_m = dict(zip(small, _unpack_rows(sm, small_shapes)))
    out_v = dict(zip(small, _unpack_rows(sv, small_shapes)))

    n_dmod = depth * 6 * d
    dmod_all = small_all[:, :n_dmod // 128].reshape(8, depth, 6 * d)
    dmod_cols = lax.dynamic_slice_in_dim(dmod_all, chip * ada_cols, ada_cols, axis=2)
    g_wada = _wada_bwd(c_all, dmod_cols.transpose(1, 0, 2).reshape(depth * 8, ada_cols), depth)
    res = _adamw("adamw_w_ada", [g_wada], w_ada.reshape(depth * d, ada_cols), m_w_ada.reshape(depth * d, ada_cols),
                 v_w_ada.reshape(depth * d, ada_cols))
    out_g["w_ada"], out_d["w_ada"], out_m["w_ada"], out_v["w_ada"] = (r.reshape(w_ada.shape) for r in res)

    to_chips = [(_rows_to_chips if n in row_sharded else _cols_to_chips)(stacked[n]) for n in big]
    landed = _exchange("scatter_grads", "scatter4", to_chips)
    partial = [_sum_slots("sum_" + n, r.reshape(4, -1, r.shape[-1])) for n, r in zip(big, landed)]
    sibling = _exchange("swap_partials", "swap", partial)
    for n, mine, theirs in zip(big, partial, sibling):
        shp = weights[n].shape
        as2d = lambda a: a.reshape(-1, shp[-1])
        res = _adamw("adamw_" + n, [mine, theirs], as2d(weights[n]), as2d(mom_m[n]), as2d(mom_v[n]))
        out_g[n], out_d[n], out_m[n], out_v[n] = (r.reshape(shp) for r in res)

    return (loss, grad_x, *[out_g[n] for n in names], *[out_d[n] for n in names], *[out_m[n] for n in names],
            *[out_v[n] for n in names])
```

```python
import functools
import math

import numpy as np
import jax
import jax.numpy as jnp
from jax import lax
from jax.experimental import pallas as pl
from jax.experimental.pallas import tpu as pltpu

F32 = jnp.float32
_MM = jnp.bfloat16
_ACT = jnp.bfloat16
_WIRE = jnp.bfloat16

N_HEADS = 8
QK_ROPE = 32
HEAD_PAD = 128
QW = N_HEADS * HEAD_PAD
ROPE_BASE = 10000.0
EPS = 1e-6
DT_MIN = 1e-3
ADAM_LR = 0.001
ADAM_B1 = 0.9
ADAM_B2 = 0.999
ADAM_EPS = 1e-08
ADAM_WD = 0.01
ADAM_STEP = 10
NEG_INF = -1e30
LOG2_E = math.log2(math.e)
ATTN_HEADS_PER_STEP = 4
ATTN_BWD_HEADS_PER_STEP = 2

V7X_VMEM_BYTES = 64 * 1024 * 1024
VMEM_RESERVE_BYTES = 6 * 1024 * 1024
MESH = pl.DeviceIdType.MESH
ANY = pl.BlockSpec(memory_space=pl.ANY)


def _vmem_limit(block_bytes, temp_bytes):
    want = 2 * block_bytes + temp_bytes
    return int(min(V7X_VMEM_BYTES - VMEM_RESERVE_BYTES, max(want, 32 * 1024 * 1024)))


def _nbytes(shape, dtype):
    return int(np.prod(shape)) * jnp.dtype(dtype).itemsize


def _tile(n, target, mult=8):
    t = min(n, target)
    while t >= mult:
        if n % t == 0 and t % mult == 0:
            return t
        t -= 1
    return n


def _dot(a, b):
    return jnp.dot(a.astype(_MM), b.astype(_MM), preferred_element_type=F32)


def _dot_nt(a, b):
    return lax.dot_general(a.astype(_MM), b.astype(_MM), (((1,), (1,)), ((), ())), preferred_element_type=F32)


def _dot_tn(a, b):
    return lax.dot_general(a.astype(_MM), b.astype(_MM), (((0,), (0,)), ((), ())), preferred_element_type=F32)


def _sigmoid(x):
    return jax.nn.sigmoid(x)


_GELU_K = math.sqrt(2.0 / math.pi)


def _gelu(x):
    return x * (0.5 * (1.0 + jnp.tanh(_GELU_K * (x + 0.044715 * (x * x * x)))))


def _gelu_grad(x):
    th = jnp.tanh(_GELU_K * (x + 0.044715 * (x * x * x)))
    return 0.5 * (1.0 + th) + 0.5 * x * (1.0 - th * th) * (_GELU_K * (1.0 + 3.0 * 0.044715 * (x * x)))


def _rows_sum(v):
    return jnp.sum(v, axis=0, keepdims=True)


def _rms_stats(x):
    rstd = lax.rsqrt(jnp.mean(x * x, axis=-1, keepdims=True) + EPS)
    return x * rstd, rstd


def _rms_bwd(dxh, xh, rstd):
    return rstd * (dxh - xh * jnp.mean(dxh * xh, axis=-1, keepdims=True))


def _rowcall(name, body, n_rows, tm, row_ins, full_ins, row_outs, acc_outs=(), temp_cols=0):
    grid = (n_rows // tm,)
    in_specs = [pl.BlockSpec((tm, a.shape[1]), lambda i: (i, 0)) for a in row_ins]
    in_specs += [pl.BlockSpec(a.shape, lambda i: (0, 0), pipeline_mode=pl.Buffered(1)) for a in full_ins]
    out_shape = [jax.ShapeDtypeStruct((n_rows, c), dt) for c, dt in row_outs]
    out_shape += [jax.ShapeDtypeStruct(s, dt) for s, dt in acc_outs]
    out_specs = [pl.BlockSpec((tm, c), lambda i: (i, 0)) for c, _ in row_outs]
    out_specs += [pl.BlockSpec(s, lambda i: (0, 0)) for s, _ in acc_outs]
    blocks = sum(_nbytes((tm, a.shape[1]), a.dtype) for a in row_ins)
    blocks += sum(_nbytes((tm, c), dt) for c, dt in row_outs) + sum(_nbytes(s, dt) for s, dt in acc_outs)
    resident = sum(_nbytes(a.shape, a.dtype) for a in full_ins)
    limit = _vmem_limit(blocks, resident + _nbytes((tm, temp_cols), F32))
    res = pl.pallas_call(
        body, name=name, grid=grid, in_specs=in_specs, out_specs=out_specs, out_shape=out_shape,
        compiler_params=pltpu.CompilerParams(
            dimension_semantics=("arbitrary" if acc_outs else "parallel",), vmem_limit_bytes=limit),
    )(*row_ins, *full_ins)
    return res


def _first_step():
    return pl.program_id(0) == 0


def _acc(ref, val):
    @pl.when(_first_step())
    def _():
        ref[...] = val

    @pl.when(jnp.logical_not(_first_step()))
    def _():
        ref[...] += val


def _mm_tn(name, a, g):
    n_rows, k = a.shape
    n = g.shape[1]
    tk = k if k <= 1024 else _tile(k, 1408, 128)
    tn = n if n <= 1024 else _tile(n, 1408, 128)
    tl = _tile(n_rows, 512, 16)

    def body(a_ref, g_ref, o_ref):
        @pl.when(pl.program_id(2) == 0)
        def _():
            o_ref[...] = jnp.zeros_like(o_ref)
        o_ref[...] += _dot_tn(a_ref[...], g_ref[...])

    blocks = _nbytes((tl, tk), a.dtype) + _nbytes((tl, tn), g.dtype) + _nbytes((tk, tn), F32)
    return pl.pallas_call(
        body, name=name, grid=(k // tk, n // tn, n_rows // tl),
        in_specs=[pl.BlockSpec((tl, tk), lambda i, j, l: (l, i)), pl.BlockSpec((tl, tn), lambda i, j, l: (l, j))],
        out_specs=pl.BlockSpec((tk, tn), lambda i, j, l: (i, j)),
        out_shape=jax.ShapeDtypeStruct((k, n), F32),
        compiler_params=pltpu.CompilerParams(
            dimension_semantics=("parallel", "parallel", "arbitrary"),
            vmem_limit_bytes=_vmem_limit(blocks, 2 * _nbytes((tl, max(tk, tn)), F32) + _nbytes((tk, tn), F32))),
    )(a, g)


def _place():
    return lax.axis_index("x"), lax.axis_index("y"), lax.axis_index("c")


def _flip(v, bit):
    return 1 - v if bit else v


def _exchange(name, mode, arrays):
    n = len(arrays)
    if mode == "gather8":
        rel = [((k >> 2) & 1, (k >> 1) & 1, k & 1) for k in range(1, 8)]
        out_shape = [jax.ShapeDtypeStruct((8,) + a.shape, a.dtype) for a in arrays]
    elif mode == "gather4":
        rel = [((k >> 1) & 1, k & 1, 0) for k in range(1, 4)]
        out_shape = [jax.ShapeDtypeStruct((4,) + a.shape, a.dtype) for a in arrays]
    elif mode == "scatter4":
        rel = [((k >> 1) & 1, k & 1, 0) for k in range(1, 4)]
        out_shape = [jax.ShapeDtypeStruct(a.shape, a.dtype) for a in arrays]
    else:
        rel = [(0, 0, 1)]
        out_shape = [jax.ShapeDtypeStruct(a.shape, a.dtype) for a in arrays]
    n_rel = len(rel)

    def body(*refs):
        ins, outs = refs[:n], refs[n:2 * n]
        send_sems, recv_sems, local_sems = refs[2 * n:]
        x, y, c = _place()

        def slot(px, py, pc):
            return 4 * px + 2 * py + pc if mode == "gather8" else 2 * px + py

        mine = slot(x, y, c)
        local = []
        if mode != "swap":
            for a in range(n):
                src = ins[a].at[mine] if mode == "scatter4" else ins[a]
                local.append(pltpu.make_async_copy(src, outs[a].at[mine], local_sems.at[a]))
            for cp in local:
                cp.start()

        def remote(r, a):
            px, py, pc = _flip(x, rel[r][0]), _flip(y, rel[r][1]), _flip(c, rel[r][2])
            theirs = slot(px, py, pc)
            if mode == "swap":
                src, dst_there, dst_here = ins[a], outs[a], outs[a]
            elif mode == "scatter4":
                src, dst_there, dst_here = ins[a].at[theirs], outs[a].at[mine], outs[a].at[theirs]
            else:
                src, dst_there, dst_here = ins[a], outs[a].at[mine], outs[a].at[theirs]
            k = r * n + a
            push = pltpu.make_async_remote_copy(src_ref=src, dst_ref=dst_there, send_sem=send_sems.at[k],
                                                recv_sem=recv_sems.at[k], device_id=(px, py, pc), device_id_type=MESH)
            land = pltpu.make_async_remote_copy(src_ref=src, dst_ref=dst_here, send_sem=send_sems.at[k],
                                                recv_sem=recv_sems.at[k], device_id=(px, py, pc), device_id_type=MESH)
            return push, land

        copies = [remote(r, a) for r in range(n_rel) for a in range(n)]
        for push, _ in copies:
            push.start()
        for _, land in copies:
            land.wait_recv()
        for push, _ in copies:
            push.wait_send()
        for cp in local:
            cp.wait()

    return pl.pallas_call(
        body, name=name, in_specs=[ANY] * n, out_specs=[ANY] * n, out_shape=out_shape,
        scratch_shapes=[pltpu.SemaphoreType.DMA((n_rel * n,)), pltpu.SemaphoreType.DMA((n_rel * n,)),
                        pltpu.SemaphoreType.DMA((max(n, 1),))],
    )(*arrays)


def _sum_slots(name, stacked):
    p, rows, cols = stacked.shape
    tr = _tile(rows, 256)

    def body(s_ref, o_ref):
        acc = s_ref[0].astype(F32)
        for j in range(1, p):
            acc = acc + s_ref[j].astype(F32)
        o_ref[...] = acc

    return pl.pallas_call(
        body, name=name, grid=(rows // tr,),
        in_specs=[pl.BlockSpec((p, tr, cols), lambda i: (0, i, 0))],
        out_specs=pl.BlockSpec((tr, cols), lambda i: (i, 0)),
        out_shape=jax.ShapeDtypeStruct((rows, cols), F32),
        compiler_params=pltpu.CompilerParams(dimension_semantics=("parallel",)),
    )(stacked)


def _adamw(name, parts, w, m, v):
    rows, cols = w.shape
    tr = _tile(rows, 256)
    n_parts = len(parts)

    def body(*refs):
        part_refs = refs[:n_parts]
        w_ref, m_ref, v_ref, g_out, d_out, m_out, v_out = refs[n_parts:]
        g = None
        for pr in part_refs:
            if len(pr.shape) == 3:
                for j in range(pr.shape[0]):
                    g = pr[j] if g is None else g + pr[j]
            else:
                g = pr[...] if g is None else g + pr[...]
        m_new = ADAM_B1 * m_ref[...] + (1.0 - ADAM_B1) * g
        v_new = ADAM_B2 * v_ref[...] + (1.0 - ADAM_B2) * jnp.square(g)
        m_hat = m_new / (1.0 - ADAM_B1 ** ADAM_STEP)
        v_hat = v_new / (1.0 - ADAM_B2 ** ADAM_STEP)
        g_out[...] = g
        d_out[...] = -ADAM_LR * (m_hat / (jnp.sqrt(v_hat) + ADAM_EPS) + ADAM_WD * w_ref[...])
        m_out[...] = m_new
        v_out[...] = v_new

    spec2 = pl.BlockSpec((tr, cols), lambda i: (i, 0))
    in_specs = [pl.BlockSpec((p.shape[0], tr, cols), lambda i: (0, i, 0)) if p.ndim == 3 else spec2 for p in parts]
    blocks = sum(_nbytes((p.shape[0] if p.ndim == 3 else 1, tr, cols), F32) for p in parts) + 7 * _nbytes((tr, cols), F32)
    return pl.pallas_call(
        body, name=name, grid=(rows // tr,),
        in_specs=in_specs + [spec2] * 3, out_specs=[spec2] * 4,
        out_shape=[jax.ShapeDtypeStruct((rows, cols), F32)] * 4,
        compiler_params=pltpu.CompilerParams(dimension_semantics=("parallel",),
                                             vmem_limit_bytes=_vmem_limit(blocks, 4 * _nbytes((tr, cols), F32))),
    )(*parts, w, m, v)


def _mod_fwd(c_all, w_ada2d, depth):
    nb, d = c_all.shape
    cols = w_ada2d.shape[1]
    tn = _tile(cols, 512, 128)

    def body(c_ref, w_ref, o_ref):
        cv = c_ref[...]
        o_ref[...] = _dot(cv * _sigmoid(cv), w_ref[...])

    return pl.pallas_call(
        body, name="mod_fwd", grid=(depth, cols // tn),
        in_specs=[pl.BlockSpec((nb, d), lambda l, j: (0, 0)), pl.BlockSpec((d, tn), lambda l, j: (l, j))],
        out_specs=pl.BlockSpec((nb, tn), lambda l, j: (l, j)),
        out_shape=jax.ShapeDtypeStruct((depth * nb, cols), F32),
        compiler_params=pltpu.CompilerParams(dimension_semantics=("parallel", "parallel")),
    )(c_all, w_ada2d)


def _wada_bwd(c_all, dmod2d, depth):
    nb, d = c_all.shape
    cols = dmod2d.shape[1]
    tn = _tile(cols, 512, 128)

    def body(c_ref, g_ref, o_ref):
        cv = c_ref[...]
        o_ref[...] = _dot_tn(cv * _sigmoid(cv), g_ref[...])

    return pl.pallas_call(
        body, name="wada_bwd", grid=(depth, cols // tn),
        in_specs=[pl.BlockSpec((nb, d), lambda l, j: (0, 0)), pl.BlockSpec((nb, tn), lambda l, j: (l, j))],
        out_specs=pl.BlockSpec((d, tn), lambda l, j: (l, j)),
        out_shape=jax.ShapeDtypeStruct((depth * d, cols), F32),
        compiler_params=pltpu.CompilerParams(dimension_semantics=("parallel", "parallel")),
    )(c_all, dmod2d)


def _rope_tables(pos_col, inv_freq_lane, nope):
    n_rows = pos_col.shape[0]
    tm = _tile(n_rows, 512)
    half = QK_ROPE // 2

    def body(p_ref, f_ref, c_ref, s1_ref, s2_ref):
        ang = p_ref[...] * f_ref[...]
        lane = lax.broadcasted_iota(jnp.int32, ang.shape, 1)
        first = (lane >= nope) & (lane < nope + half)
        second = (lane >= nope + half) & (lane < nope + 2 * half)
        cos, sin = jnp.cos(ang), jnp.sin(ang)
        c_ref[...] = jnp.where(first | second, cos, 1.0)
        s1_ref[...] = jnp.where(first, -sin, 0.0)
        s2_ref[...] = jnp.where(second, sin, 0.0)

    return _rowcall("rope_tables", body, n_rows, tm, [pos_col], [inv_freq_lane], [(HEAD_PAD, F32)] * 3)


def _rope(q, c, s1, s2):
    w = q.shape[1]
    return q * c + pltpu.roll(q, w - QK_ROPE // 2, axis=1) * s1 + pltpu.roll(q, QK_ROPE // 2, axis=1) * s2


def _rope_adjoint(dr, c, s1, s2):
    w = dr.shape[1]
    return dr * c + pltpu.roll(dr * s1, QK_ROPE // 2, axis=1) + pltpu.roll(dr * s2, w - QK_ROPE // 2, axis=1)


def _ssm_disc(ar, ai, log_dt, br, bi):
    dt = jnp.exp(log_dt)
    mag = jnp.exp(ar * dt)
    abr = mag * jnp.cos(ai * dt)
    abi = mag * jnp.sin(ai * dt)
    den = ar * ar + ai * ai
    nr = abr - 1.0
    ni = abi
    cr = (nr * ar + ni * ai) / den
    ci = (ni * ar - nr * ai) / den
    return abr, abi, cr * br - ci * bi, cr * bi + ci * br


def _ssm_disc_fwd(ar, ai, log_dt, br, bi):
    n_rows, m = br.shape
    tm = _tile(n_rows, 1024)

    def body(ar_ref, ai_ref, dt_ref, br_ref, bi_ref, o1, o2, o3, o4):
        o1[...], o2[...], o3[...], o4[...] = _ssm_disc(ar_ref[...], ai_ref[...], dt_ref[...], br_ref[...], bi_ref[...])

    return _rowcall("ssm_disc_fwd", body, n_rows, tm, [ar, ai, log_dt, br, bi], [],
                    [(1, F32), (1, F32), (m, F32), (m, F32)])


def _ssm_disc_bwd(ar, ai, log_dt, br, bi, g_abr, g_abi, g_bbr, g_bbi):
    n_rows, m = br.shape
    tm = _tile(n_rows, 1024)

    def body(ar_ref, ai_ref, dt_ref, br_ref, bi_ref, g1, g2, g3, g4, o1, o2, o3, o4, o5):
        _, vjp = jax.vjp(_ssm_disc, ar_ref[...], ai_ref[...], dt_ref[...], br_ref[...], bi_ref[...])
        o1[...], o2[...], o3[...], o4[...], o5[...] = vjp((g1[...], g2[...], g3[...], g4[...]))

    return _rowcall("ssm_disc_bwd", body, n_rows, tm, [ar, ai, log_dt, br, bi, g_abr, g_abi, g_bbr, g_bbi], [],
                    [(1, F32), (1, F32), (1, F32), (m, F32), (m, F32)])


def _lane_sum(v2d):
    def body(v_ref, o_ref):
        o_ref[...] = jnp.sum(v_ref[...], axis=1, keepdims=True)
    return pl.pallas_call(body, name="lane_sum", out_shape=jax.ShapeDtypeStruct((v2d.shape[0], 1), F32))(v2d)


def _in_fwd(x, n1g, sc1, sh1, w_parts, tm):
    n_rows = x.shape[0]
    widths = [w.shape[1] for w in w_parts]

    def body(x_ref, g_ref, sc_ref, sh_ref, *rest):
        w_refs, (hb_ref, *z_refs) = rest[:len(w_parts)], rest[len(w_parts):]
        xh, _ = _rms_stats(x_ref[...])
        h = (xh * g_ref[...]) * (1.0 + sc_ref[...]) + sh_ref[...]
        hb = h.astype(_MM)
        hb_ref[...] = hb.astype(_ACT)
        for w_ref, z_ref in zip(w_refs, z_refs):
            z_ref[...] = _dot(hb, w_ref[...])

    return _rowcall("in_fwd", body, n_rows, tm, [x], [n1g, sc1, sh1, *w_parts],
                    [(x.shape[1], _ACT)] + [(w, F32) for w in widths], temp_cols=4 * x.shape[1])


def _ssm_fwd(u, bre_blk, bim_blk, abr_row, abi_row, cre_blk, cimneg_blk, d_row):
    n_rows, sw = u.shape
    gp = abr_row.shape[1]
    t = _tile(n_rows, 256)

    def body(u_ref, bre_ref, bim_ref, ar_ref, ai_ref, cre_ref, cim_ref, d_ref, y_ref, hre_ref, him_ref, cr, ci):
        @pl.when(_first_step())
        def _():
            cr[...] = jnp.zeros_like(cr)
            ci[...] = jnp.zeros_like(ci)

        uv = u_ref[...]
        ub = uv.astype(_MM)
        hre_ref[...] = _dot(ub, bre_ref[...])
        him_ref[...] = _dot(ub, bim_ref[...])
        a_r, a_i = ar_ref[...], ai_ref[...]

        def step(k, carry):
            pr, pi = carry
            row = pl.ds(k, 1)
            hr = a_r * pr - a_i * pi + hre_ref[row, :]
            hi = a_r * pi + a_i * pr + him_ref[row, :]
            hre_ref[row, :] = hr
            him_ref[row, :] = hi
            return hr, hi

        pr, pi = lax.fori_loop(0, t, step, (cr[0:1, :], ci[0:1, :]), unroll=8)
        cr[0:1, :] = pr
        ci[0:1, :] = pi
        y_ref[...] = _dot(hre_ref[...], cre_ref[...]) + _dot(him_ref[...], cim_ref[...]) + d_ref[...] * uv

    row = lambda c: pl.BlockSpec((t, c), lambda i: (i, 0))
    full = lambda a: pl.BlockSpec(a.shape, lambda i: (0, 0), pipeline_mode=pl.Buffered(1))
    blocks = _nbytes((t, sw), F32) * 2 + 2 * _nbytes((t, gp), F32)
    resident = 4 * _nbytes((sw, gp), _MM)
    return pl.pallas_call(
        body, name="ssm_fwd", grid=(n_rows // t,),
        in_specs=[row(sw), full(bre_blk), full(bim_blk), full(abr_row), full(abi_row), full(cre_blk),
                  full(cimneg_blk), full(d_row)],
        out_specs=[row(sw), row(gp), row(gp)],
        out_shape=[jax.ShapeDtypeStruct((n_rows, sw), F32), jax.ShapeDtypeStruct((n_rows, gp), F32),
                   jax.ShapeDtypeStruct((n_rows, gp), F32)],
        scratch_shapes=[pltpu.VMEM((8, gp), F32), pltpu.VMEM((8, gp), F32)],
        compiler_params=pltpu.CompilerParams(dimension_semantics=("arbitrary",),
                                             vmem_limit_bytes=_vmem_limit(blocks, resident + 3 * _nbytes((t, gp), F32))),
    )(u, bre_blk, bim_blk, abr_row, abi_row, cre_blk, cimneg_blk, d_row)


def _mla_prep_fwd(cq, ckv, kr, rc, rs1, rs2, gq, gkv, wuq, wuk, wuv, tm):
    n_rows = cq.shape[0]

    def body(cq_ref, ckv_ref, kr_ref, c_ref, s1_ref, s2_ref, gq_ref, gkv_ref, wuq_ref, wuk_ref, wuv_ref,
             q_ref, k_ref, v_ref, cqn_ref, ckvn_ref):
        c, s1, s2 = c_ref[...], s1_ref[...], s2_ref[...]
        c8, s18, s28 = (jnp.tile(a, (1, N_HEADS)) for a in (c, s1, s2))
        xh, _ = _rms_stats(cq_ref[...])
        cqn = (xh * gq_ref[...]).astype(_MM)
        cqn_ref[...] = cqn.astype(_ACT)
        q_ref[...] = _rope(_dot(cqn, wuq_ref[...]), c8, s18, s28).astype(_ACT)
        xh, _ = _rms_stats(ckv_ref[...])
        ckvn = (xh * gkv_ref[...]).astype(_MM)
        ckvn_ref[...] = ckvn.astype(_ACT)
        kpe = _rope(kr_ref[...], c, s1, s2)
        k_ref[...] = (_dot(ckvn, wuk_ref[...]) + jnp.tile(kpe, (1, N_HEADS))).astype(_ACT)
        v_ref[...] = _dot(ckvn, wuv_ref[...]).astype(_ACT)

    return _rowcall("mla_prep_fwd", body, n_rows, tm, [cq, ckv, kr, rc, rs1, rs2], [gq, gkv, wuq, wuk, wuv],
                    [(QW, _ACT), (QW, _ACT), (QW, _ACT), (cq.shape[1], _ACT), (ckv.shape[1], _ACT)], temp_cols=6 * QW)


def _causal_steps(n_blocks, key_major):
    if key_major:
        pairs = [(qi, ki) for ki in range(n_blocks) for qi in range(ki, n_blocks)]
    else:
        pairs = [(qi, ki) for qi in range(n_blocks) for ki in range(qi + 1)]
    return (jnp.asarray(np.array([p[0] for p in pairs], np.int32)), jnp.asarray(np.array([p[1] for p in pairs], np.int32)))


def _attn_fwd(q, k, v, pos_col, pos_row, scale):
    n_rows = q.shape[0]
    ta = _tile(n_rows, 512, 128)
    nb = n_rows // ta
    hb = ATTN_HEADS_PER_STEP
    wide = hb * HEAD_PAD
    qmap, kmap = _causal_steps(nb, key_major=False)
    c2 = scale * LOG2_E

    def body(qm, km, q_ref, k_ref, v_ref, pq_ref, pk_ref, o_ref, lse_ref, m_sc, l_sc, acc_sc):
        s_id = pl.program_id(1)
        qi, ki = qm[s_id], km[s_id]

        @pl.when(ki == 0)
        def _():
            m_sc[...] = jnp.full_like(m_sc, NEG_INF)
            l_sc[...] = jnp.zeros_like(l_sc)
            acc_sc[...] = jnp.zeros_like(acc_sc)

        def update(on_diagonal):
            if on_diagonal:
                visible = pk_ref[...] <= pq_ref[...]
            for h in range(hb):
                lanes = slice(h * HEAD_PAD, (h + 1) * HEAD_PAD)
                s = _dot_nt(q_ref[:, lanes], k_ref[:, lanes])
                if on_diagonal:
                    s = jnp.where(visible, s, NEG_INF)
                m_prev = m_sc[:, lanes]
                m_new = jnp.maximum(m_prev, jnp.max(s, axis=1, keepdims=True))
                alpha = jnp.exp2((m_prev - m_new) * c2)
                p = jnp.exp2((s - m_new[:, :1]) * c2)
                l_new = alpha * l_sc[:, lanes] + jnp.sum(p, axis=1, keepdims=True)
                acc_new = alpha * acc_sc[:, lanes] + _dot(p, v_ref[:, lanes])
                if on_diagonal:
                    o_ref[:, lanes] = acc_new / l_new
                    lse_ref[:, lanes] = m_new * c2 + jnp.log2(l_new)
                else:
                    l_sc[:, lanes] = l_new
                    acc_sc[:, lanes] = acc_new
                    m_sc[:, lanes] = m_new

        pl.when(ki != qi)(functools.partial(update, False))
        pl.when(ki == qi)(functools.partial(update, True))

    qspec = pl.BlockSpec((ta, wide), lambda h, s, qm, km: (qm[s], h))
    kspec = pl.BlockSpec((ta, wide), lambda h, s, qm, km: (km[s], h))
    grid_spec = pltpu.PrefetchScalarGridSpec(
        num_scalar_prefetch=2, grid=(N_HEADS // hb, int(qmap.shape[0])),
        in_specs=[qspec, kspec, kspec,
                  pl.BlockSpec((ta, 1), lambda h, s, qm, km: (qm[s], 0)),
                  pl.BlockSpec((1, ta), lambda h, s, qm, km: (0, km[s]))],
        out_specs=[qspec, qspec],
        scratch_shapes=[pltpu.VMEM((ta, wide), F32)] * 3)
    return pl.pallas_call(
        body, name="attn_fwd", grid_spec=grid_spec,
        out_shape=[jax.ShapeDtypeStruct((n_rows, QW), F32), jax.ShapeDtypeStruct((n_rows, QW), F32)],
        compiler_params=pltpu.CompilerParams(
            dimension_semantics=("parallel", "arbitrary"),
            vmem_limit_bytes=_vmem_limit(8 * _nbytes((ta, wide), F32), 6 * hb * _nbytes((ta, ta), F32))),
    )(qmap, kmap, q, k, v, pos_col, pos_row)


def _mix_fwd(ypre, o, ga, gb, x, g1, bglu, wglu, wa, wb, wout, tm):
    n_rows, d = x.shape
    sw = ypre.shape[1]

    def body(y_ref, o_ref, ga_ref, gb_ref, x_ref, g1_ref, bglu_ref, wglu_ref, wa_ref, wb_ref, wout_ref,
             yg_ref, ya_ref, yb_ref, mg_ref, mo_ref, xo_ref):
        ys = _gelu(y_ref[...])
        yg = ys * _sigmoid(_dot(ys, wglu_ref[...]) + bglu_ref[...])
        yg_ref[...] = yg.astype(_ACT)
        ya = _dot(yg, wa_ref[...])
        yb = _dot(o_ref[...], wb_ref[...])
        ya_ref[...] = ya
        yb_ref[...] = yb
        merged = _sigmoid(ga_ref[...]) * ya + _sigmoid(gb_ref[...]) * yb
        mg_ref[...] = merged.astype(_ACT)
        mo = _dot(merged, wout_ref[...])
        mo_ref[...] = mo
        xo_ref[...] = x_ref[...] + g1_ref[...] * mo

    return _rowcall("mix_fwd", body, n_rows, tm, [ypre, o, ga, gb, x], [g1, bglu, wglu, wa, wb, wout],
                    [(sw, _ACT), (d, F32), (d, F32), (d, _ACT), (d, F32), (d, F32)], temp_cols=4 * d)


def _ffn_fwd(x, n2g, sc2, sh2, g2, wg, wu, wd, tm):
    n_rows, d = x.shape
    ff = wg.shape[1]

    def body(x_ref, g_ref, sc_ref, sh_ref, g2_ref, wg_ref, wu_ref, wd_ref, hb_ref, a_ref, b_ref, d_ref, xo_ref):
        xv = x_ref[...]
        xh, _ = _rms_stats(xv)
        hb = ((xh * g_ref[...]) * (1.0 + sc_ref[...]) + sh_ref[...]).astype(_MM)
        hb_ref[...] = hb.astype(_ACT)
        a = _dot(hb, wg_ref[...])
        b = _dot(hb, wu_ref[...])
        a_ref[...] = a
        b_ref[...] = b
        dn = _dot((a * _sigmoid(a)) * b, wd_ref[...])
        d_ref[...] = dn
        xo_ref[...] = xv + g2_ref[...] * dn

    return _rowcall("ffn_fwd", body, n_rows, tm, [x], [n2g, sc2, sh2, g2, wg, wu, wd],
                    [(d, _ACT), (ff, F32), (ff, F32), (d, F32), (d, F32)], temp_cols=3 * ff)


def _head(x, fg, target, tm):
    n_rows, d = x.shape

    def body(x_ref, t_ref, g_ref, dx_ref, loss_ref, dg_ref):
        xh, rstd = _rms_stats(x_ref[...])
        err = xh * g_ref[...] - t_ref[...]
        part = jnp.sum(jnp.mean(err * err, axis=-1, keepdims=True), axis=0, keepdims=True) * 0.5
        _acc(loss_ref, jnp.broadcast_to(part, loss_ref.shape))
        dy = err * (1.0 / d)
        _acc(dg_ref, _rows_sum(dy * xh))
        dx_ref[...] = _rms_bwd(dy * g_ref[...], xh, rstd)

    return _rowcall("head", body, n_rows, tm, [x, target], [fg], [(d, F32)], [((1, 128), F32), ((1, d), F32)],
                    temp_cols=4 * d)


def _ffn_bwd(dxo, xmid, a, b, dn, n2g, sc2, g2, wg, wu, wd, tm):
    n_rows, d = dxo.shape
    ff = a.shape[1]

    def body(dxo_ref, x_ref, a_ref, b_ref, dn_ref, g_ref, sc_ref, g2_ref, wg_ref, wu_ref, wd_ref,
             dx_ref, da_ref, db_ref, f_ref, dd_ref, dg2_ref, dsh_ref, dsc_ref, dn2_ref):
        dxo_v = dxo_ref[...]
        dd = dxo_v * g2_ref[...]
        dd_ref[...] = dd.astype(_ACT)
        _acc(dg2_ref, _rows_sum(dxo_v * dn_ref[...]))
        df = _dot_nt(dd, wd_ref[...])
        av, bv = a_ref[...], b_ref[...]
        sa = _sigmoid(av)
        si = av * sa
        f_ref[...] = (si * bv).astype(_ACT)
        da = df * bv * (sa * (1.0 + av * (1.0 - sa)))
        db = df * si
        da_ref[...] = da.astype(_ACT)
        db_ref[...] = db.astype(_ACT)
        dh = _dot_nt(da, wg_ref[...]) + _dot_nt(db, wu_ref[...])
        xh, rstd = _rms_stats(x_ref[...])
        yg = xh * g_ref[...]
        _acc(dsh_ref, _rows_sum(dh))
        _acc(dsc_ref, _rows_sum(dh * yg))
        dy = dh * (1.0 + sc_ref[...])
        _acc(dn2_ref, _rows_sum(dy * xh))
        dx_ref[...] = dxo_v + _rms_bwd(dy * g_ref[...], xh, rstd)

    return _rowcall("ffn_bwd", body, n_rows, tm, [dxo, xmid, a, b, dn], [n2g, sc2, g2, wg, wu, wd],
                    [(d, F32), (ff, _ACT), (ff, _ACT), (ff, _ACT), (d, _ACT)], [((1, d), F32)] * 4, temp_cols=5 * ff)


def _mix_bwd(dxm, mo, ya, yb, ga, gb, ypre, g1, bglu, wglu, wa, wb, wout, tm):
    n_rows, d = dxm.shape
    sw = ypre.shape[1]

    def body(dxm_ref, mo_ref, ya_ref, yb_ref, ga_ref, gb_ref, y_ref, g1_ref, bglu_ref, wglu_ref, wa_ref, wb_ref,
             wout_ref, dmo_ref, dya_ref, dyb_ref, dt_ref, ys_ref, dga_ref, dgb_ref, dy_ref, do_ref, dg1_ref, dbg_ref):
        dxm_v = dxm_ref[...]
        dmo = dxm_v * g1_ref[...]
        dmo_ref[...] = dmo.astype(_ACT)
        _acc(dg1_ref, _rows_sum(dxm_v * mo_ref[...]))
        dmg = _dot_nt(dmo, wout_ref[...])
        sa, sb = _sigmoid(ga_ref[...]), _sigmoid(gb_ref[...])
        dya, dyb = dmg * sa, dmg * sb
        dya_ref[...] = dya.astype(_ACT)
        dyb_ref[...] = dyb.astype(_ACT)
        dga_ref[...] = (dmg * ya_ref[...] * (sa * (1.0 - sa))).astype(_ACT)
        dgb_ref[...] = (dmg * yb_ref[...] * (sb * (1.0 - sb))).astype(_ACT)
        do_ref[...] = _dot_nt(dyb, wb_ref[...])
        dyg = _dot_nt(dya, wa_ref[...])
        yv = y_ref[...]
        ys = _gelu(yv)
        ys_ref[...] = ys.astype(_ACT)
        sg = _sigmoid(_dot(ys, wglu_ref[...]) + bglu_ref[...])
        dt = dyg * ys * (sg * (1.0 - sg))
        dt_ref[...] = dt.astype(_ACT)
        _acc(dbg_ref, _rows_sum(dt))
        dys = dyg * sg + _dot_nt(dt, wglu_ref[...])
        dy_ref[...] = dys * _gelu_grad(yv)

    return _rowcall("mix_bwd", body, n_rows, tm, [dxm, mo, ya, yb, ga, gb, ypre], [g1, bglu, wglu, wa, wb, wout],
                    [(d, _ACT), (d, _ACT), (d, _ACT), (sw, _ACT), (sw, _ACT), (d, _ACT), (d, _ACT), (sw, F32), (QW, F32)],
                    [((1, d), F32), ((1, sw), F32)], temp_cols=6 * d)


def _attn_bwd(q, k, v, do, o, lse, pos_col, pos_row, scale):
    n_rows = q.shape[0]
    ta = _tile(n_rows, 512, 128)
    nb = n_rows // ta
    qmap, kmap = _causal_steps(nb, key_major=True)
    hb = ATTN_BWD_HEADS_PER_STEP
    wide = hb * HEAD_PAD
    c2 = scale * LOG2_E

    def body(qm, km, q_ref, k_ref, v_ref, do_ref, o_ref, lse_ref, pq_ref, pk_ref, dq_ref, dk_ref, dv_ref,
             dk_acc, dv_acc):
        s_id = pl.program_id(1)
        qi, ki = qm[s_id], km[s_id]

        @pl.when(s_id == 0)
        def _():
            dq_ref[...] = jnp.zeros_like(dq_ref)

        @pl.when(qi == ki)
        def _():
            dk_acc[...] = jnp.zeros_like(dk_acc)
            dv_acc[...] = jnp.zeros_like(dv_acc)

        rows = pl.ds(pl.multiple_of(qi * ta, ta), ta)

        def update(on_diagonal):
            if on_diagonal:
                visible = pk_ref[...] <= pq_ref[...]
            for h in range(hb):
                lanes = slice(h * HEAD_PAD, (h + 1) * HEAD_PAD)
                qv, kv, dov = q_ref[:, lanes], k_ref[:, lanes], do_ref[:, lanes]
                e = _dot_nt(qv, kv) * c2 - lse_ref[:, lanes][:, :1]
                if on_diagonal:
                    e = jnp.where(visible, e, NEG_INF)
                p = jnp.exp2(e)
                dp = _dot_nt(dov, v_ref[:, lanes])
                delta = jnp.sum(dov * o_ref[:, lanes], axis=1, keepdims=True)
                ds = p * (dp - delta)
                dv_acc[:, lanes] += _dot_tn(p, dov)
                dk_acc[:, lanes] += _dot_tn(ds, qv)
                dq_ref[rows, lanes] += _dot(ds, kv) * scale

        pl.when(ki != qi)(functools.partial(update, False))
        pl.when(ki == qi)(functools.partial(update, True))

        @pl.when(qi == nb - 1)
        def _():
            dk_ref[...] = dk_acc[...] * scale
            dv_ref[...] = dv_acc[...]

    qspec = pl.BlockSpec((ta, wide), lambda h, s, qm, km: (qm[s], h))
    kspec = pl.BlockSpec((ta, wide), lambda h, s, qm, km: (km[s], h))
    grid_spec = pltpu.PrefetchScalarGridSpec(
        num_scalar_prefetch=2, grid=(N_HEADS // hb, int(qmap.shape[0])),
        in_specs=[qspec, kspec, kspec, qspec, qspec, qspec,
                  pl.BlockSpec((ta, 1), lambda h, s, qm, km: (qm[s], 0)),
                  pl.BlockSpec((1, ta), lambda h, s, qm, km: (0, km[s]))],
        out_specs=[pl.BlockSpec((n_rows, wide), lambda h, s, qm, km: (0, h)), kspec, kspec],
        scratch_shapes=[pltpu.VMEM((ta, wide), F32), pltpu.VMEM((ta, wide), F32)])
    return pl.pallas_call(
        body, name="attn_bwd", grid_spec=grid_spec,
        out_shape=[jax.ShapeDtypeStruct((n_rows, QW), F32)] * 3,
        compiler_params=pltpu.CompilerParams(
            dimension_semantics=("parallel", "arbitrary"),
            vmem_limit_bytes=_vmem_limit(12 * _nbytes((ta, wide), F32) + _nbytes((n_rows, wide), F32),
                                         6 * hb * _nbytes((ta, ta), F32))),
    )(qmap, kmap, q, k, v, do, o, lse, pos_col, pos_row)


def _mla_prep_bwd(dq, dk, dv, cq, ckv, rc, rs1, rs2, gq, gkv, wuq, wuk, wuv, nope, tm):
    n_rows = cq.shape[0]
    ql, kl = cq.shape[1], ckv.shape[1]

    def body(dq_ref, dk_ref, dv_ref, cq_ref, ckv_ref, c_ref, s1_ref, s2_ref, gq_ref, gkv_ref, wuq_ref, wuk_ref,
             wuv_ref, dqp_ref, dcq_ref, dckv_ref, dkr_ref, dgq_ref, dgkv_ref):
        c, s1, s2 = c_ref[...], s1_ref[...], s2_ref[...]
        c8, s18, s28 = (jnp.tile(a, (1, N_HEADS)) for a in (c, s1, s2))
        dqp = _rope_adjoint(dq_ref[...], c8, s18, s28)
        dqp_ref[...] = dqp.astype(_ACT)
        dcqn = _dot_nt(dqp, wuq_ref[...])
        xh, rstd = _rms_stats(cq_ref[...])
        _acc(dgq_ref, _rows_sum(dcqn * xh))
        dcq_ref[...] = _rms_bwd(dcqn * gq_ref[...], xh, rstd).astype(_ACT)
        dkv = dk_ref[...]
        dkpe = dkv[:, 0:HEAD_PAD]
        for h in range(1, N_HEADS):
            dkpe = dkpe + dkv[:, h * HEAD_PAD:(h + 1) * HEAD_PAD]
        lane = lax.broadcasted_iota(jnp.int32, dkpe.shape, 1)
        dkpe = jnp.where((lane >= nope) & (lane < nope + QK_ROPE), dkpe, 0.0)
        dkr_ref[...] = _rope_adjoint(dkpe, c, s1, s2).astype(_ACT)
        dckvn = _dot_nt(dkv, wuk_ref[...]) + _dot_nt(dv_ref[...], wuv_ref[...])
        xh, rstd = _rms_stats(ckv_ref[...])
        _acc(dgkv_ref, _rows_sum(dckvn * xh))
        dckv_ref[...] = _rms_bwd(dckvn * gkv_ref[...], xh, rstd).astype(_ACT)

    return _rowcall("mla_prep_bwd", body, n_rows, tm, [dq, dk, dv, cq, ckv, rc, rs1, rs2], [gq, gkv, wuq, wuk, wuv],
                    [(QW, _ACT), (ql, _ACT), (kl, _ACT), (HEAD_PAD, _ACT)], [((1, ql), F32), ((1, kl), F32)],
                    temp_cols=6 * QW)


def _ssm_bwd(dy, u, hre, him, bre_blk, bim_blk, abr_row, abi_row, cre_blk, cimneg_blk, d_row):
    n_rows, sw = u.shape
    gp = abr_row.shape[1]
    t = _tile(n_rows, 256)
    n_chunks = n_rows // t

    def body(dy_ref, u_ref, hre_ref, him_ref, hbre_ref, hbim_ref, bre_ref, bim_ref, ar_ref, ai_ref, cre_ref, cim_ref,
             d_ref, du_ref, gre_ref, gim_ref, dar_ref, dai_ref, dd_ref, g_re, g_im, hs_re, hs_im, cr, ci):
        i = pl.program_id(0)

        @pl.when(i == 0)
        def _():
            cr[...] = jnp.zeros_like(cr)
            ci[...] = jnp.zeros_like(ci)

        dyv = dy_ref[...]
        dyb = dyv.astype(_MM)
        g_re[...] = _dot_nt(dyb, cre_ref[...])
        g_im[...] = _dot_nt(dyb, cim_ref[...])
        a_r, a_i = ar_ref[...], ai_ref[...]

        def step(k, carry):
            nr, ni = carry
            row = pl.ds(t - 1 - k, 1)
            gr = g_re[row, :] + a_r * nr + a_i * ni
            gi = g_im[row, :] + a_r * ni - a_i * nr
            g_re[row, :] = gr
            g_im[row, :] = gi
            return gr, gi

        nr, ni = lax.fori_loop(0, t, step, (cr[0:1, :], ci[0:1, :]), unroll=8)
        cr[0:1, :] = nr
        ci[0:1, :] = ni
        gr_all, gi_all = g_re[...], g_im[...]
        gre_ref[...] = gr_all.astype(_ACT)
        gim_ref[...] = gi_all.astype(_ACT)
        du_ref[...] = (_dot_nt(gr_all, bre_ref[...]) + _dot_nt(gi_all, bim_ref[...]) + d_ref[...] * dyv).astype(_ACT)
        _acc(dd_ref, _rows_sum(dyv * u_ref[...]))
        is_first_chunk = i == n_chunks - 1
        hs_re[0:8, :] = jnp.where(is_first_chunk, 0.0, hbre_ref[...])
        hs_im[0:8, :] = jnp.where(is_first_chunk, 0.0, hbim_ref[...])
        hs_re[8:t + 8, :] = hre_ref[...]
        hs_im[8:t + 8, :] = him_ref[...]
        hp_re, hp_im = hs_re[pl.ds(7, t), :], hs_im[pl.ds(7, t), :]
        _acc(dar_ref, _rows_sum(gr_all * hp_re + gi_all * hp_im))
        _acc(dai_ref, _rows_sum(gi_all * hp_re - gr_all * hp_im))

    rev = lambda c: pl.BlockSpec((t, c), lambda i: (n_chunks - 1 - i, 0))
    before = pl.BlockSpec((8, gp), lambda i: (jnp.maximum((n_chunks - 1 - i) * (t // 8) - 1, 0), 0))
    full = lambda a: pl.BlockSpec(a.shape, lambda i: (0, 0), pipeline_mode=pl.Buffered(1))
    acc = lambda c: pl.BlockSpec((1, c), lambda i: (0, 0))
    blocks = 2 * _nbytes((t, sw), F32) + 2 * _nbytes((t, gp), F32) + _nbytes((t, sw), _ACT) + 2 * _nbytes((t, gp), _ACT)
    resident = 4 * _nbytes((sw, gp), _MM) + 4 * _nbytes((t + 8, gp), F32)
    return pl.pallas_call(
        body, name="ssm_bwd", grid=(n_chunks,),
        in_specs=[rev(sw), rev(sw), rev(gp), rev(gp), before, before, full(bre_blk), full(bim_blk), full(abr_row),
                  full(abi_row), full(cre_blk), full(cimneg_blk), full(d_row)],
        out_specs=[rev(sw), rev(gp), rev(gp), acc(gp), acc(gp), acc(sw)],
        out_shape=[jax.ShapeDtypeStruct((n_rows, sw), _ACT), jax.ShapeDtypeStruct((n_rows, gp), _ACT),
                   jax.ShapeDtypeStruct((n_rows, gp), _ACT), jax.ShapeDtypeStruct((1, gp), F32),
                   jax.ShapeDtypeStruct((1, gp), F32), jax.ShapeDtypeStruct((1, sw), F32)],
        scratch_shapes=[pltpu.VMEM((t, gp), F32), pltpu.VMEM((t, gp), F32), pltpu.VMEM((t + 8, gp), F32),
                        pltpu.VMEM((t + 8, gp), F32), pltpu.VMEM((8, gp), F32), pltpu.VMEM((8, gp), F32)],
        compiler_params=pltpu.CompilerParams(dimension_semantics=("arbitrary",),
                                             vmem_limit_bytes=_vmem_limit(blocks, resident + 4 * _nbytes((t, gp), F32))),
    )(dy, u, hre, him, hre, him, bre_blk, bim_blk, abr_row, abi_row, cre_blk, cimneg_blk, d_row)


def _in_bwd(dxm, x, dz_parts, n1g, sc1, w_parts, tm):
    n_rows, d = x.shape
    n = len(dz_parts)

    def body(dxm_ref, x_ref, *rest):
        dz_refs = rest[:n]
        g_ref, sc_ref = rest[n], rest[n + 1]
        w_refs = rest[n + 2:2 * n + 2]
        dx_ref, dsh_ref, dsc_ref, dn1_ref = rest[2 * n + 2:]
        dh = None
        for dz_ref, w_ref in zip(dz_refs, w_refs):
            term = _dot_nt(dz_ref[...], w_ref[...])
            dh = term if dh is None else dh + term
        xh, rstd = _rms_stats(x_ref[...])
        yg = xh * g_ref[...]
        _acc(dsh_ref, _rows_sum(dh))
        _acc(dsc_ref, _rows_sum(dh * yg))
        dy = dh * (1.0 + sc_ref[...])
        _acc(dn1_ref, _rows_sum(dy * xh))
        dx_ref[...] = dxm_ref[...] + _rms_bwd(dy * g_ref[...], xh, rstd)

    return _rowcall("in_bwd", body, n_rows, tm, [dxm, x, *dz_parts], [n1g, sc1, *w_parts],
                    [(d, F32)], [((1, d), F32)] * 3, temp_cols=5 * d)


def _pad_heads(w, per_head):
    lead = w.shape[:-1]
    w = w.reshape(lead + (N_HEADS, per_head))
    w = jnp.pad(w, [(0, 0)] * len(lead) + [(0, 0), (0, HEAD_PAD - per_head)])
    return w.reshape(lead + (QW,))


def _unpad_heads(w, per_head):
    lead = w.shape[:-1]
    return w.reshape(lead + (N_HEADS, HEAD_PAD))[..., :per_head].reshape(lead + (N_HEADS * per_head,))


def _cols_from_chips(g):
    ch, dep, r, cs = g.shape
    return g.transpose(1, 2, 0, 3).reshape(dep, r, ch * cs)


def _rows_from_chips(g):
    ch, dep, rs, c = g.shape
    return g.transpose(1, 0, 2, 3).reshape(dep, ch * rs, c)


def _cols_to_chips(w):
    dep, r, c = w.shape
    return w.reshape(dep, r, 4, c // 4).transpose(2, 0, 1, 3)


def _rows_to_chips(w):
    dep, r, c = w.shape
    return w.reshape(dep, 4, r // 4, c).transpose(1, 0, 2, 3)


def _block_diag(b_gxy):
    g, xx, yy = b_gxy.shape
    eye = jnp.eye(g, dtype=b_gxy.dtype)
    return (b_gxy[:, :, None, :] * eye[:, None, :, None]).reshape(g * xx, g * yy)


def _block_diag_extract(full, g):
    xx, yy = full.shape[0] // g, full.shape[1] // g
    eye = jnp.eye(g, dtype=full.dtype)
    return jnp.sum(full.reshape(g, xx, g, yy) * eye[:, None, :, None], axis=2)


def _pack_rows(arrays):
    parts = []
    for a in arrays:
        flat = a.reshape(-1)
        flat = jnp.pad(flat, (0, (-flat.shape[0]) % 1024))
        parts.append(flat.reshape(-1, 128))
    return jnp.concatenate(parts, axis=0)


def _unpack_rows(packed, shapes):
    out, row = [], 0
    for s in shapes:
        n = int(np.prod(s))
        rows = -(-n // 1024) * 8
        out.append(packed[row:row + rows].reshape(-1)[:n].reshape(s))
        row += rows
    return out


def kernel(x, c, positions, w_ada, b_ada, norm1_g, w_in, ssm_a_re, ssm_a_im, ssm_log_dt, ssm_b_re, ssm_b_im, ssm_c_re, ssm_c_im, ssm_d, w_glu, b_glu, w_a_out, q_norm_g, w_uq, kv_norm_g, w_uk, w_uv, w_b_out, w_out, norm2_g, w_gate, w_up, w_down, final_g, loss_target, m_w_ada, m_b_ada, m_norm1_g, m_w_in, m_ssm_a_re, m_ssm_a_im, m_ssm_log_dt, m_ssm_b_re, m_ssm_b_im, m_ssm_c_re, m_ssm_c_im, m_ssm_d, m_w_glu, m_b_glu, m_w_a_out, m_q_norm_g, m_w_uq, m_kv_norm_g, m_w_uk, m_w_uv, m_w_b_out, m_w_out, m_norm2_g, m_w_gate, m_w_up, m_w_down, m_final_g, v_w_ada, v_b_ada, v_norm1_g, v_w_in, v_ssm_a_re, v_ssm_a_im, v_ssm_log_dt, v_ssm_b_re, v_ssm_b_im, v_ssm_c_re, v_ssm_c_im, v_ssm_d, v_w_glu, v_b_glu, v_w_a_out, v_q_norm_g, v_w_uq, v_kv_norm_g, v_w_uk, v_w_uv, v_w_b_out, v_w_out, v_norm2_g, v_w_gate, v_w_up, v_w_down, v_final_g):
    weights = dict(w_ada=w_ada, b_ada=b_ada, norm1_g=norm1_g, w_in=w_in, ssm_a_re=ssm_a_re, ssm_a_im=ssm_a_im, ssm_log_dt=ssm_log_dt, ssm_b_re=ssm_b_re, ssm_b_im=ssm_b_im, ssm_c_re=ssm_c_re, ssm_c_im=ssm_c_im, ssm_d=ssm_d, w_glu=w_glu, b_glu=b_glu, w_a_out=w_a_out, q_norm_g=q_norm_g, w_uq=w_uq, kv_norm_g=kv_norm_g, w_uk=w_uk, w_uv=w_uv, w_b_out=w_b_out, w_out=w_out, norm2_g=norm2_g, w_gate=w_gate, w_up=w_up, w_down=w_down, final_g=final_g)
    mom_m = dict(w_ada=m_w_ada, b_ada=m_b_ada, norm1_g=m_norm1_g, w_in=m_w_in, ssm_a_re=m_ssm_a_re, ssm_a_im=m_ssm_a_im, ssm_log_dt=m_ssm_log_dt, ssm_b_re=m_ssm_b_re, ssm_b_im=m_ssm_b_im, ssm_c_re=m_ssm_c_re, ssm_c_im=m_ssm_c_im, ssm_d=m_ssm_d, w_glu=m_w_glu, b_glu=m_b_glu, w_a_out=m_w_a_out, q_norm_g=m_q_norm_g, w_uq=m_w_uq, kv_norm_g=m_kv_norm_g, w_uk=m_w_uk, w_uv=m_w_uv, w_b_out=m_w_b_out, w_out=m_w_out, norm2_g=m_norm2_g, w_gate=m_w_gate, w_up=m_w_up, w_down=m_w_down, final_g=m_final_g)
    mom_v = dict(w_ada=v_w_ada, b_ada=v_b_ada, norm1_g=v_norm1_g, w_in=v_w_in, ssm_a_re=v_ssm_a_re, ssm_a_im=v_ssm_a_im, ssm_log_dt=v_ssm_log_dt, ssm_b_re=v_ssm_b_re, ssm_b_im=v_ssm_b_im, ssm_c_re=v_ssm_c_re, ssm_c_im=v_ssm_c_im, ssm_d=v_ssm_d, w_glu=v_w_glu, b_glu=v_b_glu, w_a_out=v_w_a_out, q_norm_g=v_q_norm_g, w_uq=v_w_uq, kv_norm_g=v_kv_norm_g, w_uk=v_w_uk, w_uv=v_w_uv, w_b_out=v_w_b_out, w_out=v_w_out, norm2_g=v_norm2_g, w_gate=v_w_gate, w_up=v_w_up, w_down=v_w_down, final_g=v_final_g)
    names = list(weights)

    depth = w_in.shape[0]
    seq, d = x.shape[1], x.shape[2]
    sw = ssm_d.shape[1]
    groups, n_state, n_chan = ssm_b_re.shape[1:]
    gp = groups * n_state
    ql, kl = q_norm_g.shape[1], kv_norm_g.shape[1]
    nope = w_uk.shape[2] * 4 // N_HEADS
    vdim = w_uv.shape[2] * 4 // N_HEADS
    qk_dim = nope + QK_ROPE
    scale = qk_dim ** -0.5
    tm = _tile(seq, 256, 16)
    tm_ffn = _tile(seq, 128, 16)
    me = 4 * lax.axis_index("x") + 2 * lax.axis_index("y") + lax.axis_index("c")
    chip = 2 * lax.axis_index("x") + lax.axis_index("y")

    xs = x.reshape(seq, d)
    target = loss_target.reshape(seq, d)
    pos_f = positions.astype(F32)
    pos_col = pos_f.reshape(seq, 1)
    pos_row = pos_f.reshape(1, seq)

    (c_all,) = _exchange("gather_c", "gather8", [c])
    c_all = c_all.reshape(8, d)
    ada_cols = w_ada.shape[2]
    mod_part = _mod_fwd(c_all, w_ada.reshape(depth * d, ada_cols), depth)
    (mod_all,) = _exchange("gather_mod", "gather8", [mod_part])
    mod_all = mod_all.reshape(4, 2, depth, 8, ada_cols)[:, 0]
    mod_me = lax.dynamic_index_in_dim(mod_all, me, axis=2, keepdims=False)
    mod = mod_me.transpose(1, 0, 2).reshape(depth, 4 * ada_cols) + b_ada
    mod = mod.reshape(depth, 6, 1, d)

    big = ["w_in", "w_glu", "w_a_out", "w_uq", "w_uk", "w_uv", "w_b_out", "w_out", "w_gate", "w_up", "w_down"]
    row_sharded = {"w_glu", "w_out", "w_down"}
    gathered = _exchange("gather_weights", "gather4", [weights[n].astype(_MM) for n in big])
    full = {n: (_rows_from_chips(g) if n in row_sharded else _cols_from_chips(g)) for n, g in zip(big, gathered)}
    o1, o2, o3, o4, o5 = sw, sw + ql, sw + ql + kl, sw + ql + kl + QK_ROPE, sw + ql + kl + QK_ROPE + d
    wi = full["w_in"]
    w_u, w_cq, w_ckv, w_ga, w_gb = wi[:, :, :o1], wi[:, :, o1:o2], wi[:, :, o2:o3], wi[:, :, o4:o5], wi[:, :, o5:]
    w_kr = jnp.pad(wi[:, :, o3:o4], ((0, 0), (0, 0), (nope, HEAD_PAD - nope - QK_ROPE)))
    wuq_p = _pad_heads(full["w_uq"], qk_dim)
    wuk_p = _pad_heads(full["w_uk"], nope)
    wuv_p = _pad_heads(full["w_uv"], vdim)
    wb_p = _pad_heads(full["w_b_out"].transpose(0, 2, 1), vdim).transpose(0, 2, 1)

    inv_freq = ROPE_BASE ** (-jnp.arange(0, QK_ROPE, 2, dtype=F32) / QK_ROPE)
    inv_lane = jnp.pad(jnp.concatenate([inv_freq, inv_freq]), (nope, HEAD_PAD - nope - QK_ROPE)).reshape(1, HEAD_PAD)
    rc, rs1, rs2 = _rope_tables(pos_col, inv_lane, nope)
    a_re_col = ssm_a_re.reshape(depth * gp, 1)
    a_im_col = ssm_a_im.reshape(depth * gp, 1)
    ldt_col = jnp.broadcast_to(ssm_log_dt[:, :, None], (depth, groups, n_state)).reshape(depth * gp, 1)
    b_re2, b_im2 = ssm_b_re.reshape(depth * gp, n_chan), ssm_b_im.reshape(depth * gp, n_chan)
    abr, abi, bbr, bbi = _ssm_disc_fwd(a_re_col, a_im_col, ldt_col, b_re2, b_im2)
    abr_rows, abi_rows = abr.reshape(depth, 1, gp), abi.reshape(depth, 1, gp)
    bbr, bbi = bbr.reshape(depth, groups, n_state, n_chan), bbi.reshape(depth, groups, n_state, n_chan)

    saved = []
    xl = xs
    for l in range(depth):
        sh1, sc1, g1, sh2, sc2, g2 = (mod[l, j] for j in range(6))
        n1g, n2g = norm1_g[l].reshape(1, d), norm2_g[l].reshape(1, d)
        w_parts = [w_u[l], w_cq[l], w_ckv[l], w_kr[l], w_ga[l], w_gb[l]]
        hb, u, cq, ckv, kr, ga, gb = _in_fwd(xl, n1g, sc1, sh1, w_parts, tm)
        bre_blk = _block_diag(bbr[l].transpose(0, 2, 1)).astype(_MM)
        bim_blk = _block_diag(bbi[l].transpose(0, 2, 1)).astype(_MM)
        cre_blk = _block_diag(ssm_c_re[l].transpose(0, 2, 1)).astype(_MM)
        cimneg_blk = _block_diag(-ssm_c_im[l].transpose(0, 2, 1)).astype(_MM)
        d_row = ssm_d[l].reshape(1, sw)
        ssm_w = (bre_blk, bim_blk, abr_rows[l], abi_rows[l], cre_blk, cimneg_blk, d_row)
        ypre, hre, him = _ssm_fwd(u, *ssm_w)
        gq, gkv = q_norm_g[l].reshape(1, ql), kv_norm_g[l].reshape(1, kl)
        q, k, v, cqn, ckvn = _mla_prep_fwd(cq, ckv, kr, rc, rs1, rs2, gq, gkv, wuq_p[l], wuk_p[l], wuv_p[l], tm)
        o, lse = _attn_fwd(q, k, v, pos_col, pos_row, scale)
        bglu = b_glu[l].reshape(1, sw)
        yg, ya, yb, merged, mo, xmid = _mix_fwd(ypre, o, ga, gb, xl, g1, bglu, full["w_glu"][l], full["w_a_out"][l],
                                                 wb_p[l], full["w_out"][l], tm)
        hb2, fa, fb, dn, xout = _ffn_fwd(xmid, n2g, sc2, sh2, g2, full["w_gate"][l], full["w_up"][l], full["w_down"][l], tm_ffn)
        saved.append(dict(x=xl, hb=hb, u=u, cq=cq, ckv=ckv, ga=ga, gb=gb, ssm_w=ssm_w, ypre=ypre, hre=hre, him=him,
                          q=q, k=k, v=v, cqn=cqn, ckvn=ckvn, o=o, lse=lse, yg=yg, ya=ya, yb=yb, merged=merged, mo=mo,
                          xmid=xmid, hb2=hb2, fa=fa, fb=fb, dn=dn, w_parts=w_parts))
        xl = xout

    dx, loss_acc, g_final = _head(xl, final_g.reshape(1, d), target, tm)
    loss = lax.psum(loss_acc[0, 0], ("x", "y", "c"))

    per_layer = ["w_gate", "w_up", "w_down", "norm2_g", "w_out", "w_a_out", "w_b_out", "w_glu", "b_glu", "w_uq", "w_uk",
                 "w_uv", "q_norm_g", "kv_norm_g", "ssm_d", "ssm_c_re", "ssm_c_im", "w_in", "norm1_g"]
    grads = {n: [None] * depth for n in per_layer}
    dmod = [None] * depth
    for l in reversed(range(depth)):
        s = saved[l]
        sh1, sc1, g1, sh2, sc2, g2 = (mod[l, j] for j in range(6))
        n1g, n2g = norm1_g[l].reshape(1, d), norm2_g[l].reshape(1, d)
        dxm, da, db, fb16, dd, dg2, dsh2, dsc2, dn2 = _ffn_bwd(
            dx, s["xmid"], s["fa"], s["fb"], s["dn"], n2g, sc2, g2, full["w_gate"][l], full["w_up"][l], full["w_down"][l], tm_ffn)
        grads["w_gate"][l] = _mm_tn("dw_gate", s["hb2"], da)
        grads["w_up"][l] = _mm_tn("dw_up", s["hb2"], db)
        grads["w_down"][l] = _mm_tn("dw_down", fb16, dd)
        grads["norm2_g"][l] = dn2.reshape(d)

        bglu = b_glu[l].reshape(1, sw)
        dmo, dya, dyb, dt, ys, dga, dgb, dypre, do, dg1, dbglu = _mix_bwd(
            dxm, s["mo"], s["ya"], s["yb"], s["ga"], s["gb"], s["ypre"], g1, bglu, full["w_glu"][l], full["w_a_out"][l],
            wb_p[l], full["w_out"][l], tm)
        grads["w_out"][l] = _mm_tn("dw_out", s["merged"], dmo)
        grads["w_a_out"][l] = _mm_tn("dw_a_out", s["yg"], dya)
        dwb_p = _mm_tn("dw_b_out", s["o"], dyb)
        grads["w_b_out"][l] = _unpad_heads(dwb_p.T, vdim).T
        grads["w_glu"][l] = _mm_tn("dw_glu", ys, dt)
        grads["b_glu"][l] = dbglu.reshape(sw)

        dq, dk, dv = _attn_bwd(s["q"], s["k"], s["v"], do, s["o"], s["lse"], pos_col, pos_row, scale)
        gq, gkv = q_norm_g[l].reshape(1, ql), kv_norm_g[l].reshape(1, kl)
        dqp, dcq, dckv, dkr, dgq, dgkv = _mla_prep_bwd(dq, dk, dv, s["cq"], s["ckv"], rc, rs1, rs2, gq, gkv,
                                                       wuq_p[l], wuk_p[l], wuv_p[l], nope, tm)
        grads["w_uq"][l] = _unpad_heads(_mm_tn("dw_uq", s["cqn"], dqp), qk_dim)
        grads["w_uk"][l] = _unpad_heads(_mm_tn("dw_uk", s["ckvn"], dk), nope)
        grads["w_uv"][l] = _unpad_heads(_mm_tn("dw_uv", s["ckvn"], dv), vdim)
        grads["q_norm_g"][l] = dgq.reshape(ql)
        grads["kv_norm_g"][l] = dgkv.reshape(kl)

        du, gre, gim, dar, dai, ddskip = _ssm_bwd(dypre, s["u"], s["hre"], s["him"], *s["ssm_w"])
        grads["ssm_d"][l] = ddskip.reshape(sw)
        d_bre = _block_diag_extract(_mm_tn("d_bre", s["u"], gre), groups).transpose(0, 2, 1)
        d_bim = _block_diag_extract(_mm_tn("d_bim", s["u"], gim), groups).transpose(0, 2, 1)
        grads["ssm_c_re"][l] = _block_diag_extract(_mm_tn("d_cre", s["hre"], dypre), groups).transpose(0, 2, 1)
        grads["ssm_c_im"][l] = -_block_diag_extract(_mm_tn("d_cim", s["him"], dypre), groups).transpose(0, 2, 1)
        s["disc_grads"] = (dar.reshape(gp, 1), dai.reshape(gp, 1), d_bre.reshape(gp, n_chan), d_bim.reshape(gp, n_chan))

        dz_parts = [du, dcq, dckv, dkr, dga, dgb]
        dx, dsh1, dsc1, dn1 = _in_bwd(dxm, s["x"], dz_parts, n1g, sc1, s["w_parts"], tm)
        dw_parts = [_mm_tn("dw_in_%d" % j, s["hb"], dz) for j, dz in enumerate(dz_parts)]
        dw_parts[3] = dw_parts[3][:, nope:nope + QK_ROPE]
        grads["w_in"][l] = jnp.concatenate(dw_parts, axis=1)
        grads["norm1_g"][l] = dn1.reshape(d)
        dmod[l] = jnp.concatenate([dsh1, dsc1, dg1, dsh2, dsc2, dg2], axis=1).reshape(6 * d)
    grad_x = dx.reshape(x.shape)

    disc = [jnp.concatenate([saved[l]["disc_grads"][j] for l in range(depth)], axis=0) for j in range(4)]
    da_re, da_im, dldt, db_re, db_im = _ssm_disc_bwd(a_re_col, a_im_col, ldt_col, b_re2, b_im2, *disc)
    stacked = {n: jnp.stack(v) for n, v in grads.items()}
    stacked["ssm_a_re"] = da_re.reshape(ssm_a_re.shape)
    stacked["ssm_a_im"] = da_im.reshape(ssm_a_im.shape)
    stacked["ssm_log_dt"] = _lane_sum(dldt.reshape(depth * groups, n_state)).reshape(ssm_log_dt.shape)
    stacked["ssm_b_re"] = db_re.reshape(ssm_b_re.shape)
    stacked["ssm_b_im"] = db_im.reshape(ssm_b_im.shape)
    stacked["final_g"] = g_final.reshape(d)
    stacked["b_ada"] = jnp.stack(dmod)

    small = [n for n in names if n not in big and n != "w_ada"]
    small_shapes = [weights[n].shape for n in small]
    (small_all,) = _exchange("gather_small", "gather8", [_pack_rows([stacked[n] for n in small])])
    sg, sd, sm, sv = _adamw("adamw_small", [small_all], _pack_rows([weights[n] for n in small]),
                            _pack_rows([mom_m[n] for n in small]), _pack_rows([mom_v[n] for n in small]))
    out_g = dict(zip(small, _unpack_rows(sg, small_shapes)))
    out_d = dict(zip(small, _unpack_rows(sd, small_shapes)))
    out_m = dict(zip(small, _unpack_rows(sm, small_shapes)))
    out_v = dict(zip(small, _unpack_rows(sv, small_shapes)))

    n_dmod = depth * 6 * d
    dmod_all = small_all[:, :n_dmod // 128].reshape(8, depth, 6 * d)
    dmod_cols = lax.dynamic_slice_in_dim(dmod_all, chip * ada_cols, ada_cols, axis=2)
    g_wada = _wada_bwd(c_all, dmod_cols.transpose(1, 0, 2).reshape(depth * 8, ada_cols), depth)
    res = _adamw("adamw_w_ada", [g_wada], w_ada.reshape(depth * d, ada_cols), m_w_ada.reshape(depth * d, ada_cols),
                 v_w_ada.reshape(depth * d, ada_cols))
    out_g["w_ada"], out_d["w_ada"], out_m["w_ada"], out_v["w_ada"] = (r.reshape(w_ada.shape) for r in res)

    to_chips = [(_rows_to_chips if n in row_sharded else _cols_to_chips)(stacked[n]).astype(_WIRE) for n in big]
    landed = _exchange("scatter_grads", "scatter4", to_chips)
    partial = [_sum_slots("sum_" + n, r.reshape(4, -1, r.shape[-1])) for n, r in zip(big, landed)]
    sibling = _exchange("swap_partials", "swap", partial)
    for n, mine, theirs in zip(big, partial, sibling):
        shp = weights[n].shape
        as2d = lambda a: a.reshape(-1, shp[-1])
        res = _adamw("adamw_" + n, [mine, theirs], as2d(weights[n]), as2d(mom_m[n]), as2d(mom_v[n]))
        out_g[n], out_d[n], out_m[n], out_v[n] = (r.reshape(shp) for r in res)

    return (loss, grad_x, *[out_g[n] for n in names], *[out_d[n] for n in names], *[out_m[n] for n in names],
            *[out_v[n] for n in names])
```

```python
import functools
import math

import numpy as np
import jax
import jax.numpy as jnp
from jax import lax
from jax.experimental import pallas as pl
from jax.experimental.pallas import tpu as pltpu

F32 = jnp.float32
_MM = jnp.bfloat16
_ACT = jnp.bfloat16
_WIRE = jnp.bfloat16

N_HEADS = 8
QK_ROPE = 32
HEAD_PAD = 128
QW = N_HEADS * HEAD_PAD
ROPE_BASE = 10000.0
EPS = 1e-6
DT_MIN = 1e-3
ADAM_LR = 0.001
ADAM_B1 = 0.9
ADAM_B2 = 0.999
ADAM_EPS = 1e-08
ADAM_WD = 0.01
ADAM_STEP = 10
NEG_INF = -1e30
LOG2_E = math.log2(math.e)
ATTN_HEADS_PER_STEP = 4
ATTN_BWD_HEADS_PER_STEP = 2

V7X_VMEM_BYTES = 64 * 1024 * 1024
VMEM_RESERVE_BYTES = 6 * 1024 * 1024
MESH = pl.DeviceIdType.MESH
ANY = pl.BlockSpec(memory_space=pl.ANY)


def _vmem_limit(block_bytes, temp_bytes):
    want = 2 * block_bytes + temp_bytes
    return int(min(V7X_VMEM_BYTES - VMEM_RESERVE_BYTES, max(want, 32 * 1024 * 1024)))


def _nbytes(shape, dtype):
    return int(np.prod(shape)) * jnp.dtype(dtype).itemsize


def _tile(n, target, mult=8):
    t = min(n, target)
    while t >= mult:
        if n % t == 0 and t % mult == 0:
            return t
        t -= 1
    return n


def _dot(a, b):
    return jnp.dot(a.astype(_MM), b.astype(_MM), preferred_element_type=F32)


def _dot_nt(a, b):
    return lax.dot_general(a.astype(_MM), b.astype(_MM), (((1,), (1,)), ((), ())), preferred_element_type=F32)


def _dot_tn(a, b):
    return lax.dot_general(a.astype(_MM), b.astype(_MM), (((0,), (0,)), ((), ())), preferred_element_type=F32)


def _sigmoid(x):
    return jax.nn.sigmoid(x)


_GELU_K = math.sqrt(2.0 / math.pi)


def _gelu(x):
    return x * (0.5 * (1.0 + jnp.tanh(_GELU_K * (x + 0.044715 * (x * x * x)))))


def _gelu_grad(x):
    th = jnp.tanh(_GELU_K * (x + 0.044715 * (x * x * x)))
    return 0.5 * (1.0 + th) + 0.5 * x * (1.0 - th * th) * (_GELU_K * (1.0 + 3.0 * 0.044715 * (x * x)))


def _rows_sum(v):
    return jnp.sum(v, axis=0, keepdims=True)


def _rms_stats(x):
    rstd = lax.rsqrt(jnp.mean(x * x, axis=-1, keepdims=True) + EPS)
    return x * rstd, rstd


def _rms_bwd(dxh, xh, rstd):
    return rstd * (dxh - xh * jnp.mean(dxh * xh, axis=-1, keepdims=True))


def _rowcall(name, body, n_rows, tm, row_ins, full_ins, row_outs, acc_outs=(), temp_cols=0):
    grid = (n_rows // tm,)
    in_specs = [pl.BlockSpec((tm, a.shape[1]), lambda i: (i, 0)) for a in row_ins]
    in_specs += [pl.BlockSpec(a.shape, lambda i: (0, 0), pipeline_mode=pl.Buffered(1)) for a in full_ins]
    out_shape = [jax.ShapeDtypeStruct((n_rows, c), dt) for c, dt in row_outs]
    out_shape += [jax.ShapeDtypeStruct(s, dt) for s, dt in acc_outs]
    out_specs = [pl.BlockSpec((tm, c), lambda i: (i, 0)) for c, _ in row_outs]
    out_specs += [pl.BlockSpec(s, lambda i: (0, 0)) for s, _ in acc_outs]
    blocks = sum(_nbytes((tm, a.shape[1]), a.dtype) for a in row_ins)
    blocks += sum(_nbytes((tm, c), dt) for c, dt in row_outs) + sum(_nbytes(s, dt) for s, dt in acc_outs)
    resident = sum(_nbytes(a.shape, a.dtype) for a in full_ins)
    limit = _vmem_limit(blocks, resident + _nbytes((tm, temp_cols), F32))
    res = pl.pallas_call(
        body, name=name, grid=grid, in_specs=in_specs, out_specs=out_specs, out_shape=out_shape,
        compiler_params=pltpu.CompilerParams(
            dimension_semantics=("arbitrary" if acc_outs else "parallel",), vmem_limit_bytes=limit),
    )(*row_ins, *full_ins)
    return res


def _first_step():
    return pl.program_id(0) == 0


def _acc(ref, val):
    @pl.when(_first_step())
    def _():
        ref[...] = val

    @pl.when(jnp.logical_not(_first_step()))
    def _():
        ref[...] += val


def _mm_tn(name, a, g):
    n_rows, k = a.shape
    n = g.shape[1]
    tk = k if k <= 1024 else _tile(k, 1408, 128)
    tn = n if n <= 1024 else _tile(n, 1408, 128)
    tl = _tile(n_rows, 512, 16)

    def body(a_ref, g_ref, o_ref):
        @pl.when(pl.program_id(2) == 0)
        def _():
            o_ref[...] = jnp.zeros_like(o_ref)
        o_ref[...] += _dot_tn(a_ref[...], g_ref[...])

    blocks = _nbytes((tl, tk), a.dtype) + _nbytes((tl, tn), g.dtype) + _nbytes((tk, tn), F32)
    return pl.pallas_call(
        body, name=name, grid=(k // tk, n // tn, n_rows // tl),
        in_specs=[pl.BlockSpec((tl, tk), lambda i, j, l: (l, i)), pl.BlockSpec((tl, tn), lambda i, j, l: (l, j))],
        out_specs=pl.BlockSpec((tk, tn), lambda i, j, l: (i, j)),
        out_shape=jax.ShapeDtypeStruct((k, n), F32),
        compiler_params=pltpu.CompilerParams(
            dimension_semantics=("parallel", "parallel", "arbitrary"),
            vmem_limit_bytes=_vmem_limit(blocks, 2 * _nbytes((tl, max(tk, tn)), F32) + _nbytes((tk, tn), F32))),
    )(a, g)


def _mm_tn_blocks(name, a, g, n_blk):
    n_rows, k = a.shape
    n = g.shape[1]
    kb, nbk = k // n_blk, n // n_blk
    tl = _tile(n_rows, 512, 16)

    def body(a_ref, g_ref, o_ref):
        @pl.when(pl.program_id(1) == 0)
        def _():
            o_ref[...] = jnp.zeros_like(o_ref)
        o_ref[0] += _dot_tn(a_ref[...], g_ref[...])

    blocks = _nbytes((tl, kb), a.dtype) + _nbytes((tl, nbk), g.dtype) + _nbytes((kb, nbk), F32)
    return pl.pallas_call(
        body, name=name, grid=(n_blk, n_rows // tl),
        in_specs=[pl.BlockSpec((tl, kb), lambda b, l: (l, b)), pl.BlockSpec((tl, nbk), lambda b, l: (l, b))],
        out_specs=pl.BlockSpec((1, kb, nbk), lambda b, l: (b, 0, 0)),
        out_shape=jax.ShapeDtypeStruct((n_blk, kb, nbk), F32),
        compiler_params=pltpu.CompilerParams(
            dimension_semantics=("parallel", "arbitrary"),
            vmem_limit_bytes=_vmem_limit(blocks, 2 * _nbytes((tl, max(kb, nbk)), F32) + _nbytes((kb, nbk), F32))),
    )(a, g)


def _place():
    return lax.axis_index("x"), lax.axis_index("y"), lax.axis_index("c")


def _flip(v, bit):
    return 1 - v if bit else v


def _exchange(name, mode, arrays):
    n = len(arrays)
    if mode == "gather8":
        rel = [((k >> 2) & 1, (k >> 1) & 1, k & 1) for k in range(1, 8)]
        out_shape = [jax.ShapeDtypeStruct((8,) + a.shape, a.dtype) for a in arrays]
    elif mode == "gather4":
        rel = [((k >> 1) & 1, k & 1, 0) for k in range(1, 4)]
        out_shape = [jax.ShapeDtypeStruct((4,) + a.shape, a.dtype) for a in arrays]
    elif mode == "scatter4":
        rel = [((k >> 1) & 1, k & 1, 0) for k in range(1, 4)]
        out_shape = [jax.ShapeDtypeStruct(a.shape, a.dtype) for a in arrays]
    else:
        rel = [(0, 0, 1)]
        out_shape = [jax.ShapeDtypeStruct(a.shape, a.dtype) for a in arrays]
    n_rel = len(rel)

    def body(*refs):
        ins, outs = refs[:n], refs[n:2 * n]
        send_sems, recv_sems, local_sems = refs[2 * n:]
        x, y, c = _place()

        def slot(px, py, pc):
            return 4 * px + 2 * py + pc if mode == "gather8" else 2 * px + py

        mine = slot(x, y, c)
        local = []
        if mode != "swap":
            for a in range(n):
                src = ins[a].at[mine] if mode == "scatter4" else ins[a]
                local.append(pltpu.make_async_copy(src, outs[a].at[mine], local_sems.at[a]))
            for cp in local:
                cp.start()

        def remote(r, a):
            px, py, pc = _flip(x, rel[r][0]), _flip(y, rel[r][1]), _flip(c, rel[r][2])
            theirs = slot(px, py, pc)
            if mode == "swap":
                src, dst_there, dst_here = ins[a], outs[a], outs[a]
            elif mode == "scatter4":
                src, dst_there, dst_here = ins[a].at[theirs], outs[a].at[mine], outs[a].at[theirs]
            else:
                src, dst_there, dst_here = ins[a], outs[a].at[mine], outs[a].at[theirs]
            k = r * n + a
            push = pltpu.make_async_remote_copy(src_ref=src, dst_ref=dst_there, send_sem=send_sems.at[k],
                                                recv_sem=recv_sems.at[k], device_id=(px, py, pc), device_id_type=MESH)
            land = pltpu.make_async_remote_copy(src_ref=src, dst_ref=dst_here, send_sem=send_sems.at[k],
                                                recv_sem=recv_sems.at[k], device_id=(px, py, pc), device_id_type=MESH)
            return push, land

        copies = [remote(r, a) for r in range(n_rel) for a in range(n)]
        for push, _ in copies:
            push.start()
        for _, land in copies:
            land.wait_recv()
        for push, _ in copies:
            push.wait_send()
        for cp in local:
            cp.wait()

    return pl.pallas_call(
        body, name=name, in_specs=[ANY] * n, out_specs=[ANY] * n, out_shape=out_shape,
        scratch_shapes=[pltpu.SemaphoreType.DMA((n_rel * n,)), pltpu.SemaphoreType.DMA((n_rel * n,)),
                        pltpu.SemaphoreType.DMA((max(n, 1),))],
    )(*arrays)


def _sum_slots(name, stacked):
    p, rows, cols = stacked.shape
    tr = _tile(rows, 256)

    def body(s_ref, o_ref):
        acc = s_ref[0].astype(F32)
        for j in range(1, p):
            acc = acc + s_ref[j].astype(F32)
        o_ref[...] = acc

    return pl.pallas_call(
        body, name=name, grid=(rows // tr,),
        in_specs=[pl.BlockSpec((p, tr, cols), lambda i: (0, i, 0))],
        out_specs=pl.BlockSpec((tr, cols), lambda i: (i, 0)),
        out_shape=jax.ShapeDtypeStruct((rows, cols), F32),
        compiler_params=pltpu.CompilerParams(dimension_semantics=("parallel",)),
    )(stacked)


def _adamw(name, parts, w, m, v):
    rows, cols = w.shape
    tr = _tile(rows, 256)
    n_parts = len(parts)

    def body(*refs):
        part_refs = refs[:n_parts]
        w_ref, m_ref, v_ref, g_out, d_out, m_out, v_out = refs[n_parts:]
        g = None
        for pr in part_refs:
            if len(pr.shape) == 3:
                for j in range(pr.shape[0]):
                    g = pr[j] if g is None else g + pr[j]
            else:
                g = pr[...] if g is None else g + pr[...]
        m_new = ADAM_B1 * m_ref[...] + (1.0 - ADAM_B1) * g
        v_new = ADAM_B2 * v_ref[...] + (1.0 - ADAM_B2) * jnp.square(g)
        m_hat = m_new / (1.0 - ADAM_B1 ** ADAM_STEP)
        v_hat = v_new / (1.0 - ADAM_B2 ** ADAM_STEP)
        g_out[...] = g
        d_out[...] = -ADAM_LR * (m_hat / (jnp.sqrt(v_hat) + ADAM_EPS) + ADAM_WD * w_ref[...])
        m_out[...] = m_new
        v_out[...] = v_new

    spec2 = pl.BlockSpec((tr, cols), lambda i: (i, 0))
    in_specs = [pl.BlockSpec((p.shape[0], tr, cols), lambda i: (0, i, 0)) if p.ndim == 3 else spec2 for p in parts]
    blocks = sum(_nbytes((p.shape[0] if p.ndim == 3 else 1, tr, cols), F32) for p in parts) + 7 * _nbytes((tr, cols), F32)
    return pl.pallas_call(
        body, name=name, grid=(rows // tr,),
        in_specs=in_specs + [spec2] * 3, out_specs=[spec2] * 4,
        out_shape=[jax.ShapeDtypeStruct((rows, cols), F32)] * 4,
        compiler_params=pltpu.CompilerParams(dimension_semantics=("parallel",),
                                             vmem_limit_bytes=_vmem_limit(blocks, 4 * _nbytes((tr, cols), F32))),
    )(*parts, w, m, v)


def _mod_fwd(c_all, w_ada2d, depth):
    nb, d = c_all.shape
    cols = w_ada2d.shape[1]
    tn = _tile(cols, 512, 128)

    def body(c_ref, w_ref, o_ref):
        cv = c_ref[...]
        o_ref[...] = _dot(cv * _sigmoid(cv), w_ref[...])

    return pl.pallas_call(
        body, name="mod_fwd", grid=(depth, cols // tn),
        in_specs=[pl.BlockSpec((nb, d), lambda l, j: (0, 0)), pl.BlockSpec((d, tn), lambda l, j: (l, j))],
        out_specs=pl.BlockSpec((nb, tn), lambda l, j: (l, j)),
        out_shape=jax.ShapeDtypeStruct((depth * nb, cols), F32),
        compiler_params=pltpu.CompilerParams(dimension_semantics=("parallel", "parallel")),
    )(c_all, w_ada2d)


def _wada_bwd(c_all, dmod2d, depth):
    nb, d = c_all.shape
    cols = dmod2d.shape[1]
    tn = _tile(cols, 512, 128)

    def body(c_ref, g_ref, o_ref):
        cv = c_ref[...]
        o_ref[...] = _dot_tn(cv * _sigmoid(cv), g_ref[...])

    return pl.pallas_call(
        body, name="wada_bwd", grid=(depth, cols // tn),
        in_specs=[pl.BlockSpec((nb, d), lambda l, j: (0, 0)), pl.BlockSpec((nb, tn), lambda l, j: (l, j))],
        out_specs=pl.BlockSpec((d, tn), lambda l, j: (l, j)),
        out_shape=jax.ShapeDtypeStruct((depth * d, cols), F32),
        compiler_params=pltpu.CompilerParams(dimension_semantics=("parallel", "parallel")),
    )(c_all, dmod2d)


def _rope_tables(pos_col, inv_freq_lane, nope):
    n_rows = pos_col.shape[0]
    tm = _tile(n_rows, 512)
    half = QK_ROPE // 2

    def body(p_ref, f_ref, c_ref, s1_ref, s2_ref):
        ang = p_ref[...] * f_ref[...]
        lane = lax.broadcasted_iota(jnp.int32, ang.shape, 1)
        first = (lane >= nope) & (lane < nope + half)
        second = (lane >= nope + half) & (lane < nope + 2 * half)
        cos, sin = jnp.cos(ang), jnp.sin(ang)
        c_ref[...] = jnp.where(first | second, cos, 1.0)
        s1_ref[...] = jnp.where(first, -sin, 0.0)
        s2_ref[...] = jnp.where(second, sin, 0.0)

    return _rowcall("rope_tables", body, n_rows, tm, [pos_col], [inv_freq_lane], [(HEAD_PAD, F32)] * 3)


def _rope(q, c, s1, s2):
    w = q.shape[1]
    return q * c + pltpu.roll(q, w - QK_ROPE // 2, axis=1) * s1 + pltpu.roll(q, QK_ROPE // 2, axis=1) * s2


def _rope_adjoint(dr, c, s1, s2):
    w = dr.shape[1]
    return dr * c + pltpu.roll(dr * s1, QK_ROPE // 2, axis=1) + pltpu.roll(dr * s2, w - QK_ROPE // 2, axis=1)


def _ssm_disc(ar, ai, log_dt, br, bi):
    dt = jnp.exp(log_dt)
    mag = jnp.exp(ar * dt)
    abr = mag * jnp.cos(ai * dt)
    abi = mag * jnp.sin(ai * dt)
    den = ar * ar + ai * ai
    nr = abr - 1.0
    ni = abi
    cr = (nr * ar + ni * ai) / den
    ci = (ni * ar - nr * ai) / den
    return abr, abi, cr * br - ci * bi, cr * bi + ci * br


def _ssm_disc_fwd(ar, ai, log_dt, br, bi):
    n_rows, m = br.shape
    tm = _tile(n_rows, 1024)

    def body(ar_ref, ai_ref, dt_ref, br_ref, bi_ref, o1, o2, o3, o4):
        o1[...], o2[...], o3[...], o4[...] = _ssm_disc(ar_ref[...], ai_ref[...], dt_ref[...], br_ref[...], bi_ref[...])

    return _rowcall("ssm_disc_fwd", body, n_rows, tm, [ar, ai, log_dt, br, bi], [],
                    [(1, F32), (1, F32), (m, F32), (m, F32)])


def _ssm_disc_bwd(ar, ai, log_dt, br, bi, g_abr, g_abi, g_bbr, g_bbi):
    n_rows, m = br.shape
    tm = _tile(n_rows, 1024)

    def body(ar_ref, ai_ref, dt_ref, br_ref, bi_ref, g1, g2, g3, g4, o1, o2, o3, o4, o5):
        _, vjp = jax.vjp(_ssm_disc, ar_ref[...], ai_ref[...], dt_ref[...], br_ref[...], bi_ref[...])
        o1[...], o2[...], o3[...], o4[...], o5[...] = vjp((g1[...], g2[...], g3[...], g4[...]))

    return _rowcall("ssm_disc_bwd", body, n_rows, tm, [ar, ai, log_dt, br, bi, g_abr, g_abi, g_bbr, g_bbi], [],
                    [(1, F32), (1, F32), (1, F32), (m, F32), (m, F32)])


def _lane_sum(v2d):
    def body(v_ref, o_ref):
        o_ref[...] = jnp.sum(v_ref[...], axis=1, keepdims=True)
    return pl.pallas_call(body, name="lane_sum", out_shape=jax.ShapeDtypeStruct((v2d.shape[0], 1), F32))(v2d)


def _in_fwd(x, n1g, sc1, sh1, w_parts, tm):
    n_rows = x.shape[0]
    widths = [w.shape[1] for w in w_parts]

    def body(x_ref, g_ref, sc_ref, sh_ref, *rest):
        w_refs, (hb_ref, *z_refs) = rest[:len(w_parts)], rest[len(w_parts):]
        xh, _ = _rms_stats(x_ref[...])
        h = (xh * g_ref[...]) * (1.0 + sc_ref[...]) + sh_ref[...]
        hb = h.astype(_MM)
        hb_ref[...] = hb.astype(_ACT)
        for w_ref, z_ref in zip(w_refs, z_refs):
            z_ref[...] = _dot(hb, w_ref[...])

    return _rowcall("in_fwd", body, n_rows, tm, [x], [n1g, sc1, sh1, *w_parts],
                    [(x.shape[1], _ACT)] + [(w, F32) for w in widths], temp_cols=4 * x.shape[1])


def _ssm_fwd(u, bre_blk, bim_blk, abr_row, abi_row, cre_blk, cimneg_blk, d_row):
    n_rows, sw = u.shape
    gp = abr_row.shape[1]
    t = _tile(n_rows, 256)

    def body(u_ref, bre_ref, bim_ref, ar_ref, ai_ref, cre_ref, cim_ref, d_ref, y_ref, hre_ref, him_ref, cr, ci):
        @pl.when(_first_step())
        def _():
            cr[...] = jnp.zeros_like(cr)
            ci[...] = jnp.zeros_like(ci)

        uv = u_ref[...]
        ub = uv.astype(_MM)
        hre_ref[...] = _dot(ub, bre_ref[...])
        him_ref[...] = _dot(ub, bim_ref[...])
        a_r, a_i = ar_ref[...], ai_ref[...]

        def step(k, carry):
            pr, pi = carry
            row = pl.ds(k, 1)
            hr = a_r * pr - a_i * pi + hre_ref[row, :]
            hi = a_r * pi + a_i * pr + him_ref[row, :]
            hre_ref[row, :] = hr
            him_ref[row, :] = hi
            return hr, hi

        pr, pi = lax.fori_loop(0, t, step, (cr[0:1, :], ci[0:1, :]), unroll=8)
        cr[0:1, :] = pr
        ci[0:1, :] = pi
        y_ref[...] = _dot(hre_ref[...], cre_ref[...]) + _dot(him_ref[...], cim_ref[...]) + d_ref[...] * uv

    row = lambda c: pl.BlockSpec((t, c), lambda i: (i, 0))
    full = lambda a: pl.BlockSpec(a.shape, lambda i: (0, 0), pipeline_mode=pl.Buffered(1))
    blocks = _nbytes((t, sw), F32) * 2 + 2 * _nbytes((t, gp), F32)
    resident = 4 * _nbytes((sw, gp), _MM)
    return pl.pallas_call(
        body, name="ssm_fwd", grid=(n_rows // t,),
        in_specs=[row(sw), full(bre_blk), full(bim_blk), full(abr_row), full(abi_row), full(cre_blk),
                  full(cimneg_blk), full(d_row)],
        out_specs=[row(sw), row(gp), row(gp)],
        out_shape=[jax.ShapeDtypeStruct((n_rows, sw), F32), jax.ShapeDtypeStruct((n_rows, gp), F32),
                   jax.ShapeDtypeStruct((n_rows, gp), F32)],
        scratch_shapes=[pltpu.VMEM((8, gp), F32), pltpu.VMEM((8, gp), F32)],
        compiler_params=pltpu.CompilerParams(dimension_semantics=("arbitrary",),
                                             vmem_limit_bytes=_vmem_limit(blocks, resident + 3 * _nbytes((t, gp), F32))),
    )(u, bre_blk, bim_blk, abr_row, abi_row, cre_blk, cimneg_blk, d_row)


def _mla_prep_fwd(cq, ckv, kr, rc, rs1, rs2, gq, gkv, wuq, wuk, wuv, tm):
    n_rows = cq.shape[0]

    def body(cq_ref, ckv_ref, kr_ref, c_ref, s1_ref, s2_ref, gq_ref, gkv_ref, wuq_ref, wuk_ref, wuv_ref,
             q_ref, k_ref, v_ref, cqn_ref, ckvn_ref):
        c, s1, s2 = c_ref[...], s1_ref[...], s2_ref[...]
        c8, s18, s28 = (jnp.tile(a, (1, N_HEADS)) for a in (c, s1, s2))
        xh, _ = _rms_stats(cq_ref[...])
        cqn = (xh * gq_ref[...]).astype(_MM)
        cqn_ref[...] = cqn.astype(_ACT)
        q_ref[...] = _rope(_dot(cqn, wuq_ref[...]), c8, s18, s28).astype(_ACT)
        xh, _ = _rms_stats(ckv_ref[...])
        ckvn = (xh * gkv_ref[...]).astype(_MM)
        ckvn_ref[...] = ckvn.astype(_ACT)
        kpe = _rope(kr_ref[...], c, s1, s2)
        k_ref[...] = (_dot(ckvn, wuk_ref[...]) + jnp.tile(kpe, (1, N_HEADS))).astype(_ACT)
        v_ref[...] = _dot(ckvn, wuv_ref[...]).astype(_ACT)

    return _rowcall("mla_prep_fwd", body, n_rows, tm, [cq, ckv, kr, rc, rs1, rs2], [gq, gkv, wuq, wuk, wuv],
                    [(QW, _ACT), (QW, _ACT), (QW, _ACT), (cq.shape[1], _ACT), (ckv.shape[1], _ACT)], temp_cols=6 * QW)


def _causal_steps(n_blocks, key_major):
    if key_major:
        pairs = [(qi, ki) for ki in range(n_blocks) for qi in range(ki, n_blocks)]
    else:
        pairs = [(qi, ki) for qi in range(n_blocks) for ki in range(qi + 1)]
    return (jnp.asarray(np.array([p[0] for p in pairs], np.int32)), jnp.asarray(np.array([p[1] for p in pairs], np.int32)))


def _attn_fwd(q, k, v, pos_col, pos_row, scale):
    n_rows = q.shape[0]
    ta = _tile(n_rows, 512, 128)
    nb = n_rows // ta
    hb = ATTN_HEADS_PER_STEP
    wide = hb * HEAD_PAD
    qmap, kmap = _causal_steps(nb, key_major=False)
    c2 = scale * LOG2_E

    def body(qm, km, q_ref, k_ref, v_ref, pq_ref, pk_ref, o_ref, lse_ref, m_sc, l_sc, acc_sc):
        s_id = pl.program_id(1)
        qi, ki = qm[s_id], km[s_id]

        @pl.when(ki == 0)
        def _():
            m_sc[...] = jnp.full_like(m_sc, NEG_INF)
            l_sc[...] = jnp.zeros_like(l_sc)
            acc_sc[...] = jnp.zeros_like(acc_sc)

        def update(on_diagonal):
            if on_diagonal:
                visible = pk_ref[...] <= pq_ref[...]
            for h in range(hb):
                lanes = slice(h * HEAD_PAD, (h + 1) * HEAD_PAD)
                s = _dot_nt(q_ref[:, lanes], k_ref[:, lanes])
                if on_diagonal:
                    s = jnp.where(visible, s, NEG_INF)
                m_prev = m_sc[:, lanes]
                m_new = jnp.maximum(m_prev, jnp.max(s, axis=1, keepdims=True))
                alpha = jnp.exp2((m_prev - m_new) * c2)
                p = jnp.exp2((s - m_new[:, :1]) * c2)
                l_new = alpha * l_sc[:, lanes] + jnp.sum(p, axis=1, keepdims=True)
                acc_new = alpha * acc_sc[:, lanes] + _dot(p, v_ref[:, lanes])
                if on_diagonal:
                    o_ref[:, lanes] = acc_new / l_new
                    lse_ref[:, lanes] = m_new * c2 + jnp.log2(l_new)
                else:
                    l_sc[:, lanes] = l_new
                    acc_sc[:, lanes] = acc_new
                    m_sc[:, lanes] = m_new

        pl.when(ki != qi)(functools.partial(update, False))
        pl.when(ki == qi)(functools.partial(update, True))

    qspec = pl.BlockSpec((ta, wide), lambda h, s, qm, km: (qm[s], h))
    kspec = pl.BlockSpec((ta, wide), lambda h, s, qm, km: (km[s], h))
    grid_spec = pltpu.PrefetchScalarGridSpec(
        num_scalar_prefetch=2, grid=(N_HEADS // hb, int(qmap.shape[0])),
        in_specs=[qspec, kspec, kspec,
                  pl.BlockSpec((ta, 1), lambda h, s, qm, km: (qm[s], 0)),
                  pl.BlockSpec((1, ta), lambda h, s, qm, km: (0, km[s]))],
        out_specs=[qspec, qspec],
        scratch_shapes=[pltpu.VMEM((ta, wide), F32)] * 3)
    return pl.pallas_call(
        body, name="attn_fwd", grid_spec=grid_spec,
        out_shape=[jax.ShapeDtypeStruct((n_rows, QW), F32), jax.ShapeDtypeStruct((n_rows, QW), F32)],
        compiler_params=pltpu.CompilerParams(
            dimension_semantics=("parallel", "arbitrary"),
            vmem_limit_bytes=_vmem_limit(8 * _nbytes((ta, wide), F32), 6 * hb * _nbytes((ta, ta), F32))),
    )(qmap, kmap, q, k, v, pos_col, pos_row)


def _mix_fwd(ypre, o, ga, gb, x, g1, bglu, wglu, wa, wb, wout, tm):
    n_rows, d = x.shape
    sw = ypre.shape[1]

    def body(y_ref, o_ref, ga_ref, gb_ref, x_ref, g1_ref, bglu_ref, wglu_ref, wa_ref, wb_ref, wout_ref,
             yg_ref, ya_ref, yb_ref, mg_ref, mo_ref, xo_ref):
        ys = _gelu(y_ref[...])
        yg = ys * _sigmoid(_dot(ys, wglu_ref[...]) + bglu_ref[...])
        yg_ref[...] = yg.astype(_ACT)
        ya = _dot(yg, wa_ref[...])
        yb = _dot(o_ref[...], wb_ref[...])
        ya_ref[...] = ya
        yb_ref[...] = yb
        merged = _sigmoid(ga_ref[...]) * ya + _sigmoid(gb_ref[...]) * yb
        mg_ref[...] = merged.astype(_ACT)
        mo = _dot(merged, wout_ref[...])
        mo_ref[...] = mo
        xo_ref[...] = x_ref[...] + g1_ref[...] * mo

    return _rowcall("mix_fwd", body, n_rows, tm, [ypre, o, ga, gb, x], [g1, bglu, wglu, wa, wb, wout],
                    [(sw, _ACT), (d, F32), (d, F32), (d, _ACT), (d, F32), (d, F32)], temp_cols=4 * d)


def _ffn_fwd(x, n2g, sc2, sh2, g2, wg, wu, wd, tm):
    n_rows, d = x.shape
    ff = wg.shape[1]

    def body(x_ref, g_ref, sc_ref, sh_ref, g2_ref, wg_ref, wu_ref, wd_ref, hb_ref, a_ref, b_ref, d_ref, xo_ref):
        xv = x_ref[...]
        xh, _ = _rms_stats(xv)
        hb = ((xh * g_ref[...]) * (1.0 + sc_ref[...]) + sh_ref[...]).astype(_MM)
        hb_ref[...] = hb.astype(_ACT)
        a = _dot(hb, wg_ref[...])
        b = _dot(hb, wu_ref[...])
        a_ref[...] = a
        b_ref[...] = b
        dn = _dot((a * _sigmoid(a)) * b, wd_ref[...])
        d_ref[...] = dn
        xo_ref[...] = xv + g2_ref[...] * dn

    return _rowcall("ffn_fwd", body, n_rows, tm, [x], [n2g, sc2, sh2, g2, wg, wu, wd],
                    [(d, _ACT), (ff, F32), (ff, F32), (d, F32), (d, F32)], temp_cols=3 * ff)


def _head(x, fg, target, tm):
    n_rows, d = x.shape

    def body(x_ref, t_ref, g_ref, dx_ref, loss_ref, dg_ref):
        xh, rstd = _rms_stats(x_ref[...])
        err = xh * g_ref[...] - t_ref[...]
        part = jnp.sum(jnp.mean(err * err, axis=-1, keepdims=True), axis=0, keepdims=True) * 0.5
        _acc(loss_ref, jnp.broadcast_to(part, loss_ref.shape))
        dy = err * (1.0 / d)
        _acc(dg_ref, _rows_sum(dy * xh))
        dx_ref[...] = _rms_bwd(dy * g_ref[...], xh, rstd)

    return _rowcall("head", body, n_rows, tm, [x, target], [fg], [(d, F32)], [((1, 128), F32), ((1, d), F32)],
                    temp_cols=4 * d)


def _ffn_bwd(dxo, xmid, a, b, dn, n2g, sc2, g2, wg, wu, wd, tm):
    n_rows, d = dxo.shape
    ff = a.shape[1]

    def act_body(dxo_ref, a_ref, b_ref, dn_ref, g2_ref, wd_ref, da_ref, db_ref, f_ref, dd_ref, dg2_ref):
        dxo_v = dxo_ref[...]
        dd = dxo_v * g2_ref[...]
        dd_ref[...] = dd.astype(_ACT)
        _acc(dg2_ref, _rows_sum(dxo_v * dn_ref[...]))
        df = _dot_nt(dd, wd_ref[...])
        av, bv = a_ref[...], b_ref[...]
        sa = _sigmoid(av)
        si = av * sa
        f_ref[...] = (si * bv).astype(_ACT)
        da_ref[...] = (df * bv * (sa * (1.0 + av * (1.0 - sa)))).astype(_ACT)
        db_ref[...] = (df * si).astype(_ACT)

    da, db, f, dd, dg2 = _rowcall("ffn_bwd_act", act_body, n_rows, 2 * tm, [dxo, a, b, dn], [g2, wd],
                                  [(ff, _ACT), (ff, _ACT), (ff, _ACT), (d, _ACT)], [((1, d), F32)], temp_cols=4 * ff)

    def in_body(dxo_ref, x_ref, da_ref, db_ref, g_ref, sc_ref, wg_ref, wu_ref, dx_ref, dsh_ref, dsc_ref, dn2_ref):
        dh = _dot_nt(da_ref[...], wg_ref[...]) + _dot_nt(db_ref[...], wu_ref[...])
        xh, rstd = _rms_stats(x_ref[...])
        yg = xh * g_ref[...]
        _acc(dsh_ref, _rows_sum(dh))
        _acc(dsc_ref, _rows_sum(dh * yg))
        dy = dh * (1.0 + sc_ref[...])
        _acc(dn2_ref, _rows_sum(dy * xh))
        dx_ref[...] = dxo_ref[...] + _rms_bwd(dy * g_ref[...], xh, rstd)

    dx, dsh, dsc, dn2 = _rowcall("ffn_bwd_in", in_body, n_rows, 2 * tm, [dxo, xmid, da, db], [n2g, sc2, wg, wu],
                                 [(d, F32)], [((1, d), F32)] * 3, temp_cols=5 * d)
    return dx, da, db, f, dd, dg2, dsh, dsc, dn2


def _mix_bwd(dxm, mo, ya, yb, ga, gb, ypre, g1, bglu, wglu, wa, wb, wout, tm):
    n_rows, d = dxm.shape
    sw = ypre.shape[1]

    def body(dxm_ref, mo_ref, ya_ref, yb_ref, ga_ref, gb_ref, y_ref, g1_ref, bglu_ref, wglu_ref, wa_ref, wb_ref,
             wout_ref, dmo_ref, dya_ref, dyb_ref, dt_ref, ys_ref, dga_ref, dgb_ref, dy_ref, do_ref, dg1_ref, dbg_ref):
        dxm_v = dxm_ref[...]
        dmo = dxm_v * g1_ref[...]
        dmo_ref[...] = dmo.astype(_ACT)
        _acc(dg1_ref, _rows_sum(dxm_v * mo_ref[...]))
        dmg = _dot_nt(dmo, wout_ref[...])
        sa, sb = _sigmoid(ga_ref[...]), _sigmoid(gb_ref[...])
        dya, dyb = dmg * sa, dmg * sb
        dya_ref[...] = dya.astype(_ACT)
        dyb_ref[...] = dyb.astype(_ACT)
        dga_ref[...] = (dmg * ya_ref[...] * (sa * (1.0 - sa))).astype(_ACT)
        dgb_ref[...] = (dmg * yb_ref[...] * (sb * (1.0 - sb))).astype(_ACT)
        do_ref[...] = _dot_nt(dyb, wb_ref[...])
        dyg = _dot_nt(dya, wa_ref[...])
        yv = y_ref[...]
        ys = _gelu(yv)
        ys_ref[...] = ys.astype(_ACT)
        sg = _sigmoid(_dot(ys, wglu_ref[...]) + bglu_ref[...])
        dt = dyg * ys * (sg * (1.0 - sg))
        dt_ref[...] = dt.astype(_ACT)
        _acc(dbg_ref, _rows_sum(dt))
        dys = dyg * sg + _dot_nt(dt, wglu_ref[...])
        dy_ref[...] = dys * _gelu_grad(yv)

    return _rowcall("mix_bwd", body, n_rows, tm, [dxm, mo, ya, yb, ga, gb, ypre], [g1, bglu, wglu, wa, wb, wout],
                    [(d, _ACT), (d, _ACT), (d, _ACT), (sw, _ACT), (sw, _ACT), (d, _ACT), (d, _ACT), (sw, F32), (QW, F32)],
                    [((1, d), F32), ((1, sw), F32)], temp_cols=6 * d)


def _attn_bwd(q, k, v, do, o, lse, pos_col, pos_row, scale):
    n_rows = q.shape[0]
    ta = _tile(n_rows, 512, 128)
    nb = n_rows // ta
    qmap, kmap = _causal_steps(nb, key_major=True)
    hb = ATTN_BWD_HEADS_PER_STEP
    wide = hb * HEAD_PAD
    c2 = scale * LOG2_E

    def body(qm, km, q_ref, k_ref, v_ref, do_ref, o_ref, lse_ref, pq_ref, pk_ref, dq_ref, dk_ref, dv_ref,
             dk_acc, dv_acc):
        s_id = pl.program_id(1)
        qi, ki = qm[s_id], km[s_id]

        @pl.when(s_id == 0)
        def _():
            dq_ref[...] = jnp.zeros_like(dq_ref)

        @pl.when(qi == ki)
        def _():
            dk_acc[...] = jnp.zeros_like(dk_acc)
            dv_acc[...] = jnp.zeros_like(dv_acc)

        rows = pl.ds(pl.multiple_of(qi * ta, ta), ta)

        def update(on_diagonal):
            if on_diagonal:
                visible = pk_ref[...] <= pq_ref[...]
            for h in range(hb):
                lanes = slice(h * HEAD_PAD, (h + 1) * HEAD_PAD)
                qv, kv, dov = q_ref[:, lanes], k_ref[:, lanes], do_ref[:, lanes]
                e = _dot_nt(qv, kv) * c2 - lse_ref[:, lanes][:, :1]
                if on_diagonal:
                    e = jnp.where(visible, e, NEG_INF)
                p = jnp.exp2(e)
                dp = _dot_nt(dov, v_ref[:, lanes])
                delta = jnp.sum(dov * o_ref[:, lanes], axis=1, keepdims=True)
                ds = p * (dp - delta)
                dv_acc[:, lanes] += _dot_tn(p, dov)
                dk_acc[:, lanes] += _dot_tn(ds, qv)
                dq_ref[rows, lanes] += _dot(ds, kv) * scale

        pl.when(ki != qi)(functools.partial(update, False))
        pl.when(ki == qi)(functools.partial(update, True))

        @pl.when(qi == nb - 1)
        def _():
            dk_ref[...] = dk_acc[...] * scale
            dv_ref[...] = dv_acc[...]

    qspec = pl.BlockSpec((ta, wide), lambda h, s, qm, km: (qm[s], h))
    kspec = pl.BlockSpec((ta, wide), lambda h, s, qm, km: (km[s], h))
    grid_spec = pltpu.PrefetchScalarGridSpec(
        num_scalar_prefetch=2, grid=(N_HEADS // hb, int(qmap.shape[0])),
        in_specs=[qspec, kspec, kspec, qspec, qspec, qspec,
                  pl.BlockSpec((ta, 1), lambda h, s, qm, km: (qm[s], 0)),
                  pl.BlockSpec((1, ta), lambda h, s, qm, km: (0, km[s]))],
        out_specs=[pl.BlockSpec((n_rows, wide), lambda h, s, qm, km: (0, h)), kspec, kspec],
        scratch_shapes=[pltpu.VMEM((ta, wide), F32), pltpu.VMEM((ta, wide), F32)])
    return pl.pallas_call(
        body, name="attn_bwd", grid_spec=grid_spec,
        out_shape=[jax.ShapeDtypeStruct((n_rows, QW), F32)] * 3,
        compiler_params=pltpu.CompilerParams(
            dimension_semantics=("parallel", "arbitrary"),
            vmem_limit_bytes=_vmem_limit(12 * _nbytes((ta, wide), F32) + _nbytes((n_rows, wide), F32),
                                         6 * hb * _nbytes((ta, ta), F32))),
    )(qmap, kmap, q, k, v, do, o, lse, pos_col, pos_row)


def _mla_prep_bwd(dq, dk, dv, cq, ckv, rc, rs1, rs2, gq, gkv, wuq, wuk, wuv, nope, tm):
    n_rows = cq.shape[0]
    ql, kl = cq.shape[1], ckv.shape[1]

    def body(dq_ref, dk_ref, dv_ref, cq_ref, ckv_ref, c_ref, s1_ref, s2_ref, gq_ref, gkv_ref, wuq_ref, wuk_ref,
             wuv_ref, dqp_ref, dcq_ref, dckv_ref, dkr_ref, dgq_ref, dgkv_ref):
        c, s1, s2 = c_ref[...], s1_ref[...], s2_ref[...]
        c8, s18, s28 = (jnp.tile(a, (1, N_HEADS)) for a in (c, s1, s2))
        dqp = _rope_adjoint(dq_ref[...], c8, s18, s28)
        dqp_ref[...] = dqp.astype(_ACT)
        dcqn = _dot_nt(dqp, wuq_ref[...])
        xh, rstd = _rms_stats(cq_ref[...])
        _acc(dgq_ref, _rows_sum(dcqn * xh))
        dcq_ref[...] = _rms_bwd(dcqn * gq_ref[...], xh, rstd).astype(_ACT)
        dkv = dk_ref[...]
        dkpe = dkv[:, 0:HEAD_PAD]
        for h in range(1, N_HEADS):
            dkpe = dkpe + dkv[:, h * HEAD_PAD:(h + 1) * HEAD_PAD]
        lane = lax.broadcasted_iota(jnp.int32, dkpe.shape, 1)
        dkpe = jnp.where((lane >= nope) & (lane < nope + QK_ROPE), dkpe, 0.0)
        dkr_ref[...] = _rope_adjoint(dkpe, c, s1, s2).astype(_ACT)
        dckvn = _dot_nt(dkv, wuk_ref[...]) + _dot_nt(dv_ref[...], wuv_ref[...])
        xh, rstd = _rms_stats(ckv_ref[...])
        _acc(dgkv_ref, _rows_sum(dckvn * xh))
        dckv_ref[...] = _rms_bwd(dckvn * gkv_ref[...], xh, rstd).astype(_ACT)

    return _rowcall("mla_prep_bwd", body, n_rows, tm, [dq, dk, dv, cq, ckv, rc, rs1, rs2], [gq, gkv, wuq, wuk, wuv],
                    [(QW, _ACT), (ql, _ACT), (kl, _ACT), (HEAD_PAD, _ACT)], [((1, ql), F32), ((1, kl), F32)],
                    temp_cols=6 * QW)


def _ssm_bwd(dy, u, hre, him, bre_blk, bim_blk, abr_row, abi_row, cre_blk, cimneg_blk, d_row):
    n_rows, sw = u.shape
    gp = abr_row.shape[1]
    t = _tile(n_rows, 256)
    n_chunks = n_rows // t

    def body(dy_ref, u_ref, hre_ref, him_ref, hbre_ref, hbim_ref, bre_ref, bim_ref, ar_ref, ai_ref, cre_ref, cim_ref,
             d_ref, du_ref, gre_ref, gim_ref, dar_ref, dai_ref, dd_ref, g_re, g_im, hs_re, hs_im, cr, ci):
        i = pl.program_id(0)

        @pl.when(i == 0)
        def _():
            cr[...] = jnp.zeros_like(cr)
            ci[...] = jnp.zeros_like(ci)

        dyv = dy_ref[...]
        dyb = dyv.astype(_MM)
        g_re[...] = _dot_nt(dyb, cre_ref[...])
        g_im[...] = _dot_nt(dyb, cim_ref[...])
        a_r, a_i = ar_ref[...], ai_ref[...]

        def step(k, carry):
            nr, ni = carry
            row = pl.ds(t - 1 - k, 1)
            gr = g_re[row, :] + a_r * nr + a_i * ni
            gi = g_im[row, :] + a_r * ni - a_i * nr
            g_re[row, :] = gr
            g_im[row, :] = gi
            return gr, gi

        nr, ni = lax.fori_loop(0, t, step, (cr[0:1, :], ci[0:1, :]), unroll=8)
        cr[0:1, :] = nr
        ci[0:1, :] = ni
        gr_all, gi_all = g_re[...], g_im[...]
        gre_ref[...] = gr_all.astype(_ACT)
        gim_ref[...] = gi_all.astype(_ACT)
        du_ref[...] = (_dot_nt(gr_all, bre_ref[...]) + _dot_nt(gi_all, bim_ref[...]) + d_ref[...] * dyv).astype(_ACT)
        _acc(dd_ref, _rows_sum(dyv * u_ref[...]))
        is_first_chunk = i == n_chunks - 1
        hs_re[0:8, :] = jnp.where(is_first_chunk, 0.0, hbre_ref[...])
        hs_im[0:8, :] = jnp.where(is_first_chunk, 0.0, hbim_ref[...])
        hs_re[8:t + 8, :] = hre_ref[...]
        hs_im[8:t + 8, :] = him_ref[...]
        hp_re, hp_im = hs_re[pl.ds(7, t), :], hs_im[pl.ds(7, t), :]
        _acc(dar_ref, _rows_sum(gr_all * hp_re + gi_all * hp_im))
        _acc(dai_ref, _rows_sum(gi_all * hp_re - gr_all * hp_im))

    rev = lambda c: pl.BlockSpec((t, c), lambda i: (n_chunks - 1 - i, 0))
    before = pl.BlockSpec((8, gp), lambda i: (jnp.maximum((n_chunks - 1 - i) * (t // 8) - 1, 0), 0))
    full = lambda a: pl.BlockSpec(a.shape, lambda i: (0, 0), pipeline_mode=pl.Buffered(1))
    acc = lambda c: pl.BlockSpec((1, c), lambda i: (0, 0))
    blocks = 2 * _nbytes((t, sw), F32) + 2 * _nbytes((t, gp), F32) + _nbytes((t, sw), _ACT) + 2 * _nbytes((t, gp), _ACT)
    resident = 4 * _nbytes((sw, gp), _MM) + 4 * _nbytes((t + 8, gp), F32)
    return pl.pallas_call(
        body, name="ssm_bwd", grid=(n_chunks,),
        in_specs=[rev(sw), rev(sw), rev(gp), rev(gp), before, before, full(bre_blk), full(bim_blk), full(abr_row),
                  full(abi_row), full(cre_blk), full(cimneg_blk), full(d_row)],
        out_specs=[rev(sw), rev(gp), rev(gp), acc(gp), acc(gp), acc(sw)],
        out_shape=[jax.ShapeDtypeStruct((n_rows, sw), _ACT), jax.ShapeDtypeStruct((n_rows, gp), _ACT),
                   jax.ShapeDtypeStruct((n_rows, gp), _ACT), jax.ShapeDtypeStruct((1, gp), F32),
                   jax.ShapeDtypeStruct((1, gp), F32), jax.ShapeDtypeStruct((1, sw), F32)],
        scratch_shapes=[pltpu.VMEM((t, gp), F32), pltpu.VMEM((t, gp), F32), pltpu.VMEM((t + 8, gp), F32),
                        pltpu.VMEM((t + 8, gp), F32), pltpu.VMEM((8, gp), F32), pltpu.VMEM((8, gp), F32)],
        compiler_params=pltpu.CompilerParams(dimension_semantics=("arbitrary",),
                                             vmem_limit_bytes=_vmem_limit(blocks, resident + 4 * _nbytes((t, gp), F32))),
    )(dy, u, hre, him, hre, him, bre_blk, bim_blk, abr_row, abi_row, cre_blk, cimneg_blk, d_row)


def _in_bwd(dxm, x, dz_parts, n1g, sc1, w_parts, tm):
    n_rows, d = x.shape
    n = len(dz_parts)

    def body(dxm_ref, x_ref, *rest):
        dz_refs = rest[:n]
        g_ref, sc_ref = rest[n], rest[n + 1]
        w_refs = rest[n + 2:2 * n + 2]
        dx_ref, dsh_ref, dsc_ref, dn1_ref = rest[2 * n + 2:]
        dh = None
        for dz_ref, w_ref in zip(dz_refs, w_refs):
            term = _dot_nt(dz_ref[...], w_ref[...])
            dh = term if dh is None else dh + term
        xh, rstd = _rms_stats(x_ref[...])
        yg = xh * g_ref[...]
        _acc(dsh_ref, _rows_sum(dh))
        _acc(dsc_ref, _rows_sum(dh * yg))
        dy = dh * (1.0 + sc_ref[...])
        _acc(dn1_ref, _rows_sum(dy * xh))
        dx_ref[...] = dxm_ref[...] + _rms_bwd(dy * g_ref[...], xh, rstd)

    return _rowcall("in_bwd", body, n_rows, tm, [dxm, x, *dz_parts], [n1g, sc1, *w_parts],
                    [(d, F32)], [((1, d), F32)] * 3, temp_cols=5 * d)


def _pad_heads(w, per_head):
    lead = w.shape[:-1]
    w = w.reshape(lead + (N_HEADS, per_head))
    w = jnp.pad(w, [(0, 0)] * len(lead) + [(0, 0), (0, HEAD_PAD - per_head)])
    return w.reshape(lead + (QW,))


def _unpad_heads(w, per_head):
    lead = w.shape[:-1]
    return w.reshape(lead + (N_HEADS, HEAD_PAD))[..., :per_head].reshape(lead + (N_HEADS * per_head,))


def _cols_from_chips(g):
    ch, dep, r, cs = g.shape
    return g.transpose(1, 2, 0, 3).reshape(dep, r, ch * cs)


def _rows_from_chips(g):
    ch, dep, rs, c = g.shape
    return g.transpose(1, 0, 2, 3).reshape(dep, ch * rs, c)


def _cols_to_chips(w):
    dep, r, c = w.shape
    return w.reshape(dep, r, 4, c // 4).transpose(2, 0, 1, 3)


def _rows_to_chips(w):
    dep, r, c = w.shape
    return w.reshape(dep, 4, r // 4, c).transpose(1, 0, 2, 3)


def _block_diag(b_gxy):
    g, xx, yy = b_gxy.shape
    eye = jnp.eye(g, dtype=b_gxy.dtype)
    return (b_gxy[:, :, None, :] * eye[:, None, :, None]).reshape(g * xx, g * yy)


def _block_diag_extract(blocks, g):
    n_blk = blocks.shape[0]
    per = g // n_blk
    xx, yy = blocks.shape[1] // per, blocks.shape[2] // per
    eye = jnp.eye(per, dtype=blocks.dtype)
    picked = jnp.sum(blocks.reshape(n_blk, per, xx, per, yy) * eye[None, :, None, :, None], axis=3)
    return picked.reshape(g, xx, yy)


def _pack_rows(arrays):
    parts = []
    for a in arrays:
        flat = a.reshape(-1)
        flat = jnp.pad(flat, (0, (-flat.shape[0]) % 1024))
        parts.append(flat.reshape(-1, 128))
    return jnp.concatenate(parts, axis=0)


def _unpack_rows(packed, shapes):
    out, row = [], 0
    for s in shapes:
        n = int(np.prod(s))
        rows = -(-n // 1024) * 8
        out.append(packed[row:row + rows].reshape(-1)[:n].reshape(s))
        row += rows
    return out


def kernel(x, c, positions, w_ada, b_ada, norm1_g, w_in, ssm_a_re, ssm_a_im, ssm_log_dt, ssm_b_re, ssm_b_im, ssm_c_re, ssm_c_im, ssm_d, w_glu, b_glu, w_a_out, q_norm_g, w_uq, kv_norm_g, w_uk, w_uv, w_b_out, w_out, norm2_g, w_gate, w_up, w_down, final_g, loss_target, m_w_ada, m_b_ada, m_norm1_g, m_w_in, m_ssm_a_re, m_ssm_a_im, m_ssm_log_dt, m_ssm_b_re, m_ssm_b_im, m_ssm_c_re, m_ssm_c_im, m_ssm_d, m_w_glu, m_b_glu, m_w_a_out, m_q_norm_g, m_w_uq, m_kv_norm_g, m_w_uk, m_w_uv, m_w_b_out, m_w_out, m_norm2_g, m_w_gate, m_w_up, m_w_down, m_final_g, v_w_ada, v_b_ada, v_norm1_g, v_w_in, v_ssm_a_re, v_ssm_a_im, v_ssm_log_dt, v_ssm_b_re, v_ssm_b_im, v_ssm_c_re, v_ssm_c_im, v_ssm_d, v_w_glu, v_b_glu, v_w_a_out, v_q_norm_g, v_w_uq, v_kv_norm_g, v_w_uk, v_w_uv, v_w_b_out, v_w_out, v_norm2_g, v_w_gate, v_w_up, v_w_down, v_final_g):
    weights = dict(w_ada=w_ada, b_ada=b_ada, norm1_g=norm1_g, w_in=w_in, ssm_a_re=ssm_a_re, ssm_a_im=ssm_a_im, ssm_log_dt=ssm_log_dt, ssm_b_re=ssm_b_re, ssm_b_im=ssm_b_im, ssm_c_re=ssm_c_re, ssm_c_im=ssm_c_im, ssm_d=ssm_d, w_glu=w_glu, b_glu=b_glu, w_a_out=w_a_out, q_norm_g=q_norm_g, w_uq=w_uq, kv_norm_g=kv_norm_g, w_uk=w_uk, w_uv=w_uv, w_b_out=w_b_out, w_out=w_out, norm2_g=norm2_g, w_gate=w_gate, w_up=w_up, w_down=w_down, final_g=final_g)
    mom_m = dict(w_ada=m_w_ada, b_ada=m_b_ada, norm1_g=m_norm1_g, w_in=m_w_in, ssm_a_re=m_ssm_a_re, ssm_a_im=m_ssm_a_im, ssm_log_dt=m_ssm_log_dt, ssm_b_re=m_ssm_b_re, ssm_b_im=m_ssm_b_im, ssm_c_re=m_ssm_c_re, ssm_c_im=m_ssm_c_im, ssm_d=m_ssm_d, w_glu=m_w_glu, b_glu=m_b_glu, w_a_out=m_w_a_out, q_norm_g=m_q_norm_g, w_uq=m_w_uq, kv_norm_g=m_kv_norm_g, w_uk=m_w_uk, w_uv=m_w_uv, w_b_out=m_w_b_out, w_out=m_w_out, norm2_g=m_norm2_g, w_gate=m_w_gate, w_up=m_w_up, w_down=m_w_down, final_g=m_final_g)
    mom_v = dict(w_ada=v_w_ada, b_ada=v_b_ada, norm1_g=v_norm1_g, w_in=v_w_in, ssm_a_re=v_ssm_a_re, ssm_a_im=v_ssm_a_im, ssm_log_dt=v_ssm_log_dt, ssm_b_re=v_ssm_b_re, ssm_b_im=v_ssm_b_im, ssm_c_re=v_ssm_c_re, ssm_c_im=v_ssm_c_im, ssm_d=v_ssm_d, w_glu=v_w_glu, b_glu=v_b_glu, w_a_out=v_w_a_out, q_norm_g=v_q_norm_g, w_uq=v_w_uq, kv_norm_g=v_kv_norm_g, w_uk=v_w_uk, w_uv=v_w_uv, w_b_out=v_w_b_out, w_out=v_w_out, norm2_g=v_norm2_g, w_gate=v_w_gate, w_up=v_w_up, w_down=v_w_down, final_g=v_final_g)
    names = list(weights)

    depth = w_in.shape[0]
    seq, d = x.shape[1], x.shape[2]
    sw = ssm_d.shape[1]
    groups, n_state, n_chan = ssm_b_re.shape[1:]
    gp = groups * n_state
    ql, kl = q_norm_g.shape[1], kv_norm_g.shape[1]
    nope = w_uk.shape[2] * 4 // N_HEADS
    vdim = w_uv.shape[2] * 4 // N_HEADS
    qk_dim = nope + QK_ROPE
    scale = qk_dim ** -0.5
    tm = _tile(seq, 256, 16)
    ssm_blocks = max(sw // 128, 1)
    tm_ffn = _tile(seq, 128, 16)
    me = 4 * lax.axis_index("x") + 2 * lax.axis_index("y") + lax.axis_index("c")
    chip = 2 * lax.axis_index("x") + lax.axis_index("y")

    xs = x.reshape(seq, d)
    target = loss_target.reshape(seq, d)
    pos_f = positions.astype(F32)
    pos_col = pos_f.reshape(seq, 1)
    pos_row = pos_f.reshape(1, seq)

    (c_all,) = _exchange("gather_c", "gather8", [c])
    c_all = c_all.reshape(8, d)
    ada_cols = w_ada.shape[2]
    mod_part = _mod_fwd(c_all, w_ada.reshape(depth * d, ada_cols), depth)
    (mod_all,) = _exchange("gather_mod", "gather8", [mod_part])
    mod_all = mod_all.reshape(4, 2, depth, 8, ada_cols)[:, 0]
    mod_me = lax.dynamic_index_in_dim(mod_all, me, axis=2, keepdims=False)
    mod = mod_me.transpose(1, 0, 2).reshape(depth, 4 * ada_cols) + b_ada
    mod = mod.reshape(depth, 6, 1, d)

    big = ["w_in", "w_glu", "w_a_out", "w_uq", "w_uk", "w_uv", "w_b_out", "w_out", "w_gate", "w_up", "w_down"]
    row_sharded = {"w_glu", "w_out", "w_down"}
    gathered = _exchange("gather_weights", "gather4", [weights[n].astype(_MM) for n in big])
    full = {n: (_rows_from_chips(g) if n in row_sharded else _cols_from_chips(g)) for n, g in zip(big, gathered)}
    o1, o2, o3, o4, o5 = sw, sw + ql, sw + ql + kl, sw + ql + kl + QK_ROPE, sw + ql + kl + QK_ROPE + d
    wi = full["w_in"]
    w_u, w_cq, w_ckv, w_ga, w_gb = wi[:, :, :o1], wi[:, :, o1:o2], wi[:, :, o2:o3], wi[:, :, o4:o5], wi[:, :, o5:]
    w_kr = jnp.pad(wi[:, :, o3:o4], ((0, 0), (0, 0), (nope, HEAD_PAD - nope - QK_ROPE)))
    wuq_p = _pad_heads(full["w_uq"], qk_dim)
    wuk_p = _pad_heads(full["w_uk"], nope)
    wuv_p = _pad_heads(full["w_uv"], vdim)
    wb_p = _pad_heads(full["w_b_out"].transpose(0, 2, 1), vdim).transpose(0, 2, 1)

    inv_freq = ROPE_BASE ** (-jnp.arange(0, QK_ROPE, 2, dtype=F32) / QK_ROPE)
    inv_lane = jnp.pad(jnp.concatenate([inv_freq, inv_freq]), (nope, HEAD_PAD - nope - QK_ROPE)).reshape(1, HEAD_PAD)
    rc, rs1, rs2 = _rope_tables(pos_col, inv_lane, nope)
    a_re_col = ssm_a_re.reshape(depth * gp, 1)
    a_im_col = ssm_a_im.reshape(depth * gp, 1)
    ldt_col = jnp.broadcast_to(ssm_log_dt[:, :, None], (depth, groups, n_state)).reshape(depth * gp, 1)
    b_re2, b_im2 = ssm_b_re.reshape(depth * gp, n_chan), ssm_b_im.reshape(depth * gp, n_chan)
    abr, abi, bbr, bbi = _ssm_disc_fwd(a_re_col, a_im_col, ldt_col, b_re2, b_im2)
    abr_rows, abi_rows = abr.reshape(depth, 1, gp), abi.reshape(depth, 1, gp)
    bbr, bbi = bbr.reshape(depth, groups, n_state, n_chan), bbi.reshape(depth, groups, n_state, n_chan)

    saved = []
    xl = xs
    for l in range(depth):
        sh1, sc1, g1, sh2, sc2, g2 = (mod[l, j] for j in range(6))
        n1g, n2g = norm1_g[l].reshape(1, d), norm2_g[l].reshape(1, d)
        w_parts = [w_u[l], w_cq[l], w_ckv[l], w_kr[l], w_ga[l], w_gb[l]]
        hb, u, cq, ckv, kr, ga, gb = _in_fwd(xl, n1g, sc1, sh1, w_parts, tm)
        bre_blk = _block_diag(bbr[l].transpose(0, 2, 1)).astype(_MM)
        bim_blk = _block_diag(bbi[l].transpose(0, 2, 1)).astype(_MM)
        cre_blk = _block_diag(ssm_c_re[l].transpose(0, 2, 1)).astype(_MM)
        cimneg_blk = _block_diag(-ssm_c_im[l].transpose(0, 2, 1)).astype(_MM)
        d_row = ssm_d[l].reshape(1, sw)
        ssm_w = (bre_blk, bim_blk, abr_rows[l], abi_rows[l], cre_blk, cimneg_blk, d_row)
        ypre, hre, him = _ssm_fwd(u, *ssm_w)
        gq, gkv = q_norm_g[l].reshape(1, ql), kv_norm_g[l].reshape(1, kl)
        q, k, v, cqn, ckvn = _mla_prep_fwd(cq, ckv, kr, rc, rs1, rs2, gq, gkv, wuq_p[l], wuk_p[l], wuv_p[l], tm)
        o, lse = _attn_fwd(q, k, v, pos_col, pos_row, scale)
        bglu = b_glu[l].reshape(1, sw)
        yg, ya, yb, merged, mo, xmid = _mix_fwd(ypre, o, ga, gb, xl, g1, bglu, full["w_glu"][l], full["w_a_out"][l],
                                                 wb_p[l], full["w_out"][l], tm)
        hb2, fa, fb, dn, xout = _ffn_fwd(xmid, n2g, sc2, sh2, g2, full["w_gate"][l], full["w_up"][l], full["w_down"][l], tm_ffn)
        saved.append(dict(x=xl, hb=hb, u=u, cq=cq, ckv=ckv, ga=ga, gb=gb, ssm_w=ssm_w, ypre=ypre, hre=hre, him=him,
                          q=q, k=k, v=v, cqn=cqn, ckvn=ckvn, o=o, lse=lse, yg=yg, ya=ya, yb=yb, merged=merged, mo=mo,
                          xmid=xmid, hb2=hb2, fa=fa, fb=fb, dn=dn, w_parts=w_parts))
        xl = xout

    dx, loss_acc, g_final = _head(xl, final_g.reshape(1, d), target, tm)
    loss = lax.psum(loss_acc[0, 0], ("x", "y", "c"))

    per_layer = ["w_gate", "w_up", "w_down", "norm2_g", "w_out", "w_a_out", "w_b_out", "w_glu", "b_glu", "w_uq", "w_uk",
                 "w_uv", "q_norm_g", "kv_norm_g", "ssm_d", "ssm_c_re", "ssm_c_im", "w_in", "norm1_g"]
    grads = {n: [None] * depth for n in per_layer}
    dmod = [None] * depth
    for l in reversed(range(depth)):
        s = saved[l]
        sh1, sc1, g1, sh2, sc2, g2 = (mod[l, j] for j in range(6))
        n1g, n2g = norm1_g[l].reshape(1, d), norm2_g[l].reshape(1, d)
        dxm, da, db, fb16, dd, dg2, dsh2, dsc2, dn2 = _ffn_bwd(
            dx, s["xmid"], s["fa"], s["fb"], s["dn"], n2g, sc2, g2, full["w_gate"][l], full["w_up"][l], full["w_down"][l], tm_ffn)
        grads["w_gate"][l] = _mm_tn("dw_gate", s["hb2"], da)
        grads["w_up"][l] = _mm_tn("dw_up", s["hb2"], db)
        grads["w_down"][l] = _mm_tn("dw_down", fb16, dd)
        grads["norm2_g"][l] = dn2.reshape(d)

        bglu = b_glu[l].reshape(1, sw)
        dmo, dya, dyb, dt, ys, dga, dgb, dypre, do, dg1, dbglu = _mix_bwd(
            dxm, s["mo"], s["ya"], s["yb"], s["ga"], s["gb"], s["ypre"], g1, bglu, full["w_glu"][l], full["w_a_out"][l],
            wb_p[l], full["w_out"][l], tm)
        grads["w_out"][l] = _mm_tn("dw_out", s["merged"], dmo)
        grads["w_a_out"][l] = _mm_tn("dw_a_out", s["yg"], dya)
        dwb_p = _mm_tn("dw_b_out", s["o"], dyb)
        grads["w_b_out"][l] = _unpad_heads(dwb_p.T, vdim).T
        grads["w_glu"][l] = _mm_tn("dw_glu", ys, dt)
        grads["b_glu"][l] = dbglu.reshape(sw)

        dq, dk, dv = _attn_bwd(s["q"], s["k"], s["v"], do, s["o"], s["lse"], pos_col, pos_row, scale)
        gq, gkv = q_norm_g[l].reshape(1, ql), kv_norm_g[l].reshape(1, kl)
        dqp, dcq, dckv, dkr, dgq, dgkv = _mla_prep_bwd(dq, dk, dv, s["cq"], s["ckv"], rc, rs1, rs2, gq, gkv,
                                                       wuq_p[l], wuk_p[l], wuv_p[l], nope, tm)
        grads["w_uq"][l] = _unpad_heads(_mm_tn("dw_uq", s["cqn"], dqp), qk_dim)
        grads["w_uk"][l] = _unpad_heads(_mm_tn("dw_uk", s["ckvn"], dk), nope)
        grads["w_uv"][l] = _unpad_heads(_mm_tn("dw_uv", s["ckvn"], dv), vdim)
        grads["q_norm_g"][l] = dgq.reshape(ql)
        grads["kv_norm_g"][l] = dgkv.reshape(kl)

        du, gre, gim, dar, dai, ddskip = _ssm_bwd(dypre, s["u"], s["hre"], s["him"], *s["ssm_w"])
        grads["ssm_d"][l] = ddskip.reshape(sw)
        d_bre = _block_diag_extract(_mm_tn_blocks("d_bre", s["u"], gre, ssm_blocks), groups).transpose(0, 2, 1)
        d_bim = _block_diag_extract(_mm_tn_blocks("d_bim", s["u"], gim, ssm_blocks), groups).transpose(0, 2, 1)
        grads["ssm_c_re"][l] = _block_diag_extract(_mm_tn_blocks("d_cre", s["hre"], dypre, ssm_blocks), groups).transpose(0, 2, 1)
        grads["ssm_c_im"][l] = -_block_diag_extract(_mm_tn_blocks("d_cim", s["him"], dypre, ssm_blocks), groups).transpose(0, 2, 1)
        s["disc_grads"] = (dar.reshape(gp, 1), dai.reshape(gp, 1), d_bre.reshape(gp, n_chan), d_bim.reshape(gp, n_chan))

        dz_parts = [du, dcq, dckv, dkr, dga, dgb]
        dx, dsh1, dsc1, dn1 = _in_bwd(dxm, s["x"], dz_parts, n1g, sc1, s["w_parts"], tm)
        dw_parts = [_mm_tn("dw_in_%d" % j, s["hb"], dz) for j, dz in enumerate(dz_parts)]
        dw_parts[3] = dw_parts[3][:, nope:nope + QK_ROPE]
        grads["w_in"][l] = jnp.concatenate(dw_parts, axis=1)
        grads["norm1_g"][l] = dn1.reshape(d)
        dmod[l] = jnp.concatenate([dsh1, dsc1, dg1, dsh2, dsc2, dg2], axis=1).reshape(6 * d)
    grad_x = dx.reshape(x.shape)

    disc = [jnp.concatenate([saved[l]["disc_grads"][j] for l in range(depth)], axis=0) for j in range(4)]
    da_re, da_im, dldt, db_re, db_im = _ssm_disc_bwd(a_re_col, a_im_col, ldt_col, b_re2, b_im2, *disc)
    stacked = {n: jnp.stack(v) for n, v in grads.items()}
    stacked["ssm_a_re"] = da_re.reshape(ssm_a_re.shape)
    stacked["ssm_a_im"] = da_im.reshape(ssm_a_im.shape)
    stacked["ssm_log_dt"] = _lane_sum(dldt.reshape(depth * groups, n_state)).reshape(ssm_log_dt.shape)
    stacked["ssm_b_re"] = db_re.reshape(ssm_b_re.shape)
    stacked["ssm_b_im"] = db_im.reshape(ssm_b_im.shape)
    stacked["final_g"] = g_final.reshape(d)
    stacked["b_ada"] = jnp.stack(dmod)

    small = [n for n in names if n not in big and n != "w_ada"]
    small_shapes = [weights[n].shape for n in small]
    (small_all,) = _exchange("gather_small", "gather8", [_pack_rows([stacked[n] for n in small])])
    sg, sd, sm, sv = _adamw("adamw_small", [small_all], _pack_rows([weights[n] for n in small]),
                            _pack_rows([mom_m[n] for n in small]), _pack_rows([mom_v[n] for n in small]))
    out_g = dict(zip(small, _unpack_rows(sg, small_shapes)))
    out_d = dict(zip(small, _unpack_rows(sd, small_shapes)))
    out_m = dict(zip(small, _unpack_rows(sm, small_shapes)))
    out_v = dict(zip(small, _unpack_rows(sv, small_shapes)))

    n_dmod = depth * 6 * d
    dmod_all = small_all[:, :n_dmod // 128].reshape(8, depth, 6 * d)
    dmod_cols = lax.dynamic_slice_in_dim(dmod_all, chip * ada_cols, ada_cols, axis=2)
    g_wada = _wada_bwd(c_all, dmod_cols.transpose(1, 0, 2).reshape(depth * 8, ada_cols), depth)
    res = _adamw("adamw_w_ada", [g_wada], w_ada.reshape(depth * d, ada_cols), m_w_ada.reshape(depth * d, ada_cols),
                 v_w_ada.reshape(depth * d, ada_cols))
    out_g["w_ada"], out_d["w_ada"], out_m["w_ada"], out_v["w_ada"] = (r.reshape(w_ada.shape) for r in res)

    to_chips = [(_rows_to_chips if n in row_sharded else _cols_to_chips)(stacked[n]).astype(_WIRE) for n in big]
    landed = _exchange("scatter_grads", "scatter4", to_chips)
    partial = [_sum_slots("sum_" + n, r.reshape(4, -1, r.shape[-1])) for n, r in zip(big, landed)]
    sibling = _exchange("swap_partials", "swap", partial)
    for n, mine, theirs in zip(big, partial, sibling):
        shp = weights[n].shape
        as2d = lambda a: a.reshape(-1, shp[-1])
        res = _adamw("adamw_" + n, [mine, theirs], as2d(weights[n]), as2d(mom_m[n]), as2d(mom_v[n]))
        out_g[n], out_d[n], out_m[n], out_v[n] = (r.reshape(shp) for r in res)

    return (loss, grad_x, *[out_g[n] for n in names], *[out_d[n] for n in names], *[out_m[n] for n in names],
            *[out_v[n] for n in names])
```

```python
import functools
import math

import numpy as np
import jax
import jax.numpy as jnp
from jax import lax
from jax.experimental import pallas as pl
from jax.experimental.pallas import tpu as pltpu

F32 = jnp.float32
_MM = jnp.bfloat16
_ACT = jnp.bfloat16
_WIRE = jnp.bfloat16

N_HEADS = 8
QK_ROPE = 32
HEAD_PAD = 128
QW = N_HEADS * HEAD_PAD
ROPE_BASE = 10000.0
EPS = 1e-6
DT_MIN = 1e-3
ADAM_LR = 0.001
ADAM_B1 = 0.9
ADAM_B2 = 0.999
ADAM_EPS = 1e-08
ADAM_WD = 0.01
ADAM_STEP = 10
NEG_INF = -1e30
LOG2_E = math.log2(math.e)
ATTN_HEADS_PER_STEP = 4
ATTN_BWD_HEADS_PER_STEP = 2

V7X_VMEM_BYTES = 64 * 1024 * 1024
VMEM_RESERVE_BYTES = 6 * 1024 * 1024
MESH = pl.DeviceIdType.MESH
ANY = pl.BlockSpec(memory_space=pl.ANY)


def _vmem_limit(block_bytes, temp_bytes):
    want = 2 * block_bytes + temp_bytes
    return int(min(V7X_VMEM_BYTES - VMEM_RESERVE_BYTES, max(want, 32 * 1024 * 1024)))


def _nbytes(shape, dtype):
    return int(np.prod(shape)) * jnp.dtype(dtype).itemsize


def _tile(n, target, mult=8):
    t = min(n, target)
    while t >= mult:
        if n % t == 0 and t % mult == 0:
            return t
        t -= 1
    return n


def _dot(a, b):
    return jnp.dot(a.astype(_MM), b.astype(_MM), preferred_element_type=F32)


def _dot_nt(a, b):
    return lax.dot_general(a.astype(_MM), b.astype(_MM), (((1,), (1,)), ((), ())), preferred_element_type=F32)


def _dot_tn(a, b):
    return lax.dot_general(a.astype(_MM), b.astype(_MM), (((0,), (0,)), ((), ())), preferred_element_type=F32)


def _sigmoid(x):
    return jax.nn.sigmoid(x)


_GELU_K = math.sqrt(2.0 / math.pi)


def _gelu(x):
    return x * (0.5 * (1.0 + jnp.tanh(_GELU_K * (x + 0.044715 * (x * x * x)))))


def _gelu_grad(x):
    th = jnp.tanh(_GELU_K * (x + 0.044715 * (x * x * x)))
    return 0.5 * (1.0 + th) + 0.5 * x * (1.0 - th * th) * (_GELU_K * (1.0 + 3.0 * 0.044715 * (x * x)))


def _rows_sum(v):
    return jnp.sum(v, axis=0, keepdims=True)


def _rms_stats(x):
    rstd = lax.rsqrt(jnp.mean(x * x, axis=-1, keepdims=True) + EPS)
    return x * rstd, rstd


def _rms_bwd(dxh, xh, rstd):
    return rstd * (dxh - xh * jnp.mean(dxh * xh, axis=-1, keepdims=True))


def _rowcall(name, body, n_rows, tm, row_ins, full_ins, row_outs, acc_outs=(), temp_cols=0):
    grid = (n_rows // tm,)
    in_specs = [pl.BlockSpec((tm, a.shape[1]), lambda i: (i, 0)) for a in row_ins]
    in_specs += [pl.BlockSpec(a.shape, lambda i: (0, 0), pipeline_mode=pl.Buffered(1)) for a in full_ins]
    out_shape = [jax.ShapeDtypeStruct((n_rows, c), dt) for c, dt in row_outs]
    out_shape += [jax.ShapeDtypeStruct(s, dt) for s, dt in acc_outs]
    out_specs = [pl.BlockSpec((tm, c), lambda i: (i, 0)) for c, _ in row_outs]
    out_specs += [pl.BlockSpec(s, lambda i: (0, 0)) for s, _ in acc_outs]
    blocks = sum(_nbytes((tm, a.shape[1]), a.dtype) for a in row_ins)
    blocks += sum(_nbytes((tm, c), dt) for c, dt in row_outs) + sum(_nbytes(s, dt) for s, dt in acc_outs)
    resident = sum(_nbytes(a.shape, a.dtype) for a in full_ins)
    limit = _vmem_limit(blocks, resident + _nbytes((tm, temp_cols), F32))
    res = pl.pallas_call(
        body, name=name, grid=grid, in_specs=in_specs, out_specs=out_specs, out_shape=out_shape,
        compiler_params=pltpu.CompilerParams(
            dimension_semantics=("arbitrary" if acc_outs else "parallel",), vmem_limit_bytes=limit),
    )(*row_ins, *full_ins)
    return res


def _first_step():
    return pl.program_id(0) == 0


def _acc(ref, val):
    @pl.when(_first_step())
    def _():
        ref[...] = val

    @pl.when(jnp.logical_not(_first_step()))
    def _():
        ref[...] += val


def _mm_tn(name, a, g):
    n_rows, k = a.shape
    n = g.shape[1]
    tk = k if k <= 1024 else _tile(k, 1408, 128)
    tn = n if n <= 1024 else _tile(n, 1408, 128)
    tl = _tile(n_rows, 512, 16)

    def body(a_ref, g_ref, o_ref):
        @pl.when(pl.program_id(2) == 0)
        def _():
            o_ref[...] = jnp.zeros_like(o_ref)
        o_ref[...] += _dot_tn(a_ref[...], g_ref[...])

    blocks = _nbytes((tl, tk), a.dtype) + _nbytes((tl, tn), g.dtype) + _nbytes((tk, tn), F32)
    return pl.pallas_call(
        body, name=name, grid=(k // tk, n // tn, n_rows // tl),
        in_specs=[pl.BlockSpec((tl, tk), lambda i, j, l: (l, i)), pl.BlockSpec((tl, tn), lambda i, j, l: (l, j))],
        out_specs=pl.BlockSpec((tk, tn), lambda i, j, l: (i, j)),
        out_shape=jax.ShapeDtypeStruct((k, n), F32),
        compiler_params=pltpu.CompilerParams(
            dimension_semantics=("parallel", "parallel", "arbitrary"),
            vmem_limit_bytes=_vmem_limit(blocks, 2 * _nbytes((tl, max(tk, tn)), F32) + _nbytes((tk, tn), F32))),
    )(a, g)


def _place():
    return lax.axis_index("x"), lax.axis_index("y"), lax.axis_index("c")


def _flip(v, bit):
    return 1 - v if bit else v


def _exchange(name, mode, arrays):
    n = len(arrays)
    if mode == "gather8":
        rel = [((k >> 2) & 1, (k >> 1) & 1, k & 1) for k in range(1, 8)]
        out_shape = [jax.ShapeDtypeStruct((8,) + a.shape, a.dtype) for a in arrays]
    elif mode == "gather4":
        rel = [((k >> 1) & 1, k & 1, 0) for k in range(1, 4)]
        out_shape = [jax.ShapeDtypeStruct((4,) + a.shape, a.dtype) for a in arrays]
    elif mode == "scatter4":
        rel = [((k >> 1) & 1, k & 1, 0) for k in range(1, 4)]
        out_shape = [jax.ShapeDtypeStruct(a.shape, a.dtype) for a in arrays]
    else:
        rel = [(0, 0, 1)]
        out_shape = [jax.ShapeDtypeStruct(a.shape, a.dtype) for a in arrays]
    n_rel = len(rel)

    def body(*refs):
        ins, outs = refs[:n], refs[n:2 * n]
        send_sems, recv_sems, local_sems = refs[2 * n:]
        x, y, c = _place()

        def slot(px, py, pc):
            return 4 * px + 2 * py + pc if mode == "gather8" else 2 * px + py

        mine = slot(x, y, c)
        local = []
        if mode != "swap":
            for a in range(n):
                src = ins[a].at[mine] if mode == "scatter4" else ins[a]
                local.append(pltpu.make_async_copy(src, outs[a].at[mine], local_sems.at[a]))
            for cp in local:
                cp.start()

        def remote(r, a):
            px, py, pc = _flip(x, rel[r][0]), _flip(y, rel[r][1]), _flip(c, rel[r][2])
            theirs = slot(px, py, pc)
            if mode == "swap":
                src, dst_there, dst_here = ins[a], outs[a], outs[a]
            elif mode == "scatter4":
                src, dst_there, dst_here = ins[a].at[theirs], outs[a].at[mine], outs[a].at[theirs]
            else:
                src, dst_there, dst_here = ins[a], outs[a].at[mine], outs[a].at[theirs]
            k = r * n + a
            push = pltpu.make_async_remote_copy(src_ref=src, dst_ref=dst_there, send_sem=send_sems.at[k],
                                                recv_sem=recv_sems.at[k], device_id=(px, py, pc), device_id_type=MESH)
            land = pltpu.make_async_remote_copy(src_ref=src, dst_ref=dst_here, send_sem=send_sems.at[k],
                                                recv_sem=recv_sems.at[k], device_id=(px, py, pc), device_id_type=MESH)
            return push, land

        copies = [remote(r, a) for r in range(n_rel) for a in range(n)]
        for push, _ in copies:
            push.start()
        for _, land in copies:
            land.wait_recv()
        for push, _ in copies:
            push.wait_send()
        for cp in local:
            cp.wait()

    return pl.pallas_call(
        body, name=name, in_specs=[ANY] * n, out_specs=[ANY] * n, out_shape=out_shape,
        scratch_shapes=[pltpu.SemaphoreType.DMA((n_rel * n,)), pltpu.SemaphoreType.DMA((n_rel * n,)),
                        pltpu.SemaphoreType.DMA((max(n, 1),))],
    )(*arrays)


def _sum_slots(name, stacked):
    p, rows, cols = stacked.shape
    tr = _tile(rows, 256)

    def body(s_ref, o_ref):
        acc = s_ref[0].astype(F32)
        for j in range(1, p):
            acc = acc + s_ref[j].astype(F32)
        o_ref[...] = acc

    return pl.pallas_call(
        body, name=name, grid=(rows // tr,),
        in_specs=[pl.BlockSpec((p, tr, cols), lambda i: (0, i, 0))],
        out_specs=pl.BlockSpec((tr, cols), lambda i: (i, 0)),
        out_shape=jax.ShapeDtypeStruct((rows, cols), F32),
        compiler_params=pltpu.CompilerParams(dimension_semantics=("parallel",)),
    )(stacked)


def _adamw(name, parts, w, m, v):
    rows, cols = w.shape
    tr = _tile(rows, 256)
    n_parts = len(parts)

    def body(*refs):
        part_refs = refs[:n_parts]
        w_ref, m_ref, v_ref, g_out, d_out, m_out, v_out = refs[n_parts:]
        g = None
        for pr in part_refs:
            if len(pr.shape) == 3:
                for j in range(pr.shape[0]):
                    g = pr[j] if g is None else g + pr[j]
            else:
                g = pr[...] if g is None else g + pr[...]
        m_new = ADAM_B1 * m_ref[...] + (1.0 - ADAM_B1) * g
        v_new = ADAM_B2 * v_ref[...] + (1.0 - ADAM_B2) * jnp.square(g)
        m_hat = m_new / (1.0 - ADAM_B1 ** ADAM_STEP)
        v_hat = v_new / (1.0 - ADAM_B2 ** ADAM_STEP)
        g_out[...] = g
        d_out[...] = -ADAM_LR * (m_hat / (jnp.sqrt(v_hat) + ADAM_EPS) + ADAM_WD * w_ref[...])
        m_out[...] = m_new
        v_out[...] = v_new

    spec2 = pl.BlockSpec((tr, cols), lambda i: (i, 0))
    in_specs = [pl.BlockSpec((p.shape[0], tr, cols), lambda i: (0, i, 0)) if p.ndim == 3 else spec2 for p in parts]
    blocks = sum(_nbytes((p.shape[0] if p.ndim == 3 else 1, tr, cols), F32) for p in parts) + 7 * _nbytes((tr, cols), F32)
    return pl.pallas_call(
        body, name=name, grid=(rows // tr,),
        in_specs=in_specs + [spec2] * 3, out_specs=[spec2] * 4,
        out_shape=[jax.ShapeDtypeStruct((rows, cols), F32)] * 4,
        compiler_params=pltpu.CompilerParams(dimension_semantics=("parallel",),
                                             vmem_limit_bytes=_vmem_limit(blocks, 4 * _nbytes((tr, cols), F32))),
    )(*parts, w, m, v)


def _mod_fwd(c_all, w_ada2d, depth):
    nb, d = c_all.shape
    cols = w_ada2d.shape[1]
    tn = _tile(cols, 512, 128)

    def body(c_ref, w_ref, o_ref):
        cv = c_ref[...]
        o_ref[...] = _dot(cv * _sigmoid(cv), w_ref[...])

    return pl.pallas_call(
        body, name="mod_fwd", grid=(depth, cols // tn),
        in_specs=[pl.BlockSpec((nb, d), lambda l, j: (0, 0)), pl.BlockSpec((d, tn), lambda l, j: (l, j))],
        out_specs=pl.BlockSpec((nb, tn), lambda l, j: (l, j)),
        out_shape=jax.ShapeDtypeStruct((depth * nb, cols), F32),
        compiler_params=pltpu.CompilerParams(dimension_semantics=("parallel", "parallel")),
    )(c_all, w_ada2d)


def _wada_bwd(c_all, dmod2d, depth):
    nb, d = c_all.shape
    cols = dmod2d.shape[1]
    tn = _tile(cols, 512, 128)

    def body(c_ref, g_ref, o_ref):
        cv = c_ref[...]
        o_ref[...] = _dot_tn(cv * _sigmoid(cv), g_ref[...])

    return pl.pallas_call(
        body, name="wada_bwd", grid=(depth, cols // tn),
        in_specs=[pl.BlockSpec((nb, d), lambda l, j: (0, 0)), pl.BlockSpec((nb, tn), lambda l, j: (l, j))],
        out_specs=pl.BlockSpec((d, tn), lambda l, j: (l, j)),
        out_shape=jax.ShapeDtypeStruct((depth * d, cols), F32),
        compiler_params=pltpu.CompilerParams(dimension_semantics=("parallel", "parallel")),
    )(c_all, dmod2d)


def _rope_tables(pos_col, inv_freq_lane, nope):
    n_rows = pos_col.shape[0]
    tm = _tile(n_rows, 512)
    half = QK_ROPE // 2

    def body(p_ref, f_ref, c_ref, s1_ref, s2_ref):
        ang = p_ref[...] * f_ref[...]
        lane = lax.broadcasted_iota(jnp.int32, ang.shape, 1)
        first = (lane >= nope) & (lane < nope + half)
        second = (lane >= nope + half) & (lane < nope + 2 * half)
        cos, sin = jnp.cos(ang), jnp.sin(ang)
        c_ref[...] = jnp.where(first | second, cos, 1.0)
        s1_ref[...] = jnp.where(first, -sin, 0.0)
        s2_ref[...] = jnp.where(second, sin, 0.0)

    return _rowcall("rope_tables", body, n_rows, tm, [pos_col], [inv_freq_lane], [(HEAD_PAD, F32)] * 3)


def _rope(q, c, s1, s2):
    w = q.shape[1]
    return q * c + pltpu.roll(q, w - QK_ROPE // 2, axis=1) * s1 + pltpu.roll(q, QK_ROPE // 2, axis=1) * s2


def _rope_adjoint(dr, c, s1, s2):
    w = dr.shape[1]
    return dr * c + pltpu.roll(dr * s1, QK_ROPE // 2, axis=1) + pltpu.roll(dr * s2, w - QK_ROPE // 2, axis=1)


def _ssm_disc(ar, ai, log_dt, br, bi):
    dt = jnp.exp(log_dt)
    mag = jnp.exp(ar * dt)
    abr = mag * jnp.cos(ai * dt)
    abi = mag * jnp.sin(ai * dt)
    den = ar * ar + ai * ai
    nr = abr - 1.0
    ni = abi
    cr = (nr * ar + ni * ai) / den
    ci = (ni * ar - nr * ai) / den
    return abr, abi, cr * br - ci * bi, cr * bi + ci * br


def _ssm_disc_fwd(ar, ai, log_dt, br, bi):
    n_rows, m = br.shape
    tm = _tile(n_rows, 1024)

    def body(ar_ref, ai_ref, dt_ref, br_ref, bi_ref, o1, o2, o3, o4):
        o1[...], o2[...], o3[...], o4[...] = _ssm_disc(ar_ref[...], ai_ref[...], dt_ref[...], br_ref[...], bi_ref[...])

    return _rowcall("ssm_disc_fwd", body, n_rows, tm, [ar, ai, log_dt, br, bi], [],
                    [(1, F32), (1, F32), (m, F32), (m, F32)])


def _ssm_disc_bwd(ar, ai, log_dt, br, bi, g_abr, g_abi, g_bbr, g_bbi):
    n_rows, m = br.shape
    tm = _tile(n_rows, 1024)

    def body(ar_ref, ai_ref, dt_ref, br_ref, bi_ref, g1, g2, g3, g4, o1, o2, o3, o4, o5):
        _, vjp = jax.vjp(_ssm_disc, ar_ref[...], ai_ref[...], dt_ref[...], br_ref[...], bi_ref[...])
        o1[...], o2[...], o3[...], o4[...], o5[...] = vjp((g1[...], g2[...], g3[...], g4[...]))

    return _rowcall("ssm_disc_bwd", body, n_rows, tm, [ar, ai, log_dt, br, bi, g_abr, g_abi, g_bbr, g_bbi], [],
                    [(1, F32), (1, F32), (1, F32), (m, F32), (m, F32)])


def _lane_sum(v2d):
    def body(v_ref, o_ref):
        o_ref[...] = jnp.sum(v_ref[...], axis=1, keepdims=True)
    return pl.pallas_call(body, name="lane_sum", out_shape=jax.ShapeDtypeStruct((v2d.shape[0], 1), F32))(v2d)


def _in_fwd(x, n1g, sc1, sh1, w_parts, tm):
    n_rows = x.shape[0]
    widths = [w.shape[1] for w in w_parts]

    def body(x_ref, g_ref, sc_ref, sh_ref, *rest):
        w_refs, (hb_ref, *z_refs) = rest[:len(w_parts)], rest[len(w_parts):]
        xh, _ = _rms_stats(x_ref[...])
        h = (xh * g_ref[...]) * (1.0 + sc_ref[...]) + sh_ref[...]
        hb = h.astype(_MM)
        hb_ref[...] = hb.astype(_ACT)
        for w_ref, z_ref in zip(w_refs, z_refs):
            z_ref[...] = _dot(hb, w_ref[...])

    return _rowcall("in_fwd", body, n_rows, tm, [x], [n1g, sc1, sh1, *w_parts],
                    [(x.shape[1], _ACT)] + [(w, F32) for w in widths], temp_cols=4 * x.shape[1])


def _ssm_fwd(u, bre_blk, bim_blk, abr_row, abi_row, cre_blk, cimneg_blk, d_row):
    n_rows, sw = u.shape
    gp = abr_row.shape[1]
    t = _tile(n_rows, 256)

    def body(u_ref, bre_ref, bim_ref, ar_ref, ai_ref, cre_ref, cim_ref, d_ref, y_ref, hre_ref, him_ref, cr, ci):
        @pl.when(_first_step())
        def _():
            cr[...] = jnp.zeros_like(cr)
            ci[...] = jnp.zeros_like(ci)

        uv = u_ref[...]
        ub = uv.astype(_MM)
        hre_ref[...] = _dot(ub, bre_ref[...])
        him_ref[...] = _dot(ub, bim_ref[...])
        a_r, a_i = ar_ref[...], ai_ref[...]

        def step(k, carry):
            pr, pi = carry
            row = pl.ds(k, 1)
            hr = a_r * pr - a_i * pi + hre_ref[row, :]
            hi = a_r * pi + a_i * pr + him_ref[row, :]
            hre_ref[row, :] = hr
            him_ref[row, :] = hi
            return hr, hi

        pr, pi = lax.fori_loop(0, t, step, (cr[0:1, :], ci[0:1, :]), unroll=8)
        cr[0:1, :] = pr
        ci[0:1, :] = pi
        y_ref[...] = _dot(hre_ref[...], cre_ref[...]) + _dot(him_ref[...], cim_ref[...]) + d_ref[...] * uv

    row = lambda c: pl.BlockSpec((t, c), lambda i: (i, 0))
    full = lambda a: pl.BlockSpec(a.shape, lambda i: (0, 0), pipeline_mode=pl.Buffered(1))
    blocks = _nbytes((t, sw), F32) * 2 + 2 * _nbytes((t, gp), F32)
    resident = 4 * _nbytes((sw, gp), _MM)
    return pl.pallas_call(
        body, name="ssm_fwd", grid=(n_rows // t,),
        in_specs=[row(sw), full(bre_blk), full(bim_blk), full(abr_row), full(abi_row), full(cre_blk),
                  full(cimneg_blk), full(d_row)],
        out_specs=[row(sw), row(gp), row(gp)],
        out_shape=[jax.ShapeDtypeStruct((n_rows, sw), F32), jax.ShapeDtypeStruct((n_rows, gp), F32),
                   jax.ShapeDtypeStruct((n_rows, gp), F32)],
        scratch_shapes=[pltpu.VMEM((8, gp), F32), pltpu.VMEM((8, gp), F32)],
        compiler_params=pltpu.CompilerParams(dimension_semantics=("arbitrary",),
                                             vmem_limit_bytes=_vmem_limit(blocks, resident + 3 * _nbytes((t, gp), F32))),
    )(u, bre_blk, bim_blk, abr_row, abi_row, cre_blk, cimneg_blk, d_row)


def _mla_prep_fwd(cq, ckv, kr, rc, rs1, rs2, gq, gkv, wuq, wuk, wuv, vdim, tm):
    n_rows = cq.shape[0]

    def body(cq_ref, ckv_ref, kr_ref, c_ref, s1_ref, s2_ref, gq_ref, gkv_ref, wuq_ref, wuk_ref, wuv_ref,
             q_ref, k_ref, v_ref, cqn_ref, ckvn_ref):
        c, s1, s2 = c_ref[...], s1_ref[...], s2_ref[...]
        c8, s18, s28 = (jnp.tile(a, (1, N_HEADS)) for a in (c, s1, s2))
        xh, _ = _rms_stats(cq_ref[...])
        cqn = (xh * gq_ref[...]).astype(_MM)
        cqn_ref[...] = cqn.astype(_ACT)
        q_ref[...] = _rope(_dot(cqn, wuq_ref[...]), c8, s18, s28).astype(_ACT)
        xh, _ = _rms_stats(ckv_ref[...])
        ckvn = (xh * gkv_ref[...]).astype(_MM)
        ckvn_ref[...] = ckvn.astype(_ACT)
        kpe = _rope(kr_ref[...], c, s1, s2)
        k_ref[...] = (_dot(ckvn, wuk_ref[...]) + jnp.tile(kpe, (1, N_HEADS))).astype(_ACT)
        v = _dot(ckvn, wuv_ref[...])
        lane = lax.broadcasted_iota(jnp.int32, v.shape, 1)
        v_ref[...] = jnp.where((lane & (HEAD_PAD - 1)) == vdim, 1.0, v).astype(_ACT)

    return _rowcall("mla_prep_fwd", body, n_rows, tm, [cq, ckv, kr, rc, rs1, rs2], [gq, gkv, wuq, wuk, wuv],
                    [(QW, _ACT), (QW, _ACT), (QW, _ACT), (cq.shape[1], _ACT), (ckv.shape[1], _ACT)], temp_cols=6 * QW)


def _causal_steps(n_blocks, key_major):
    if key_major:
        pairs = [(qi, ki) for ki in range(n_blocks) for qi in range(ki, n_blocks)]
    else:
        pairs = [(qi, ki) for qi in range(n_blocks) for ki in range(qi + 1)]
    return (jnp.asarray(np.array([p[0] for p in pairs], np.int32)), jnp.asarray(np.array([p[1] for p in pairs], np.int32)))


def _attn_fwd(q, k, v, pos_col, pos_row, scale, vdim):
    n_rows = q.shape[0]
    ta = _tile(n_rows, 512, 128)
    nb = n_rows // ta
    hb = ATTN_HEADS_PER_STEP
    wide = hb * HEAD_PAD
    qmap, kmap = _causal_steps(nb, key_major=False)
    c2 = scale * LOG2_E

    def body(qm, km, q_ref, k_ref, v_ref, pq_ref, pk_ref, o_ref, lse_ref, m_sc, acc_sc):
        s_id = pl.program_id(1)
        qi, ki = qm[s_id], km[s_id]

        @pl.when(ki == 0)
        def _():
            m_sc[...] = jnp.full_like(m_sc, NEG_INF)
            acc_sc[...] = jnp.zeros_like(acc_sc)

        def update(on_diagonal):
            if on_diagonal:
                visible = pk_ref[...] <= pq_ref[...]
            for h in range(hb):
                lanes = slice(h * HEAD_PAD, (h + 1) * HEAD_PAD)
                s = _dot_nt(q_ref[:, lanes], k_ref[:, lanes])
                if on_diagonal:
                    s = jnp.where(visible, s, NEG_INF)
                m_prev = m_sc[:, lanes]
                m_new = jnp.maximum(m_prev, jnp.max(s, axis=1, keepdims=True))
                alpha = jnp.exp2((m_prev - m_new) * c2)
                p = jnp.exp2((s - m_new[:, :1]) * c2)
                acc_new = alpha * acc_sc[:, lanes] + _dot(p, v_ref[:, lanes])
                if on_diagonal:
                    l_new = acc_new[:, vdim:vdim + 1]
                    o_ref[:, lanes] = acc_new / l_new
                    lse_ref[:, lanes] = m_new * c2 + jnp.log2(l_new)
                else:
                    acc_sc[:, lanes] = acc_new
                    m_sc[:, lanes] = m_new

        pl.when(ki != qi)(functools.partial(update, False))
        pl.when(ki == qi)(functools.partial(update, True))

    qspec = pl.BlockSpec((ta, wide), lambda h, s, qm, km: (qm[s], h))
    kspec = pl.BlockSpec((ta, wide), lambda h, s, qm, km: (km[s], h))
    grid_spec = pltpu.PrefetchScalarGridSpec(
        num_scalar_prefetch=2, grid=(N_HEADS // hb, int(qmap.shape[0])),
        in_specs=[qspec, kspec, kspec,
                  pl.BlockSpec((ta, 1), lambda h, s, qm, km: (qm[s], 0)),
                  pl.BlockSpec((1, ta), lambda h, s, qm, km: (0, km[s]))],
        out_specs=[qspec, qspec],
        scratch_shapes=[pltpu.VMEM((ta, wide), F32)] * 2)
    return pl.pallas_call(
        body, name="attn_fwd", grid_spec=grid_spec,
        out_shape=[jax.ShapeDtypeStruct((n_rows, QW), F32), jax.ShapeDtypeStruct((n_rows, QW), F32)],
        compiler_params=pltpu.CompilerParams(
            dimension_semantics=("parallel", "arbitrary"),
            vmem_limit_bytes=_vmem_limit(8 * _nbytes((ta, wide), F32), 6 * hb * _nbytes((ta, ta), F32))),
    )(qmap, kmap, q, k, v, pos_col, pos_row)


def _mix_fwd(ypre, o, ga, gb, x, g1, bglu, wglu, wa, wb, wout, tm):
    n_rows, d = x.shape
    sw = ypre.shape[1]

    def body(y_ref, o_ref, ga_ref, gb_ref, x_ref, g1_ref, bglu_ref, wglu_ref, wa_ref, wb_ref, wout_ref,
             yg_ref, ya_ref, yb_ref, mg_ref, mo_ref, xo_ref):
        ys = _gelu(y_ref[...])
        yg = ys * _sigmoid(_dot(ys, wglu_ref[...]) + bglu_ref[...])
        yg_ref[...] = yg.astype(_ACT)
        ya = _dot(yg, wa_ref[...])
        yb = _dot(o_ref[...], wb_ref[...])
        ya_ref[...] = ya
        yb_ref[...] = yb
        merged = _sigmoid(ga_ref[...]) * ya + _sigmoid(gb_ref[...]) * yb
        mg_ref[...] = merged.astype(_ACT)
        mo = _dot(merged, wout_ref[...])
        mo_ref[...] = mo
        xo_ref[...] = x_ref[...] + g1_ref[...] * mo

    return _rowcall("mix_fwd", body, n_rows, tm, [ypre, o, ga, gb, x], [g1, bglu, wglu, wa, wb, wout],
                    [(sw, _ACT), (d, F32), (d, F32), (d, _ACT), (d, F32), (d, F32)], temp_cols=4 * d)


def _ffn_fwd(x, n2g, sc2, sh2, g2, wg, wu, wd, tm):
    n_rows, d = x.shape
    ff = wg.shape[1]

    def body(x_ref, g_ref, sc_ref, sh_ref, g2_ref, wg_ref, wu_ref, wd_ref, hb_ref, a_ref, b_ref, d_ref, xo_ref):
        xv = x_ref[...]
        xh, _ = _rms_stats(xv)
        hb = ((xh * g_ref[...]) * (1.0 + sc_ref[...]) + sh_ref[...]).astype(_MM)
        hb_ref[...] = hb.astype(_ACT)
        a = _dot(hb, wg_ref[...])
        b = _dot(hb, wu_ref[...])
        a_ref[...] = a
        b_ref[...] = b
        dn = _dot((a * _sigmoid(a)) * b, wd_ref[...])
        d_ref[...] = dn
        xo_ref[...] = xv + g2_ref[...] * dn

    return _rowcall("ffn_fwd", body, n_rows, tm, [x], [n2g, sc2, sh2, g2, wg, wu, wd],
                    [(d, _ACT), (ff, F32), (ff, F32), (d, F32), (d, F32)], temp_cols=3 * ff)


def _head(x, fg, target, tm):
    n_rows, d = x.shape

    def body(x_ref, t_ref, g_ref, dx_ref, loss_ref, dg_ref):
        xh, rstd = _rms_stats(x_ref[...])
        err = xh * g_ref[...] - t_ref[...]
        part = jnp.sum(jnp.mean(err * err, axis=-1, keepdims=True), axis=0, keepdims=True) * 0.5
        _acc(loss_ref, jnp.broadcast_to(part, loss_ref.shape))
        dy = err * (1.0 / d)
        _acc(dg_ref, _rows_sum(dy * xh))
        dx_ref[...] = _rms_bwd(dy * g_ref[...], xh, rstd)

    return _rowcall("head", body, n_rows, tm, [x, target], [fg], [(d, F32)], [((1, 128), F32), ((1, d), F32)],
                    temp_cols=4 * d)


def _ffn_bwd(dxo, xmid, a, b, dn, n2g, sc2, g2, wg, wu, wd, tm):
    n_rows, d = dxo.shape
    ff = a.shape[1]

    def act_body(dxo_ref, a_ref, b_ref, dn_ref, g2_ref, wd_ref, da_ref, db_ref, f_ref, dd_ref, dg2_ref):
        dxo_v = dxo_ref[...]
        dd = dxo_v * g2_ref[...]
        dd_ref[...] = dd.astype(_ACT)
        _acc(dg2_ref, _rows_sum(dxo_v * dn_ref[...]))
        df = _dot_nt(dd, wd_ref[...])
        av, bv = a_ref[...], b_ref[...]
        sa = _sigmoid(av)
        si = av * sa
        f_ref[...] = (si * bv).astype(_ACT)
        da_ref[...] = (df * bv * (sa * (1.0 + av * (1.0 - sa)))).astype(_ACT)
        db_ref[...] = (df * si).astype(_ACT)

    da, db, f, dd, dg2 = _rowcall("ffn_bwd_act", act_body, n_rows, 2 * tm, [dxo, a, b, dn], [g2, wd],
                                  [(ff, _ACT), (ff, _ACT), (ff, _ACT), (d, _ACT)], [((1, d), F32)], temp_cols=4 * ff)

    def in_body(dxo_ref, x_ref, da_ref, db_ref, g_ref, sc_ref, wg_ref, wu_ref, dx_ref, dsh_ref, dsc_ref, dn2_ref):
        dh = _dot_nt(da_ref[...], wg_ref[...]) + _dot_nt(db_ref[...], wu_ref[...])
        xh, rstd = _rms_stats(x_ref[...])
        yg = xh * g_ref[...]
        _acc(dsh_ref, _rows_sum(dh))
        _acc(dsc_ref, _rows_sum(dh * yg))
        dy = dh * (1.0 + sc_ref[...])
        _acc(dn2_ref, _rows_sum(dy * xh))
        dx_ref[...] = dxo_ref[...] + _rms_bwd(dy * g_ref[...], xh, rstd)

    dx, dsh, dsc, dn2 = _rowcall("ffn_bwd_in", in_body, n_rows, 2 * tm, [dxo, xmid, da, db], [n2g, sc2, wg, wu],
                                 [(d, F32)], [((1, d), F32)] * 3, temp_cols=5 * d)
    return dx, da, db, f, dd, dg2, dsh, dsc, dn2


def _mix_bwd(dxm, mo, ya, yb, ga, gb, ypre, o, g1, bglu, wglu, wa, wb, wout, tm):
    n_rows, d = dxm.shape
    sw = ypre.shape[1]

    def body(dxm_ref, mo_ref, ya_ref, yb_ref, ga_ref, gb_ref, y_ref, o_ref, g1_ref, bglu_ref, wglu_ref, wa_ref, wb_ref,
             wout_ref, dmo_ref, dya_ref, dyb_ref, dt_ref, ys_ref, dga_ref, dgb_ref, dy_ref, do_ref, delta_ref,
             dg1_ref, dbg_ref):
        dxm_v = dxm_ref[...]
        dmo = dxm_v * g1_ref[...]
        dmo_ref[...] = dmo.astype(_ACT)
        _acc(dg1_ref, _rows_sum(dxm_v * mo_ref[...]))
        dmg = _dot_nt(dmo, wout_ref[...])
        sa, sb = _sigmoid(ga_ref[...]), _sigmoid(gb_ref[...])
        dya, dyb = dmg * sa, dmg * sb
        dya_ref[...] = dya.astype(_ACT)
        dyb_ref[...] = dyb.astype(_ACT)
        dga_ref[...] = (dmg * ya_ref[...] * (sa * (1.0 - sa))).astype(_ACT)
        dgb_ref[...] = (dmg * yb_ref[...] * (sb * (1.0 - sb))).astype(_ACT)
        do = _dot_nt(dyb, wb_ref[...])
        do_ref[...] = do
        prod = do * o_ref[...]
        for h in range(N_HEADS):
            lanes = slice(h * HEAD_PAD, (h + 1) * HEAD_PAD)
            delta_ref[:, lanes] = jnp.broadcast_to(jnp.sum(prod[:, lanes], axis=1, keepdims=True), (prod.shape[0], HEAD_PAD))
        dyg = _dot_nt(dya, wa_ref[...])
        yv = y_ref[...]
        ys = _gelu(yv)
        ys_ref[...] = ys.astype(_ACT)
        sg = _sigmoid(_dot(ys, wglu_ref[...]) + bglu_ref[...])
        dt = dyg * ys * (sg * (1.0 - sg))
        dt_ref[...] = dt.astype(_ACT)
        _acc(dbg_ref, _rows_sum(dt))
        dys = dyg * sg + _dot_nt(dt, wglu_ref[...])
        dy_ref[...] = dys * _gelu_grad(yv)

    return _rowcall("mix_bwd", body, n_rows, tm, [dxm, mo, ya, yb, ga, gb, ypre, o], [g1, bglu, wglu, wa, wb, wout],
                    [(d, _ACT), (d, _ACT), (d, _ACT), (sw, _ACT), (sw, _ACT), (d, _ACT), (d, _ACT), (sw, F32), (QW, F32),
                     (QW, F32)],
                    [((1, d), F32), ((1, sw), F32)], temp_cols=6 * d)


def _attn_bwd(q, k, v, do, delta, lse, pos_col, pos_row, scale):
    n_rows = q.shape[0]
    ta = _tile(n_rows, 512, 128)
    nb = n_rows // ta
    qmap, kmap = _causal_steps(nb, key_major=True)
    hb = ATTN_BWD_HEADS_PER_STEP
    wide = hb * HEAD_PAD
    c2 = scale * LOG2_E

    def body(qm, km, q_ref, k_ref, v_ref, do_ref, delta_ref, lse_ref, pq_ref, pk_ref, dq_ref, dk_ref, dv_ref,
             dk_acc, dv_acc):
        s_id = pl.program_id(1)
        qi, ki = qm[s_id], km[s_id]

        @pl.when(s_id == 0)
        def _():
            dq_ref[...] = jnp.zeros_like(dq_ref)

        @pl.when(qi == ki)
        def _():
            dk_acc[...] = jnp.zeros_like(dk_acc)
            dv_acc[...] = jnp.zeros_like(dv_acc)

        rows = pl.ds(pl.multiple_of(qi * ta, ta), ta)

        def update(on_diagonal):
            if on_diagonal:
                visible = pk_ref[...] <= pq_ref[...]
            for h in range(hb):
                lanes = slice(h * HEAD_PAD, (h + 1) * HEAD_PAD)
                qv, kv, dov = q_ref[:, lanes], k_ref[:, lanes], do_ref[:, lanes]
                e = _dot_nt(qv, kv) * c2 - lse_ref[:, lanes][:, :1]
                if on_diagonal:
                    e = jnp.where(visible, e, NEG_INF)
                p = jnp.exp2(e)
                dp = _dot_nt(dov, v_ref[:, lanes])
                ds = p * (dp - delta_ref[:, lanes][:, :1])
                dv_acc[:, lanes] += _dot_tn(p, dov)
                dk_acc[:, lanes] += _dot_tn(ds, qv)
                dq_ref[rows, lanes] += _dot(ds, kv) * scale

        pl.when(ki != qi)(functools.partial(update, False))
        pl.when(ki == qi)(functools.partial(update, True))

        @pl.when(qi == nb - 1)
        def _():
            dk_ref[...] = dk_acc[...] * scale
            dv_ref[...] = dv_acc[...]

    qspec = pl.BlockSpec((ta, wide), lambda h, s, qm, km: (qm[s], h))
    kspec = pl.BlockSpec((ta, wide), lambda h, s, qm, km: (km[s], h))
    grid_spec = pltpu.PrefetchScalarGridSpec(
        num_scalar_prefetch=2, grid=(N_HEADS // hb, int(qmap.shape[0])),
        in_specs=[qspec, kspec, kspec, qspec, qspec, qspec,
                  pl.BlockSpec((ta, 1), lambda h, s, qm, km: (qm[s], 0)),
                  pl.BlockSpec((1, ta), lambda h, s, qm, km: (0, km[s]))],
        out_specs=[pl.BlockSpec((n_rows, wide), lambda h, s, qm, km: (0, h)), kspec, kspec],
        scratch_shapes=[pltpu.VMEM((ta, wide), F32), pltpu.VMEM((ta, wide), F32)])
    return pl.pallas_call(
        body, name="attn_bwd", grid_spec=grid_spec,
        out_shape=[jax.ShapeDtypeStruct((n_rows, QW), F32)] * 3,
        compiler_params=pltpu.CompilerParams(
            dimension_semantics=("parallel", "arbitrary"),
            vmem_limit_bytes=_vmem_limit(12 * _nbytes((ta, wide), F32) + _nbytes((n_rows, wide), F32),
                                         6 * hb * _nbytes((ta, ta), F32))),
    )(qmap, kmap, q, k, v, do, delta, lse, pos_col, pos_row)


def _mla_prep_bwd(dq, dk, dv, cq, ckv, rc, rs1, rs2, gq, gkv, wuq, wuk, wuv, nope, tm):
    n_rows = cq.shape[0]
    ql, kl = cq.shape[1], ckv.shape[1]

    def body(dq_ref, dk_ref, dv_ref, cq_ref, ckv_ref, c_ref, s1_ref, s2_ref, gq_ref, gkv_ref, wuq_ref, wuk_ref,
             wuv_ref, dqp_ref, dcq_ref, dckv_ref, dkr_ref, dgq_ref, dgkv_ref):
        c, s1, s2 = c_ref[...], s1_ref[...], s2_ref[...]
        c8, s18, s28 = (jnp.tile(a, (1, N_HEADS)) for a in (c, s1, s2))
        dqp = _rope_adjoint(dq_ref[...], c8, s18, s28)
        dqp_ref[...] = dqp.astype(_ACT)
        dcqn = _dot_nt(dqp, wuq_ref[...])
        xh, rstd = _rms_stats(cq_ref[...])
        _acc(dgq_ref, _rows_sum(dcqn * xh))
        dcq_ref[...] = _rms_bwd(dcqn * gq_ref[...], xh, rstd).astype(_ACT)
        dkv = dk_ref[...]
        dkpe = dkv[:, 0:HEAD_PAD]
        for h in range(1, N_HEADS):
            dkpe = dkpe + dkv[:, h * HEAD_PAD:(h + 1) * HEAD_PAD]
        lane = lax.broadcasted_iota(jnp.int32, dkpe.shape, 1)
        dkpe = jnp.where((lane >= nope) & (lane < nope + QK_ROPE), dkpe, 0.0)
        dkr_ref[...] = _rope_adjoint(dkpe, c, s1, s2).astype(_ACT)
        dckvn = _dot_nt(dkv, wuk_ref[...]) + _dot_nt(dv_ref[...], wuv_ref[...])
        xh, rstd = _rms_stats(ckv_ref[...])
        _acc(dgkv_ref, _rows_sum(dckvn * xh))
        dckv_ref[...] = _rms_bwd(dckvn * gkv_ref[...], xh, rstd).astype(_ACT)

    return _rowcall("mla_prep_bwd", body, n_rows, tm, [dq, dk, dv, cq, ckv, rc, rs1, rs2], [gq, gkv, wuq, wuk, wuv],
                    [(QW, _ACT), (ql, _ACT), (kl, _ACT), (HEAD_PAD, _ACT)], [((1, ql), F32), ((1, kl), F32)],
                    temp_cols=6 * QW)


def _ssm_bwd(dy, u, hre, him, bre_blk, bim_blk, abr_row, abi_row, cre_blk, cimneg_blk, d_row):
    n_rows, sw = u.shape
    gp = abr_row.shape[1]
    t = _tile(n_rows, 256)
    n_chunks = n_rows // t

    def body(dy_ref, u_ref, hre_ref, him_ref, hbre_ref, hbim_ref, bre_ref, bim_ref, ar_ref, ai_ref, cre_ref, cim_ref,
             d_ref, du_ref, gre_ref, gim_ref, dar_ref, dai_ref, dd_ref, g_re, g_im, hs_re, hs_im, cr, ci):
        i = pl.program_id(0)

        @pl.when(i == 0)
        def _():
            cr[...] = jnp.zeros_like(cr)
            ci[...] = jnp.zeros_like(ci)

        dyv = dy_ref[...]
        dyb = dyv.astype(_MM)
        g_re[...] = _dot_nt(dyb, cre_ref[...])
        g_im[...] = _dot_nt(dyb, cim_ref[...])
        a_r, a_i = ar_ref[...], ai_ref[...]

        def step(k, carry):
            nr, ni = carry
            row = pl.ds(t - 1 - k, 1)
            gr = g_re[row, :] + a_r * nr + a_i * ni
            gi = g_im[row, :] + a_r * ni - a_i * nr
            g_re[row, :] = gr
            g_im[row, :] = gi
            return gr, gi

        nr, ni = lax.fori_loop(0, t, step, (cr[0:1, :], ci[0:1, :]), unroll=8)
        cr[0:1, :] = nr
        ci[0:1, :] = ni
        gr_all, gi_all = g_re[...], g_im[...]
        gre_ref[...] = gr_all.astype(_ACT)
        gim_ref[...] = gi_all.astype(_ACT)
        du_ref[...] = (_dot_nt(gr_all, bre_ref[...]) + _dot_nt(gi_all, bim_ref[...]) + d_ref[...] * dyv).astype(_ACT)
        _acc(dd_ref, _rows_sum(dyv * u_ref[...]))
        is_first_chunk = i == n_chunks - 1
        hs_re[0:8, :] = jnp.where(is_first_chunk, 0.0, hbre_ref[...])
        hs_im[0:8, :] = jnp.where(is_first_chunk, 0.0, hbim_ref[...])
        hs_re[8:t + 8, :] = hre_ref[...]
        hs_im[8:t + 8, :] = him_ref[...]
        hp_re, hp_im = hs_re[pl.ds(7, t), :], hs_im[pl.ds(7, t), :]
        _acc(dar_ref, _rows_sum(gr_all * hp_re + gi_all * hp_im))
        _acc(dai_ref, _rows_sum(gi_all * hp_re - gr_all * hp_im))

    rev = lambda c: pl.BlockSpec((t, c), lambda i: (n_chunks - 1 - i, 0))
    before = pl.BlockSpec((8, gp), lambda i: (jnp.maximum((n_chunks - 1 - i) * (t // 8) - 1, 0), 0))
    full = lambda a: pl.BlockSpec(a.shape, lambda i: (0, 0), pipeline_mode=pl.Buffered(1))
    acc = lambda c: pl.BlockSpec((1, c), lambda i: (0, 0))
    blocks = 2 * _nbytes((t, sw), F32) + 2 * _nbytes((t, gp), F32) + _nbytes((t, sw), _ACT) + 2 * _nbytes((t, gp), _ACT)
    resident = 4 * _nbytes((sw, gp), _MM) + 4 * _nbytes((t + 8, gp), F32)
    return pl.pallas_call(
        body, name="ssm_bwd", grid=(n_chunks,),
        in_specs=[rev(sw), rev(sw), rev(gp), rev(gp), before, before, full(bre_blk), full(bim_blk), full(abr_row),
                  full(abi_row), full(cre_blk), full(cimneg_blk), full(d_row)],
        out_specs=[rev(sw), rev(gp), rev(gp), acc(gp), acc(gp), acc(sw)],
        out_shape=[jax.ShapeDtypeStruct((n_rows, sw), _ACT), jax.ShapeDtypeStruct((n_rows, gp), _ACT),
                   jax.ShapeDtypeStruct((n_rows, gp), _ACT), jax.ShapeDtypeStruct((1, gp), F32),
                   jax.ShapeDtypeStruct((1, gp), F32), jax.ShapeDtypeStruct((1, sw), F32)],
        scratch_shapes=[pltpu.VMEM((t, gp), F32), pltpu.VMEM((t, gp), F32), pltpu.VMEM((t + 8, gp), F32),
                        pltpu.VMEM((t + 8, gp), F32), pltpu.VMEM((8, gp), F32), pltpu.VMEM((8, gp), F32)],
        compiler_params=pltpu.CompilerParams(dimension_semantics=("arbitrary",),
                                             vmem_limit_bytes=_vmem_limit(blocks, resident + 4 * _nbytes((t, gp), F32))),
    )(dy, u, hre, him, hre, him, bre_blk, bim_blk, abr_row, abi_row, cre_blk, cimneg_blk, d_row)


def _in_bwd(dxm, x, dz_parts, n1g, sc1, w_parts, tm):
    n_rows, d = x.shape
    n = len(dz_parts)

    def body(dxm_ref, x_ref, *rest):
        dz_refs = rest[:n]
        g_ref, sc_ref = rest[n], rest[n + 1]
        w_refs = rest[n + 2:2 * n + 2]
        dx_ref, dsh_ref, dsc_ref, dn1_ref = rest[2 * n + 2:]
        dh = None
        for dz_ref, w_ref in zip(dz_refs, w_refs):
            term = _dot_nt(dz_ref[...], w_ref[...])
            dh = term if dh is None else dh + term
        xh, rstd = _rms_stats(x_ref[...])
        yg = xh * g_ref[...]
        _acc(dsh_ref, _rows_sum(dh))
        _acc(dsc_ref, _rows_sum(dh * yg))
        dy = dh * (1.0 + sc_ref[...])
        _acc(dn1_ref, _rows_sum(dy * xh))
        dx_ref[...] = dxm_ref[...] + _rms_bwd(dy * g_ref[...], xh, rstd)

    return _rowcall("in_bwd", body, n_rows, tm, [dxm, x, *dz_parts], [n1g, sc1, *w_parts],
                    [(d, F32)], [((1, d), F32)] * 3, temp_cols=5 * d)


def _pad_heads(w, per_head):
    lead = w.shape[:-1]
    w = w.reshape(lead + (N_HEADS, per_head))
    w = jnp.pad(w, [(0, 0)] * len(lead) + [(0, 0), (0, HEAD_PAD - per_head)])
    return w.reshape(lead + (QW,))


def _unpad_heads(w, per_head):
    lead = w.shape[:-1]
    return w.reshape(lead + (N_HEADS, HEAD_PAD))[..., :per_head].reshape(lead + (N_HEADS * per_head,))


def _cols_from_chips(g):
    ch, dep, r, cs = g.shape
    return g.transpose(1, 2, 0, 3).reshape(dep, r, ch * cs)


def _rows_from_chips(g):
    ch, dep, rs, c = g.shape
    return g.transpose(1, 0, 2, 3).reshape(dep, ch * rs, c)


def _cols_to_chips(w):
    dep, r, c = w.shape
    return w.reshape(dep, r, 4, c // 4).transpose(2, 0, 1, 3)


def _rows_to_chips(w):
    dep, r, c = w.shape
    return w.reshape(dep, 4, r // 4, c).transpose(1, 0, 2, 3)


def _block_diag(b_gxy):
    g, xx, yy = b_gxy.shape
    eye = jnp.eye(g, dtype=b_gxy.dtype)
    return (b_gxy[:, :, None, :] * eye[:, None, :, None]).reshape(g * xx, g * yy)


def _block_diag_extract(full, g):
    xx, yy = full.shape[0] // g, full.shape[1] // g
    eye = jnp.eye(g, dtype=full.dtype)
    return jnp.sum(full.reshape(g, xx, g, yy) * eye[:, None, :, None], axis=2)


def _pack_rows(arrays):
    parts = []
    for a in arrays:
        flat = a.reshape(-1)
        flat = jnp.pad(flat, (0, (-flat.shape[0]) % 1024))
        parts.append(flat.reshape(-1, 128))
    return jnp.concatenate(parts, axis=0)


def _unpack_rows(packed, shapes):
    out, row = [], 0
    for s in shapes:
        n = int(np.prod(s))
        rows = -(-n // 1024) * 8
        out.append(packed[row:row + rows].reshape(-1)[:n].reshape(s))
        row += rows
    return out


def kernel(x, c, positions, w_ada, b_ada, norm1_g, w_in, ssm_a_re, ssm_a_im, ssm_log_dt, ssm_b_re, ssm_b_im, ssm_c_re, ssm_c_im, ssm_d, w_glu, b_glu, w_a_out, q_norm_g, w_uq, kv_norm_g, w_uk, w_uv, w_b_out, w_out, norm2_g, w_gate, w_up, w_down, final_g, loss_target, m_w_ada, m_b_ada, m_norm1_g, m_w_in, m_ssm_a_re, m_ssm_a_im, m_ssm_log_dt, m_ssm_b_re, m_ssm_b_im, m_ssm_c_re, m_ssm_c_im, m_ssm_d, m_w_glu, m_b_glu, m_w_a_out, m_q_norm_g, m_w_uq, m_kv_norm_g, m_w_uk, m_w_uv, m_w_b_out, m_w_out, m_norm2_g, m_w_gate, m_w_up, m_w_down, m_final_g, v_w_ada, v_b_ada, v_norm1_g, v_w_in, v_ssm_a_re, v_ssm_a_im, v_ssm_log_dt, v_ssm_b_re, v_ssm_b_im, v_ssm_c_re, v_ssm_c_im, v_ssm_d, v_w_glu, v_b_glu, v_w_a_out, v_q_norm_g, v_w_uq, v_kv_norm_g, v_w_uk, v_w_uv, v_w_b_out, v_w_out, v_norm2_g, v_w_gate, v_w_up, v_w_down, v_final_g):
    weights = dict(w_ada=w_ada, b_ada=b_ada, norm1_g=norm1_g, w_in=w_in, ssm_a_re=ssm_a_re, ssm_a_im=ssm_a_im, ssm_log_dt=ssm_log_dt, ssm_b_re=ssm_b_re, ssm_b_im=ssm_b_im, ssm_c_re=ssm_c_re, ssm_c_im=ssm_c_im, ssm_d=ssm_d, w_glu=w_glu, b_glu=b_glu, w_a_out=w_a_out, q_norm_g=q_norm_g, w_uq=w_uq, kv_norm_g=kv_norm_g, w_uk=w_uk, w_uv=w_uv, w_b_out=w_b_out, w_out=w_out, norm2_g=norm2_g, w_gate=w_gate, w_up=w_up, w_down=w_down, final_g=final_g)
    mom_m = dict(w_ada=m_w_ada, b_ada=m_b_ada, norm1_g=m_norm1_g, w_in=m_w_in, ssm_a_re=m_ssm_a_re, ssm_a_im=m_ssm_a_im, ssm_log_dt=m_ssm_log_dt, ssm_b_re=m_ssm_b_re, ssm_b_im=m_ssm_b_im, ssm_c_re=m_ssm_c_re, ssm_c_im=m_ssm_c_im, ssm_d=m_ssm_d, w_glu=m_w_glu, b_glu=m_b_glu, w_a_out=m_w_a_out, q_norm_g=m_q_norm_g, w_uq=m_w_uq, kv_norm_g=m_kv_norm_g, w_uk=m_w_uk, w_uv=m_w_uv, w_b_out=m_w_b_out, w_out=m_w_out, norm2_g=m_norm2_g, w_gate=m_w_gate, w_up=m_w_up, w_down=m_w_down, final_g=m_final_g)
    mom_v = dict(w_ada=v_w_ada, b_ada=v_b_ada, norm1_g=v_norm1_g, w_in=v_w_in, ssm_a_re=v_ssm_a_re, ssm_a_im=v_ssm_a_im, ssm_log_dt=v_ssm_log_dt, ssm_b_re=v_ssm_b_re, ssm_b_im=v_ssm_b_im, ssm_c_re=v_ssm_c_re, ssm_c_im=v_ssm_c_im, ssm_d=v_ssm_d, w_glu=v_w_glu, b_glu=v_b_glu, w_a_out=v_w_a_out, q_norm_g=v_q_norm_g, w_uq=v_w_uq, kv_norm_g=v_kv_norm_g, w_uk=v_w_uk, w_uv=v_w_uv, w_b_out=v_w_b_out, w_out=v_w_out, norm2_g=v_norm2_g, w_gate=v_w_gate, w_up=v_w_up, w_down=v_w_down, final_g=v_final_g)
    names = list(weights)

    depth = w_in.shape[0]
    seq, d = x.shape[1], x.shape[2]
    sw = ssm_d.shape[1]
    groups, n_state, n_chan = ssm_b_re.shape[1:]
    gp = groups * n_state
    ql, kl = q_norm_g.shape[1], kv_norm_g.shape[1]
    nope = w_uk.shape[2] * 4 // N_HEADS
    vdim = w_uv.shape[2] * 4 // N_HEADS
    qk_dim = nope + QK_ROPE
    scale = qk_dim ** -0.5
    tm = _tile(seq, 256, 16)
    tm_ffn = _tile(seq, 128, 16)
    me = 4 * lax.axis_index("x") + 2 * lax.axis_index("y") + lax.axis_index("c")
    chip = 2 * lax.axis_index("x") + lax.axis_index("y")

    xs = x.reshape(seq, d)
    target = loss_target.reshape(seq, d)
    pos_f = positions.astype(F32)
    pos_col = pos_f.reshape(seq, 1)
    pos_row = pos_f.reshape(1, seq)

    (c_all,) = _exchange("gather_c", "gather8", [c])
    c_all = c_all.reshape(8, d)
    ada_cols = w_ada.shape[2]
    mod_part = _mod_fwd(c_all, w_ada.reshape(depth * d, ada_cols), depth)
    (mod_all,) = _exchange("gather_mod", "gather8", [mod_part])
    mod_all = mod_all.reshape(4, 2, depth, 8, ada_cols)[:, 0]
    mod_me = lax.dynamic_index_in_dim(mod_all, me, axis=2, keepdims=False)
    mod = mod_me.transpose(1, 0, 2).reshape(depth, 4 * ada_cols) + b_ada
    mod = mod.reshape(depth, 6, 1, d)

    big = ["w_in", "w_glu", "w_a_out", "w_uq", "w_uk", "w_uv", "w_b_out", "w_out", "w_gate", "w_up", "w_down"]
    row_sharded = {"w_glu", "w_out", "w_down"}
    gathered = _exchange("gather_weights", "gather4", [weights[n].astype(_MM) for n in big])
    full = {n: (_rows_from_chips(g) if n in row_sharded else _cols_from_chips(g)) for n, g in zip(big, gathered)}
    o1, o2, o3, o4, o5 = sw, sw + ql, sw + ql + kl, sw + ql + kl + QK_ROPE, sw + ql + kl + QK_ROPE + d
    wi = full["w_in"]
    w_u, w_cq, w_ckv, w_ga, w_gb = wi[:, :, :o1], wi[:, :, o1:o2], wi[:, :, o2:o3], wi[:, :, o4:o5], wi[:, :, o5:]
    w_kr = jnp.pad(wi[:, :, o3:o4], ((0, 0), (0, 0), (nope, HEAD_PAD - nope - QK_ROPE)))
    wuq_p = _pad_heads(full["w_uq"], qk_dim)
    wuk_p = _pad_heads(full["w_uk"], nope)
    wuv_p = _pad_heads(full["w_uv"], vdim)
    wb_p = _pad_heads(full["w_b_out"].transpose(0, 2, 1), vdim).transpose(0, 2, 1)

    inv_freq = ROPE_BASE ** (-jnp.arange(0, QK_ROPE, 2, dtype=F32) / QK_ROPE)
    inv_lane = jnp.pad(jnp.concatenate([inv_freq, inv_freq]), (nope, HEAD_PAD - nope - QK_ROPE)).reshape(1, HEAD_PAD)
    rc, rs1, rs2 = _rope_tables(pos_col, inv_lane, nope)
    a_re_col = ssm_a_re.reshape(depth * gp, 1)
    a_im_col = ssm_a_im.reshape(depth * gp, 1)
    ldt_col = jnp.broadcast_to(ssm_log_dt[:, :, None], (depth, groups, n_state)).reshape(depth * gp, 1)
    b_re2, b_im2 = ssm_b_re.reshape(depth * gp, n_chan), ssm_b_im.reshape(depth * gp, n_chan)
    abr, abi, bbr, bbi = _ssm_disc_fwd(a_re_col, a_im_col, ldt_col, b_re2, b_im2)
    abr_rows, abi_rows = abr.reshape(depth, 1, gp), abi.reshape(depth, 1, gp)
    bbr, bbi = bbr.reshape(depth, groups, n_state, n_chan), bbi.reshape(depth, groups, n_state, n_chan)

    saved = []
    xl = xs
    for l in range(depth):
        sh1, sc1, g1, sh2, sc2, g2 = (mod[l, j] for j in range(6))
        n1g, n2g = norm1_g[l].reshape(1, d), norm2_g[l].reshape(1, d)
        w_parts = [w_u[l], w_cq[l], w_ckv[l], w_kr[l], w_ga[l], w_gb[l]]
        hb, u, cq, ckv, kr, ga, gb = _in_fwd(xl, n1g, sc1, sh1, w_parts, tm)
        bre_blk = _block_diag(bbr[l].transpose(0, 2, 1)).astype(_MM)
        bim_blk = _block_diag(bbi[l].transpose(0, 2, 1)).astype(_MM)
        cre_blk = _block_diag(ssm_c_re[l].transpose(0, 2, 1)).astype(_MM)
        cimneg_blk = _block_diag(-ssm_c_im[l].transpose(0, 2, 1)).astype(_MM)
        d_row = ssm_d[l].reshape(1, sw)
        ssm_w = (bre_blk, bim_blk, abr_rows[l], abi_rows[l], cre_blk, cimneg_blk, d_row)
        ypre, hre, him = _ssm_fwd(u, *ssm_w)
        gq, gkv = q_norm_g[l].reshape(1, ql), kv_norm_g[l].reshape(1, kl)
        q, k, v, cqn, ckvn = _mla_prep_fwd(cq, ckv, kr, rc, rs1, rs2, gq, gkv, wuq_p[l], wuk_p[l], wuv_p[l], vdim, tm)
        o, lse = _attn_fwd(q, k, v, pos_col, pos_row, scale, vdim)
        bglu = b_glu[l].reshape(1, sw)
        yg, ya, yb, merged, mo, xmid = _mix_fwd(ypre, o, ga, gb, xl, g1, bglu, full["w_glu"][l], full["w_a_out"][l],
                                                 wb_p[l], full["w_out"][l], tm)
        hb2, fa, fb, dn, xout = _ffn_fwd(xmid, n2g, sc2, sh2, g2, full["w_gate"][l], full["w_up"][l], full["w_down"][l], tm_ffn)
        saved.append(dict(x=xl, hb=hb, u=u, cq=cq, ckv=ckv, ga=ga, gb=gb, ssm_w=ssm_w, ypre=ypre, hre=hre, him=him,
                          q=q, k=k, v=v, cqn=cqn, ckvn=ckvn, o=o, lse=lse, yg=yg, ya=ya, yb=yb, merged=merged, mo=mo,
                          xmid=xmid, hb2=hb2, fa=fa, fb=fb, dn=dn, w_parts=w_parts))
        xl = xout

    dx, loss_acc, g_final = _head(xl, final_g.reshape(1, d), target, tm)
    loss = lax.psum(loss_acc[0, 0], ("x", "y", "c"))

    per_layer = ["w_gate", "w_up", "w_down", "norm2_g", "w_out", "w_a_out", "w_b_out", "w_glu", "b_glu", "w_uq", "w_uk",
                 "w_uv", "q_norm_g", "kv_norm_g", "ssm_d", "ssm_c_re", "ssm_c_im", "w_in", "norm1_g"]
    grads = {n: [None] * depth for n in per_layer}
    dmod = [None] * depth
    for l in reversed(range(depth)):
        s = saved[l]
        sh1, sc1, g1, sh2, sc2, g2 = (mod[l, j] for j in range(6))
        n1g, n2g = norm1_g[l].reshape(1, d), norm2_g[l].reshape(1, d)
        dxm, da, db, fb16, dd, dg2, dsh2, dsc2, dn2 = _ffn_bwd(
            dx, s["xmid"], s["fa"], s["fb"], s["dn"], n2g, sc2, g2, full["w_gate"][l], full["w_up"][l], full["w_down"][l], tm_ffn)
        grads["w_gate"][l] = _mm_tn("dw_gate", s["hb2"], da)
        grads["w_up"][l] = _mm_tn("dw_up", s["hb2"], db)
        grads["w_down"][l] = _mm_tn("dw_down", fb16, dd)
        grads["norm2_g"][l] = dn2.reshape(d)

        bglu = b_glu[l].reshape(1, sw)
        dmo, dya, dyb, dt, ys, dga, dgb, dypre, do, delta, dg1, dbglu = _mix_bwd(
            dxm, s["mo"], s["ya"], s["yb"], s["ga"], s["gb"], s["ypre"], s["o"], g1, bglu, full["w_glu"][l],
            full["w_a_out"][l], wb_p[l], full["w_out"][l], tm)
        grads["w_out"][l] = _mm_tn("dw_out", s["merged"], dmo)
        grads["w_a_out"][l] = _mm_tn("dw_a_out", s["yg"], dya)
        dwb_p = _mm_tn("dw_b_out", s["o"], dyb)
        grads["w_b_out"][l] = _unpad_heads(dwb_p.T, vdim).T
        grads["w_glu"][l] = _mm_tn("dw_glu", ys, dt)
        grads["b_glu"][l] = dbglu.reshape(sw)

        dq, dk, dv = _attn_bwd(s["q"], s["k"], s["v"], do, delta, s["lse"], pos_col, pos_row, scale)
        gq, gkv = q_norm_g[l].reshape(1, ql), kv_norm_g[l].reshape(1, kl)
        dqp, dcq, dckv, dkr, dgq, dgkv = _mla_prep_bwd(dq, dk, dv, s["cq"], s["ckv"], rc, rs1, rs2, gq, gkv,
                                                       wuq_p[l], wuk_p[l], wuv_p[l], nope, tm)
        grads["w_uq"][l] = _unpad_heads(_mm_tn("dw_uq", s["cqn"], dqp), qk_dim)
        grads["w_uk"][l] = _unpad_heads(_mm_tn("dw_uk", s["ckvn"], dk), nope)
        grads["w_uv"][l] = _unpad_heads(_mm_tn("dw_uv", s["ckvn"], dv), vdim)
        grads["q_norm_g"][l] = dgq.reshape(ql)
        grads["kv_norm_g"][l] = dgkv.reshape(kl)

        du, gre, gim, dar, dai, ddskip = _ssm_bwd(dypre, s["u"], s["hre"], s["him"], *s["ssm_w"])
        grads["ssm_d"][l] = ddskip.reshape(sw)
        d_bre = _block_diag_extract(_mm_tn("d_bre", s["u"], gre), groups).transpose(0, 2, 1)
        d_bim = _block_diag_extract(_mm_tn("d_bim", s["u"], gim), groups).transpose(0, 2, 1)
        grads["ssm_c_re"][l] = _block_diag_extract(_mm_tn("d_cre", s["hre"], dypre), groups).transpose(0, 2, 1)
        grads["ssm_c_im"][l] = -_block_diag_extract(_mm_tn("d_cim", s["him"], dypre), groups).transpose(0, 2, 1)
        s["disc_grads"] = (dar.reshape(gp, 1), dai.reshape(gp, 1), d_bre.reshape(gp, n_chan), d_bim.reshape(gp, n_chan))

        dz_parts = [du, dcq, dckv, dkr, dga, dgb]
        dx, dsh1, dsc1, dn1 = _in_bwd(dxm, s["x"], dz_parts, n1g, sc1, s["w_parts"], tm)
        dw_parts = [_mm_tn("dw_in_%d" % j, s["hb"], dz) for j, dz in enumerate(dz_parts)]
        dw_parts[3] = dw_parts[3][:, nope:nope + QK_ROPE]
        grads["w_in"][l] = jnp.concatenate(dw_parts, axis=1)
        grads["norm1_g"][l] = dn1.reshape(d)
        dmod[l] = jnp.concatenate([dsh1, dsc1, dg1, dsh2, dsc2, dg2], axis=1).reshape(6 * d)
    grad_x = dx.reshape(x.shape)

    disc = [jnp.concatenate([saved[l]["disc_grads"][j] for l in range(depth)], axis=0) for j in range(4)]
    da_re, da_im, dldt, db_re, db_im = _ssm_disc_bwd(a_re_col, a_im_col, ldt_col, b_re2, b_im2, *disc)
    stacked = {n: jnp.stack(v) for n, v in grads.items()}
    stacked["ssm_a_re"] = da_re.reshape(ssm_a_re.shape)
    stacked["ssm_a_im"] = da_im.reshape(ssm_a_im.shape)
    stacked["ssm_log_dt"] = _lane_sum(dldt.reshape(depth * groups, n_state)).reshape(ssm_log_dt.shape)
    stacked["ssm_b_re"] = db_re.reshape(ssm_b_re.shape)
    stacked["ssm_b_im"] = db_im.reshape(ssm_b_im.shape)
    stacked["final_g"] = g_final.reshape(d)
    stacked["b_ada"] = jnp.stack(dmod)

    small = [n for n in names if n not in big and n != "w_ada"]
    small_shapes = [weights[n].shape for n in small]
    (small_all,) = _exchange("gather_small", "gather8", [_pack_rows([stacked[n] for n in small])])
    sg, sd, sm, sv = _adamw("adamw_small", [small_all], _pack_rows([weights[n] for n in small]),
                            _pack_rows([mom_m[n] for n in small]), _pack_rows([mom_v[n] for n in small]))
    out_g = dict(zip(small, _unpack_rows(sg, small_shapes)))
    out_d = dict(zip(small, _unpack_rows(sd, small_shapes)))
    out_m = dict(zip(small, _unpack_rows(sm, small_shapes)))
    out_v = dict(zip(small, _unpack_rows(sv, small_shapes)))

    n_dmod = depth * 6 * d
    dmod_all = small_all[:, :n_dmod // 128].reshape(8, depth, 6 * d)
    dmod_cols = lax.dynamic_slice_in_dim(dmod_all, chip * ada_cols, ada_cols, axis=2)
    g_wada = _wada_bwd(c_all, dmod_cols.transpose(1, 0, 2).reshape(depth * 8, ada_cols), depth)
    res = _adamw("adamw_w_ada", [g_wada], w_ada.reshape(depth * d, ada_cols), m_w_ada.reshape(depth * d, ada_cols),
                 v_w_ada.reshape(depth * d, ada_cols))
    out_g["w_ada"], out_d["w_ada"], out_m["w_ada"], out_v["w_ada"] = (r.reshape(w_ada.shape) for r in res)

    to_chips = [(_rows_to_chips if n in row_sharded else _cols_to_chips)(stacked[n]).astype(_WIRE) for n in big]
    landed = _exchange("scatter_grads", "scatter4", to_chips)
    partial = [_sum_slots("sum_" + n, r.reshape(4, -1, r.shape[-1])) for n, r in zip(big, landed)]
    sibling = _exchange("swap_partials", "swap", partial)
    for n, mine, theirs in zip(big, partial, sibling):
        shp = weights[n].shape
        as2d = lambda a: a.reshape(-1, shp[-1])
        res = _adamw("adamw_" + n, [mine, theirs], as2d(weights[n]), as2d(mom_m[n]), as2d(mom_v[n]))
        out_g[n], out_d[n], out_m[n], out_v[n] = (r.reshape(shp) for r in res)

    return (loss, grad_x, *[out_g[n] for n in names], *[out_d[n] for n in names], *[out_m[n] for n in names],
            *[out_v[n] for n in names])
```

```python
import functools
import math

import numpy as np
import jax
import jax.numpy as jnp
from jax import lax
from jax.experimental import pallas as pl
from jax.experimental.pallas import tpu as pltpu

F32 = jnp.float32
_MM = jnp.bfloat16
_ACT = jnp.bfloat16
_WIRE = jnp.bfloat16

N_HEADS = 8
QK_ROPE = 32
HEAD_PAD = 128
QW = N_HEADS * HEAD_PAD
ROPE_BASE = 10000.0
EPS = 1e-6
DT_MIN = 1e-3
ADAM_LR = 0.001
ADAM_B1 = 0.9
ADAM_B2 = 0.999
ADAM_EPS = 1e-08
ADAM_WD = 0.01
ADAM_STEP = 10
NEG_INF = -1e30
LOG2_E = math.log2(math.e)
ATTN_HEADS_PER_STEP = 4
ATTN_BWD_HEADS_PER_STEP = 2

V7X_VMEM_BYTES = 64 * 1024 * 1024
VMEM_RESERVE_BYTES = 6 * 1024 * 1024
MESH = pl.DeviceIdType.MESH
ANY = pl.BlockSpec(memory_space=pl.ANY)


def _vmem_limit(block_bytes, temp_bytes):
    want = 2 * block_bytes + temp_bytes
    return int(min(V7X_VMEM_BYTES - VMEM_RESERVE_BYTES, max(want, 32 * 1024 * 1024)))


def _nbytes(shape, dtype):
    return int(np.prod(shape)) * jnp.dtype(dtype).itemsize


def _tile(n, target, mult=8):
    t = min(n, target)
    while t >= mult:
        if n % t == 0 and t % mult == 0:
            return t
        t -= 1
    return n


def _dot(a, b):
    return jnp.dot(a.astype(_MM), b.astype(_MM), preferred_element_type=F32)


def _dot_nt(a, b):
    return lax.dot_general(a.astype(_MM), b.astype(_MM), (((1,), (1,)), ((), ())), preferred_element_type=F32)


def _dot_tn(a, b):
    return lax.dot_general(a.astype(_MM), b.astype(_MM), (((0,), (0,)), ((), ())), preferred_element_type=F32)


def _sigmoid(x):
    return jax.nn.sigmoid(x)


_GELU_K = math.sqrt(2.0 / math.pi)


def _gelu(x):
    return x * (0.5 * (1.0 + jnp.tanh(_GELU_K * (x + 0.044715 * (x * x * x)))))


def _gelu_grad(x):
    th = jnp.tanh(_GELU_K * (x + 0.044715 * (x * x * x)))
    return 0.5 * (1.0 + th) + 0.5 * x * (1.0 - th * th) * (_GELU_K * (1.0 + 3.0 * 0.044715 * (x * x)))


def _rows_sum(v):
    return jnp.sum(v, axis=0, keepdims=True)


def _rms_stats(x):
    rstd = lax.rsqrt(jnp.mean(x * x, axis=-1, keepdims=True) + EPS)
    return x * rstd, rstd


def _rms_bwd(dxh, xh, rstd):
    return rstd * (dxh - xh * jnp.mean(dxh * xh, axis=-1, keepdims=True))


def _rowcall(name, body, n_rows, tm, row_ins, full_ins, row_outs, acc_outs=(), temp_cols=0):
    grid = (n_rows // tm,)
    in_specs = [pl.BlockSpec((tm, a.shape[1]), lambda i: (i, 0)) for a in row_ins]
    in_specs += [pl.BlockSpec(a.shape, lambda i: (0, 0), pipeline_mode=pl.Buffered(1)) for a in full_ins]
    out_shape = [jax.ShapeDtypeStruct((n_rows, c), dt) for c, dt in row_outs]
    out_shape += [jax.ShapeDtypeStruct(s, dt) for s, dt in acc_outs]
    out_specs = [pl.BlockSpec((tm, c), lambda i: (i, 0)) for c, _ in row_outs]
    out_specs += [pl.BlockSpec(s, lambda i: (0, 0)) for s, _ in acc_outs]
    blocks = sum(_nbytes((tm, a.shape[1]), a.dtype) for a in row_ins)
    blocks += sum(_nbytes((tm, c), dt) for c, dt in row_outs) + sum(_nbytes(s, dt) for s, dt in acc_outs)
    resident = sum(_nbytes(a.shape, a.dtype) for a in full_ins)
    limit = _vmem_limit(blocks, resident + _nbytes((tm, temp_cols), F32))
    res = pl.pallas_call(
        body, name=name, grid=grid, in_specs=in_specs, out_specs=out_specs, out_shape=out_shape,
        compiler_params=pltpu.CompilerParams(
            dimension_semantics=("arbitrary" if acc_outs else "parallel",), vmem_limit_bytes=limit),
    )(*row_ins, *full_ins)
    return res


def _first_step():
    return pl.program_id(0) == 0


def _acc(ref, val):
    @pl.when(_first_step())
    def _():
        ref[...] = val

    @pl.when(jnp.logical_not(_first_step()))
    def _():
        ref[...] += val


def _mm_tn(name, a, g):
    n_rows, k = a.shape
    n = g.shape[1]
    tk = k if k <= 1024 else _tile(k, 1408, 128)
    tn = n if n <= 1024 else _tile(n, 1408, 128)
    tl = _tile(n_rows, 512, 16)

    def body(a_ref, g_ref, o_ref):
        @pl.when(pl.program_id(2) == 0)
        def _():
            o_ref[...] = jnp.zeros_like(o_ref)
        o_ref[...] += _dot_tn(a_ref[...], g_ref[...])

    blocks = _nbytes((tl, tk), a.dtype) + _nbytes((tl, tn), g.dtype) + _nbytes((tk, tn), F32)
    return pl.pallas_call(
        body, name=name, grid=(k // tk, n // tn, n_rows // tl),
        in_specs=[pl.BlockSpec((tl, tk), lambda i, j, l: (l, i)), pl.BlockSpec((tl, tn), lambda i, j, l: (l, j))],
        out_specs=pl.BlockSpec((tk, tn), lambda i, j, l: (i, j)),
        out_shape=jax.ShapeDtypeStruct((k, n), F32),
        compiler_params=pltpu.CompilerParams(
            dimension_semantics=("parallel", "parallel", "arbitrary"),
            vmem_limit_bytes=_vmem_limit(blocks, 2 * _nbytes((tl, max(tk, tn)), F32) + _nbytes((tk, tn), F32))),
    )(a, g)


def _place():
    return lax.axis_index("x"), lax.axis_index("y"), lax.axis_index("c")


def _flip(v, bit):
    return 1 - v if bit else v


def _exchange(name, mode, arrays):
    n = len(arrays)
    if mode == "gather8":
        rel = [((k >> 2) & 1, (k >> 1) & 1, k & 1) for k in range(1, 8)]
        out_shape = [jax.ShapeDtypeStruct((8,) + a.shape, a.dtype) for a in arrays]
    elif mode == "scatter4":
        rel = [((k >> 1) & 1, k & 1, 0) for k in range(1, 4)]
        out_shape = [jax.ShapeDtypeStruct(a.shape, a.dtype) for a in arrays]
    else:
        rel = [(0, 0, 1)]
        out_shape = [jax.ShapeDtypeStruct(a.shape, a.dtype) for a in arrays]
    n_rel = len(rel)

    def body(*refs):
        ins, outs = refs[:n], refs[n:2 * n]
        send_sems, recv_sems, local_sems = refs[2 * n:]
        x, y, c = _place()

        def slot(px, py, pc):
            return 4 * px + 2 * py + pc if mode == "gather8" else 2 * px + py

        mine = slot(x, y, c)
        local = []
        if mode != "swap":
            for a in range(n):
                src = ins[a].at[mine] if mode == "scatter4" else ins[a]
                local.append(pltpu.make_async_copy(src, outs[a].at[mine], local_sems.at[a]))
            for cp in local:
                cp.start()

        def remote(r, a):
            px, py, pc = _flip(x, rel[r][0]), _flip(y, rel[r][1]), _flip(c, rel[r][2])
            theirs = slot(px, py, pc)
            if mode == "swap":
                src, dst_there, dst_here = ins[a], outs[a], outs[a]
            elif mode == "scatter4":
                src, dst_there, dst_here = ins[a].at[theirs], outs[a].at[mine], outs[a].at[theirs]
            else:
                src, dst_there, dst_here = ins[a], outs[a].at[mine], outs[a].at[theirs]
            k = r * n + a
            push = pltpu.make_async_remote_copy(src_ref=src, dst_ref=dst_there, send_sem=send_sems.at[k],
                                                recv_sem=recv_sems.at[k], device_id=(px, py, pc), device_id_type=MESH)
            land = pltpu.make_async_remote_copy(src_ref=src, dst_ref=dst_here, send_sem=send_sems.at[k],
                                                recv_sem=recv_sems.at[k], device_id=(px, py, pc), device_id_type=MESH)
            return push, land

        copies = [remote(r, a) for r in range(n_rel) for a in range(n)]
        for push, _ in copies:
            push.start()
        for _, land in copies:
            land.wait_recv()
        for push, _ in copies:
            push.wait_send()
        for cp in local:
            cp.wait()

    return pl.pallas_call(
        body, name=name, in_specs=[ANY] * n, out_specs=[ANY] * n, out_shape=out_shape,
        scratch_shapes=[pltpu.SemaphoreType.DMA((n_rel * n,)), pltpu.SemaphoreType.DMA((n_rel * n,)),
                        pltpu.SemaphoreType.DMA((max(n, 1),))],
    )(*arrays)


def _gather_chips_two_level(name, arrays):
    n = len(arrays)
    rel = [((k >> 1) & 1, k & 1) for k in range(1, 4)]
    halves = [a.shape[0] // 2 for a in arrays]

    def body(*refs):
        ins, outs = refs[:n], refs[n:2 * n]
        ici_send, ici_recv, d2d_send, d2d_recv, local_sems = refs[2 * n:]
        x, y, c = _place()
        mine = 2 * x + y
        local = [pltpu.make_async_copy(ins[a], outs[a].at[mine], local_sems.at[a]) for a in range(n)]
        for cp in local:
            cp.start()

        def ici(r, a):
            px, py = _flip(x, rel[r][0]), _flip(y, rel[r][1])
            theirs = 2 * px + py
            rows = pl.ds(c * halves[a], halves[a])
            k = r * n + a
            push = pltpu.make_async_remote_copy(src_ref=ins[a].at[rows], dst_ref=outs[a].at[mine, rows],
                                                send_sem=ici_send.at[k], recv_sem=ici_recv.at[k],
                                                device_id=(px, py, c), device_id_type=MESH)
            land = pltpu.make_async_remote_copy(src_ref=ins[a].at[rows], dst_ref=outs[a].at[theirs, rows],
                                                send_sem=ici_send.at[k], recv_sem=ici_recv.at[k],
                                                device_id=(px, py, c), device_id_type=MESH)
            there = pl.ds((1 - c) * halves[a], halves[a])
            forward = pltpu.make_async_remote_copy(src_ref=outs[a].at[theirs, rows], dst_ref=outs[a].at[theirs, rows],
                                                   send_sem=d2d_send.at[k], recv_sem=d2d_recv.at[k],
                                                   device_id=(x, y, 1 - c), device_id_type=MESH)
            back = pltpu.make_async_remote_copy(src_ref=outs[a].at[theirs, rows], dst_ref=outs[a].at[theirs, there],
                                                send_sem=d2d_send.at[k], recv_sem=d2d_recv.at[k],
                                                device_id=(x, y, 1 - c), device_id_type=MESH)
            return push, land, forward, back

        copies = [ici(r, a) for r in range(len(rel)) for a in range(n)]
        for push, _, _, _ in copies:
            push.start()
        for _, land, forward, _ in copies:
            land.wait_recv()
            forward.start()
        for _, _, _, back in copies:
            back.wait_recv()
        for push, _, forward, _ in copies:
            push.wait_send()
            forward.wait_send()
        for cp in local:
            cp.wait()

    n_sem = len(rel) * n
    return pl.pallas_call(
        body, name=name, in_specs=[ANY] * n, out_specs=[ANY] * n,
        out_shape=[jax.ShapeDtypeStruct((4,) + a.shape, a.dtype) for a in arrays],
        scratch_shapes=[pltpu.SemaphoreType.DMA((n_sem,))] * 4 + [pltpu.SemaphoreType.DMA((n,))],
    )(*arrays)


def _sum_slots(name, stacked):
    p, rows, cols = stacked.shape
    tr = _tile(rows, 256)

    def body(s_ref, o_ref):
        acc = s_ref[0].astype(F32)
        for j in range(1, p):
            acc = acc + s_ref[j].astype(F32)
        o_ref[...] = acc

    return pl.pallas_call(
        body, name=name, grid=(rows // tr,),
        in_specs=[pl.BlockSpec((p, tr, cols), lambda i: (0, i, 0))],
        out_specs=pl.BlockSpec((tr, cols), lambda i: (i, 0)),
        out_shape=jax.ShapeDtypeStruct((rows, cols), F32),
        compiler_params=pltpu.CompilerParams(dimension_semantics=("parallel",)),
    )(stacked)


def _adamw(name, parts, w, m, v):
    rows, cols = w.shape
    tr = _tile(rows, 256)
    n_parts = len(parts)

    def body(*refs):
        part_refs = refs[:n_parts]
        w_ref, m_ref, v_ref, g_out, d_out, m_out, v_out = refs[n_parts:]
        g = None
        for pr in part_refs:
            if len(pr.shape) == 3:
                for j in range(pr.shape[0]):
                    g = pr[j] if g is None else g + pr[j]
            else:
                g = pr[...] if g is None else g + pr[...]
        m_new = ADAM_B1 * m_ref[...] + (1.0 - ADAM_B1) * g
        v_new = ADAM_B2 * v_ref[...] + (1.0 - ADAM_B2) * jnp.square(g)
        m_hat = m_new / (1.0 - ADAM_B1 ** ADAM_STEP)
        v_hat = v_new / (1.0 - ADAM_B2 ** ADAM_STEP)
        g_out[...] = g
        d_out[...] = -ADAM_LR * (m_hat / (jnp.sqrt(v_hat) + ADAM_EPS) + ADAM_WD * w_ref[...])
        m_out[...] = m_new
        v_out[...] = v_new

    spec2 = pl.BlockSpec((tr, cols), lambda i: (i, 0))
    in_specs = [pl.BlockSpec((p.shape[0], tr, cols), lambda i: (0, i, 0)) if p.ndim == 3 else spec2 for p in parts]
    blocks = sum(_nbytes((p.shape[0] if p.ndim == 3 else 1, tr, cols), F32) for p in parts) + 7 * _nbytes((tr, cols), F32)
    return pl.pallas_call(
        body, name=name, grid=(rows // tr,),
        in_specs=in_specs + [spec2] * 3, out_specs=[spec2] * 4,
        out_shape=[jax.ShapeDtypeStruct((rows, cols), F32)] * 4,
        compiler_params=pltpu.CompilerParams(dimension_semantics=("parallel",),
                                             vmem_limit_bytes=_vmem_limit(blocks, 4 * _nbytes((tr, cols), F32))),
    )(*parts, w, m, v)


def _mod_fwd(c_all, w_ada2d, depth):
    nb, d = c_all.shape
    cols = w_ada2d.shape[1]
    tn = _tile(cols, 512, 128)

    def body(c_ref, w_ref, o_ref):
        cv = c_ref[...]
        o_ref[...] = _dot(cv * _sigmoid(cv), w_ref[...])

    return pl.pallas_call(
        body, name="mod_fwd", grid=(depth, cols // tn),
        in_specs=[pl.BlockSpec((nb, d), lambda l, j: (0, 0)), pl.BlockSpec((d, tn), lambda l, j: (l, j))],
        out_specs=pl.BlockSpec((nb, tn), lambda l, j: (l, j)),
        out_shape=jax.ShapeDtypeStruct((depth * nb, cols), F32),
        compiler_params=pltpu.CompilerParams(dimension_semantics=("parallel", "parallel")),
    )(c_all, w_ada2d)


def _wada_bwd(c_all, dmod2d, depth):
    nb, d = c_all.shape
    cols = dmod2d.shape[1]
    tn = _tile(cols, 512, 128)

    def body(c_ref, g_ref, o_ref):
        cv = c_ref[...]
        o_ref[...] = _dot_tn(cv * _sigmoid(cv), g_ref[...])

    return pl.pallas_call(
        body, name="wada_bwd", grid=(depth, cols // tn),
        in_specs=[pl.BlockSpec((nb, d), lambda l, j: (0, 0)), pl.BlockSpec((nb, tn), lambda l, j: (l, j))],
        out_specs=pl.BlockSpec((d, tn), lambda l, j: (l, j)),
        out_shape=jax.ShapeDtypeStruct((depth * d, cols), F32),
        compiler_params=pltpu.CompilerParams(dimension_semantics=("parallel", "parallel")),
    )(c_all, dmod2d)


def _rope_tables(pos_col, inv_freq_lane, nope):
    n_rows = pos_col.shape[0]
    tm = _tile(n_rows, 512)
    half = QK_ROPE // 2

    def body(p_ref, f_ref, c_ref, s1_ref, s2_ref):
        ang = p_ref[...] * f_ref[...]
        lane = lax.broadcasted_iota(jnp.int32, ang.shape, 1)
        first = (lane >= nope) & (lane < nope + half)
        second = (lane >= nope + half) & (lane < nope + 2 * half)
        cos, sin = jnp.cos(ang), jnp.sin(ang)
        c_ref[...] = jnp.where(first | second, cos, 1.0)
        s1_ref[...] = jnp.where(first, -sin, 0.0)
        s2_ref[...] = jnp.where(second, sin, 0.0)

    return _rowcall("rope_tables", body, n_rows, tm, [pos_col], [inv_freq_lane], [(HEAD_PAD, F32)] * 3)


def _rope(q, c, s1, s2):
    w = q.shape[1]
    return q * c + pltpu.roll(q, w - QK_ROPE // 2, axis=1) * s1 + pltpu.roll(q, QK_ROPE // 2, axis=1) * s2


def _rope_adjoint(dr, c, s1, s2):
    w = dr.shape[1]
    return dr * c + pltpu.roll(dr * s1, QK_ROPE // 2, axis=1) + pltpu.roll(dr * s2, w - QK_ROPE // 2, axis=1)


def _ssm_disc(ar, ai, log_dt, br, bi):
    dt = jnp.exp(log_dt)
    mag = jnp.exp(ar * dt)
    abr = mag * jnp.cos(ai * dt)
    abi = mag * jnp.sin(ai * dt)
    den = ar * ar + ai * ai
    nr = abr - 1.0
    ni = abi
    cr = (nr * ar + ni * ai) / den
    ci = (ni * ar - nr * ai) / den
    return abr, abi, cr * br - ci * bi, cr * bi + ci * br


def _ssm_disc_fwd(ar, ai, log_dt, br, bi):
    n_rows, m = br.shape
    tm = _tile(n_rows, 1024)

    def body(ar_ref, ai_ref, dt_ref, br_ref, bi_ref, o1, o2, o3, o4):
        o1[...], o2[...], o3[...], o4[...] = _ssm_disc(ar_ref[...], ai_ref[...], dt_ref[...], br_ref[...], bi_ref[...])

    return _rowcall("ssm_disc_fwd", body, n_rows, tm, [ar, ai, log_dt, br, bi], [],
                    [(1, F32), (1, F32), (m, F32), (m, F32)])


def _ssm_disc_bwd(ar, ai, log_dt, br, bi, g_abr, g_abi, g_bbr, g_bbi):
    n_rows, m = br.shape
    tm = _tile(n_rows, 1024)

    def body(ar_ref, ai_ref, dt_ref, br_ref, bi_ref, g1, g2, g3, g4, o1, o2, o3, o4, o5):
        _, vjp = jax.vjp(_ssm_disc, ar_ref[...], ai_ref[...], dt_ref[...], br_ref[...], bi_ref[...])
        o1[...], o2[...], o3[...], o4[...], o5[...] = vjp((g1[...], g2[...], g3[...], g4[...]))

    return _rowcall("ssm_disc_bwd", body, n_rows, tm, [ar, ai, log_dt, br, bi, g_abr, g_abi, g_bbr, g_bbi], [],
                    [(1, F32), (1, F32), (1, F32), (m, F32), (m, F32)])


def _lane_sum(v2d):
    def body(v_ref, o_ref):
        o_ref[...] = jnp.sum(v_ref[...], axis=1, keepdims=True)
    return pl.pallas_call(body, name="lane_sum", out_shape=jax.ShapeDtypeStruct((v2d.shape[0], 1), F32))(v2d)


def _in_fwd(x, n1g, sc1, sh1, w_parts, tm):
    n_rows = x.shape[0]
    widths = [w.shape[1] for w in w_parts]

    def body(x_ref, g_ref, sc_ref, sh_ref, *rest):
        w_refs, (hb_ref, *z_refs) = rest[:len(w_parts)], rest[len(w_parts):]
        xh, _ = _rms_stats(x_ref[...])
        h = (xh * g_ref[...]) * (1.0 + sc_ref[...]) + sh_ref[...]
        hb = h.astype(_MM)
        hb_ref[...] = hb.astype(_ACT)
        for w_ref, z_ref in zip(w_refs, z_refs):
            z_ref[...] = _dot(hb, w_ref[...])

    return _rowcall("in_fwd", body, n_rows, tm, [x], [n1g, sc1, sh1, *w_parts],
                    [(x.shape[1], _ACT)] + [(w, F32) for w in widths], temp_cols=4 * x.shape[1])


def _ssm_fwd(u, bre_blk, bim_blk, abr_row, abi_row, cre_blk, cimneg_blk, d_row):
    n_rows, sw = u.shape
    gp = abr_row.shape[1]
    t = _tile(n_rows, 256)

    def body(u_ref, bre_ref, bim_ref, ar_ref, ai_ref, cre_ref, cim_ref, d_ref, y_ref, hre_ref, him_ref, cr, ci):
        @pl.when(_first_step())
        def _():
            cr[...] = jnp.zeros_like(cr)
            ci[...] = jnp.zeros_like(ci)

        uv = u_ref[...]
        ub = uv.astype(_MM)
        hre_ref[...] = _dot(ub, bre_ref[...])
        him_ref[...] = _dot(ub, bim_ref[...])
        a_r, a_i = ar_ref[...], ai_ref[...]

        def step(k, carry):
            pr, pi = carry
            row = pl.ds(k, 1)
            hr = a_r * pr - a_i * pi + hre_ref[row, :]
            hi = a_r * pi + a_i * pr + him_ref[row, :]
            hre_ref[row, :] = hr
            him_ref[row, :] = hi
            return hr, hi

        pr, pi = lax.fori_loop(0, t, step, (cr[0:1, :], ci[0:1, :]), unroll=8)
        cr[0:1, :] = pr
        ci[0:1, :] = pi
        y_ref[...] = _dot(hre_ref[...], cre_ref[...]) + _dot(him_ref[...], cim_ref[...]) + d_ref[...] * uv

    row = lambda c: pl.BlockSpec((t, c), lambda i: (i, 0))
    full = lambda a: pl.BlockSpec(a.shape, lambda i: (0, 0), pipeline_mode=pl.Buffered(1))
    blocks = _nbytes((t, sw), F32) * 2 + 2 * _nbytes((t, gp), F32)
    resident = 4 * _nbytes((sw, gp), _MM)
    return pl.pallas_call(
        body, name="ssm_fwd", grid=(n_rows // t,),
        in_specs=[row(sw), full(bre_blk), full(bim_blk), full(abr_row), full(abi_row), full(cre_blk),
                  full(cimneg_blk), full(d_row)],
        out_specs=[row(sw), row(gp), row(gp)],
        out_shape=[jax.ShapeDtypeStruct((n_rows, sw), F32), jax.ShapeDtypeStruct((n_rows, gp), F32),
                   jax.ShapeDtypeStruct((n_rows, gp), F32)],
        scratch_shapes=[pltpu.VMEM((8, gp), F32), pltpu.VMEM((8, gp), F32)],
        compiler_params=pltpu.CompilerParams(dimension_semantics=("arbitrary",),
                                             vmem_limit_bytes=_vmem_limit(blocks, resident + 3 * _nbytes((t, gp), F32))),
    )(u, bre_blk, bim_blk, abr_row, abi_row, cre_blk, cimneg_blk, d_row)


def _mla_prep_fwd(cq, ckv, kr, rc, rs1, rs2, gq, gkv, wuq, wuk, wuv, vdim, tm):
    n_rows = cq.shape[0]

    def body(cq_ref, ckv_ref, kr_ref, c_ref, s1_ref, s2_ref, gq_ref, gkv_ref, wuq_ref, wuk_ref, wuv_ref,
             q_ref, k_ref, v_ref, cqn_ref, ckvn_ref):
        c, s1, s2 = c_ref[...], s1_ref[...], s2_ref[...]
        c8, s18, s28 = (jnp.tile(a, (1, N_HEADS)) for a in (c, s1, s2))
        xh, _ = _rms_stats(cq_ref[...])
        cqn = (xh * gq_ref[...]).astype(_MM)
        cqn_ref[...] = cqn.astype(_ACT)
        q_ref[...] = _rope(_dot(cqn, wuq_ref[...]), c8, s18, s28).astype(_ACT)
        xh, _ = _rms_stats(ckv_ref[...])
        ckvn = (xh * gkv_ref[...]).astype(_MM)
        ckvn_ref[...] = ckvn.astype(_ACT)
        kpe = _rope(kr_ref[...], c, s1, s2)
        k_ref[...] = (_dot(ckvn, wuk_ref[...]) + jnp.tile(kpe, (1, N_HEADS))).astype(_ACT)
        v = _dot(ckvn, wuv_ref[...])
        lane = lax.broadcasted_iota(jnp.int32, v.shape, 1)
        v_ref[...] = jnp.where((lane & (HEAD_PAD - 1)) == vdim, 1.0, v).astype(_ACT)

    return _rowcall("mla_prep_fwd", body, n_rows, tm, [cq, ckv, kr, rc, rs1, rs2], [gq, gkv, wuq, wuk, wuv],
                    [(QW, _ACT), (QW, _ACT), (QW, _ACT), (cq.shape[1], _ACT), (ckv.shape[1], _ACT)], temp_cols=6 * QW)


def _causal_steps(n_blocks, key_major):
    if key_major:
        pairs = [(qi, ki) for ki in range(n_blocks) for qi in range(ki, n_blocks)]
    else:
        pairs = [(qi, ki) for qi in range(n_blocks) for ki in range(qi + 1)]
    return (jnp.asarray(np.array([p[0] for p in pairs], np.int32)), jnp.asarray(np.array([p[1] for p in pairs], np.int32)))


def _attn_fwd(q, k, v, pos_col, pos_row, scale, vdim):
    n_rows = q.shape[0]
    ta = _tile(n_rows, 512, 128)
    nb = n_rows // ta
    hb = ATTN_HEADS_PER_STEP
    wide = hb * HEAD_PAD
    qmap, kmap = _causal_steps(nb, key_major=False)
    c2 = scale * LOG2_E

    def body(qm, km, q_ref, k_ref, v_ref, pq_ref, pk_ref, o_ref, lse_ref, m_sc, acc_sc):
        s_id = pl.program_id(1)
        qi, ki = qm[s_id], km[s_id]

        @pl.when(ki == 0)
        def _():
            m_sc[...] = jnp.full_like(m_sc, NEG_INF)
            acc_sc[...] = jnp.zeros_like(acc_sc)

        def update(on_diagonal):
            if on_diagonal:
                visible = pk_ref[...] <= pq_ref[...]
            for h in range(hb):
                lanes = slice(h * HEAD_PAD, (h + 1) * HEAD_PAD)
                s = _dot_nt(q_ref[:, lanes], k_ref[:, lanes])
                if on_diagonal:
                    s = jnp.where(visible, s, NEG_INF)
                m_prev = m_sc[:, lanes]
                m_new = jnp.maximum(m_prev, jnp.max(s, axis=1, keepdims=True))
                alpha = jnp.exp2((m_prev - m_new) * c2)
                p = jnp.exp2((s - m_new[:, :1]) * c2)
                acc_new = alpha * acc_sc[:, lanes] + _dot(p, v_ref[:, lanes])
                if on_diagonal:
                    l_new = acc_new[:, vdim:vdim + 1]
                    o_ref[:, lanes] = acc_new / l_new
                    lse_ref[:, lanes] = m_new * c2 + jnp.log2(l_new)
                else:
                    acc_sc[:, lanes] = acc_new
                    m_sc[:, lanes] = m_new

        pl.when(ki != qi)(functools.partial(update, False))
        pl.when(ki == qi)(functools.partial(update, True))

    qspec = pl.BlockSpec((ta, wide), lambda h, s, qm, km: (qm[s], h))
    kspec = pl.BlockSpec((ta, wide), lambda h, s, qm, km: (km[s], h))
    grid_spec = pltpu.PrefetchScalarGridSpec(
        num_scalar_prefetch=2, grid=(N_HEADS // hb, int(qmap.shape[0])),
        in_specs=[qspec, kspec, kspec,
                  pl.BlockSpec((ta, 1), lambda h, s, qm, km: (qm[s], 0)),
                  pl.BlockSpec((1, ta), lambda h, s, qm, km: (0, km[s]))],
        out_specs=[qspec, qspec],
        scratch_shapes=[pltpu.VMEM((ta, wide), F32)] * 2)
    return pl.pallas_call(
        body, name="attn_fwd", grid_spec=grid_spec,
        out_shape=[jax.ShapeDtypeStruct((n_rows, QW), F32), jax.ShapeDtypeStruct((n_rows, QW), F32)],
        compiler_params=pltpu.CompilerParams(
            dimension_semantics=("parallel", "arbitrary"),
            vmem_limit_bytes=_vmem_limit(8 * _nbytes((ta, wide), F32), 6 * hb * _nbytes((ta, ta), F32))),
    )(qmap, kmap, q, k, v, pos_col, pos_row)


def _mix_fwd(ypre, o, ga, gb, x, g1, bglu, wglu, wa, wb, wout, tm):
    n_rows, d = x.shape
    sw = ypre.shape[1]

    def body(y_ref, o_ref, ga_ref, gb_ref, x_ref, g1_ref, bglu_ref, wglu_ref, wa_ref, wb_ref, wout_ref,
             yg_ref, ya_ref, yb_ref, mg_ref, mo_ref, xo_ref):
        ys = _gelu(y_ref[...])
        yg = ys * _sigmoid(_dot(ys, wglu_ref[...]) + bglu_ref[...])
        yg_ref[...] = yg.astype(_ACT)
        ya = _dot(yg, wa_ref[...])
        yb = _dot(o_ref[...], wb_ref[...])
        ya_ref[...] = ya
        yb_ref[...] = yb
        merged = _sigmoid(ga_ref[...]) * ya + _sigmoid(gb_ref[...]) * yb
        mg_ref[...] = merged.astype(_ACT)
        mo = _dot(merged, wout_ref[...])
        mo_ref[...] = mo
        xo_ref[...] = x_ref[...] + g1_ref[...] * mo

    return _rowcall("mix_fwd", body, n_rows, tm, [ypre, o, ga, gb, x], [g1, bglu, wglu, wa, wb, wout],
                    [(sw, _ACT), (d, F32), (d, F32), (d, _ACT), (d, F32), (d, F32)], temp_cols=4 * d)


def _ffn_fwd(x, n2g, sc2, sh2, g2, wg, wu, wd, tm):
    n_rows, d = x.shape
    ff = wg.shape[1]

    def body(x_ref, g_ref, sc_ref, sh_ref, g2_ref, wg_ref, wu_ref, wd_ref, hb_ref, a_ref, b_ref, d_ref, xo_ref):
        xv = x_ref[...]
        xh, _ = _rms_stats(xv)
        hb = ((xh * g_ref[...]) * (1.0 + sc_ref[...]) + sh_ref[...]).astype(_MM)
        hb_ref[...] = hb.astype(_ACT)
        a = _dot(hb, wg_ref[...])
        b = _dot(hb, wu_ref[...])
        a_ref[...] = a
        b_ref[...] = b
        dn = _dot((a * _sigmoid(a)) * b, wd_ref[...])
        d_ref[...] = dn
        xo_ref[...] = xv + g2_ref[...] * dn

    return _rowcall("ffn_fwd", body, n_rows, tm, [x], [n2g, sc2, sh2, g2, wg, wu, wd],
                    [(d, _ACT), (ff, F32), (ff, F32), (d, F32), (d, F32)], temp_cols=3 * ff)


def _head(x, fg, target, tm):
    n_rows, d = x.shape

    def body(x_ref, t_ref, g_ref, dx_ref, loss_ref, dg_ref):
        xh, rstd = _rms_stats(x_ref[...])
        err = xh * g_ref[...] - t_ref[...]
        part = jnp.sum(jnp.mean(err * err, axis=-1, keepdims=True), axis=0, keepdims=True) * 0.5
        _acc(loss_ref, jnp.broadcast_to(part, loss_ref.shape))
        dy = err * (1.0 / d)
        _acc(dg_ref, _rows_sum(dy * xh))
        dx_ref[...] = _rms_bwd(dy * g_ref[...], xh, rstd)

    return _rowcall("head", body, n_rows, tm, [x, target], [fg], [(d, F32)], [((1, 128), F32), ((1, d), F32)],
                    temp_cols=4 * d)


def _ffn_bwd(dxo, xmid, a, b, dn, n2g, sc2, g2, wg, wu, wd, tm):
    n_rows, d = dxo.shape
    ff = a.shape[1]

    def act_body(dxo_ref, a_ref, b_ref, dn_ref, g2_ref, wd_ref, da_ref, db_ref, f_ref, dd_ref, dg2_ref):
        dxo_v = dxo_ref[...]
        dd = dxo_v * g2_ref[...]
        dd_ref[...] = dd.astype(_ACT)
        _acc(dg2_ref, _rows_sum(dxo_v * dn_ref[...]))
        df = _dot_nt(dd, wd_ref[...])
        av, bv = a_ref[...], b_ref[...]
        sa = _sigmoid(av)
        si = av * sa
        f_ref[...] = (si * bv).astype(_ACT)
        da_ref[...] = (df * bv * (sa * (1.0 + av * (1.0 - sa)))).astype(_ACT)
        db_ref[...] = (df * si).astype(_ACT)

    da, db, f, dd, dg2 = _rowcall("ffn_bwd_act", act_body, n_rows, 2 * tm, [dxo, a, b, dn], [g2, wd],
                                  [(ff, _ACT), (ff, _ACT), (ff, _ACT), (d, _ACT)], [((1, d), F32)], temp_cols=4 * ff)

    def in_body(dxo_ref, x_ref, da_ref, db_ref, g_ref, sc_ref, wg_ref, wu_ref, dx_ref, dsh_ref, dsc_ref, dn2_ref):
        dh = _dot_nt(da_ref[...], wg_ref[...]) + _dot_nt(db_ref[...], wu_ref[...])
        xh, rstd = _rms_stats(x_ref[...])
        yg = xh * g_ref[...]
        _acc(dsh_ref, _rows_sum(dh))
        _acc(dsc_ref, _rows_sum(dh * yg))
        dy = dh * (1.0 + sc_ref[...])
        _acc(dn2_ref, _rows_sum(dy * xh))
        dx_ref[...] = dxo_ref[...] + _rms_bwd(dy * g_ref[...], xh, rstd)

    dx, dsh, dsc, dn2 = _rowcall("ffn_bwd_in", in_body, n_rows, 2 * tm, [dxo, xmid, da, db], [n2g, sc2, wg, wu],
                                 [(d, F32)], [((1, d), F32)] * 3, temp_cols=5 * d)
    return dx, da, db, f, dd, dg2, dsh, dsc, dn2


def _mix_bwd(dxm, mo, ya, yb, ga, gb, ypre, o, g1, bglu, wglu, wa, wb, wout, tm):
    n_rows, d = dxm.shape
    sw = ypre.shape[1]

    def body(dxm_ref, mo_ref, ya_ref, yb_ref, ga_ref, gb_ref, y_ref, o_ref, g1_ref, bglu_ref, wglu_ref, wa_ref, wb_ref,
             wout_ref, dmo_ref, dya_ref, dyb_ref, dt_ref, ys_ref, dga_ref, dgb_ref, dy_ref, do_ref, delta_ref,
             dg1_ref, dbg_ref):
        dxm_v = dxm_ref[...]
        dmo = dxm_v * g1_ref[...]
        dmo_ref[...] = dmo.astype(_ACT)
        _acc(dg1_ref, _rows_sum(dxm_v * mo_ref[...]))
        dmg = _dot_nt(dmo, wout_ref[...])
        sa, sb = _sigmoid(ga_ref[...]), _sigmoid(gb_ref[...])
        dya, dyb = dmg * sa, dmg * sb
        dya_ref[...] = dya.astype(_ACT)
        dyb_ref[...] = dyb.astype(_ACT)
        dga_ref[...] = (dmg * ya_ref[...] * (sa * (1.0 - sa))).astype(_ACT)
        dgb_ref[...] = (dmg * yb_ref[...] * (sb * (1.0 - sb))).astype(_ACT)
        do = _dot_nt(dyb, wb_ref[...])
        do_ref[...] = do
        prod = do * o_ref[...]
        for h in range(N_HEADS):
            lanes = slice(h * HEAD_PAD, (h + 1) * HEAD_PAD)
            delta_ref[:, lanes] = jnp.broadcast_to(jnp.sum(prod[:, lanes], axis=1, keepdims=True), (prod.shape[0], HEAD_PAD))
        dyg = _dot_nt(dya, wa_ref[...])
        yv = y_ref[...]
        ys = _gelu(yv)
        ys_ref[...] = ys.astype(_ACT)
        sg = _sigmoid(_dot(ys, wglu_ref[...]) + bglu_ref[...])
        dt = dyg * ys * (sg * (1.0 - sg))
        dt_ref[...] = dt.astype(_ACT)
        _acc(dbg_ref, _rows_sum(dt))
        dys = dyg * sg + _dot_nt(dt, wglu_ref[...])
        dy_ref[...] = dys * _gelu_grad(yv)

    return _rowcall("mix_bwd", body, n_rows, tm, [dxm, mo, ya, yb, ga, gb, ypre, o], [g1, bglu, wglu, wa, wb, wout],
                    [(d, _ACT), (d, _ACT), (d, _ACT), (sw, _ACT), (sw, _ACT), (d, _ACT), (d, _ACT), (sw, F32), (QW, F32),
                     (QW, F32)],
                    [((1, d), F32), ((1, sw), F32)], temp_cols=6 * d)


def _attn_bwd(q, k, v, do, delta, lse, pos_col, pos_row, scale):
    n_rows = q.shape[0]
    ta = _tile(n_rows, 512, 128)
    nb = n_rows // ta
    qmap, kmap = _causal_steps(nb, key_major=True)
    hb = ATTN_BWD_HEADS_PER_STEP
    wide = hb * HEAD_PAD
    c2 = scale * LOG2_E

    def body(qm, km, q_ref, k_ref, v_ref, do_ref, delta_ref, lse_ref, pq_ref, pk_ref, dq_ref, dk_ref, dv_ref,
             dk_acc, dv_acc):
        s_id = pl.program_id(1)
        qi, ki = qm[s_id], km[s_id]

        @pl.when(s_id == 0)
        def _():
            dq_ref[...] = jnp.zeros_like(dq_ref)

        @pl.when(qi == ki)
        def _():
            dk_acc[...] = jnp.zeros_like(dk_acc)
            dv_acc[...] = jnp.zeros_like(dv_acc)

        rows = pl.ds(pl.multiple_of(qi * ta, ta), ta)

        def update(on_diagonal):
            if on_diagonal:
                visible = pk_ref[...] <= pq_ref[...]
            for h in range(hb):
                lanes = slice(h * HEAD_PAD, (h + 1) * HEAD_PAD)
                qv, kv, dov = q_ref[:, lanes], k_ref[:, lanes], do_ref[:, lanes]
                e = _dot_nt(qv, kv) * c2 - lse_ref[:, lanes][:, :1]
                if on_diagonal:
                    e = jnp.where(visible, e, NEG_INF)
                p = jnp.exp2(e)
                dp = _dot_nt(dov, v_ref[:, lanes])
                ds = p * (dp - delta_ref[:, lanes][:, :1])
                dv_acc[:, lanes] += _dot_tn(p, dov)
                dk_acc[:, lanes] += _dot_tn(ds, qv)
                dq_ref[rows, lanes] += _dot(ds, kv) * scale

        pl.when(ki != qi)(functools.partial(update, False))
        pl.when(ki == qi)(functools.partial(update, True))

        @pl.when(qi == nb - 1)
        def _():
            dk_ref[...] = dk_acc[...] * scale
            dv_ref[...] = dv_acc[...]

    qspec = pl.BlockSpec((ta, wide), lambda h, s, qm, km: (qm[s], h))
    kspec = pl.BlockSpec((ta, wide), lambda h, s, qm, km: (km[s], h))
    grid_spec = pltpu.PrefetchScalarGridSpec(
        num_scalar_prefetch=2, grid=(N_HEADS // hb, int(qmap.shape[0])),
        in_specs=[qspec, kspec, kspec, qspec, qspec, qspec,
                  pl.BlockSpec((ta, 1), lambda h, s, qm, km: (qm[s], 0)),
                  pl.BlockSpec((1, ta), lambda h, s, qm, km: (0, km[s]))],
        out_specs=[pl.BlockSpec((n_rows, wide), lambda h, s, qm, km: (0, h)), kspec, kspec],
        scratch_shapes=[pltpu.VMEM((ta, wide), F32), pltpu.VMEM((ta, wide), F32)])
    return pl.pallas_call(
        body, name="attn_bwd", grid_spec=grid_spec,
        out_shape=[jax.ShapeDtypeStruct((n_rows, QW), F32)] * 3,
        compiler_params=pltpu.CompilerParams(
            dimension_semantics=("parallel", "arbitrary"),
            vmem_limit_bytes=_vmem_limit(12 * _nbytes((ta, wide), F32) + _nbytes((n_rows, wide), F32),
                                         6 * hb * _nbytes((ta, ta), F32))),
    )(qmap, kmap, q, k, v, do, delta, lse, pos_col, pos_row)


def _mla_prep_bwd(dq, dk, dv, cq, ckv, rc, rs1, rs2, gq, gkv, wuq, wuk, wuv, nope, tm):
    n_rows = cq.shape[0]
    ql, kl = cq.shape[1], ckv.shape[1]

    def body(dq_ref, dk_ref, dv_ref, cq_ref, ckv_ref, c_ref, s1_ref, s2_ref, gq_ref, gkv_ref, wuq_ref, wuk_ref,
             wuv_ref, dqp_ref, dcq_ref, dckv_ref, dkr_ref, dgq_ref, dgkv_ref):
        c, s1, s2 = c_ref[...], s1_ref[...], s2_ref[...]
        c8, s18, s28 = (jnp.tile(a, (1, N_HEADS)) for a in (c, s1, s2))
        dqp = _rope_adjoint(dq_ref[...], c8, s18, s28)
        dqp_ref[...] = dqp.astype(_ACT)
        dcqn = _dot_nt(dqp, wuq_ref[...])
        xh, rstd = _rms_stats(cq_ref[...])
        _acc(dgq_ref, _rows_sum(dcqn * xh))
        dcq_ref[...] = _rms_bwd(dcqn * gq_ref[...], xh, rstd).astype(_ACT)
        dkv = dk_ref[...]
        dkpe = dkv[:, 0:HEAD_PAD]
        for h in range(1, N_HEADS):
            dkpe = dkpe + dkv[:, h * HEAD_PAD:(h + 1) * HEAD_PAD]
        lane = lax.broadcasted_iota(jnp.int32, dkpe.shape, 1)
        dkpe = jnp.where((lane >= nope) & (lane < nope + QK_ROPE), dkpe, 0.0)
        dkr_ref[...] = _rope_adjoint(dkpe, c, s1, s2).astype(_ACT)
        dckvn = _dot_nt(dkv, wuk_ref[...]) + _dot_nt(dv_ref[...], wuv_ref[...])
        xh, rstd = _rms_stats(ckv_ref[...])
        _acc(dgkv_ref, _rows_sum(dckvn * xh))
        dckv_ref[...] = _rms_bwd(dckvn * gkv_ref[...], xh, rstd).astype(_ACT)

    return _rowcall("mla_prep_bwd", body, n_rows, tm, [dq, dk, dv, cq, ckv, rc, rs1, rs2], [gq, gkv, wuq, wuk, wuv],
                    [(QW, _ACT), (ql, _ACT), (kl, _ACT), (HEAD_PAD, _ACT)], [((1, ql), F32), ((1, kl), F32)],
                    temp_cols=6 * QW)


def _ssm_bwd(dy, u, hre, him, bre_blk, bim_blk, abr_row, abi_row, cre_blk, cimneg_blk, d_row):
    n_rows, sw = u.shape
    gp = abr_row.shape[1]
    t = _tile(n_rows, 256)
    n_chunks = n_rows // t

    def body(dy_ref, u_ref, hre_ref, him_ref, hbre_ref, hbim_ref, bre_ref, bim_ref, ar_ref, ai_ref, cre_ref, cim_ref,
             d_ref, du_ref, gre_ref, gim_ref, dar_ref, dai_ref, dd_ref, g_re, g_im, hs_re, hs_im, cr, ci):
        i = pl.program_id(0)

        @pl.when(i == 0)
        def _():
            cr[...] = jnp.zeros_like(cr)
            ci[...] = jnp.zeros_like(ci)

        dyv = dy_ref[...]
        dyb = dyv.astype(_MM)
        g_re[...] = _dot_nt(dyb, cre_ref[...])
        g_im[...] = _dot_nt(dyb, cim_ref[...])
        a_r, a_i = ar_ref[...], ai_ref[...]

        def step(k, carry):
            nr, ni = carry
            row = pl.ds(t - 1 - k, 1)
            gr = g_re[row, :] + a_r * nr + a_i * ni
            gi = g_im[row, :] + a_r * ni - a_i * nr
            g_re[row, :] = gr
            g_im[row, :] = gi
            return gr, gi

        nr, ni = lax.fori_loop(0, t, step, (cr[0:1, :], ci[0:1, :]), unroll=8)
        cr[0:1, :] = nr
        ci[0:1, :] = ni
        gr_all, gi_all = g_re[...], g_im[...]
        gre_ref[...] = gr_all.astype(_ACT)
        gim_ref[...] = gi_all.astype(_ACT)
        du_ref[...] = (_dot_nt(gr_all, bre_ref[...]) + _dot_nt(gi_all, bim_ref[...]) + d_ref[...] * dyv).astype(_ACT)
        _acc(dd_ref, _rows_sum(dyv * u_ref[...]))
        is_first_chunk = i == n_chunks - 1
        hs_re[0:8, :] = jnp.where(is_first_chunk, 0.0, hbre_ref[...])
        hs_im[0:8, :] = jnp.where(is_first_chunk, 0.0, hbim_ref[...])
        hs_re[8:t + 8, :] = hre_ref[...]
        hs_im[8:t + 8, :] = him_ref[...]
        hp_re, hp_im = hs_re[pl.ds(7, t), :], hs_im[pl.ds(7, t), :]
        _acc(dar_ref, _rows_sum(gr_all * hp_re + gi_all * hp_im))
        _acc(dai_ref, _rows_sum(gi_all * hp_re - gr_all * hp_im))

    rev = lambda c: pl.BlockSpec((t, c), lambda i: (n_chunks - 1 - i, 0))
    before = pl.BlockSpec((8, gp), lambda i: (jnp.maximum((n_chunks - 1 - i) * (t // 8) - 1, 0), 0))
    full = lambda a: pl.BlockSpec(a.shape, lambda i: (0, 0), pipeline_mode=pl.Buffered(1))
    acc = lambda c: pl.BlockSpec((1, c), lambda i: (0, 0))
    blocks = 2 * _nbytes((t, sw), F32) + 2 * _nbytes((t, gp), F32) + _nbytes((t, sw), _ACT) + 2 * _nbytes((t, gp), _ACT)
    resident = 4 * _nbytes((sw, gp), _MM) + 4 * _nbytes((t + 8, gp), F32)
    return pl.pallas_call(
        body, name="ssm_bwd", grid=(n_chunks,),
        in_specs=[rev(sw), rev(sw), rev(gp), rev(gp), before, before, full(bre_blk), full(bim_blk), full(abr_row),
                  full(abi_row), full(cre_blk), full(cimneg_blk), full(d_row)],
        out_specs=[rev(sw), rev(gp), rev(gp), acc(gp), acc(gp), acc(sw)],
        out_shape=[jax.ShapeDtypeStruct((n_rows, sw), _ACT), jax.ShapeDtypeStruct((n_rows, gp), _ACT),
                   jax.ShapeDtypeStruct((n_rows, gp), _ACT), jax.ShapeDtypeStruct((1, gp), F32),
                   jax.ShapeDtypeStruct((1, gp), F32), jax.ShapeDtypeStruct((1, sw), F32)],
        scratch_shapes=[pltpu.VMEM((t, gp), F32), pltpu.VMEM((t, gp), F32), pltpu.VMEM((t + 8, gp), F32),
                        pltpu.VMEM((t + 8, gp), F32), pltpu.VMEM((8, gp), F32), pltpu.VMEM((8, gp), F32)],
        compiler_params=pltpu.CompilerParams(dimension_semantics=("arbitrary",),
                                             vmem_limit_bytes=_vmem_limit(blocks, resident + 4 * _nbytes((t, gp), F32))),
    )(dy, u, hre, him, hre, him, bre_blk, bim_blk, abr_row, abi_row, cre_blk, cimneg_blk, d_row)


def _in_bwd(dxm, x, dz_parts, n1g, sc1, w_parts, tm):
    n_rows, d = x.shape
    n = len(dz_parts)

    def body(dxm_ref, x_ref, *rest):
        dz_refs = rest[:n]
        g_ref, sc_ref = rest[n], rest[n + 1]
        w_refs = rest[n + 2:2 * n + 2]
        dx_ref, dsh_ref, dsc_ref, dn1_ref = rest[2 * n + 2:]
        dh = None
        for dz_ref, w_ref in zip(dz_refs, w_refs):
            term = _dot_nt(dz_ref[...], w_ref[...])
            dh = term if dh is None else dh + term
        xh, rstd = _rms_stats(x_ref[...])
        yg = xh * g_ref[...]
        _acc(dsh_ref, _rows_sum(dh))
        _acc(dsc_ref, _rows_sum(dh * yg))
        dy = dh * (1.0 + sc_ref[...])
        _acc(dn1_ref, _rows_sum(dy * xh))
        dx_ref[...] = dxm_ref[...] + _rms_bwd(dy * g_ref[...], xh, rstd)

    return _rowcall("in_bwd", body, n_rows, tm, [dxm, x, *dz_parts], [n1g, sc1, *w_parts],
                    [(d, F32)], [((1, d), F32)] * 3, temp_cols=5 * d)


def _pad_heads(w, per_head):
    lead = w.shape[:-1]
    w = w.reshape(lead + (N_HEADS, per_head))
    w = jnp.pad(w, [(0, 0)] * len(lead) + [(0, 0), (0, HEAD_PAD - per_head)])
    return w.reshape(lead + (QW,))


def _unpad_heads(w, per_head):
    lead = w.shape[:-1]
    return w.reshape(lead + (N_HEADS, HEAD_PAD))[..., :per_head].reshape(lead + (N_HEADS * per_head,))


def _cols_from_chips(g):
    ch, dep, r, cs = g.shape
    return g.transpose(1, 2, 0, 3).reshape(dep, r, ch * cs)


def _rows_from_chips(g):
    ch, dep, rs, c = g.shape
    return g.transpose(1, 0, 2, 3).reshape(dep, ch * rs, c)


def _cols_to_chips(w):
    dep, r, c = w.shape
    return w.reshape(dep, r, 4, c // 4).transpose(2, 0, 1, 3)


def _rows_to_chips(w):
    dep, r, c = w.shape
    return w.reshape(dep, 4, r // 4, c).transpose(1, 0, 2, 3)


def _block_diag(b_gxy):
    g, xx, yy = b_gxy.shape
    eye = jnp.eye(g, dtype=b_gxy.dtype)
    return (b_gxy[:, :, None, :] * eye[:, None, :, None]).reshape(g * xx, g * yy)


def _block_diag_extract(full, g):
    xx, yy = full.shape[0] // g, full.shape[1] // g
    eye = jnp.eye(g, dtype=full.dtype)
    return jnp.sum(full.reshape(g, xx, g, yy) * eye[:, None, :, None], axis=2)


def _pack_rows(arrays):
    parts = []
    for a in arrays:
        flat = a.reshape(-1)
        flat = jnp.pad(flat, (0, (-flat.shape[0]) % 1024))
        parts.append(flat.reshape(-1, 128))
    return jnp.concatenate(parts, axis=0)


def _unpack_rows(packed, shapes):
    out, row = [], 0
    for s in shapes:
        n = int(np.prod(s))
        rows = -(-n // 1024) * 8
        out.append(packed[row:row + rows].reshape(-1)[:n].reshape(s))
        row += rows
    return out


def kernel(x, c, positions, w_ada, b_ada, norm1_g, w_in, ssm_a_re, ssm_a_im, ssm_log_dt, ssm_b_re, ssm_b_im, ssm_c_re, ssm_c_im, ssm_d, w_glu, b_glu, w_a_out, q_norm_g, w_uq, kv_norm_g, w_uk, w_uv, w_b_out, w_out, norm2_g, w_gate, w_up, w_down, final_g, loss_target, m_w_ada, m_b_ada, m_norm1_g, m_w_in, m_ssm_a_re, m_ssm_a_im, m_ssm_log_dt, m_ssm_b_re, m_ssm_b_im, m_ssm_c_re, m_ssm_c_im, m_ssm_d, m_w_glu, m_b_glu, m_w_a_out, m_q_norm_g, m_w_uq, m_kv_norm_g, m_w_uk, m_w_uv, m_w_b_out, m_w_out, m_norm2_g, m_w_gate, m_w_up, m_w_down, m_final_g, v_w_ada, v_b_ada, v_norm1_g, v_w_in, v_ssm_a_re, v_ssm_a_im, v_ssm_log_dt, v_ssm_b_re, v_ssm_b_im, v_ssm_c_re, v_ssm_c_im, v_ssm_d, v_w_glu, v_b_glu, v_w_a_out, v_q_norm_g, v_w_uq, v_kv_norm_g, v_w_uk, v_w_uv, v_w_b_out, v_w_out, v_norm2_g, v_w_gate, v_w_up, v_w_down, v_final_g):
    weights = dict(w_ada=w_ada, b_ada=b_ada, norm1_g=norm1_g, w_in=w_in, ssm_a_re=ssm_a_re, ssm_a_im=ssm_a_im, ssm_log_dt=ssm_log_dt, ssm_b_re=ssm_b_re, ssm_b_im=ssm_b_im, ssm_c_re=ssm_c_re, ssm_c_im=ssm_c_im, ssm_d=ssm_d, w_glu=w_glu, b_glu=b_glu, w_a_out=w_a_out, q_norm_g=q_norm_g, w_uq=w_uq, kv_norm_g=kv_norm_g, w_uk=w_uk, w_uv=w_uv, w_b_out=w_b_out, w_out=w_out, norm2_g=norm2_g, w_gate=w_gate, w_up=w_up, w_down=w_down, final_g=final_g)
    mom_m = dict(w_ada=m_w_ada, b_ada=m_b_ada, norm1_g=m_norm1_g, w_in=m_w_in, ssm_a_re=m_ssm_a_re, ssm_a_im=m_ssm_a_im, ssm_log_dt=m_ssm_log_dt, ssm_b_re=m_ssm_b_re, ssm_b_im=m_ssm_b_im, ssm_c_re=m_ssm_c_re, ssm_c_im=m_ssm_c_im, ssm_d=m_ssm_d, w_glu=m_w_glu, b_glu=m_b_glu, w_a_out=m_w_a_out, q_norm_g=m_q_norm_g, w_uq=m_w_uq, kv_norm_g=m_kv_norm_g, w_uk=m_w_uk, w_uv=m_w_uv, w_b_out=m_w_b_out, w_out=m_w_out, norm2_g=m_norm2_g, w_gate=m_w_gate, w_up=m_w_up, w_down=m_w_down, final_g=m_final_g)
    mom_v = dict(w_ada=v_w_ada, b_ada=v_b_ada, norm1_g=v_norm1_g, w_in=v_w_in, ssm_a_re=v_ssm_a_re, ssm_a_im=v_ssm_a_im, ssm_log_dt=v_ssm_log_dt, ssm_b_re=v_ssm_b_re, ssm_b_im=v_ssm_b_im, ssm_c_re=v_ssm_c_re, ssm_c_im=v_ssm_c_im, ssm_d=v_ssm_d, w_glu=v_w_glu, b_glu=v_b_glu, w_a_out=v_w_a_out, q_norm_g=v_q_norm_g, w_uq=v_w_uq, kv_norm_g=v_kv_norm_g, w_uk=v_w_uk, w_uv=v_w_uv, w_b_out=v_w_b_out, w_out=v_w_out, norm2_g=v_norm2_g, w_gate=v_w_gate, w_up=v_w_up, w_down=v_w_down, final_g=v_final_g)
    names = list(weights)

    depth = w_in.shape[0]
    seq, d = x.shape[1], x.shape[2]
    sw = ssm_d.shape[1]
    groups, n_state, n_chan = ssm_b_re.shape[1:]
    gp = groups * n_state
    ql, kl = q_norm_g.shape[1], kv_norm_g.shape[1]
    nope = w_uk.shape[2] * 4 // N_HEADS
    vdim = w_uv.shape[2] * 4 // N_HEADS
    qk_dim = nope + QK_ROPE
    scale = qk_dim ** -0.5
    tm = _tile(seq, 256, 16)
    tm_ffn = _tile(seq, 128, 16)
    me = 4 * lax.axis_index("x") + 2 * lax.axis_index("y") + lax.axis_index("c")
    chip = 2 * lax.axis_index("x") + lax.axis_index("y")

    xs = x.reshape(seq, d)
    target = loss_target.reshape(seq, d)
    pos_f = positions.astype(F32)
    pos_col = pos_f.reshape(seq, 1)
    pos_row = pos_f.reshape(1, seq)

    (c_all,) = _exchange("gather_c", "gather8", [c])
    c_all = c_all.reshape(8, d)
    ada_cols = w_ada.shape[2]
    mod_part = _mod_fwd(c_all, w_ada.reshape(depth * d, ada_cols), depth)
    (mod_all,) = _exchange("gather_mod", "gather8", [mod_part])
    mod_all = mod_all.reshape(4, 2, depth, 8, ada_cols)[:, 0]
    mod_me = lax.dynamic_index_in_dim(mod_all, me, axis=2, keepdims=False)
    mod = mod_me.transpose(1, 0, 2).reshape(depth, 4 * ada_cols) + b_ada
    mod = mod.reshape(depth, 6, 1, d)

    big = ["w_in", "w_glu", "w_a_out", "w_uq", "w_uk", "w_uv", "w_b_out", "w_out", "w_gate", "w_up", "w_down"]
    row_sharded = {"w_glu", "w_out", "w_down"}
    gathered = _gather_chips_two_level("gather_weights", [weights[n].astype(_MM) for n in big])
    full = {n: (_rows_from_chips(g) if n in row_sharded else _cols_from_chips(g)) for n, g in zip(big, gathered)}
    o1, o2, o3, o4, o5 = sw, sw + ql, sw + ql + kl, sw + ql + kl + QK_ROPE, sw + ql + kl + QK_ROPE + d
    wi = full["w_in"]
    w_u, w_cq, w_ckv, w_ga, w_gb = wi[:, :, :o1], wi[:, :, o1:o2], wi[:, :, o2:o3], wi[:, :, o4:o5], wi[:, :, o5:]
    w_kr = jnp.pad(wi[:, :, o3:o4], ((0, 0), (0, 0), (nope, HEAD_PAD - nope - QK_ROPE)))
    wuq_p = _pad_heads(full["w_uq"], qk_dim)
    wuk_p = _pad_heads(full["w_uk"], nope)
    wuv_p = _pad_heads(full["w_uv"], vdim)
    wb_p = _pad_heads(full["w_b_out"].transpose(0, 2, 1), vdim).transpose(0, 2, 1)

    inv_freq = ROPE_BASE ** (-jnp.arange(0, QK_ROPE, 2, dtype=F32) / QK_ROPE)
    inv_lane = jnp.pad(jnp.concatenate([inv_freq, inv_freq]), (nope, HEAD_PAD - nope - QK_ROPE)).reshape(1, HEAD_PAD)
    rc, rs1, rs2 = _rope_tables(pos_col, inv_lane, nope)
    a_re_col = ssm_a_re.reshape(depth * gp, 1)
    a_im_col = ssm_a_im.reshape(depth * gp, 1)
    ldt_col = jnp.broadcast_to(ssm_log_dt[:, :, None], (depth, groups, n_state)).reshape(depth * gp, 1)
    b_re2, b_im2 = ssm_b_re.reshape(depth * gp, n_chan), ssm_b_im.reshape(depth * gp, n_chan)
    abr, abi, bbr, bbi = _ssm_disc_fwd(a_re_col, a_im_col, ldt_col, b_re2, b_im2)
    abr_rows, abi_rows = abr.reshape(depth, 1, gp), abi.reshape(depth, 1, gp)
    bbr, bbi = bbr.reshape(depth, groups, n_state, n_chan), bbi.reshape(depth, groups, n_state, n_chan)

    saved = []
    xl = xs
    for l in range(depth):
        sh1, sc1, g1, sh2, sc2, g2 = (mod[l, j] for j in range(6))
        n1g, n2g = norm1_g[l].reshape(1, d), norm2_g[l].reshape(1, d)
        w_parts = [w_u[l], w_cq[l], w_ckv[l], w_kr[l], w_ga[l], w_gb[l]]
        hb, u, cq, ckv, kr, ga, gb = _in_fwd(xl, n1g, sc1, sh1, w_parts, tm)
        bre_blk = _block_diag(bbr[l].transpose(0, 2, 1)).astype(_MM)
        bim_blk = _block_diag(bbi[l].transpose(0, 2, 1)).astype(_MM)
        cre_blk = _block_diag(ssm_c_re[l].transpose(0, 2, 1)).astype(_MM)
        cimneg_blk = _block_diag(-ssm_c_im[l].transpose(0, 2, 1)).astype(_MM)
        d_row = ssm_d[l].reshape(1, sw)
        ssm_w = (bre_blk, bim_blk, abr_rows[l], abi_rows[l], cre_blk, cimneg_blk, d_row)
        ypre, hre, him = _ssm_fwd(u, *ssm_w)
        gq, gkv = q_norm_g[l].reshape(1, ql), kv_norm_g[l].reshape(1, kl)
        q, k, v, cqn, ckvn = _mla_prep_fwd(cq, ckv, kr, rc, rs1, rs2, gq, gkv, wuq_p[l], wuk_p[l], wuv_p[l], vdim, tm)
        o, lse = _attn_fwd(q, k, v, pos_col, pos_row, scale, vdim)
        bglu = b_glu[l].reshape(1, sw)
        yg, ya, yb, merged, mo, xmid = _mix_fwd(ypre, o, ga, gb, xl, g1, bglu, full["w_glu"][l], full["w_a_out"][l],
                                                 wb_p[l], full["w_out"][l], tm)
        hb2, fa, fb, dn, xout = _ffn_fwd(xmid, n2g, sc2, sh2, g2, full["w_gate"][l], full["w_up"][l], full["w_down"][l], tm_ffn)
        saved.append(dict(x=xl, hb=hb, u=u, cq=cq, ckv=ckv, ga=ga, gb=gb, ssm_w=ssm_w, ypre=ypre, hre=hre, him=him,
                          q=q, k=k, v=v, cqn=cqn, ckvn=ckvn, o=o, lse=lse, yg=yg, ya=ya, yb=yb, merged=merged, mo=mo,
                          xmid=xmid, hb2=hb2, fa=fa, fb=fb, dn=dn, w_parts=w_parts))
        xl = xout

    dx, loss_acc, g_final = _head(xl, final_g.reshape(1, d), target, tm)
    loss = lax.psum(loss_acc[0, 0], ("x", "y", "c"))

    per_layer = ["w_gate", "w_up", "w_down", "norm2_g", "w_out", "w_a_out", "w_b_out", "w_glu", "b_glu", "w_uq", "w_uk",
                 "w_uv", "q_norm_g", "kv_norm_g", "ssm_d", "ssm_c_re", "ssm_c_im", "w_in", "norm1_g"]
    grads = {n: [None] * depth for n in per_layer}
    dmod = [None] * depth
    for l in reversed(range(depth)):
        s = saved[l]
        sh1, sc1, g1, sh2, sc2, g2 = (mod[l, j] for j in range(6))
        n1g, n2g = norm1_g[l].reshape(1, d), norm2_g[l].reshape(1, d)
        dxm, da, db, fb16, dd, dg2, dsh2, dsc2, dn2 = _ffn_bwd(
            dx, s["xmid"], s["fa"], s["fb"], s["dn"], n2g, sc2, g2, full["w_gate"][l], full["w_up"][l], full["w_down"][l], tm_ffn)
        grads["w_gate"][l] = _mm_tn("dw_gate", s["hb2"], da)
        grads["w_up"][l] = _mm_tn("dw_up", s["hb2"], db)
        grads["w_down"][l] = _mm_tn("dw_down", fb16, dd)
        grads["norm2_g"][l] = dn2.reshape(d)

        bglu = b_glu[l].reshape(1, sw)
        dmo, dya, dyb, dt, ys, dga, dgb, dypre, do, delta, dg1, dbglu = _mix_bwd(
            dxm, s["mo"], s["ya"], s["yb"], s["ga"], s["gb"], s["ypre"], s["o"], g1, bglu, full["w_glu"][l],
            full["w_a_out"][l], wb_p[l], full["w_out"][l], tm)
        grads["w_out"][l] = _mm_tn("dw_out", s["merged"], dmo)
        grads["w_a_out"][l] = _mm_tn("dw_a_out", s["yg"], dya)
        dwb_p = _mm_tn("dw_b_out", s["o"], dyb)
        grads["w_b_out"][l] = _unpad_heads(dwb_p.T, vdim).T
        grads["w_glu"][l] = _mm_tn("dw_glu", ys, dt)
        grads["b_glu"][l] = dbglu.reshape(sw)

        dq, dk, dv = _attn_bwd(s["q"], s["k"], s["v"], do, delta, s["lse"], pos_col, pos_row, scale)
        gq, gkv = q_norm_g[l].reshape(1, ql), kv_norm_g[l].reshape(1, kl)
        dqp, dcq, dckv, dkr, dgq, dgkv = _mla_prep_bwd(dq, dk, dv, s["cq"], s["ckv"], rc, rs1, rs2, gq, gkv,
                                                       wuq_p[l], wuk_p[l], wuv_p[l], nope, tm)
        grads["w_uq"][l] = _unpad_heads(_mm_tn("dw_uq", s["cqn"], dqp), qk_dim)
        grads["w_uk"][l] = _unpad_heads(_mm_tn("dw_uk", s["ckvn"], dk), nope)
        grads["w_uv"][l] = _unpad_heads(_mm_tn("dw_uv", s["ckvn"], dv), vdim)
        grads["q_norm_g"][l] = dgq.reshape(ql)
        grads["kv_norm_g"][l] = dgkv.reshape(kl)

        du, gre, gim, dar, dai, ddskip = _ssm_bwd(dypre, s["u"], s["hre"], s["him"], *s["ssm_w"])
        grads["ssm_d"][l] = ddskip.reshape(sw)
        d_bre = _block_diag_extract(_mm_tn("d_bre", s["u"], gre), groups).transpose(0, 2, 1)
        d_bim = _block_diag_extract(_mm_tn("d_bim", s["u"], gim), groups).transpose(0, 2, 1)
        grads["ssm_c_re"][l] = _block_diag_extract(_mm_tn("d_cre", s["hre"], dypre), groups).transpose(0, 2, 1)
        grads["ssm_c_im"][l] = -_block_diag_extract(_mm_tn("d_cim", s["him"], dypre), groups).transpose(0, 2, 1)
        s["disc_grads"] = (dar.reshape(gp, 1), dai.reshape(gp, 1), d_bre.reshape(gp, n_chan), d_bim.reshape(gp, n_chan))

        dz_parts = [du, dcq, dckv, dkr, dga, dgb]
        dx, dsh1, dsc1, dn1 = _in_bwd(dxm, s["x"], dz_parts, n1g, sc1, s["w_parts"], tm)
        dw_parts = [_mm_tn("dw_in_%d" % j, s["hb"], dz) for j, dz in enumerate(dz_parts)]
        dw_parts[3] = dw_parts[3][:, nope:nope + QK_ROPE]
        grads["w_in"][l] = jnp.concatenate(dw_parts, axis=1)
        grads["norm1_g"][l] = dn1.reshape(d)
        dmod[l] = jnp.concatenate([dsh1, dsc1, dg1, dsh2, dsc2, dg2], axis=1).reshape(6 * d)
    grad_x = dx.reshape(x.shape)

    disc = [jnp.concatenate([saved[l]["disc_grads"][j] for l in range(depth)], axis=0) for j in range(4)]
    da_re, da_im, dldt, db_re, db_im = _ssm_disc_bwd(a_re_col, a_im_col, ldt_col, b_re2, b_im2, *disc)
    stacked = {n: jnp.stack(v) for n, v in grads.items()}
    stacked["ssm_a_re"] = da_re.reshape(ssm_a_re.shape)
    stacked["ssm_a_im"] = da_im.reshape(ssm_a_im.shape)
    stacked["ssm_log_dt"] = _lane_sum(dldt.reshape(depth * groups, n_state)).reshape(ssm_log_dt.shape)
    stacked["ssm_b_re"] = db_re.reshape(ssm_b_re.shape)
    stacked["ssm_b_im"] = db_im.reshape(ssm_b_im.shape)
    stacked["final_g"] = g_final.reshape(d)
    stacked["b_ada"] = jnp.stack(dmod)

    small = [n for n in names if n not in big and n != "w_ada"]
    small_shapes = [weights[n].shape for n in small]
    (small_all,) = _exchange("gather_small", "gather8", [_pack_rows([stacked[n] for n in small])])
    sg, sd, sm, sv = _adamw("adamw_small", [small_all], _pack_rows([weights[n] for n in small]),
                            _pack_rows([mom_m[n] for n in small]), _pack_rows([mom_v[n] for n in small]))
    out_g = dict(zip(small, _unpack_rows(sg, small_shapes)))
    out_d = dict(zip(small, _unpack_rows(sd, small_shapes)))
    out_m = dict(zip(small, _unpack_rows(sm, small_shapes)))
    out_v = dict(zip(small, _unpack_rows(sv, small_shapes)))

    n_dmod = depth * 6 * d
    dmod_all = small_all[:, :n_dmod // 128].reshape(8, depth, 6 * d)
    dmod_cols = lax.dynamic_slice_in_dim(dmod_all, chip * ada_cols, ada_cols, axis=2)
    g_wada = _wada_bwd(c_all, dmod_cols.transpose(1, 0, 2).reshape(depth * 8, ada_cols), depth)
    res = _adamw("adamw_w_ada", [g_wada], w_ada.reshape(depth * d, ada_cols), m_w_ada.reshape(depth * d, ada_cols),
                 v_w_ada.reshape(depth * d, ada_cols))
    out_g["w_ada"], out_d["w_ada"], out_m["w_ada"], out_v["w_ada"] = (r.reshape(w_ada.shape) for r in res)

    to_chips = [(_rows_to_chips if n in row_sharded else _cols_to_chips)(stacked[n]).astype(_WIRE) for n in big]
    landed = _exchange("scatter_grads", "scatter4", to_chips)
    partial = [_sum_slots("sum_" + n, r.reshape(4, -1, r.shape[-1])) for n, r in zip(big, landed)]
    sibling = _exchange("swap_partials", "swap", partial)
    for n, mine, theirs in zip(big, partial, sibling):
        shp = weights[n].shape
        as2d = lambda a: a.reshape(-1, shp[-1])
        res = _adamw("adamw_" + n, [mine, theirs], as2d(weights[n]), as2d(mom_m[n]), as2d(mom_v[n]))
        out_g[n], out_d[n], out_m[n], out_v[n] = (r.reshape(shp) for r in res)

    return (loss, grad_x, *[out_g[n] for n in names], *[out_d[n] for n in names], *[out_m[n] for n in names],
            *[out_v[n] for n in names])
```

```python
import functools
import math

import numpy as np
import jax
import jax.numpy as jnp
from jax import lax
from jax.experimental import pallas as pl
from jax.experimental.pallas import tpu as pltpu

F32 = jnp.float32
_MM = jnp.bfloat16
_ACT = jnp.bfloat16
_WIRE = jnp.bfloat16

N_HEADS = 8
QK_ROPE = 32
HEAD_PAD = 128
QW = N_HEADS * HEAD_PAD
ROPE_BASE = 10000.0
EPS = 1e-6
DT_MIN = 1e-3
ADAM_LR = 0.001
ADAM_B1 = 0.9
ADAM_B2 = 0.999
ADAM_EPS = 1e-08
ADAM_WD = 0.01
ADAM_STEP = 10
NEG_INF = -1e30
LOG2_E = math.log2(math.e)
ATTN_HEADS_PER_STEP = 4
ATTN_BWD_HEADS_PER_STEP = 2

V7X_VMEM_BYTES = 64 * 1024 * 1024
VMEM_RESERVE_BYTES = 6 * 1024 * 1024
MESH = pl.DeviceIdType.MESH
ANY = pl.BlockSpec(memory_space=pl.ANY)


def _vmem_limit(block_bytes, temp_bytes):
    want = 2 * block_bytes + temp_bytes
    return int(min(V7X_VMEM_BYTES - VMEM_RESERVE_BYTES, max(want, 32 * 1024 * 1024)))


def _nbytes(shape, dtype):
    return int(np.prod(shape)) * jnp.dtype(dtype).itemsize


def _tile(n, target, mult=8):
    t = min(n, target)
    while t >= mult:
        if n % t == 0 and t % mult == 0:
            return t
        t -= 1
    return n


def _dot(a, b):
    return jnp.dot(a.astype(_MM), b.astype(_MM), preferred_element_type=F32)


def _dot_nt(a, b):
    return lax.dot_general(a.astype(_MM), b.astype(_MM), (((1,), (1,)), ((), ())), preferred_element_type=F32)


def _dot_tn(a, b):
    return lax.dot_general(a.astype(_MM), b.astype(_MM), (((0,), (0,)), ((), ())), preferred_element_type=F32)


def _sigmoid(x):
    return jax.nn.sigmoid(x)


_GELU_K = math.sqrt(2.0 / math.pi)


def _gelu(x):
    return x * (0.5 * (1.0 + jnp.tanh(_GELU_K * (x + 0.044715 * (x * x * x)))))


def _gelu_grad(x):
    th = jnp.tanh(_GELU_K * (x + 0.044715 * (x * x * x)))
    return 0.5 * (1.0 + th) + 0.5 * x * (1.0 - th * th) * (_GELU_K * (1.0 + 3.0 * 0.044715 * (x * x)))


def _rows_sum(v):
    return jnp.sum(v, axis=0, keepdims=True)


def _rms_stats(x):
    rstd = lax.rsqrt(jnp.mean(x * x, axis=-1, keepdims=True) + EPS)
    return x * rstd, rstd


def _rms_bwd(dxh, xh, rstd):
    return rstd * (dxh - xh * jnp.mean(dxh * xh, axis=-1, keepdims=True))


def _rowcall(name, body, n_rows, tm, row_ins, full_ins, row_outs, acc_outs=(), temp_cols=0):
    grid = (n_rows // tm,)
    in_specs = [pl.BlockSpec((tm, a.shape[1]), lambda i: (i, 0)) for a in row_ins]
    in_specs += [pl.BlockSpec(a.shape, lambda i: (0, 0), pipeline_mode=pl.Buffered(1)) for a in full_ins]
    out_shape = [jax.ShapeDtypeStruct((n_rows, c), dt) for c, dt in row_outs]
    out_shape += [jax.ShapeDtypeStruct(s, dt) for s, dt in acc_outs]
    out_specs = [pl.BlockSpec((tm, c), lambda i: (i, 0)) for c, _ in row_outs]
    out_specs += [pl.BlockSpec(s, lambda i: (0, 0)) for s, _ in acc_outs]
    blocks = sum(_nbytes((tm, a.shape[1]), a.dtype) for a in row_ins)
    blocks += sum(_nbytes((tm, c), dt) for c, dt in row_outs) + sum(_nbytes(s, dt) for s, dt in acc_outs)
    resident = sum(_nbytes(a.shape, a.dtype) for a in full_ins)
    limit = _vmem_limit(blocks, resident + _nbytes((tm, temp_cols), F32))
    res = pl.pallas_call(
        body, name=name, grid=grid, in_specs=in_specs, out_specs=out_specs, out_shape=out_shape,
        compiler_params=pltpu.CompilerParams(
            dimension_semantics=("arbitrary" if acc_outs else "parallel",), vmem_limit_bytes=limit),
    )(*row_ins, *full_ins)
    return res


def _first_step():
    return pl.program_id(0) == 0


def _acc(ref, val):
    @pl.when(_first_step())
    def _():
        ref[...] = val

    @pl.when(jnp.logical_not(_first_step()))
    def _():
        ref[...] += val


def _mm_tn(name, a, g):
    n_rows, k = a.shape
    n = g.shape[1]
    tk = k if k <= 1024 else _tile(k, 1408, 128)
    tn = n if n <= 1024 else _tile(n, 1408, 128)
    tl = _tile(n_rows, 1024, 16)

    def body(a_ref, g_ref, o_ref):
        @pl.when(pl.program_id(2) == 0)
        def _():
            o_ref[...] = jnp.zeros_like(o_ref)
        o_ref[...] += _dot_tn(a_ref[...], g_ref[...])

    blocks = _nbytes((tl, tk), a.dtype) + _nbytes((tl, tn), g.dtype) + _nbytes((tk, tn), F32)
    return pl.pallas_call(
        body, name=name, grid=(k // tk, n // tn, n_rows // tl),
        in_specs=[pl.BlockSpec((tl, tk), lambda i, j, l: (l, i)), pl.BlockSpec((tl, tn), lambda i, j, l: (l, j))],
        out_specs=pl.BlockSpec((tk, tn), lambda i, j, l: (i, j)),
        out_shape=jax.ShapeDtypeStruct((k, n), F32),
        compiler_params=pltpu.CompilerParams(
            dimension_semantics=("parallel", "parallel", "arbitrary"),
            vmem_limit_bytes=_vmem_limit(blocks, 2 * _nbytes((tl, max(tk, tn)), F32) + _nbytes((tk, tn), F32))),
    )(a, g)


def _place():
    return lax.axis_index("x"), lax.axis_index("y"), lax.axis_index("c")


def _flip(v, bit):
    return 1 - v if bit else v


def _exchange(name, mode, arrays):
    n = len(arrays)
    if mode == "gather8":
        rel = [((k >> 2) & 1, (k >> 1) & 1, k & 1) for k in range(1, 8)]
        out_shape = [jax.ShapeDtypeStruct((8,) + a.shape, a.dtype) for a in arrays]
    elif mode == "scatter4":
        rel = [((k >> 1) & 1, k & 1, 0) for k in range(1, 4)]
        out_shape = [jax.ShapeDtypeStruct(a.shape, a.dtype) for a in arrays]
    else:
        rel = [(0, 0, 1)]
        out_shape = [jax.ShapeDtypeStruct(a.shape, a.dtype) for a in arrays]
    n_rel = len(rel)

    def body(*refs):
        ins, outs = refs[:n], refs[n:2 * n]
        send_sems, recv_sems, local_sems = refs[2 * n:]
        x, y, c = _place()

        def slot(px, py, pc):
            return 4 * px + 2 * py + pc if mode == "gather8" else 2 * px + py

        mine = slot(x, y, c)
        local = []
        if mode != "swap":
            for a in range(n):
                src = ins[a].at[mine] if mode == "scatter4" else ins[a]
                local.append(pltpu.make_async_copy(src, outs[a].at[mine], local_sems.at[a]))
            for cp in local:
                cp.start()

        def remote(r, a):
            px, py, pc = _flip(x, rel[r][0]), _flip(y, rel[r][1]), _flip(c, rel[r][2])
            theirs = slot(px, py, pc)
            if mode == "swap":
                src, dst_there, dst_here = ins[a], outs[a], outs[a]
            elif mode == "scatter4":
                src, dst_there, dst_here = ins[a].at[theirs], outs[a].at[mine], outs[a].at[theirs]
            else:
                src, dst_there, dst_here = ins[a], outs[a].at[mine], outs[a].at[theirs]
            k = r * n + a
            push = pltpu.make_async_remote_copy(src_ref=src, dst_ref=dst_there, send_sem=send_sems.at[k],
                                                recv_sem=recv_sems.at[k], device_id=(px, py, pc), device_id_type=MESH)
            land = pltpu.make_async_remote_copy(src_ref=src, dst_ref=dst_here, send_sem=send_sems.at[k],
                                                recv_sem=recv_sems.at[k], device_id=(px, py, pc), device_id_type=MESH)
            return push, land

        copies = [remote(r, a) for r in range(n_rel) for a in range(n)]
        for push, _ in copies:
            push.start()
        for _, land in copies:
            land.wait_recv()
        for push, _ in copies:
            push.wait_send()
        for cp in local:
            cp.wait()

    return pl.pallas_call(
        body, name=name, in_specs=[ANY] * n, out_specs=[ANY] * n, out_shape=out_shape,
        scratch_shapes=[pltpu.SemaphoreType.DMA((n_rel * n,)), pltpu.SemaphoreType.DMA((n_rel * n,)),
                        pltpu.SemaphoreType.DMA((max(n, 1),))],
    )(*arrays)


def _gather_chips_two_level(name, arrays):
    n = len(arrays)
    rel = [((k >> 1) & 1, k & 1) for k in range(1, 4)]
    halves = [a.shape[0] // 2 for a in arrays]

    def body(*refs):
        ins, outs = refs[:n], refs[n:2 * n]
        ici_send, ici_recv, d2d_send, d2d_recv, local_sems = refs[2 * n:]
        x, y, c = _place()
        mine = 2 * x + y
        local = [pltpu.make_async_copy(ins[a], outs[a].at[mine], local_sems.at[a]) for a in range(n)]
        for cp in local:
            cp.start()

        def ici(r, a):
            px, py = _flip(x, rel[r][0]), _flip(y, rel[r][1])
            theirs = 2 * px + py
            rows = pl.ds(c * halves[a], halves[a])
            k = r * n + a
            push = pltpu.make_async_remote_copy(src_ref=ins[a].at[rows], dst_ref=outs[a].at[mine, rows],
                                                send_sem=ici_send.at[k], recv_sem=ici_recv.at[k],
                                                device_id=(px, py, c), device_id_type=MESH)
            land = pltpu.make_async_remote_copy(src_ref=ins[a].at[rows], dst_ref=outs[a].at[theirs, rows],
                                                send_sem=ici_send.at[k], recv_sem=ici_recv.at[k],
                                                device_id=(px, py, c), device_id_type=MESH)
            there = pl.ds((1 - c) * halves[a], halves[a])
            forward = pltpu.make_async_remote_copy(src_ref=outs[a].at[theirs, rows], dst_ref=outs[a].at[theirs, rows],
                                                   send_sem=d2d_send.at[k], recv_sem=d2d_recv.at[k],
                                                   device_id=(x, y, 1 - c), device_id_type=MESH)
            back = pltpu.make_async_remote_copy(src_ref=outs[a].at[theirs, rows], dst_ref=outs[a].at[theirs, there],
                                                send_sem=d2d_send.at[k], recv_sem=d2d_recv.at[k],
                                                device_id=(x, y, 1 - c), device_id_type=MESH)
            return push, land, forward, back

        copies = [ici(r, a) for r in range(len(rel)) for a in range(n)]
        for push, _, _, _ in copies:
            push.start()
        for _, land, forward, _ in copies:
            land.wait_recv()
            forward.start()
        for _, _, _, back in copies:
            back.wait_recv()
        for push, _, forward, _ in copies:
            push.wait_send()
            forward.wait_send()
        for cp in local:
            cp.wait()

    n_sem = len(rel) * n
    return pl.pallas_call(
        body, name=name, in_specs=[ANY] * n, out_specs=[ANY] * n,
        out_shape=[jax.ShapeDtypeStruct((4,) + a.shape, a.dtype) for a in arrays],
        scratch_shapes=[pltpu.SemaphoreType.DMA((n_sem,))] * 4 + [pltpu.SemaphoreType.DMA((n,))],
    )(*arrays)


def _sum_slots(name, stacked):
    p, rows, cols = stacked.shape
    tr = _tile(rows, 256)

    def body(s_ref, o_ref):
        acc = s_ref[0].astype(F32)
        for j in range(1, p):
            acc = acc + s_ref[j].astype(F32)
        o_ref[...] = acc

    return pl.pallas_call(
        body, name=name, grid=(rows // tr,),
        in_specs=[pl.BlockSpec((p, tr, cols), lambda i: (0, i, 0))],
        out_specs=pl.BlockSpec((tr, cols), lambda i: (i, 0)),
        out_shape=jax.ShapeDtypeStruct((rows, cols), F32),
        compiler_params=pltpu.CompilerParams(dimension_semantics=("parallel",)),
    )(stacked)


def _add_pair(name, a, b):
    rows, cols = a.shape
    tr = _tile(rows, 512, 16)

    def body(a_ref, b_ref, o_ref):
        o_ref[...] = (a_ref[...].astype(F32) + b_ref[...].astype(F32)).astype(o_ref.dtype)

    spec = pl.BlockSpec((tr, cols), lambda i: (i, 0))
    return pl.pallas_call(
        body, name=name, grid=(rows // tr,), in_specs=[spec, spec], out_specs=spec,
        out_shape=jax.ShapeDtypeStruct((rows, cols), a.dtype),
        compiler_params=pltpu.CompilerParams(dimension_semantics=("parallel",)),
    )(a, b)


def _adamw(name, parts, w, m, v):
    rows, cols = w.shape
    tr = _tile(rows, 256)
    n_parts = len(parts)

    def body(*refs):
        part_refs = refs[:n_parts]
        w_ref, m_ref, v_ref, g_out, d_out, m_out, v_out = refs[n_parts:]
        g = None
        for pr in part_refs:
            if len(pr.shape) == 3:
                for j in range(pr.shape[0]):
                    g = pr[j] if g is None else g + pr[j]
            else:
                g = pr[...] if g is None else g + pr[...]
        m_new = ADAM_B1 * m_ref[...] + (1.0 - ADAM_B1) * g
        v_new = ADAM_B2 * v_ref[...] + (1.0 - ADAM_B2) * jnp.square(g)
        m_hat = m_new / (1.0 - ADAM_B1 ** ADAM_STEP)
        v_hat = v_new / (1.0 - ADAM_B2 ** ADAM_STEP)
        g_out[...] = g
        d_out[...] = -ADAM_LR * (m_hat / (jnp.sqrt(v_hat) + ADAM_EPS) + ADAM_WD * w_ref[...])
        m_out[...] = m_new
        v_out[...] = v_new

    spec2 = pl.BlockSpec((tr, cols), lambda i: (i, 0))
    in_specs = [pl.BlockSpec((p.shape[0], tr, cols), lambda i: (0, i, 0)) if p.ndim == 3 else spec2 for p in parts]
    blocks = sum(_nbytes((p.shape[0] if p.ndim == 3 else 1, tr, cols), F32) for p in parts) + 7 * _nbytes((tr, cols), F32)
    return pl.pallas_call(
        body, name=name, grid=(rows // tr,),
        in_specs=in_specs + [spec2] * 3, out_specs=[spec2] * 4,
        out_shape=[jax.ShapeDtypeStruct((rows, cols), F32)] * 4,
        compiler_params=pltpu.CompilerParams(dimension_semantics=("parallel",),
                                             vmem_limit_bytes=_vmem_limit(blocks, 4 * _nbytes((tr, cols), F32))),
    )(*parts, w, m, v)


def _mod_fwd(c_all, w_ada2d, depth):
    nb, d = c_all.shape
    cols = w_ada2d.shape[1]
    tn = _tile(cols, 512, 128)

    def body(c_ref, w_ref, o_ref):
        cv = c_ref[...]
        o_ref[...] = _dot(cv * _sigmoid(cv), w_ref[...])

    return pl.pallas_call(
        body, name="mod_fwd", grid=(depth, cols // tn),
        in_specs=[pl.BlockSpec((nb, d), lambda l, j: (0, 0)), pl.BlockSpec((d, tn), lambda l, j: (l, j))],
        out_specs=pl.BlockSpec((nb, tn), lambda l, j: (l, j)),
        out_shape=jax.ShapeDtypeStruct((depth * nb, cols), F32),
        compiler_params=pltpu.CompilerParams(dimension_semantics=("parallel", "parallel")),
    )(c_all, w_ada2d)


def _wada_bwd(c_all, dmod2d, depth):
    nb, d = c_all.shape
    cols = dmod2d.shape[1]
    tn = _tile(cols, 512, 128)

    def body(c_ref, g_ref, o_ref):
        cv = c_ref[...]
        o_ref[...] = _dot_tn(cv * _sigmoid(cv), g_ref[...])

    return pl.pallas_call(
        body, name="wada_bwd", grid=(depth, cols // tn),
        in_specs=[pl.BlockSpec((nb, d), lambda l, j: (0, 0)), pl.BlockSpec((nb, tn), lambda l, j: (l, j))],
        out_specs=pl.BlockSpec((d, tn), lambda l, j: (l, j)),
        out_shape=jax.ShapeDtypeStruct((depth * d, cols), F32),
        compiler_params=pltpu.CompilerParams(dimension_semantics=("parallel", "parallel")),
    )(c_all, dmod2d)


def _rope_tables(pos_col, inv_freq_lane, nope):
    n_rows = pos_col.shape[0]
    tm = _tile(n_rows, 512)
    half = QK_ROPE // 2

    def body(p_ref, f_ref, c_ref, s1_ref, s2_ref):
        ang = p_ref[...] * f_ref[...]
        lane = lax.broadcasted_iota(jnp.int32, ang.shape, 1)
        first = (lane >= nope) & (lane < nope + half)
        second = (lane >= nope + half) & (lane < nope + 2 * half)
        cos, sin = jnp.cos(ang), jnp.sin(ang)
        c_ref[...] = jnp.where(first | second, cos, 1.0)
        s1_ref[...] = jnp.where(first, -sin, 0.0)
        s2_ref[...] = jnp.where(second, sin, 0.0)

    return _rowcall("rope_tables", body, n_rows, tm, [pos_col], [inv_freq_lane], [(HEAD_PAD, F32)] * 3)


def _rope(q, c, s1, s2):
    w = q.shape[1]
    return q * c + pltpu.roll(q, w - QK_ROPE // 2, axis=1) * s1 + pltpu.roll(q, QK_ROPE // 2, axis=1) * s2


def _rope_adjoint(dr, c, s1, s2):
    w = dr.shape[1]
    return dr * c + pltpu.roll(dr * s1, QK_ROPE // 2, axis=1) + pltpu.roll(dr * s2, w - QK_ROPE // 2, axis=1)


def _ssm_disc(ar, ai, log_dt, br, bi):
    dt = jnp.exp(log_dt)
    mag = jnp.exp(ar * dt)
    abr = mag * jnp.cos(ai * dt)
    abi = mag * jnp.sin(ai * dt)
    den = ar * ar + ai * ai
    nr = abr - 1.0
    ni = abi
    cr = (nr * ar + ni * ai) / den
    ci = (ni * ar - nr * ai) / den
    return abr, abi, cr * br - ci * bi, cr * bi + ci * br


def _ssm_disc_fwd(ar, ai, log_dt, br, bi):
    n_rows, m = br.shape
    tm = _tile(n_rows, 1024)

    def body(ar_ref, ai_ref, dt_ref, br_ref, bi_ref, o1, o2, o3, o4):
        o1[...], o2[...], o3[...], o4[...] = _ssm_disc(ar_ref[...], ai_ref[...], dt_ref[...], br_ref[...], bi_ref[...])

    return _rowcall("ssm_disc_fwd", body, n_rows, tm, [ar, ai, log_dt, br, bi], [],
                    [(1, F32), (1, F32), (m, F32), (m, F32)])


def _ssm_disc_bwd(ar, ai, log_dt, br, bi, g_abr, g_abi, g_bbr, g_bbi):
    n_rows, m = br.shape
    tm = _tile(n_rows, 1024)

    def body(ar_ref, ai_ref, dt_ref, br_ref, bi_ref, g1, g2, g3, g4, o1, o2, o3, o4, o5):
        _, vjp = jax.vjp(_ssm_disc, ar_ref[...], ai_ref[...], dt_ref[...], br_ref[...], bi_ref[...])
        o1[...], o2[...], o3[...], o4[...], o5[...] = vjp((g1[...], g2[...], g3[...], g4[...]))

    return _rowcall("ssm_disc_bwd", body, n_rows, tm, [ar, ai, log_dt, br, bi, g_abr, g_abi, g_bbr, g_bbi], [],
                    [(1, F32), (1, F32), (1, F32), (m, F32), (m, F32)])


def _lane_sum(v2d):
    def body(v_ref, o_ref):
        o_ref[...] = jnp.sum(v_ref[...], axis=1, keepdims=True)
    return pl.pallas_call(body, name="lane_sum", out_shape=jax.ShapeDtypeStruct((v2d.shape[0], 1), F32))(v2d)


def _in_fwd(x, n1g, sc1, sh1, w_parts, tm):
    n_rows = x.shape[0]
    widths = [w.shape[1] for w in w_parts]

    def body(x_ref, g_ref, sc_ref, sh_ref, *rest):
        w_refs, (hb_ref, *z_refs) = rest[:len(w_parts)], rest[len(w_parts):]
        xh, _ = _rms_stats(x_ref[...])
        h = (xh * g_ref[...]) * (1.0 + sc_ref[...]) + sh_ref[...]
        hb = h.astype(_MM)
        hb_ref[...] = hb.astype(_ACT)
        for w_ref, z_ref in zip(w_refs, z_refs):
            z_ref[...] = _dot(hb, w_ref[...])

    return _rowcall("in_fwd", body, n_rows, tm, [x], [n1g, sc1, sh1, *w_parts],
                    [(x.shape[1], _ACT)] + [(w, F32) for w in widths], temp_cols=4 * x.shape[1])


def _ssm_fwd(u, bre_blk, bim_blk, abr_row, abi_row, cre_blk, cimneg_blk, d_row):
    n_rows, sw = u.shape
    gp = abr_row.shape[1]
    t = _tile(n_rows, 256)

    def body(u_ref, bre_ref, bim_ref, ar_ref, ai_ref, cre_ref, cim_ref, d_ref, y_ref, hre_ref, him_ref, cr, ci):
        @pl.when(_first_step())
        def _():
            cr[...] = jnp.zeros_like(cr)
            ci[...] = jnp.zeros_like(ci)

        uv = u_ref[...]
        ub = uv.astype(_MM)
        hre_ref[...] = _dot(ub, bre_ref[...])
        him_ref[...] = _dot(ub, bim_ref[...])
        a_r, a_i = ar_ref[...], ai_ref[...]

        def step(k, carry):
            pr, pi = carry
            row = pl.ds(k, 1)
            hr = a_r * pr - a_i * pi + hre_ref[row, :]
            hi = a_r * pi + a_i * pr + him_ref[row, :]
            hre_ref[row, :] = hr
            him_ref[row, :] = hi
            return hr, hi

        pr, pi = lax.fori_loop(0, t, step, (cr[0:1, :], ci[0:1, :]), unroll=8)
        cr[0:1, :] = pr
        ci[0:1, :] = pi
        y_ref[...] = _dot(hre_ref[...], cre_ref[...]) + _dot(him_ref[...], cim_ref[...]) + d_ref[...] * uv

    row = lambda c: pl.BlockSpec((t, c), lambda i: (i, 0))
    full = lambda a: pl.BlockSpec(a.shape, lambda i: (0, 0), pipeline_mode=pl.Buffered(1))
    blocks = _nbytes((t, sw), F32) * 2 + 2 * _nbytes((t, gp), F32)
    resident = 4 * _nbytes((sw, gp), _MM)
    return pl.pallas_call(
        body, name="ssm_fwd", grid=(n_rows // t,),
        in_specs=[row(sw), full(bre_blk), full(bim_blk), full(abr_row), full(abi_row), full(cre_blk),
                  full(cimneg_blk), full(d_row)],
        out_specs=[row(sw), row(gp), row(gp)],
        out_shape=[jax.ShapeDtypeStruct((n_rows, sw), F32), jax.ShapeDtypeStruct((n_rows, gp), F32),
                   jax.ShapeDtypeStruct((n_rows, gp), F32)],
        scratch_shapes=[pltpu.VMEM((8, gp), F32), pltpu.VMEM((8, gp), F32)],
        compiler_params=pltpu.CompilerParams(dimension_semantics=("arbitrary",),
                                             vmem_limit_bytes=_vmem_limit(blocks, resident + 3 * _nbytes((t, gp), F32))),
    )(u, bre_blk, bim_blk, abr_row, abi_row, cre_blk, cimneg_blk, d_row)


def _mla_prep_fwd(cq, ckv, kr, rc, rs1, rs2, gq, gkv, wuq, wuk, wuv, vdim, tm):
    n_rows = cq.shape[0]

    def body(cq_ref, ckv_ref, kr_ref, c_ref, s1_ref, s2_ref, gq_ref, gkv_ref, wuq_ref, wuk_ref, wuv_ref,
             q_ref, k_ref, v_ref, cqn_ref, ckvn_ref):
        c, s1, s2 = c_ref[...], s1_ref[...], s2_ref[...]
        c8, s18, s28 = (jnp.tile(a, (1, N_HEADS)) for a in (c, s1, s2))
        xh, _ = _rms_stats(cq_ref[...])
        cqn = (xh * gq_ref[...]).astype(_MM)
        cqn_ref[...] = cqn.astype(_ACT)
        q_ref[...] = _rope(_dot(cqn, wuq_ref[...]), c8, s18, s28).astype(_ACT)
        xh, _ = _rms_stats(ckv_ref[...])
        ckvn = (xh * gkv_ref[...]).astype(_MM)
        ckvn_ref[...] = ckvn.astype(_ACT)
        kpe = _rope(kr_ref[...], c, s1, s2)
        k_ref[...] = (_dot(ckvn, wuk_ref[...]) + jnp.tile(kpe, (1, N_HEADS))).astype(_ACT)
        v = _dot(ckvn, wuv_ref[...])
        lane = lax.broadcasted_iota(jnp.int32, v.shape, 1)
        v_ref[...] = jnp.where((lane & (HEAD_PAD - 1)) == vdim, 1.0, v).astype(_ACT)

    return _rowcall("mla_prep_fwd", body, n_rows, tm, [cq, ckv, kr, rc, rs1, rs2], [gq, gkv, wuq, wuk, wuv],
                    [(QW, _ACT), (QW, _ACT), (QW, _ACT), (cq.shape[1], _ACT), (ckv.shape[1], _ACT)], temp_cols=6 * QW)


def _causal_steps(n_blocks, key_major):
    if key_major:
        pairs = [(qi, ki) for ki in range(n_blocks) for qi in range(ki, n_blocks)]
    else:
        pairs = [(qi, ki) for qi in range(n_blocks) for ki in range(qi + 1)]
    return (jnp.asarray(np.array([p[0] for p in pairs], np.int32)), jnp.asarray(np.array([p[1] for p in pairs], np.int32)))


def _attn_fwd(q, k, v, pos_col, pos_row, scale, vdim):
    n_rows = q.shape[0]
    ta = _tile(n_rows, 512, 128)
    nb = n_rows // ta
    hb = ATTN_HEADS_PER_STEP
    wide = hb * HEAD_PAD
    qmap, kmap = _causal_steps(nb, key_major=False)
    c2 = scale * LOG2_E

    def body(qm, km, q_ref, k_ref, v_ref, pq_ref, pk_ref, o_ref, lse_ref, m_sc, acc_sc):
        s_id = pl.program_id(1)
        qi, ki = qm[s_id], km[s_id]

        @pl.when(ki == 0)
        def _():
            m_sc[...] = jnp.full_like(m_sc, NEG_INF)
            acc_sc[...] = jnp.zeros_like(acc_sc)

        def update(on_diagonal):
            if on_diagonal:
                visible = pk_ref[...] <= pq_ref[...]
            for h in range(hb):
                lanes = slice(h * HEAD_PAD, (h + 1) * HEAD_PAD)
                s = _dot_nt(q_ref[:, lanes], k_ref[:, lanes])
                if on_diagonal:
                    s = jnp.where(visible, s, NEG_INF)
                m_prev = m_sc[:, lanes]
                m_new = jnp.maximum(m_prev, jnp.max(s, axis=1, keepdims=True))
                alpha = jnp.exp2((m_prev - m_new) * c2)
                p = jnp.exp2((s - m_new[:, :1]) * c2)
                acc_new = alpha * acc_sc[:, lanes] + _dot(p, v_ref[:, lanes])
                if on_diagonal:
                    l_new = acc_new[:, vdim:vdim + 1]
                    o_ref[:, lanes] = acc_new / l_new
                    lse_ref[:, lanes] = m_new * c2 + jnp.log2(l_new)
                else:
                    acc_sc[:, lanes] = acc_new
                    m_sc[:, lanes] = m_new

        pl.when(ki != qi)(functools.partial(update, False))
        pl.when(ki == qi)(functools.partial(update, True))

    qspec = pl.BlockSpec((ta, wide), lambda h, s, qm, km: (qm[s], h))
    kspec = pl.BlockSpec((ta, wide), lambda h, s, qm, km: (km[s], h))
    grid_spec = pltpu.PrefetchScalarGridSpec(
        num_scalar_prefetch=2, grid=(N_HEADS // hb, int(qmap.shape[0])),
        in_specs=[qspec, kspec, kspec,
                  pl.BlockSpec((ta, 1), lambda h, s, qm, km: (qm[s], 0)),
                  pl.BlockSpec((1, ta), lambda h, s, qm, km: (0, km[s]))],
        out_specs=[qspec, qspec],
        scratch_shapes=[pltpu.VMEM((ta, wide), F32)] * 2)
    return pl.pallas_call(
        body, name="attn_fwd", grid_spec=grid_spec,
        out_shape=[jax.ShapeDtypeStruct((n_rows, QW), F32), jax.ShapeDtypeStruct((n_rows, QW), F32)],
        compiler_params=pltpu.CompilerParams(
            dimension_semantics=("parallel", "arbitrary"),
            vmem_limit_bytes=_vmem_limit(8 * _nbytes((ta, wide), F32), 6 * hb * _nbytes((ta, ta), F32))),
    )(qmap, kmap, q, k, v, pos_col, pos_row)


def _mix_fwd(ypre, o, ga, gb, x, g1, bglu, wglu, wa, wb, wout, tm):
    n_rows, d = x.shape
    sw = ypre.shape[1]

    def body(y_ref, o_ref, ga_ref, gb_ref, x_ref, g1_ref, bglu_ref, wglu_ref, wa_ref, wb_ref, wout_ref,
             yg_ref, ya_ref, yb_ref, mg_ref, mo_ref, xo_ref):
        ys = _gelu(y_ref[...])
        yg = ys * _sigmoid(_dot(ys, wglu_ref[...]) + bglu_ref[...])
        yg_ref[...] = yg.astype(_ACT)
        ya = _dot(yg, wa_ref[...])
        yb = _dot(o_ref[...], wb_ref[...])
        ya_ref[...] = ya
        yb_ref[...] = yb
        merged = _sigmoid(ga_ref[...]) * ya + _sigmoid(gb_ref[...]) * yb
        mg_ref[...] = merged.astype(_ACT)
        mo = _dot(merged, wout_ref[...])
        mo_ref[...] = mo
        xo_ref[...] = x_ref[...] + g1_ref[...] * mo

    return _rowcall("mix_fwd", body, n_rows, tm, [ypre, o, ga, gb, x], [g1, bglu, wglu, wa, wb, wout],
                    [(sw, _ACT), (d, F32), (d, F32), (d, _ACT), (d, F32), (d, F32)], temp_cols=4 * d)


def _ffn_fwd(x, n2g, sc2, sh2, g2, wg, wu, wd, tm):
    n_rows, d = x.shape
    ff = wg.shape[1]

    def body(x_ref, g_ref, sc_ref, sh_ref, g2_ref, wg_ref, wu_ref, wd_ref, hb_ref, a_ref, b_ref, d_ref, xo_ref):
        xv = x_ref[...]
        xh, _ = _rms_stats(xv)
        hb = ((xh * g_ref[...]) * (1.0 + sc_ref[...]) + sh_ref[...]).astype(_MM)
        hb_ref[...] = hb.astype(_ACT)
        a = _dot(hb, wg_ref[...])
        b = _dot(hb, wu_ref[...])
        a_ref[...] = a
        b_ref[...] = b
        dn = _dot((a * _sigmoid(a)) * b, wd_ref[...])
        d_ref[...] = dn
        xo_ref[...] = xv + g2_ref[...] * dn

    return _rowcall("ffn_fwd", body, n_rows, tm, [x], [n2g, sc2, sh2, g2, wg, wu, wd],
                    [(d, _ACT), (ff, F32), (ff, F32), (d, F32), (d, F32)], temp_cols=3 * ff)


def _head(x, fg, target, tm):
    n_rows, d = x.shape

    def body(x_ref, t_ref, g_ref, dx_ref, loss_ref, dg_ref):
        xh, rstd = _rms_stats(x_ref[...])
        err = xh * g_ref[...] - t_ref[...]
        part = jnp.sum(jnp.mean(err * err, axis=-1, keepdims=True), axis=0, keepdims=True) * 0.5
        _acc(loss_ref, jnp.broadcast_to(part, loss_ref.shape))
        dy = err * (1.0 / d)
        _acc(dg_ref, _rows_sum(dy * xh))
        dx_ref[...] = _rms_bwd(dy * g_ref[...], xh, rstd)

    return _rowcall("head", body, n_rows, tm, [x, target], [fg], [(d, F32)], [((1, 128), F32), ((1, d), F32)],
                    temp_cols=4 * d)


def _ffn_bwd(dxo, xmid, a, b, dn, n2g, sc2, g2, wg, wu, wd, tm):
    n_rows, d = dxo.shape
    ff = a.shape[1]

    def act_body(dxo_ref, a_ref, b_ref, dn_ref, g2_ref, wd_ref, da_ref, db_ref, f_ref, dd_ref, dg2_ref):
        dxo_v = dxo_ref[...]
        dd = dxo_v * g2_ref[...]
        dd_ref[...] = dd.astype(_ACT)
        _acc(dg2_ref, _rows_sum(dxo_v * dn_ref[...]))
        df = _dot_nt(dd, wd_ref[...])
        av, bv = a_ref[...], b_ref[...]
        sa = _sigmoid(av)
        si = av * sa
        f_ref[...] = (si * bv).astype(_ACT)
        da_ref[...] = (df * bv * (sa * (1.0 + av * (1.0 - sa)))).astype(_ACT)
        db_ref[...] = (df * si).astype(_ACT)

    da, db, f, dd, dg2 = _rowcall("ffn_bwd_act", act_body, n_rows, 2 * tm, [dxo, a, b, dn], [g2, wd],
                                  [(ff, _ACT), (ff, _ACT), (ff, _ACT), (d, _ACT)], [((1, d), F32)], temp_cols=4 * ff)

    def in_body(dxo_ref, x_ref, da_ref, db_ref, g_ref, sc_ref, wg_ref, wu_ref, dx_ref, dsh_ref, dsc_ref, dn2_ref):
        dh = _dot_nt(da_ref[...], wg_ref[...]) + _dot_nt(db_ref[...], wu_ref[...])
        xh, rstd = _rms_stats(x_ref[...])
        yg = xh * g_ref[...]
        _acc(dsh_ref, _rows_sum(dh))
        _acc(dsc_ref, _rows_sum(dh * yg))
        dy = dh * (1.0 + sc_ref[...])
        _acc(dn2_ref, _rows_sum(dy * xh))
        dx_ref[...] = dxo_ref[...] + _rms_bwd(dy * g_ref[...], xh, rstd)

    dx, dsh, dsc, dn2 = _rowcall("ffn_bwd_in", in_body, n_rows, 2 * tm, [dxo, xmid, da, db], [n2g, sc2, wg, wu],
                                 [(d, F32)], [((1, d), F32)] * 3, temp_cols=5 * d)
    return dx, da, db, f, dd, dg2, dsh, dsc, dn2


def _mix_bwd(dxm, mo, ya, yb, ga, gb, ypre, o, g1, bglu, wglu, wa, wb, wout, tm):
    n_rows, d = dxm.shape
    sw = ypre.shape[1]

    def body(dxm_ref, mo_ref, ya_ref, yb_ref, ga_ref, gb_ref, y_ref, o_ref, g1_ref, bglu_ref, wglu_ref, wa_ref, wb_ref,
             wout_ref, dmo_ref, dya_ref, dyb_ref, dt_ref, ys_ref, dga_ref, dgb_ref, dy_ref, do_ref, delta_ref,
             dg1_ref, dbg_ref):
        dxm_v = dxm_ref[...]
        dmo = dxm_v * g1_ref[...]
        dmo_ref[...] = dmo.astype(_ACT)
        _acc(dg1_ref, _rows_sum(dxm_v * mo_ref[...]))
        dmg = _dot_nt(dmo, wout_ref[...])
        sa, sb = _sigmoid(ga_ref[...]), _sigmoid(gb_ref[...])
        dya, dyb = dmg * sa, dmg * sb
        dya_ref[...] = dya.astype(_ACT)
        dyb_ref[...] = dyb.astype(_ACT)
        dga_ref[...] = (dmg * ya_ref[...] * (sa * (1.0 - sa))).astype(_ACT)
        dgb_ref[...] = (dmg * yb_ref[...] * (sb * (1.0 - sb))).astype(_ACT)
        do = _dot_nt(dyb, wb_ref[...])
        do_ref[...] = do
        prod = do * o_ref[...]
        for h in range(N_HEADS):
            lanes = slice(h * HEAD_PAD, (h + 1) * HEAD_PAD)
            delta_ref[:, lanes] = jnp.broadcast_to(jnp.sum(prod[:, lanes], axis=1, keepdims=True), (prod.shape[0], HEAD_PAD))
        dyg = _dot_nt(dya, wa_ref[...])
        yv = y_ref[...]
        ys = _gelu(yv)
        ys_ref[...] = ys.astype(_ACT)
        sg = _sigmoid(_dot(ys, wglu_ref[...]) + bglu_ref[...])
        dt = dyg * ys * (sg * (1.0 - sg))
        dt_ref[...] = dt.astype(_ACT)
        _acc(dbg_ref, _rows_sum(dt))
        dys = dyg * sg + _dot_nt(dt, wglu_ref[...])
        dy_ref[...] = dys * _gelu_grad(yv)

    return _rowcall("mix_bwd", body, n_rows, tm, [dxm, mo, ya, yb, ga, gb, ypre, o], [g1, bglu, wglu, wa, wb, wout],
                    [(d, _ACT), (d, _ACT), (d, _ACT), (sw, _ACT), (sw, _ACT), (d, _ACT), (d, _ACT), (sw, F32), (QW, F32),
                     (QW, F32)],
                    [((1, d), F32), ((1, sw), F32)], temp_cols=6 * d)


def _attn_bwd(q, k, v, do, delta, lse, pos_col, pos_row, scale):
    n_rows = q.shape[0]
    ta = _tile(n_rows, 512, 128)
    nb = n_rows // ta
    qmap, kmap = _causal_steps(nb, key_major=True)
    hb = ATTN_BWD_HEADS_PER_STEP
    wide = hb * HEAD_PAD
    c2 = scale * LOG2_E

    def body(qm, km, q_ref, k_ref, v_ref, do_ref, delta_ref, lse_ref, pq_ref, pk_ref, dq_ref, dk_ref, dv_ref,
             dk_acc, dv_acc):
        s_id = pl.program_id(1)
        qi, ki = qm[s_id], km[s_id]

        @pl.when(s_id == 0)
        def _():
            dq_ref[...] = jnp.zeros_like(dq_ref)

        @pl.when(qi == ki)
        def _():
            dk_acc[...] = jnp.zeros_like(dk_acc)
            dv_acc[...] = jnp.zeros_like(dv_acc)

        rows = pl.ds(pl.multiple_of(qi * ta, ta), ta)

        def update(on_diagonal):
            if on_diagonal:
                visible = pk_ref[...] <= pq_ref[...]
            for h in range(hb):
                lanes = slice(h * HEAD_PAD, (h + 1) * HEAD_PAD)
                qv, kv, dov = q_ref[:, lanes], k_ref[:, lanes], do_ref[:, lanes]
                e = _dot_nt(qv, kv) * c2 - lse_ref[:, lanes][:, :1]
                if on_diagonal:
                    e = jnp.where(visible, e, NEG_INF)
                p = jnp.exp2(e)
                dp = _dot_nt(dov, v_ref[:, lanes])
                ds = p * (dp - delta_ref[:, lanes][:, :1])
                dv_acc[:, lanes] += _dot_tn(p, dov)
                dk_acc[:, lanes] += _dot_tn(ds, qv)
                dq_ref[rows, lanes] += _dot(ds, kv) * scale

        pl.when(ki != qi)(functools.partial(update, False))
        pl.when(ki == qi)(functools.partial(update, True))

        @pl.when(qi == nb - 1)
        def _():
            dk_ref[...] = dk_acc[...] * scale
            dv_ref[...] = dv_acc[...]

    qspec = pl.BlockSpec((ta, wide), lambda h, s, qm, km: (qm[s], h))
    kspec = pl.BlockSpec((ta, wide), lambda h, s, qm, km: (km[s], h))
    grid_spec = pltpu.PrefetchScalarGridSpec(
        num_scalar_prefetch=2, grid=(N_HEADS // hb, int(qmap.shape[0])),
        in_specs=[qspec, kspec, kspec, qspec, qspec, qspec,
                  pl.BlockSpec((ta, 1), lambda h, s, qm, km: (qm[s], 0)),
                  pl.BlockSpec((1, ta), lambda h, s, qm, km: (0, km[s]))],
        out_specs=[pl.BlockSpec((n_rows, wide), lambda h, s, qm, km: (0, h)), kspec, kspec],
        scratch_shapes=[pltpu.VMEM((ta, wide), F32), pltpu.VMEM((ta, wide), F32)])
    return pl.pallas_call(
        body, name="attn_bwd", grid_spec=grid_spec,
        out_shape=[jax.ShapeDtypeStruct((n_rows, QW), F32)] * 3,
        compiler_params=pltpu.CompilerParams(
            dimension_semantics=("parallel", "arbitrary"),
            vmem_limit_bytes=_vmem_limit(12 * _nbytes((ta, wide), F32) + _nbytes((n_rows, wide), F32),
                                         6 * hb * _nbytes((ta, ta), F32))),
    )(qmap, kmap, q, k, v, do, delta, lse, pos_col, pos_row)


def _mla_prep_bwd(dq, dk, dv, cq, ckv, rc, rs1, rs2, gq, gkv, wuq, wuk, wuv, nope, tm):
    n_rows = cq.shape[0]
    ql, kl = cq.shape[1], ckv.shape[1]

    def body(dq_ref, dk_ref, dv_ref, cq_ref, ckv_ref, c_ref, s1_ref, s2_ref, gq_ref, gkv_ref, wuq_ref, wuk_ref,
             wuv_ref, dqp_ref, dcq_ref, dckv_ref, dkr_ref, dgq_ref, dgkv_ref):
        c, s1, s2 = c_ref[...], s1_ref[...], s2_ref[...]
        c8, s18, s28 = (jnp.tile(a, (1, N_HEADS)) for a in (c, s1, s2))
        dqp = _rope_adjoint(dq_ref[...], c8, s18, s28)
        dqp_ref[...] = dqp.astype(_ACT)
        dcqn = _dot_nt(dqp, wuq_ref[...])
        xh, rstd = _rms_stats(cq_ref[...])
        _acc(dgq_ref, _rows_sum(dcqn * xh))
        dcq_ref[...] = _rms_bwd(dcqn * gq_ref[...], xh, rstd).astype(_ACT)
        dkv = dk_ref[...]
        dkpe = dkv[:, 0:HEAD_PAD]
        for h in range(1, N_HEADS):
            dkpe = dkpe + dkv[:, h * HEAD_PAD:(h + 1) * HEAD_PAD]
        lane = lax.broadcasted_iota(jnp.int32, dkpe.shape, 1)
        dkpe = jnp.where((lane >= nope) & (lane < nope + QK_ROPE), dkpe, 0.0)
        dkr_ref[...] = _rope_adjoint(dkpe, c, s1, s2).astype(_ACT)
        dckvn = _dot_nt(dkv, wuk_ref[...]) + _dot_nt(dv_ref[...], wuv_ref[...])
        xh, rstd = _rms_stats(ckv_ref[...])
        _acc(dgkv_ref, _rows_sum(dckvn * xh))
        dckv_ref[...] = _rms_bwd(dckvn * gkv_ref[...], xh, rstd).astype(_ACT)

    return _rowcall("mla_prep_bwd", body, n_rows, tm, [dq, dk, dv, cq, ckv, rc, rs1, rs2], [gq, gkv, wuq, wuk, wuv],
                    [(QW, _ACT), (ql, _ACT), (kl, _ACT), (HEAD_PAD, _ACT)], [((1, ql), F32), ((1, kl), F32)],
                    temp_cols=6 * QW)


def _ssm_bwd(dy, u, hre, him, bre_blk, bim_blk, abr_row, abi_row, cre_blk, cimneg_blk, d_row):
    n_rows, sw = u.shape
    gp = abr_row.shape[1]
    t = _tile(n_rows, 256)
    n_chunks = n_rows // t

    def body(dy_ref, u_ref, hre_ref, him_ref, hbre_ref, hbim_ref, bre_ref, bim_ref, ar_ref, ai_ref, cre_ref, cim_ref,
             d_ref, du_ref, gre_ref, gim_ref, dar_ref, dai_ref, dd_ref, g_re, g_im, hs_re, hs_im, cr, ci):
        i = pl.program_id(0)

        @pl.when(i == 0)
        def _():
            cr[...] = jnp.zeros_like(cr)
            ci[...] = jnp.zeros_like(ci)

        dyv = dy_ref[...]
        dyb = dyv.astype(_MM)
        g_re[...] = _dot_nt(dyb, cre_ref[...])
        g_im[...] = _dot_nt(dyb, cim_ref[...])
        a_r, a_i = ar_ref[...], ai_ref[...]

        def step(k, carry):
            nr, ni = carry
            row = pl.ds(t - 1 - k, 1)
            gr = g_re[row, :] + a_r * nr + a_i * ni
            gi = g_im[row, :] + a_r * ni - a_i * nr
            g_re[row, :] = gr
            g_im[row, :] = gi
            return gr, gi

        nr, ni = lax.fori_loop(0, t, step, (cr[0:1, :], ci[0:1, :]), unroll=8)
        cr[0:1, :] = nr
        ci[0:1, :] = ni
        gr_all, gi_all = g_re[...], g_im[...]
        gre_ref[...] = gr_all.astype(_ACT)
        gim_ref[...] = gi_all.astype(_ACT)
        du_ref[...] = (_dot_nt(gr_all, bre_ref[...]) + _dot_nt(gi_all, bim_ref[...]) + d_ref[...] * dyv).astype(_ACT)
        _acc(dd_ref, _rows_sum(dyv * u_ref[...]))
        is_first_chunk = i == n_chunks - 1
        hs_re[0:8, :] = jnp.where(is_first_chunk, 0.0, hbre_ref[...])
        hs_im[0:8, :] = jnp.where(is_first_chunk, 0.0, hbim_ref[...])
        hs_re[8:t + 8, :] = hre_ref[...]
        hs_im[8:t + 8, :] = him_ref[...]
        hp_re, hp_im = hs_re[pl.ds(7, t), :], hs_im[pl.ds(7, t), :]
        _acc(dar_ref, _rows_sum(gr_all * hp_re + gi_all * hp_im))
        _acc(dai_ref, _rows_sum(gi_all * hp_re - gr_all * hp_im))

    rev = lambda c: pl.BlockSpec((t, c), lambda i: (n_chunks - 1 - i, 0))
    before = pl.BlockSpec((8, gp), lambda i: (jnp.maximum((n_chunks - 1 - i) * (t // 8) - 1, 0), 0))
    full = lambda a: pl.BlockSpec(a.shape, lambda i: (0, 0), pipeline_mode=pl.Buffered(1))
    acc = lambda c: pl.BlockSpec((1, c), lambda i: (0, 0))
    blocks = 2 * _nbytes((t, sw), F32) + 2 * _nbytes((t, gp), F32) + _nbytes((t, sw), _ACT) + 2 * _nbytes((t, gp), _ACT)
    resident = 4 * _nbytes((sw, gp), _MM) + 4 * _nbytes((t + 8, gp), F32)
    return pl.pallas_call(
        body, name="ssm_bwd", grid=(n_chunks,),
        in_specs=[rev(sw), rev(sw), rev(gp), rev(gp), before, before, full(bre_blk), full(bim_blk), full(abr_row),
                  full(abi_row), full(cre_blk), full(cimneg_blk), full(d_row)],
        out_specs=[rev(sw), rev(gp), rev(gp), acc(gp), acc(gp), acc(sw)],
        out_shape=[jax.ShapeDtypeStruct((n_rows, sw), _ACT), jax.ShapeDtypeStruct((n_rows, gp), _ACT),
                   jax.ShapeDtypeStruct((n_rows, gp), _ACT), jax.ShapeDtypeStruct((1, gp), F32),
                   jax.ShapeDtypeStruct((1, gp), F32), jax.ShapeDtypeStruct((1, sw), F32)],
        scratch_shapes=[pltpu.VMEM((t, gp), F32), pltpu.VMEM((t, gp), F32), pltpu.VMEM((t + 8, gp), F32),
                        pltpu.VMEM((t + 8, gp), F32), pltpu.VMEM((8, gp), F32), pltpu.VMEM((8, gp), F32)],
        compiler_params=pltpu.CompilerParams(dimension_semantics=("arbitrary",),
                                             vmem_limit_bytes=_vmem_limit(blocks, resident + 4 * _nbytes((t, gp), F32))),
    )(dy, u, hre, him, hre, him, bre_blk, bim_blk, abr_row, abi_row, cre_blk, cimneg_blk, d_row)


def _in_bwd(dxm, x, dz_parts, n1g, sc1, w_parts, tm):
    n_rows, d = x.shape
    n = len(dz_parts)

    def body(dxm_ref, x_ref, *rest):
        dz_refs = rest[:n]
        g_ref, sc_ref = rest[n], rest[n + 1]
        w_refs = rest[n + 2:2 * n + 2]
        dx_ref, dsh_ref, dsc_ref, dn1_ref = rest[2 * n + 2:]
        dh = None
        for dz_ref, w_ref in zip(dz_refs, w_refs):
            term = _dot_nt(dz_ref[...], w_ref[...])
            dh = term if dh is None else dh + term
        xh, rstd = _rms_stats(x_ref[...])
        yg = xh * g_ref[...]
        _acc(dsh_ref, _rows_sum(dh))
        _acc(dsc_ref, _rows_sum(dh * yg))
        dy = dh * (1.0 + sc_ref[...])
        _acc(dn1_ref, _rows_sum(dy * xh))
        dx_ref[...] = dxm_ref[...] + _rms_bwd(dy * g_ref[...], xh, rstd)

    return _rowcall("in_bwd", body, n_rows, tm, [dxm, x, *dz_parts], [n1g, sc1, *w_parts],
                    [(d, F32)], [((1, d), F32)] * 3, temp_cols=5 * d)


def _pad_heads(w, per_head):
    lead = w.shape[:-1]
    w = w.reshape(lead + (N_HEADS, per_head))
    w = jnp.pad(w, [(0, 0)] * len(lead) + [(0, 0), (0, HEAD_PAD - per_head)])
    return w.reshape(lead + (QW,))


def _unpad_heads(w, per_head):
    lead = w.shape[:-1]
    return w.reshape(lead + (N_HEADS, HEAD_PAD))[..., :per_head].reshape(lead + (N_HEADS * per_head,))


def _cols_from_chips(g):
    ch, dep, r, cs = g.shape
    return g.transpose(1, 2, 0, 3).reshape(dep, r, ch * cs)


def _rows_from_chips(g):
    ch, dep, rs, c = g.shape
    return g.transpose(1, 0, 2, 3).reshape(dep, ch * rs, c)


def _cols_to_chips(w):
    dep, r, c = w.shape
    return w.reshape(dep, r, 4, c // 4).transpose(2, 0, 1, 3)


def _rows_to_chips(w):
    dep, r, c = w.shape
    return w.reshape(dep, 4, r // 4, c).transpose(1, 0, 2, 3)


def _block_diag(b_gxy):
    g, xx, yy = b_gxy.shape
    eye = jnp.eye(g, dtype=b_gxy.dtype)
    return (b_gxy[:, :, None, :] * eye[:, None, :, None]).reshape(g * xx, g * yy)


def _block_diag_extract(full, g):
    xx, yy = full.shape[0] // g, full.shape[1] // g
    eye = jnp.eye(g, dtype=full.dtype)
    return jnp.sum(full.reshape(g, xx, g, yy) * eye[:, None, :, None], axis=2)


def _pack_rows(arrays):
    parts = []
    for a in arrays:
        flat = a.reshape(-1)
        flat = jnp.pad(flat, (0, (-flat.shape[0]) % 1024))
        parts.append(flat.reshape(-1, 128))
    return jnp.concatenate(parts, axis=0)


def _unpack_rows(packed, shapes):
    out, row = [], 0
    for s in shapes:
        n = int(np.prod(s))
        rows = -(-n // 1024) * 8
        out.append(packed[row:row + rows].reshape(-1)[:n].reshape(s))
        row += rows
    return out


def kernel(x, c, positions, w_ada, b_ada, norm1_g, w_in, ssm_a_re, ssm_a_im, ssm_log_dt, ssm_b_re, ssm_b_im, ssm_c_re, ssm_c_im, ssm_d, w_glu, b_glu, w_a_out, q_norm_g, w_uq, kv_norm_g, w_uk, w_uv, w_b_out, w_out, norm2_g, w_gate, w_up, w_down, final_g, loss_target, m_w_ada, m_b_ada, m_norm1_g, m_w_in, m_ssm_a_re, m_ssm_a_im, m_ssm_log_dt, m_ssm_b_re, m_ssm_b_im, m_ssm_c_re, m_ssm_c_im, m_ssm_d, m_w_glu, m_b_glu, m_w_a_out, m_q_norm_g, m_w_uq, m_kv_norm_g, m_w_uk, m_w_uv, m_w_b_out, m_w_out, m_norm2_g, m_w_gate, m_w_up, m_w_down, m_final_g, v_w_ada, v_b_ada, v_norm1_g, v_w_in, v_ssm_a_re, v_ssm_a_im, v_ssm_log_dt, v_ssm_b_re, v_ssm_b_im, v_ssm_c_re, v_ssm_c_im, v_ssm_d, v_w_glu, v_b_glu, v_w_a_out, v_q_norm_g, v_w_uq, v_kv_norm_g, v_w_uk, v_w_uv, v_w_b_out, v_w_out, v_norm2_g, v_w_gate, v_w_up, v_w_down, v_final_g):
    weights = dict(w_ada=w_ada, b_ada=b_ada, norm1_g=norm1_g, w_in=w_in, ssm_a_re=ssm_a_re, ssm_a_im=ssm_a_im, ssm_log_dt=ssm_log_dt, ssm_b_re=ssm_b_re, ssm_b_im=ssm_b_im, ssm_c_re=ssm_c_re, ssm_c_im=ssm_c_im, ssm_d=ssm_d, w_glu=w_glu, b_glu=b_glu, w_a_out=w_a_out, q_norm_g=q_norm_g, w_uq=w_uq, kv_norm_g=kv_norm_g, w_uk=w_uk, w_uv=w_uv, w_b_out=w_b_out, w_out=w_out, norm2_g=norm2_g, w_gate=w_gate, w_up=w_up, w_down=w_down, final_g=final_g)
    mom_m = dict(w_ada=m_w_ada, b_ada=m_b_ada, norm1_g=m_norm1_g, w_in=m_w_in, ssm_a_re=m_ssm_a_re, ssm_a_im=m_ssm_a_im, ssm_log_dt=m_ssm_log_dt, ssm_b_re=m_ssm_b_re, ssm_b_im=m_ssm_b_im, ssm_c_re=m_ssm_c_re, ssm_c_im=m_ssm_c_im, ssm_d=m_ssm_d, w_glu=m_w_glu, b_glu=m_b_glu, w_a_out=m_w_a_out, q_norm_g=m_q_norm_g, w_uq=m_w_uq, kv_norm_g=m_kv_norm_g, w_uk=m_w_uk, w_uv=m_w_uv, w_b_out=m_w_b_out, w_out=m_w_out, norm2_g=m_norm2_g, w_gate=m_w_gate, w_up=m_w_up, w_down=m_w_down, final_g=m_final_g)
    mom_v = dict(w_ada=v_w_ada, b_ada=v_b_ada, norm1_g=v_norm1_g, w_in=v_w_in, ssm_a_re=v_ssm_a_re, ssm_a_im=v_ssm_a_im, ssm_log_dt=v_ssm_log_dt, ssm_b_re=v_ssm_b_re, ssm_b_im=v_ssm_b_im, ssm_c_re=v_ssm_c_re, ssm_c_im=v_ssm_c_im, ssm_d=v_ssm_d, w_glu=v_w_glu, b_glu=v_b_glu, w_a_out=v_w_a_out, q_norm_g=v_q_norm_g, w_uq=v_w_uq, kv_norm_g=v_kv_norm_g, w_uk=v_w_uk, w_uv=v_w_uv, w_b_out=v_w_b_out, w_out=v_w_out, norm2_g=v_norm2_g, w_gate=v_w_gate, w_up=v_w_up, w_down=v_w_down, final_g=v_final_g)
    names = list(weights)

    depth = w_in.shape[0]
    seq, d = x.shape[1], x.shape[2]
    sw = ssm_d.shape[1]
    groups, n_state, n_chan = ssm_b_re.shape[1:]
    gp = groups * n_state
    ql, kl = q_norm_g.shape[1], kv_norm_g.shape[1]
    nope = w_uk.shape[2] * 4 // N_HEADS
    vdim = w_uv.shape[2] * 4 // N_HEADS
    qk_dim = nope + QK_ROPE
    scale = qk_dim ** -0.5
    tm = _tile(seq, 256, 16)
    tm_ffn = _tile(seq, 128, 16)
    me = 4 * lax.axis_index("x") + 2 * lax.axis_index("y") + lax.axis_index("c")
    chip = 2 * lax.axis_index("x") + lax.axis_index("y")

    xs = x.reshape(seq, d)
    target = loss_target.reshape(seq, d)
    pos_f = positions.astype(F32)
    pos_col = pos_f.reshape(seq, 1)
    pos_row = pos_f.reshape(1, seq)

    (c_all,) = _exchange("gather_c", "gather8", [c])
    c_all = c_all.reshape(8, d)
    ada_cols = w_ada.shape[2]
    mod_part = _mod_fwd(c_all, w_ada.reshape(depth * d, ada_cols), depth)
    (mod_all,) = _exchange("gather_mod", "gather8", [mod_part])
    mod_all = mod_all.reshape(4, 2, depth, 8, ada_cols)[:, 0]
    mod_me = lax.dynamic_index_in_dim(mod_all, me, axis=2, keepdims=False)
    mod = mod_me.transpose(1, 0, 2).reshape(depth, 4 * ada_cols) + b_ada
    mod = mod.reshape(depth, 6, 1, d)

    big = ["w_in", "w_glu", "w_a_out", "w_uq", "w_uk", "w_uv", "w_b_out", "w_out", "w_gate", "w_up", "w_down"]
    row_sharded = {"w_glu", "w_out", "w_down"}
    gathered = _gather_chips_two_level("gather_weights", [weights[n].astype(_MM) for n in big])
    full = {n: (_rows_from_chips(g) if n in row_sharded else _cols_from_chips(g)) for n, g in zip(big, gathered)}
    o1, o2, o3, o4, o5 = sw, sw + ql, sw + ql + kl, sw + ql + kl + QK_ROPE, sw + ql + kl + QK_ROPE + d
    wi = full["w_in"]
    w_u, w_cq, w_ckv, w_ga, w_gb = wi[:, :, :o1], wi[:, :, o1:o2], wi[:, :, o2:o3], wi[:, :, o4:o5], wi[:, :, o5:]
    w_kr = jnp.pad(wi[:, :, o3:o4], ((0, 0), (0, 0), (nope, HEAD_PAD - nope - QK_ROPE)))
    wuq_p = _pad_heads(full["w_uq"], qk_dim)
    wuk_p = _pad_heads(full["w_uk"], nope)
    wuv_p = _pad_heads(full["w_uv"], vdim)
    wb_p = _pad_heads(full["w_b_out"].transpose(0, 2, 1), vdim).transpose(0, 2, 1)

    inv_freq = ROPE_BASE ** (-jnp.arange(0, QK_ROPE, 2, dtype=F32) / QK_ROPE)
    inv_lane = jnp.pad(jnp.concatenate([inv_freq, inv_freq]), (nope, HEAD_PAD - nope - QK_ROPE)).reshape(1, HEAD_PAD)
    rc, rs1, rs2 = _rope_tables(pos_col, inv_lane, nope)
    a_re_col = ssm_a_re.reshape(depth * gp, 1)
    a_im_col = ssm_a_im.reshape(depth * gp, 1)
    ldt_col = jnp.broadcast_to(ssm_log_dt[:, :, None], (depth, groups, n_state)).reshape(depth * gp, 1)
    b_re2, b_im2 = ssm_b_re.reshape(depth * gp, n_chan), ssm_b_im.reshape(depth * gp, n_chan)
    abr, abi, bbr, bbi = _ssm_disc_fwd(a_re_col, a_im_col, ldt_col, b_re2, b_im2)
    abr_rows, abi_rows = abr.reshape(depth, 1, gp), abi.reshape(depth, 1, gp)
    bbr, bbi = bbr.reshape(depth, groups, n_state, n_chan), bbi.reshape(depth, groups, n_state, n_chan)

    saved = []
    xl = xs
    for l in range(depth):
        sh1, sc1, g1, sh2, sc2, g2 = (mod[l, j] for j in range(6))
        n1g, n2g = norm1_g[l].reshape(1, d), norm2_g[l].reshape(1, d)
        w_parts = [w_u[l], w_cq[l], w_ckv[l], w_kr[l], w_ga[l], w_gb[l]]
        hb, u, cq, ckv, kr, ga, gb = _in_fwd(xl, n1g, sc1, sh1, w_parts, tm)
        bre_blk = _block_diag(bbr[l].transpose(0, 2, 1)).astype(_MM)
        bim_blk = _block_diag(bbi[l].transpose(0, 2, 1)).astype(_MM)
        cre_blk = _block_diag(ssm_c_re[l].transpose(0, 2, 1)).astype(_MM)
        cimneg_blk = _block_diag(-ssm_c_im[l].transpose(0, 2, 1)).astype(_MM)
        d_row = ssm_d[l].reshape(1, sw)
        ssm_w = (bre_blk, bim_blk, abr_rows[l], abi_rows[l], cre_blk, cimneg_blk, d_row)
        ypre, hre, him = _ssm_fwd(u, *ssm_w)
        gq, gkv = q_norm_g[l].reshape(1, ql), kv_norm_g[l].reshape(1, kl)
        q, k, v, cqn, ckvn = _mla_prep_fwd(cq, ckv, kr, rc, rs1, rs2, gq, gkv, wuq_p[l], wuk_p[l], wuv_p[l], vdim, tm)
        o, lse = _attn_fwd(q, k, v, pos_col, pos_row, scale, vdim)
        bglu = b_glu[l].reshape(1, sw)
        yg, ya, yb, merged, mo, xmid = _mix_fwd(ypre, o, ga, gb, xl, g1, bglu, full["w_glu"][l], full["w_a_out"][l],
                                                 wb_p[l], full["w_out"][l], tm)
        hb2, fa, fb, dn, xout = _ffn_fwd(xmid, n2g, sc2, sh2, g2, full["w_gate"][l], full["w_up"][l], full["w_down"][l], tm_ffn)
        saved.append(dict(x=xl, hb=hb, u=u, cq=cq, ckv=ckv, ga=ga, gb=gb, ssm_w=ssm_w, ypre=ypre, hre=hre, him=him,
                          q=q, k=k, v=v, cqn=cqn, ckvn=ckvn, o=o, lse=lse, yg=yg, ya=ya, yb=yb, merged=merged, mo=mo,
                          xmid=xmid, hb2=hb2, fa=fa, fb=fb, dn=dn, w_parts=w_parts))
        xl = xout

    dx, loss_acc, g_final = _head(xl, final_g.reshape(1, d), target, tm)
    loss = lax.psum(loss_acc[0, 0], ("x", "y", "c"))

    per_layer = ["w_gate", "w_up", "w_down", "norm2_g", "w_out", "w_a_out", "w_b_out", "w_glu", "b_glu", "w_uq", "w_uk",
                 "w_uv", "q_norm_g", "kv_norm_g", "ssm_d", "ssm_c_re", "ssm_c_im", "w_in", "norm1_g"]
    grads = {n: [None] * depth for n in per_layer}
    dmod = [None] * depth
    for l in reversed(range(depth)):
        s = saved[l]
        sh1, sc1, g1, sh2, sc2, g2 = (mod[l, j] for j in range(6))
        n1g, n2g = norm1_g[l].reshape(1, d), norm2_g[l].reshape(1, d)
        dxm, da, db, fb16, dd, dg2, dsh2, dsc2, dn2 = _ffn_bwd(
            dx, s["xmid"], s["fa"], s["fb"], s["dn"], n2g, sc2, g2, full["w_gate"][l], full["w_up"][l], full["w_down"][l], tm_ffn)
        grads["w_gate"][l] = _mm_tn("dw_gate", s["hb2"], da)
        grads["w_up"][l] = _mm_tn("dw_up", s["hb2"], db)
        grads["w_down"][l] = _mm_tn("dw_down", fb16, dd)
        grads["norm2_g"][l] = dn2.reshape(d)

        bglu = b_glu[l].reshape(1, sw)
        dmo, dya, dyb, dt, ys, dga, dgb, dypre, do, delta, dg1, dbglu = _mix_bwd(
            dxm, s["mo"], s["ya"], s["yb"], s["ga"], s["gb"], s["ypre"], s["o"], g1, bglu, full["w_glu"][l],
            full["w_a_out"][l], wb_p[l], full["w_out"][l], tm)
        grads["w_out"][l] = _mm_tn("dw_out", s["merged"], dmo)
        grads["w_a_out"][l] = _mm_tn("dw_a_out", s["yg"], dya)
        dwb_p = _mm_tn("dw_b_out", s["o"], dyb)
        grads["w_b_out"][l] = _unpad_heads(dwb_p.T, vdim).T
        grads["w_glu"][l] = _mm_tn("dw_glu", ys, dt)
        grads["b_glu"][l] = dbglu.reshape(sw)

        dq, dk, dv = _attn_bwd(s["q"], s["k"], s["v"], do, delta, s["lse"], pos_col, pos_row, scale)
        gq, gkv = q_norm_g[l].reshape(1, ql), kv_norm_g[l].reshape(1, kl)
        dqp, dcq, dckv, dkr, dgq, dgkv = _mla_prep_bwd(dq, dk, dv, s["cq"], s["ckv"], rc, rs1, rs2, gq, gkv,
                                                       wuq_p[l], wuk_p[l], wuv_p[l], nope, tm)
        grads["w_uq"][l] = _unpad_heads(_mm_tn("dw_uq", s["cqn"], dqp), qk_dim)
        grads["w_uk"][l] = _unpad_heads(_mm_tn("dw_uk", s["ckvn"], dk), nope)
        grads["w_uv"][l] = _unpad_heads(_mm_tn("dw_uv", s["ckvn"], dv), vdim)
        grads["q_norm_g"][l] = dgq.reshape(ql)
        grads["kv_norm_g"][l] = dgkv.reshape(kl)

        du, gre, gim, dar, dai, ddskip = _ssm_bwd(dypre, s["u"], s["hre"], s["him"], *s["ssm_w"])
        grads["ssm_d"][l] = ddskip.reshape(sw)
        d_bre = _block_diag_extract(_mm_tn("d_bre", s["u"], gre), groups).transpose(0, 2, 1)
        d_bim = _block_diag_extract(_mm_tn("d_bim", s["u"], gim), groups).transpose(0, 2, 1)
        grads["ssm_c_re"][l] = _block_diag_extract(_mm_tn("d_cre", s["hre"], dypre), groups).transpose(0, 2, 1)
        grads["ssm_c_im"][l] = -_block_diag_extract(_mm_tn("d_cim", s["him"], dypre), groups).transpose(0, 2, 1)
        s["disc_grads"] = (dar.reshape(gp, 1), dai.reshape(gp, 1), d_bre.reshape(gp, n_chan), d_bim.reshape(gp, n_chan))

        dz_parts = [du, dcq, dckv, dkr, dga, dgb]
        dx, dsh1, dsc1, dn1 = _in_bwd(dxm, s["x"], dz_parts, n1g, sc1, s["w_parts"], tm)
        dw_parts = [_mm_tn("dw_in_%d" % j, s["hb"], dz) for j, dz in enumerate(dz_parts)]
        dw_parts[3] = dw_parts[3][:, nope:nope + QK_ROPE]
        grads["w_in"][l] = jnp.concatenate(dw_parts, axis=1)
        grads["norm1_g"][l] = dn1.reshape(d)
        dmod[l] = jnp.concatenate([dsh1, dsc1, dg1, dsh2, dsc2, dg2], axis=1).reshape(6 * d)
    grad_x = dx.reshape(x.shape)

    disc = [jnp.concatenate([saved[l]["disc_grads"][j] for l in range(depth)], axis=0) for j in range(4)]
    da_re, da_im, dldt, db_re, db_im = _ssm_disc_bwd(a_re_col, a_im_col, ldt_col, b_re2, b_im2, *disc)
    stacked = {n: jnp.stack(v) for n, v in grads.items()}
    stacked["ssm_a_re"] = da_re.reshape(ssm_a_re.shape)
    stacked["ssm_a_im"] = da_im.reshape(ssm_a_im.shape)
    stacked["ssm_log_dt"] = _lane_sum(dldt.reshape(depth * groups, n_state)).reshape(ssm_log_dt.shape)
    stacked["ssm_b_re"] = db_re.reshape(ssm_b_re.shape)
    stacked["ssm_b_im"] = db_im.reshape(ssm_b_im.shape)
    stacked["final_g"] = g_final.reshape(d)
    stacked["b_ada"] = jnp.stack(dmod)

    small = [n for n in names if n not in big and n != "w_ada"]
    small_shapes = [weights[n].shape for n in small]
    (small_all,) = _exchange("gather_small", "gather8", [_pack_rows([stacked[n] for n in small])])
    sg, sd, sm, sv = _adamw("adamw_small", [small_all], _pack_rows([weights[n] for n in small]),
                            _pack_rows([mom_m[n] for n in small]), _pack_rows([mom_v[n] for n in small]))
    out_g = dict(zip(small, _unpack_rows(sg, small_shapes)))
    out_d = dict(zip(small, _unpack_rows(sd, small_shapes)))
    out_m = dict(zip(small, _unpack_rows(sm, small_shapes)))
    out_v = dict(zip(small, _unpack_rows(sv, small_shapes)))

    n_dmod = depth * 6 * d
    dmod_all = small_all[:, :n_dmod // 128].reshape(8, depth, 6 * d)
    dmod_cols = lax.dynamic_slice_in_dim(dmod_all, chip * ada_cols, ada_cols, axis=2)
    g_wada = _wada_bwd(c_all, dmod_cols.transpose(1, 0, 2).reshape(depth * 8, ada_cols), depth)
    res = _adamw("adamw_w_ada", [g_wada], w_ada.reshape(depth * d, ada_cols), m_w_ada.reshape(depth * d, ada_cols),
                 v_w_ada.reshape(depth * d, ada_cols))
    out_g["w_ada"], out_d["w_ada"], out_m["w_ada"], out_v["w_ada"] = (r.reshape(w_ada.shape) for r in res)

    core = lax.axis_index("c")
    half = depth // 2
    to_chips = [(_rows_to_chips if n in row_sharded else _cols_to_chips)(stacked[n]).astype(_WIRE) for n in big]
    own_half = [lax.dynamic_slice_in_dim(p, core * half, half, axis=1) for p in to_chips]
    other_half = [lax.dynamic_slice_in_dim(p, (1 - core) * half, half, axis=1) for p in to_chips]
    from_sibling = _exchange("swap_halves", "swap", other_half)
    chip_sum = [_add_pair("pair_" + n, a.reshape(-1, a.shape[-1]), b.reshape(-1, b.shape[-1])).reshape(a.shape)
                for n, a, b in zip(big, own_half, from_sibling)]
    landed = _exchange("scatter_grads", "scatter4", chip_sum)
    finished = [_sum_slots("sum_" + n, r.reshape(4, -1, r.shape[-1])) for n, r in zip(big, landed)]
    sibling = _exchange("swap_partials", "swap", finished)
    for n, mine, theirs in zip(big, finished, sibling):
        shp = weights[n].shape
        as2d = lambda a: a.reshape(-1, shp[-1])
        rows = mine.shape[0]
        g = lax.dynamic_update_slice_in_dim(jnp.zeros((2 * rows, shp[-1]), F32), mine, core * rows, axis=0)
        g = lax.dynamic_update_slice_in_dim(g, theirs, (1 - core) * rows, axis=0)
        res = _adamw("adamw_" + n, [g], as2d(weights[n]), as2d(mom_m[n]), as2d(mom_v[n]))
        out_g[n], out_d[n], out_m[n], out_v[n] = (r.reshape(shp) for r in res)

    return (loss, grad_x, *[out_g[n] for n in names], *[out_d[n] for n in names], *[out_m[n] for n in names],
            *[out_v[n] for n in names])
```

```python
import functools
import math

import numpy as np
import jax
import jax.numpy as jnp
from jax import lax
from jax.experimental import pallas as pl
from jax.experimental.pallas import tpu as pltpu

F32 = jnp.float32
_MM = jnp.bfloat16
_ACT = jnp.bfloat16
_WIRE = jnp.bfloat16

N_HEADS = 8
QK_ROPE = 32
HEAD_PAD = 128
QW = N_HEADS * HEAD_PAD
ROPE_BASE = 10000.0
EPS = 1e-6
DT_MIN = 1e-3
ADAM_LR = 0.001
ADAM_B1 = 0.9
ADAM_B2 = 0.999
ADAM_EPS = 1e-08
ADAM_WD = 0.01
ADAM_STEP = 10
NEG_INF = -1e30
LOG2_E = math.log2(math.e)
ATTN_HEADS_PER_STEP = 4
ATTN_BWD_HEADS_PER_STEP = 2

V7X_VMEM_BYTES = 64 * 1024 * 1024
VMEM_RESERVE_BYTES = 6 * 1024 * 1024
MESH = pl.DeviceIdType.MESH
ANY = pl.BlockSpec(memory_space=pl.ANY)


def _vmem_limit(block_bytes, temp_bytes):
    want = 2 * block_bytes + temp_bytes
    return int(min(V7X_VMEM_BYTES - VMEM_RESERVE_BYTES, max(want, 32 * 1024 * 1024)))


def _nbytes(shape, dtype):
    return int(np.prod(shape)) * jnp.dtype(dtype).itemsize


def _tile(n, target, mult=8):
    t = min(n, target)
    while t >= mult:
        if n % t == 0 and t % mult == 0:
            return t
        t -= 1
    return n


def _dot(a, b):
    return jnp.dot(a.astype(_MM), b.astype(_MM), preferred_element_type=F32)


def _dot_nt(a, b):
    return lax.dot_general(a.astype(_MM), b.astype(_MM), (((1,), (1,)), ((), ())), preferred_element_type=F32)


def _dot_tn(a, b):
    return lax.dot_general(a.astype(_MM), b.astype(_MM), (((0,), (0,)), ((), ())), preferred_element_type=F32)


def _sigmoid(x):
    return jax.nn.sigmoid(x)


_GELU_K = math.sqrt(2.0 / math.pi)


def _gelu(x):
    return x * (0.5 * (1.0 + jnp.tanh(_GELU_K * (x + 0.044715 * (x * x * x)))))


def _gelu_grad(x):
    th = jnp.tanh(_GELU_K * (x + 0.044715 * (x * x * x)))
    return 0.5 * (1.0 + th) + 0.5 * x * (1.0 - th * th) * (_GELU_K * (1.0 + 3.0 * 0.044715 * (x * x)))


def _rows_sum(v):
    return jnp.sum(v, axis=0, keepdims=True)


def _rms_stats(x):
    rstd = lax.rsqrt(jnp.mean(x * x, axis=-1, keepdims=True) + EPS)
    return x * rstd, rstd


def _rms_bwd(dxh, xh, rstd):
    return rstd * (dxh - xh * jnp.mean(dxh * xh, axis=-1, keepdims=True))


def _rowcall(name, body, n_rows, tm, row_ins, full_ins, row_outs, acc_outs=(), temp_cols=0):
    grid = (n_rows // tm,)
    in_specs = [pl.BlockSpec((tm, a.shape[1]), lambda i: (i, 0)) for a in row_ins]
    in_specs += [pl.BlockSpec(a.shape, lambda i: (0, 0), pipeline_mode=pl.Buffered(1)) for a in full_ins]
    out_shape = [jax.ShapeDtypeStruct((n_rows, c), dt) for c, dt in row_outs]
    out_shape += [jax.ShapeDtypeStruct(s, dt) for s, dt in acc_outs]
    out_specs = [pl.BlockSpec((tm, c), lambda i: (i, 0)) for c, _ in row_outs]
    out_specs += [pl.BlockSpec(s, lambda i: (0, 0)) for s, _ in acc_outs]
    blocks = sum(_nbytes((tm, a.shape[1]), a.dtype) for a in row_ins)
    blocks += sum(_nbytes((tm, c), dt) for c, dt in row_outs) + sum(_nbytes(s, dt) for s, dt in acc_outs)
    resident = sum(_nbytes(a.shape, a.dtype) for a in full_ins)
    limit = _vmem_limit(blocks, resident + _nbytes((tm, temp_cols), F32))
    res = pl.pallas_call(
        body, name=name, grid=grid, in_specs=in_specs, out_specs=out_specs, out_shape=out_shape,
        compiler_params=pltpu.CompilerParams(
            dimension_semantics=("arbitrary" if acc_outs else "parallel",), vmem_limit_bytes=limit),
    )(*row_ins, *full_ins)
    return res


def _first_step():
    return pl.program_id(0) == 0


def _acc(ref, val):
    @pl.when(_first_step())
    def _():
        ref[...] = val

    @pl.when(jnp.logical_not(_first_step()))
    def _():
        ref[...] += val


def _mm_tn(name, a, g):
    n_rows, k = a.shape
    n = g.shape[1]
    tk = k if k <= 1024 else _tile(k, 1408, 128)
    tn = n if n <= 1024 else _tile(n, 1408, 128)
    tl = _tile(n_rows, 2048, 16)

    def body(a_ref, g_ref, o_ref):
        @pl.when(pl.program_id(2) == 0)
        def _():
            o_ref[...] = jnp.zeros_like(o_ref)
        o_ref[...] += _dot_tn(a_ref[...], g_ref[...])

    blocks = _nbytes((tl, tk), a.dtype) + _nbytes((tl, tn), g.dtype) + _nbytes((tk, tn), F32)
    return pl.pallas_call(
        body, name=name, grid=(k // tk, n // tn, n_rows // tl),
        in_specs=[pl.BlockSpec((tl, tk), lambda i, j, l: (l, i)), pl.BlockSpec((tl, tn), lambda i, j, l: (l, j))],
        out_specs=pl.BlockSpec((tk, tn), lambda i, j, l: (i, j)),
        out_shape=jax.ShapeDtypeStruct((k, n), F32),
        compiler_params=pltpu.CompilerParams(
            dimension_semantics=("parallel", "parallel", "arbitrary"),
            vmem_limit_bytes=_vmem_limit(blocks, 2 * _nbytes((tl, max(tk, tn)), F32) + _nbytes((tk, tn), F32))),
    )(a, g)


def _place():
    return lax.axis_index("x"), lax.axis_index("y"), lax.axis_index("c")


def _flip(v, bit):
    return 1 - v if bit else v


def _exchange(name, mode, arrays):
    n = len(arrays)
    if mode == "gather8":
        rel = [((k >> 2) & 1, (k >> 1) & 1, k & 1) for k in range(1, 8)]
        out_shape = [jax.ShapeDtypeStruct((8,) + a.shape, a.dtype) for a in arrays]
    elif mode == "scatter4":
        rel = [((k >> 1) & 1, k & 1, 0) for k in range(1, 4)]
        out_shape = [jax.ShapeDtypeStruct(a.shape, a.dtype) for a in arrays]
    else:
        rel = [(0, 0, 1)]
        out_shape = [jax.ShapeDtypeStruct(a.shape, a.dtype) for a in arrays]
    n_rel = len(rel)

    def body(*refs):
        ins, outs = refs[:n], refs[n:2 * n]
        send_sems, recv_sems, local_sems = refs[2 * n:]
        x, y, c = _place()

        def slot(px, py, pc):
            return 4 * px + 2 * py + pc if mode == "gather8" else 2 * px + py

        mine = slot(x, y, c)
        local = []
        if mode != "swap":
            for a in range(n):
                src = ins[a].at[mine] if mode == "scatter4" else ins[a]
                local.append(pltpu.make_async_copy(src, outs[a].at[mine], local_sems.at[a]))
            for cp in local:
                cp.start()

        def remote(r, a):
            px, py, pc = _flip(x, rel[r][0]), _flip(y, rel[r][1]), _flip(c, rel[r][2])
            theirs = slot(px, py, pc)
            if mode == "swap":
                src, dst_there, dst_here = ins[a], outs[a], outs[a]
            elif mode == "scatter4":
                src, dst_there, dst_here = ins[a].at[theirs], outs[a].at[mine], outs[a].at[theirs]
            else:
                src, dst_there, dst_here = ins[a], outs[a].at[mine], outs[a].at[theirs]
            k = r * n + a
            push = pltpu.make_async_remote_copy(src_ref=src, dst_ref=dst_there, send_sem=send_sems.at[k],
                                                recv_sem=recv_sems.at[k], device_id=(px, py, pc), device_id_type=MESH)
            land = pltpu.make_async_remote_copy(src_ref=src, dst_ref=dst_here, send_sem=send_sems.at[k],
                                                recv_sem=recv_sems.at[k], device_id=(px, py, pc), device_id_type=MESH)
            return push, land

        copies = [remote(r, a) for r in range(n_rel) for a in range(n)]
        for push, _ in copies:
            push.start()
        for _, land in copies:
            land.wait_recv()
        for push, _ in copies:
            push.wait_send()
        for cp in local:
            cp.wait()

    return pl.pallas_call(
        body, name=name, in_specs=[ANY] * n, out_specs=[ANY] * n, out_shape=out_shape,
        scratch_shapes=[pltpu.SemaphoreType.DMA((n_rel * n,)), pltpu.SemaphoreType.DMA((n_rel * n,)),
                        pltpu.SemaphoreType.DMA((max(n, 1),))],
    )(*arrays)


def _gather_chips_two_level(name, arrays):
    n = len(arrays)
    rel = [((k >> 1) & 1, k & 1) for k in range(1, 4)]
    halves = [a.shape[0] // 2 for a in arrays]

    def body(*refs):
        ins, outs = refs[:n], refs[n:2 * n]
        ici_send, ici_recv, d2d_send, d2d_recv, local_sems = refs[2 * n:]
        x, y, c = _place()
        mine = 2 * x + y
        local = [pltpu.make_async_copy(ins[a], outs[a].at[mine], local_sems.at[a]) for a in range(n)]
        for cp in local:
            cp.start()

        def ici(r, a):
            px, py = _flip(x, rel[r][0]), _flip(y, rel[r][1])
            theirs = 2 * px + py
            rows = pl.ds(c * halves[a], halves[a])
            k = r * n + a
            push = pltpu.make_async_remote_copy(src_ref=ins[a].at[rows], dst_ref=outs[a].at[mine, rows],
                                                send_sem=ici_send.at[k], recv_sem=ici_recv.at[k],
                                                device_id=(px, py, c), device_id_type=MESH)
            land = pltpu.make_async_remote_copy(src_ref=ins[a].at[rows], dst_ref=outs[a].at[theirs, rows],
                                                send_sem=ici_send.at[k], recv_sem=ici_recv.at[k],
                                                device_id=(px, py, c), device_id_type=MESH)
            there = pl.ds((1 - c) * halves[a], halves[a])
            forward = pltpu.make_async_remote_copy(src_ref=outs[a].at[theirs, rows], dst_ref=outs[a].at[theirs, rows],
                                                   send_sem=d2d_send.at[k], recv_sem=d2d_recv.at[k],
                                                   device_id=(x, y, 1 - c), device_id_type=MESH)
            back = pltpu.make_async_remote_copy(src_ref=outs[a].at[theirs, rows], dst_ref=outs[a].at[theirs, there],
                                                send_sem=d2d_send.at[k], recv_sem=d2d_recv.at[k],
                                                device_id=(x, y, 1 - c), device_id_type=MESH)
            return push, land, forward, back

        copies = [ici(r, a) for r in range(len(rel)) for a in range(n)]
        for push, _, _, _ in copies:
            push.start()
        for _, land, forward, _ in copies:
            land.wait_recv()
            forward.start()
        for _, _, _, back in copies:
            back.wait_recv()
        for push, _, forward, _ in copies:
            push.wait_send()
            forward.wait_send()
        for cp in local:
            cp.wait()

    n_sem = len(rel) * n
    return pl.pallas_call(
        body, name=name, in_specs=[ANY] * n, out_specs=[ANY] * n,
        out_shape=[jax.ShapeDtypeStruct((4,) + a.shape, a.dtype) for a in arrays],
        scratch_shapes=[pltpu.SemaphoreType.DMA((n_sem,))] * 4 + [pltpu.SemaphoreType.DMA((n,))],
    )(*arrays)


def _sum_slots(name, stacked):
    p, rows, cols = stacked.shape
    tr = _tile(rows, 256)

    def body(s_ref, o_ref):
        acc = s_ref[0].astype(F32)
        for j in range(1, p):
            acc = acc + s_ref[j].astype(F32)
        o_ref[...] = acc

    return pl.pallas_call(
        body, name=name, grid=(rows // tr,),
        in_specs=[pl.BlockSpec((p, tr, cols), lambda i: (0, i, 0))],
        out_specs=pl.BlockSpec((tr, cols), lambda i: (i, 0)),
        out_shape=jax.ShapeDtypeStruct((rows, cols), F32),
        compiler_params=pltpu.CompilerParams(dimension_semantics=("parallel",)),
    )(stacked)


def _add_pair(name, a, b):
    rows, cols = a.shape
    tr = _tile(rows, 512, 16)

    def body(a_ref, b_ref, o_ref):
        o_ref[...] = (a_ref[...].astype(F32) + b_ref[...].astype(F32)).astype(o_ref.dtype)

    spec = pl.BlockSpec((tr, cols), lambda i: (i, 0))
    return pl.pallas_call(
        body, name=name, grid=(rows // tr,), in_specs=[spec, spec], out_specs=spec,
        out_shape=jax.ShapeDtypeStruct((rows, cols), a.dtype),
        compiler_params=pltpu.CompilerParams(dimension_semantics=("parallel",)),
    )(a, b)


def _adamw(name, parts, w, m, v):
    rows, cols = w.shape
    tr = _tile(rows, 256)
    n_parts = len(parts)

    def body(*refs):
        part_refs = refs[:n_parts]
        w_ref, m_ref, v_ref, g_out, d_out, m_out, v_out = refs[n_parts:]
        g = None
        for pr in part_refs:
            if len(pr.shape) == 3:
                for j in range(pr.shape[0]):
                    g = pr[j] if g is None else g + pr[j]
            else:
                g = pr[...] if g is None else g + pr[...]
        m_new = ADAM_B1 * m_ref[...] + (1.0 - ADAM_B1) * g
        v_new = ADAM_B2 * v_ref[...] + (1.0 - ADAM_B2) * jnp.square(g)
        m_hat = m_new / (1.0 - ADAM_B1 ** ADAM_STEP)
        v_hat = v_new / (1.0 - ADAM_B2 ** ADAM_STEP)
        g_out[...] = g
        d_out[...] = -ADAM_LR * (m_hat / (jnp.sqrt(v_hat) + ADAM_EPS) + ADAM_WD * w_ref[...])
        m_out[...] = m_new
        v_out[...] = v_new

    spec2 = pl.BlockSpec((tr, cols), lambda i: (i, 0))
    in_specs = [pl.BlockSpec((p.shape[0], tr, cols), lambda i: (0, i, 0)) if p.ndim == 3 else spec2 for p in parts]
    blocks = sum(_nbytes((p.shape[0] if p.ndim == 3 else 1, tr, cols), F32) for p in parts) + 7 * _nbytes((tr, cols), F32)
    return pl.pallas_call(
        body, name=name, grid=(rows // tr,),
        in_specs=in_specs + [spec2] * 3, out_specs=[spec2] * 4,
        out_shape=[jax.ShapeDtypeStruct((rows, cols), F32)] * 4,
        compiler_params=pltpu.CompilerParams(dimension_semantics=("parallel",),
                                             vmem_limit_bytes=_vmem_limit(blocks, 4 * _nbytes((tr, cols), F32))),
    )(*parts, w, m, v)


def _mod_fwd(c_all, w_ada2d, depth):
    nb, d = c_all.shape
    cols = w_ada2d.shape[1]
    tn = _tile(cols, 512, 128)

    def body(c_ref, w_ref, o_ref):
        cv = c_ref[...]
        o_ref[...] = _dot(cv * _sigmoid(cv), w_ref[...])

    return pl.pallas_call(
        body, name="mod_fwd", grid=(depth, cols // tn),
        in_specs=[pl.BlockSpec((nb, d), lambda l, j: (0, 0)), pl.BlockSpec((d, tn), lambda l, j: (l, j))],
        out_specs=pl.BlockSpec((nb, tn), lambda l, j: (l, j)),
        out_shape=jax.ShapeDtypeStruct((depth * nb, cols), F32),
        compiler_params=pltpu.CompilerParams(dimension_semantics=("parallel", "parallel")),
    )(c_all, w_ada2d)


def _wada_bwd(c_all, dmod2d, depth):
    nb, d = c_all.shape
    cols = dmod2d.shape[1]
    tn = _tile(cols, 512, 128)

    def body(c_ref, g_ref, o_ref):
        cv = c_ref[...]
        o_ref[...] = _dot_tn(cv * _sigmoid(cv), g_ref[...])

    return pl.pallas_call(
        body, name="wada_bwd", grid=(depth, cols // tn),
        in_specs=[pl.BlockSpec((nb, d), lambda l, j: (0, 0)), pl.BlockSpec((nb, tn), lambda l, j: (l, j))],
        out_specs=pl.BlockSpec((d, tn), lambda l, j: (l, j)),
        out_shape=jax.ShapeDtypeStruct((depth * d, cols), F32),
        compiler_params=pltpu.CompilerParams(dimension_semantics=("parallel", "parallel")),
    )(c_all, dmod2d)


def _rope_tables(pos_col, inv_freq_lane, nope):
    n_rows = pos_col.shape[0]
    tm = _tile(n_rows, 512)
    half = QK_ROPE // 2

    def body(p_ref, f_ref, c_ref, s1_ref, s2_ref):
        ang = p_ref[...] * f_ref[...]
        lane = lax.broadcasted_iota(jnp.int32, ang.shape, 1)
        first = (lane >= nope) & (lane < nope + half)
        second = (lane >= nope + half) & (lane < nope + 2 * half)
        cos, sin = jnp.cos(ang), jnp.sin(ang)
        c_ref[...] = jnp.where(first | second, cos, 1.0)
        s1_ref[...] = jnp.where(first, -sin, 0.0)
        s2_ref[...] = jnp.where(second, sin, 0.0)

    return _rowcall("rope_tables", body, n_rows, tm, [pos_col], [inv_freq_lane], [(HEAD_PAD, F32)] * 3)


def _rope(q, c, s1, s2):
    w = q.shape[1]
    return q * c + pltpu.roll(q, w - QK_ROPE // 2, axis=1) * s1 + pltpu.roll(q, QK_ROPE // 2, axis=1) * s2


def _rope_adjoint(dr, c, s1, s2):
    w = dr.shape[1]
    return dr * c + pltpu.roll(dr * s1, QK_ROPE // 2, axis=1) + pltpu.roll(dr * s2, w - QK_ROPE // 2, axis=1)


def _ssm_disc(ar, ai, log_dt, br, bi):
    dt = jnp.exp(log_dt)
    mag = jnp.exp(ar * dt)
    abr = mag * jnp.cos(ai * dt)
    abi = mag * jnp.sin(ai * dt)
    den = ar * ar + ai * ai
    nr = abr - 1.0
    ni = abi
    cr = (nr * ar + ni * ai) / den
    ci = (ni * ar - nr * ai) / den
    return abr, abi, cr * br - ci * bi, cr * bi + ci * br


def _ssm_disc_fwd(ar, ai, log_dt, br, bi):
    n_rows, m = br.shape
    tm = _tile(n_rows, 1024)

    def body(ar_ref, ai_ref, dt_ref, br_ref, bi_ref, o1, o2, o3, o4):
        o1[...], o2[...], o3[...], o4[...] = _ssm_disc(ar_ref[...], ai_ref[...], dt_ref[...], br_ref[...], bi_ref[...])

    return _rowcall("ssm_disc_fwd", body, n_rows, tm, [ar, ai, log_dt, br, bi], [],
                    [(1, F32), (1, F32), (m, F32), (m, F32)])


def _ssm_disc_bwd(ar, ai, log_dt, br, bi, g_abr, g_abi, g_bbr, g_bbi):
    n_rows, m = br.shape
    tm = _tile(n_rows, 1024)

    def body(ar_ref, ai_ref, dt_ref, br_ref, bi_ref, g1, g2, g3, g4, o1, o2, o3, o4, o5):
        _, vjp = jax.vjp(_ssm_disc, ar_ref[...], ai_ref[...], dt_ref[...], br_ref[...], bi_ref[...])
        o1[...], o2[...], o3[...], o4[...], o5[...] = vjp((g1[...], g2[...], g3[...], g4[...]))

    return _rowcall("ssm_disc_bwd", body, n_rows, tm, [ar, ai, log_dt, br, bi, g_abr, g_abi, g_bbr, g_bbi], [],
                    [(1, F32), (1, F32), (1, F32), (m, F32), (m, F32)])


def _lane_sum(v2d):
    def body(v_ref, o_ref):
        o_ref[...] = jnp.sum(v_ref[...], axis=1, keepdims=True)
    return pl.pallas_call(body, name="lane_sum", out_shape=jax.ShapeDtypeStruct((v2d.shape[0], 1), F32))(v2d)


def _in_fwd(x, n1g, sc1, sh1, w_parts, tm):
    n_rows = x.shape[0]
    widths = [w.shape[1] for w in w_parts]

    def body(x_ref, g_ref, sc_ref, sh_ref, *rest):
        w_refs, (hb_ref, *z_refs) = rest[:len(w_parts)], rest[len(w_parts):]
        xh, _ = _rms_stats(x_ref[...])
        h = (xh * g_ref[...]) * (1.0 + sc_ref[...]) + sh_ref[...]
        hb = h.astype(_MM)
        hb_ref[...] = hb.astype(_ACT)
        for w_ref, z_ref in zip(w_refs, z_refs):
            z_ref[...] = _dot(hb, w_ref[...])

    return _rowcall("in_fwd", body, n_rows, tm, [x], [n1g, sc1, sh1, *w_parts],
                    [(x.shape[1], _ACT)] + [(w, F32) for w in widths], temp_cols=4 * x.shape[1])


def _ssm_fwd(u, bre_blk, bim_blk, abr_row, abi_row, cre_blk, cimneg_blk, d_row):
    n_rows, sw = u.shape
    gp = abr_row.shape[1]
    t = _tile(n_rows, 256)

    def body(u_ref, bre_ref, bim_ref, ar_ref, ai_ref, cre_ref, cim_ref, d_ref, y_ref, hre_ref, him_ref, cr, ci):
        @pl.when(_first_step())
        def _():
            cr[...] = jnp.zeros_like(cr)
            ci[...] = jnp.zeros_like(ci)

        uv = u_ref[...]
        ub = uv.astype(_MM)
        hre_ref[...] = _dot(ub, bre_ref[...])
        him_ref[...] = _dot(ub, bim_ref[...])
        a_r, a_i = ar_ref[...], ai_ref[...]

        def step(k, carry):
            pr, pi = carry
            row = pl.ds(k, 1)
            hr = a_r * pr - a_i * pi + hre_ref[row, :]
            hi = a_r * pi + a_i * pr + him_ref[row, :]
            hre_ref[row, :] = hr
            him_ref[row, :] = hi
            return hr, hi

        pr, pi = lax.fori_loop(0, t, step, (cr[0:1, :], ci[0:1, :]), unroll=8)
        cr[0:1, :] = pr
        ci[0:1, :] = pi
        y_ref[...] = _dot(hre_ref[...], cre_ref[...]) + _dot(him_ref[...], cim_ref[...]) + d_ref[...] * uv

    row = lambda c: pl.BlockSpec((t, c), lambda i: (i, 0))
    full = lambda a: pl.BlockSpec(a.shape, lambda i: (0, 0), pipeline_mode=pl.Buffered(1))
    blocks = _nbytes((t, sw), F32) * 2 + 2 * _nbytes((t, gp), F32)
    resident = 4 * _nbytes((sw, gp), _MM)
    return pl.pallas_call(
        body, name="ssm_fwd", grid=(n_rows // t,),
        in_specs=[row(sw), full(bre_blk), full(bim_blk), full(abr_row), full(abi_row), full(cre_blk),
                  full(cimneg_blk), full(d_row)],
        out_specs=[row(sw), row(gp), row(gp)],
        out_shape=[jax.ShapeDtypeStruct((n_rows, sw), F32), jax.ShapeDtypeStruct((n_rows, gp), F32),
                   jax.ShapeDtypeStruct((n_rows, gp), F32)],
        scratch_shapes=[pltpu.VMEM((8, gp), F32), pltpu.VMEM((8, gp), F32)],
        compiler_params=pltpu.CompilerParams(dimension_semantics=("arbitrary",),
                                             vmem_limit_bytes=_vmem_limit(blocks, resident + 3 * _nbytes((t, gp), F32))),
    )(u, bre_blk, bim_blk, abr_row, abi_row, cre_blk, cimneg_blk, d_row)


def _mla_prep_fwd(cq, ckv, kr, rc, rs1, rs2, gq, gkv, wuq, wuk, wuv, vdim, tm):
    n_rows = cq.shape[0]

    def body(cq_ref, ckv_ref, kr_ref, c_ref, s1_ref, s2_ref, gq_ref, gkv_ref, wuq_ref, wuk_ref, wuv_ref,
             q_ref, k_ref, v_ref, cqn_ref, ckvn_ref):
        c, s1, s2 = c_ref[...], s1_ref[...], s2_ref[...]
        c8, s18, s28 = (jnp.tile(a, (1, N_HEADS)) for a in (c, s1, s2))
        xh, _ = _rms_stats(cq_ref[...])
        cqn = (xh * gq_ref[...]).astype(_MM)
        cqn_ref[...] = cqn.astype(_ACT)
        q_ref[...] = _rope(_dot(cqn, wuq_ref[...]), c8, s18, s28).astype(_ACT)
        xh, _ = _rms_stats(ckv_ref[...])
        ckvn = (xh * gkv_ref[...]).astype(_MM)
        ckvn_ref[...] = ckvn.astype(_ACT)
        kpe = _rope(kr_ref[...], c, s1, s2)
        k_ref[...] = (_dot(ckvn, wuk_ref[...]) + jnp.tile(kpe, (1, N_HEADS))).astype(_ACT)
        v = _dot(ckvn, wuv_ref[...])
        lane = lax.broadcasted_iota(jnp.int32, v.shape, 1)
        v_ref[...] = jnp.where((lane & (HEAD_PAD - 1)) == vdim, 1.0, v).astype(_ACT)

    return _rowcall("mla_prep_fwd", body, n_rows, tm, [cq, ckv, kr, rc, rs1, rs2], [gq, gkv, wuq, wuk, wuv],
                    [(QW, _ACT), (QW, _ACT), (QW, _ACT), (cq.shape[1], _ACT), (ckv.shape[1], _ACT)], temp_cols=6 * QW)


def _causal_steps(n_blocks, key_major):
    if key_major:
        pairs = [(qi, ki) for ki in range(n_blocks) for qi in range(ki, n_blocks)]
    else:
        pairs = [(qi, ki) for qi in range(n_blocks) for ki in range(qi + 1)]
    return (jnp.asarray(np.array([p[0] for p in pairs], np.int32)), jnp.asarray(np.array([p[1] for p in pairs], np.int32)))


def _attn_fwd(q, k, v, pos_col, pos_row, scale, vdim):
    n_rows = q.shape[0]
    ta = _tile(n_rows, 512, 128)
    nb = n_rows // ta
    hb = ATTN_HEADS_PER_STEP
    wide = hb * HEAD_PAD
    qmap, kmap = _causal_steps(nb, key_major=False)
    c2 = scale * LOG2_E

    def body(qm, km, q_ref, k_ref, v_ref, pq_ref, pk_ref, o_ref, lse_ref, m_sc, acc_sc):
        s_id = pl.program_id(1)
        qi, ki = qm[s_id], km[s_id]

        @pl.when(ki == 0)
        def _():
            m_sc[...] = jnp.full_like(m_sc, NEG_INF)
            acc_sc[...] = jnp.zeros_like(acc_sc)

        def update(on_diagonal):
            if on_diagonal:
                visible = pk_ref[...] <= pq_ref[...]
            for h in range(hb):
                lanes = slice(h * HEAD_PAD, (h + 1) * HEAD_PAD)
                s = _dot_nt(q_ref[:, lanes], k_ref[:, lanes])
                if on_diagonal:
                    s = jnp.where(visible, s, NEG_INF)
                m_prev = m_sc[:, lanes]
                m_new = jnp.maximum(m_prev, jnp.max(s, axis=1, keepdims=True))
                alpha = jnp.exp2((m_prev - m_new) * c2)
                p = jnp.exp2((s - m_new[:, :1]) * c2)
                acc_new = alpha * acc_sc[:, lanes] + _dot(p, v_ref[:, lanes])
                if on_diagonal:
                    l_new = acc_new[:, vdim:vdim + 1]
                    o_ref[:, lanes] = acc_new / l_new
                    lse_ref[:, lanes] = m_new * c2 + jnp.log2(l_new)
                else:
                    acc_sc[:, lanes] = acc_new
                    m_sc[:, lanes] = m_new

        pl.when(ki != qi)(functools.partial(update, False))
        pl.when(ki == qi)(functools.partial(update, True))

    qspec = pl.BlockSpec((ta, wide), lambda h, s, qm, km: (qm[s], h))
    kspec = pl.BlockSpec((ta, wide), lambda h, s, qm, km: (km[s], h))
    grid_spec = pltpu.PrefetchScalarGridSpec(
        num_scalar_prefetch=2, grid=(N_HEADS // hb, int(qmap.shape[0])),
        in_specs=[qspec, kspec, kspec,
                  pl.BlockSpec((ta, 1), lambda h, s, qm, km: (qm[s], 0)),
                  pl.BlockSpec((1, ta), lambda h, s, qm, km: (0, km[s]))],
        out_specs=[qspec, qspec],
        scratch_shapes=[pltpu.VMEM((ta, wide), F32)] * 2)
    return pl.pallas_call(
        body, name="attn_fwd", grid_spec=grid_spec,
        out_shape=[jax.ShapeDtypeStruct((n_rows, QW), F32), jax.ShapeDtypeStruct((n_rows, QW), F32)],
        compiler_params=pltpu.CompilerParams(
            dimension_semantics=("parallel", "arbitrary"),
            vmem_limit_bytes=_vmem_limit(8 * _nbytes((ta, wide), F32), 6 * hb * _nbytes((ta, ta), F32))),
    )(qmap, kmap, q, k, v, pos_col, pos_row)


def _mix_fwd(ypre, o, ga, gb, x, g1, bglu, wglu, wa, wb, wout, tm):
    n_rows, d = x.shape
    sw = ypre.shape[1]

    def body(y_ref, o_ref, ga_ref, gb_ref, x_ref, g1_ref, bglu_ref, wglu_ref, wa_ref, wb_ref, wout_ref,
             yg_ref, ya_ref, yb_ref, mg_ref, mo_ref, xo_ref):
        ys = _gelu(y_ref[...])
        yg = ys * _sigmoid(_dot(ys, wglu_ref[...]) + bglu_ref[...])
        yg_ref[...] = yg.astype(_ACT)
        ya = _dot(yg, wa_ref[...])
        yb = _dot(o_ref[...], wb_ref[...])
        ya_ref[...] = ya
        yb_ref[...] = yb
        merged = _sigmoid(ga_ref[...]) * ya + _sigmoid(gb_ref[...]) * yb
        mg_ref[...] = merged.astype(_ACT)
        mo = _dot(merged, wout_ref[...])
        mo_ref[...] = mo
        xo_ref[...] = x_ref[...] + g1_ref[...] * mo

    return _rowcall("mix_fwd", body, n_rows, tm, [ypre, o, ga, gb, x], [g1, bglu, wglu, wa, wb, wout],
                    [(sw, _ACT), (d, F32), (d, F32), (d, _ACT), (d, F32), (d, F32)], temp_cols=4 * d)


def _ffn_fwd(x, n2g, sc2, sh2, g2, wg, wu, wd, tm):
    n_rows, d = x.shape
    ff = wg.shape[1]

    def body(x_ref, g_ref, sc_ref, sh_ref, g2_ref, wg_ref, wu_ref, wd_ref, hb_ref, a_ref, b_ref, d_ref, xo_ref):
        xv = x_ref[...]
        xh, _ = _rms_stats(xv)
        hb = ((xh * g_ref[...]) * (1.0 + sc_ref[...]) + sh_ref[...]).astype(_MM)
        hb_ref[...] = hb.astype(_ACT)
        a = _dot(hb, wg_ref[...])
        b = _dot(hb, wu_ref[...])
        a_ref[...] = a
        b_ref[...] = b
        dn = _dot((a * _sigmoid(a)) * b, wd_ref[...])
        d_ref[...] = dn
        xo_ref[...] = xv + g2_ref[...] * dn

    return _rowcall("ffn_fwd", body, n_rows, tm, [x], [n2g, sc2, sh2, g2, wg, wu, wd],
                    [(d, _ACT), (ff, F32), (ff, F32), (d, F32), (d, F32)], temp_cols=3 * ff)


def _head(x, fg, target, tm):
    n_rows, d = x.shape

    def body(x_ref, t_ref, g_ref, dx_ref, loss_ref, dg_ref):
        xh, rstd = _rms_stats(x_ref[...])
        err = xh * g_ref[...] - t_ref[...]
        part = jnp.sum(jnp.mean(err * err, axis=-1, keepdims=True), axis=0, keepdims=True) * 0.5
        _acc(loss_ref, jnp.broadcast_to(part, loss_ref.shape))
        dy = err * (1.0 / d)
        _acc(dg_ref, _rows_sum(dy * xh))
        dx_ref[...] = _rms_bwd(dy * g_ref[...], xh, rstd)

    return _rowcall("head", body, n_rows, tm, [x, target], [fg], [(d, F32)], [((1, 128), F32), ((1, d), F32)],
                    temp_cols=4 * d)


def _ffn_bwd(dxo, xmid, a, b, dn, n2g, sc2, g2, wg, wu, wd, tm):
    n_rows, d = dxo.shape
    ff = a.shape[1]

    def act_body(dxo_ref, a_ref, b_ref, dn_ref, g2_ref, wd_ref, da_ref, db_ref, f_ref, dd_ref, dg2_ref):
        dxo_v = dxo_ref[...]
        dd = dxo_v * g2_ref[...]
        dd_ref[...] = dd.astype(_ACT)
        _acc(dg2_ref, _rows_sum(dxo_v * dn_ref[...]))
        df = _dot_nt(dd, wd_ref[...])
        av, bv = a_ref[...], b_ref[...]
        sa = _sigmoid(av)
        si = av * sa
        f_ref[...] = (si * bv).astype(_ACT)
        da_ref[...] = (df * bv * (sa * (1.0 + av * (1.0 - sa)))).astype(_ACT)
        db_ref[...] = (df * si).astype(_ACT)

    da, db, f, dd, dg2 = _rowcall("ffn_bwd_act", act_body, n_rows, 2 * tm, [dxo, a, b, dn], [g2, wd],
                                  [(ff, _ACT), (ff, _ACT), (ff, _ACT), (d, _ACT)], [((1, d), F32)], temp_cols=4 * ff)

    def in_body(dxo_ref, x_ref, da_ref, db_ref, g_ref, sc_ref, wg_ref, wu_ref, dx_ref, dsh_ref, dsc_ref, dn2_ref):
        dh = _dot_nt(da_ref[...], wg_ref[...]) + _dot_nt(db_ref[...], wu_ref[...])
        xh, rstd = _rms_stats(x_ref[...])
        yg = xh * g_ref[...]
        _acc(dsh_ref, _rows_sum(dh))
        _acc(dsc_ref, _rows_sum(dh * yg))
        dy = dh * (1.0 + sc_ref[...])
        _acc(dn2_ref, _rows_sum(dy * xh))
        dx_ref[...] = dxo_ref[...] + _rms_bwd(dy * g_ref[...], xh, rstd)

    dx, dsh, dsc, dn2 = _rowcall("ffn_bwd_in", in_body, n_rows, 2 * tm, [dxo, xmid, da, db], [n2g, sc2, wg, wu],
                                 [(d, F32)], [((1, d), F32)] * 3, temp_cols=5 * d)
    return dx, da, db, f, dd, dg2, dsh, dsc, dn2


def _mix_bwd(dxm, mo, ya, yb, ga, gb, ypre, o, g1, bglu, wglu, wa, wb, wout, tm):
    n_rows, d = dxm.shape
    sw = ypre.shape[1]

    def body(dxm_ref, mo_ref, ya_ref, yb_ref, ga_ref, gb_ref, y_ref, o_ref, g1_ref, bglu_ref, wglu_ref, wa_ref, wb_ref,
             wout_ref, dmo_ref, dya_ref, dyb_ref, dt_ref, ys_ref, dga_ref, dgb_ref, dy_ref, do_ref, delta_ref,
             dg1_ref, dbg_ref):
        dxm_v = dxm_ref[...]
        dmo = dxm_v * g1_ref[...]
        dmo_ref[...] = dmo.astype(_ACT)
        _acc(dg1_ref, _rows_sum(dxm_v * mo_ref[...]))
        dmg = _dot_nt(dmo, wout_ref[...])
        sa, sb = _sigmoid(ga_ref[...]), _sigmoid(gb_ref[...])
        dya, dyb = dmg * sa, dmg * sb
        dya_ref[...] = dya.astype(_ACT)
        dyb_ref[...] = dyb.astype(_ACT)
        dga_ref[...] = (dmg * ya_ref[...] * (sa * (1.0 - sa))).astype(_ACT)
        dgb_ref[...] = (dmg * yb_ref[...] * (sb * (1.0 - sb))).astype(_ACT)
        do = _dot_nt(dyb, wb_ref[...])
        do_ref[...] = do
        prod = do * o_ref[...]
        for h in range(N_HEADS):
            lanes = slice(h * HEAD_PAD, (h + 1) * HEAD_PAD)
            delta_ref[:, lanes] = jnp.broadcast_to(jnp.sum(prod[:, lanes], axis=1, keepdims=True), (prod.shape[0], HEAD_PAD))
        dyg = _dot_nt(dya, wa_ref[...])
        yv = y_ref[...]
        ys = _gelu(yv)
        ys_ref[...] = ys.astype(_ACT)
        sg = _sigmoid(_dot(ys, wglu_ref[...]) + bglu_ref[...])
        dt = dyg * ys * (sg * (1.0 - sg))
        dt_ref[...] = dt.astype(_ACT)
        _acc(dbg_ref, _rows_sum(dt))
        dys = dyg * sg + _dot_nt(dt, wglu_ref[...])
        dy_ref[...] = dys * _gelu_grad(yv)

    return _rowcall("mix_bwd", body, n_rows, tm, [dxm, mo, ya, yb, ga, gb, ypre, o], [g1, bglu, wglu, wa, wb, wout],
                    [(d, _ACT), (d, _ACT), (d, _ACT), (sw, _ACT), (sw, _ACT), (d, _ACT), (d, _ACT), (sw, F32), (QW, F32),
                     (QW, F32)],
                    [((1, d), F32), ((1, sw), F32)], temp_cols=6 * d)


def _attn_bwd(q, k, v, do, delta, lse, pos_col, pos_row, scale):
    n_rows = q.shape[0]
    ta = _tile(n_rows, 512, 128)
    nb = n_rows // ta
    qmap, kmap = _causal_steps(nb, key_major=True)
    hb = ATTN_BWD_HEADS_PER_STEP
    wide = hb * HEAD_PAD
    c2 = scale * LOG2_E

    def body(qm, km, q_ref, k_ref, v_ref, do_ref, delta_ref, lse_ref, pq_ref, pk_ref, dq_ref, dk_ref, dv_ref,
             dk_acc, dv_acc):
        s_id = pl.program_id(1)
        qi, ki = qm[s_id], km[s_id]

        @pl.when(s_id == 0)
        def _():
            dq_ref[...] = jnp.zeros_like(dq_ref)

        @pl.when(qi == ki)
        def _():
            dk_acc[...] = jnp.zeros_like(dk_acc)
            dv_acc[...] = jnp.zeros_like(dv_acc)

        rows = pl.ds(pl.multiple_of(qi * ta, ta), ta)

        def update(on_diagonal):
            if on_diagonal:
                visible = pk_ref[...] <= pq_ref[...]
            for h in range(hb):
                lanes = slice(h * HEAD_PAD, (h + 1) * HEAD_PAD)
                qv, kv, dov = q_ref[:, lanes], k_ref[:, lanes], do_ref[:, lanes]
                e = _dot_nt(qv, kv) * c2 - lse_ref[:, lanes][:, :1]
                if on_diagonal:
                    e = jnp.where(visible, e, NEG_INF)
                p = jnp.exp2(e)
                dp = _dot_nt(dov, v_ref[:, lanes])
                ds = p * (dp - delta_ref[:, lanes][:, :1])
                dv_acc[:, lanes] += _dot_tn(p, dov)
                dk_acc[:, lanes] += _dot_tn(ds, qv)
                dq_ref[rows, lanes] += _dot(ds, kv) * scale

        pl.when(ki != qi)(functools.partial(update, False))
        pl.when(ki == qi)(functools.partial(update, True))

        @pl.when(qi == nb - 1)
        def _():
            dk_ref[...] = dk_acc[...] * scale
            dv_ref[...] = dv_acc[...]

    qspec = pl.BlockSpec((ta, wide), lambda h, s, qm, km: (qm[s], h))
    kspec = pl.BlockSpec((ta, wide), lambda h, s, qm, km: (km[s], h))
    grid_spec = pltpu.PrefetchScalarGridSpec(
        num_scalar_prefetch=2, grid=(N_HEADS // hb, int(qmap.shape[0])),
        in_specs=[qspec, kspec, kspec, qspec, qspec, qspec,
                  pl.BlockSpec((ta, 1), lambda h, s, qm, km: (qm[s], 0)),
                  pl.BlockSpec((1, ta), lambda h, s, qm, km: (0, km[s]))],
        out_specs=[pl.BlockSpec((n_rows, wide), lambda h, s, qm, km: (0, h)), kspec, kspec],
        scratch_shapes=[pltpu.VMEM((ta, wide), F32), pltpu.VMEM((ta, wide), F32)])
    return pl.pallas_call(
        body, name="attn_bwd", grid_spec=grid_spec,
        out_shape=[jax.ShapeDtypeStruct((n_rows, QW), F32)] * 3,
        compiler_params=pltpu.CompilerParams(
            dimension_semantics=("parallel", "arbitrary"),
            vmem_limit_bytes=_vmem_limit(12 * _nbytes((ta, wide), F32) + _nbytes((n_rows, wide), F32),
                                         6 * hb * _nbytes((ta, ta), F32))),
    )(qmap, kmap, q, k, v, do, delta, lse, pos_col, pos_row)


def _mla_prep_bwd(dq, dk, dv, cq, ckv, rc, rs1, rs2, gq, gkv, wuq, wuk, wuv, nope, tm):
    n_rows = cq.shape[0]
    ql, kl = cq.shape[1], ckv.shape[1]

    def body(dq_ref, dk_ref, dv_ref, cq_ref, ckv_ref, c_ref, s1_ref, s2_ref, gq_ref, gkv_ref, wuq_ref, wuk_ref,
             wuv_ref, dqp_ref, dcq_ref, dckv_ref, dkr_ref, dgq_ref, dgkv_ref):
        c, s1, s2 = c_ref[...], s1_ref[...], s2_ref[...]
        c8, s18, s28 = (jnp.tile(a, (1, N_HEADS)) for a in (c, s1, s2))
        dqp = _rope_adjoint(dq_ref[...], c8, s18, s28)
        dqp_ref[...] = dqp.astype(_ACT)
        dcqn = _dot_nt(dqp, wuq_ref[...])
        xh, rstd = _rms_stats(cq_ref[...])
        _acc(dgq_ref, _rows_sum(dcqn * xh))
        dcq_ref[...] = _rms_bwd(dcqn * gq_ref[...], xh, rstd).astype(_ACT)
        dkv = dk_ref[...]
        dkpe = dkv[:, 0:HEAD_PAD]
        for h in range(1, N_HEADS):
            dkpe = dkpe + dkv[:, h * HEAD_PAD:(h + 1) * HEAD_PAD]
        lane = lax.broadcasted_iota(jnp.int32, dkpe.shape, 1)
        dkpe = jnp.where((lane >= nope) & (lane < nope + QK_ROPE), dkpe, 0.0)
        dkr_ref[...] = _rope_adjoint(dkpe, c, s1, s2).astype(_ACT)
        dckvn = _dot_nt(dkv, wuk_ref[...]) + _dot_nt(dv_ref[...], wuv_ref[...])
        xh, rstd = _rms_stats(ckv_ref[...])
        _acc(dgkv_ref, _rows_sum(dckvn * xh))
        dckv_ref[...] = _rms_bwd(dckvn * gkv_ref[...], xh, rstd).astype(_ACT)

    return _rowcall("mla_prep_bwd", body, n_rows, tm, [dq, dk, dv, cq, ckv, rc, rs1, rs2], [gq, gkv, wuq, wuk, wuv],
                    [(QW, _ACT), (ql, _ACT), (kl, _ACT), (HEAD_PAD, _ACT)], [((1, ql), F32), ((1, kl), F32)],
                    temp_cols=6 * QW)


def _ssm_bwd(dy, u, hre, him, bre_blk, bim_blk, abr_row, abi_row, cre_blk, cimneg_blk, d_row):
    n_rows, sw = u.shape
    gp = abr_row.shape[1]
    t = _tile(n_rows, 256)
    n_chunks = n_rows // t

    def body(dy_ref, u_ref, hre_ref, him_ref, hbre_ref, hbim_ref, bre_ref, bim_ref, ar_ref, ai_ref, cre_ref, cim_ref,
             d_ref, du_ref, gre_ref, gim_ref, dar_ref, dai_ref, dd_ref, g_re, g_im, hs_re, hs_im, cr, ci):
        i = pl.program_id(0)

        @pl.when(i == 0)
        def _():
            cr[...] = jnp.zeros_like(cr)
            ci[...] = jnp.zeros_like(ci)

        dyv = dy_ref[...]
        dyb = dyv.astype(_MM)
        g_re[...] = _dot_nt(dyb, cre_ref[...])
        g_im[...] = _dot_nt(dyb, cim_ref[...])
        a_r, a_i = ar_ref[...], ai_ref[...]

        def step(k, carry):
            nr, ni = carry
            row = pl.ds(t - 1 - k, 1)
            gr = g_re[row, :] + a_r * nr + a_i * ni
            gi = g_im[row, :] + a_r * ni - a_i * nr
            g_re[row, :] = gr
            g_im[row, :] = gi
            return gr, gi

        nr, ni = lax.fori_loop(0, t, step, (cr[0:1, :], ci[0:1, :]), unroll=8)
        cr[0:1, :] = nr
        ci[0:1, :] = ni
        gr_all, gi_all = g_re[...], g_im[...]
        gre_ref[...] = gr_all.astype(_ACT)
        gim_ref[...] = gi_all.astype(_ACT)
        du_ref[...] = (_dot_nt(gr_all, bre_ref[...]) + _dot_nt(gi_all, bim_ref[...]) + d_ref[...] * dyv).astype(_ACT)
        _acc(dd_ref, _rows_sum(dyv * u_ref[...]))
        is_first_chunk = i == n_chunks - 1
        hs_re[0:8, :] = jnp.where(is_first_chunk, 0.0, hbre_ref[...])
        hs_im[0:8, :] = jnp.where(is_first_chunk, 0.0, hbim_ref[...])
        hs_re[8:t + 8, :] = hre_ref[...]
        hs_im[8:t + 8, :] = him_ref[...]
        hp_re, hp_im = hs_re[pl.ds(7, t), :], hs_im[pl.ds(7, t), :]
        _acc(dar_ref, _rows_sum(gr_all * hp_re + gi_all * hp_im))
        _acc(dai_ref, _rows_sum(gi_all * hp_re - gr_all * hp_im))

    rev = lambda c: pl.BlockSpec((t, c), lambda i: (n_chunks - 1 - i, 0))
    before = pl.BlockSpec((8, gp), lambda i: (jnp.maximum((n_chunks - 1 - i) * (t // 8) - 1, 0), 0))
    full = lambda a: pl.BlockSpec(a.shape, lambda i: (0, 0), pipeline_mode=pl.Buffered(1))
    acc = lambda c: pl.BlockSpec((1, c), lambda i: (0, 0))
    blocks = 2 * _nbytes((t, sw), F32) + 2 * _nbytes((t, gp), F32) + _nbytes((t, sw), _ACT) + 2 * _nbytes((t, gp), _ACT)
    resident = 4 * _nbytes((sw, gp), _MM) + 4 * _nbytes((t + 8, gp), F32)
    return pl.pallas_call(
        body, name="ssm_bwd", grid=(n_chunks,),
        in_specs=[rev(sw), rev(sw), rev(gp), rev(gp), before, before, full(bre_blk), full(bim_blk), full(abr_row),
                  full(abi_row), full(cre_blk), full(cimneg_blk), full(d_row)],
        out_specs=[rev(sw), rev(gp), rev(gp), acc(gp), acc(gp), acc(sw)],
        out_shape=[jax.ShapeDtypeStruct((n_rows, sw), _ACT), jax.ShapeDtypeStruct((n_rows, gp), _ACT),
                   jax.ShapeDtypeStruct((n_rows, gp), _ACT), jax.ShapeDtypeStruct((1, gp), F32),
                   jax.ShapeDtypeStruct((1, gp), F32), jax.ShapeDtypeStruct((1, sw), F32)],
        scratch_shapes=[pltpu.VMEM((t, gp), F32), pltpu.VMEM((t, gp), F32), pltpu.VMEM((t + 8, gp), F32),
                        pltpu.VMEM((t + 8, gp), F32), pltpu.VMEM((8, gp), F32), pltpu.VMEM((8, gp), F32)],
        compiler_params=pltpu.CompilerParams(dimension_semantics=("arbitrary",),
                                             vmem_limit_bytes=_vmem_limit(blocks, resident + 4 * _nbytes((t, gp), F32))),
    )(dy, u, hre, him, hre, him, bre_blk, bim_blk, abr_row, abi_row, cre_blk, cimneg_blk, d_row)


def _in_bwd(dxm, x, dz_parts, n1g, sc1, w_parts, tm):
    n_rows, d = x.shape
    n = len(dz_parts)

    def body(dxm_ref, x_ref, *rest):
        dz_refs = rest[:n]
        g_ref, sc_ref = rest[n], rest[n + 1]
        w_refs = rest[n + 2:2 * n + 2]
        dx_ref, dsh_ref, dsc_ref, dn1_ref = rest[2 * n + 2:]
        dh = None
        for dz_ref, w_ref in zip(dz_refs, w_refs):
            term = _dot_nt(dz_ref[...], w_ref[...])
            dh = term if dh is None else dh + term
        xh, rstd = _rms_stats(x_ref[...])
        yg = xh * g_ref[...]
        _acc(dsh_ref, _rows_sum(dh))
        _acc(dsc_ref, _rows_sum(dh * yg))
        dy = dh * (1.0 + sc_ref[...])
        _acc(dn1_ref, _rows_sum(dy * xh))
        dx_ref[...] = dxm_ref[...] + _rms_bwd(dy * g_ref[...], xh, rstd)

    return _rowcall("in_bwd", body, n_rows, tm, [dxm, x, *dz_parts], [n1g, sc1, *w_parts],
                    [(d, F32)], [((1, d), F32)] * 3, temp_cols=5 * d)


def _pad_heads(w, per_head):
    lead = w.shape[:-1]
    w = w.reshape(lead + (N_HEADS, per_head))
    w = jnp.pad(w, [(0, 0)] * len(lead) + [(0, 0), (0, HEAD_PAD - per_head)])
    return w.reshape(lead + (QW,))


def _unpad_heads(w, per_head):
    lead = w.shape[:-1]
    return w.reshape(lead + (N_HEADS, HEAD_PAD))[..., :per_head].reshape(lead + (N_HEADS * per_head,))


def _cols_from_chips(g):
    ch, dep, r, cs = g.shape
    return g.transpose(1, 2, 0, 3).reshape(dep, r, ch * cs)


def _rows_from_chips(g):
    ch, dep, rs, c = g.shape
    return g.transpose(1, 0, 2, 3).reshape(dep, ch * rs, c)


def _cols_to_chips(w):
    dep, r, c = w.shape
    return w.reshape(dep, r, 4, c // 4).transpose(2, 0, 1, 3)


def _rows_to_chips(w):
    dep, r, c = w.shape
    return w.reshape(dep, 4, r // 4, c).transpose(1, 0, 2, 3)


def _block_diag(b_gxy):
    g, xx, yy = b_gxy.shape
    eye = jnp.eye(g, dtype=b_gxy.dtype)
    return (b_gxy[:, :, None, :] * eye[:, None, :, None]).reshape(g * xx, g * yy)


def _block_diag_extract(full, g):
    xx, yy = full.shape[0] // g, full.shape[1] // g
    eye = jnp.eye(g, dtype=full.dtype)
    return jnp.sum(full.reshape(g, xx, g, yy) * eye[:, None, :, None], axis=2)


def _pack_rows(arrays):
    parts = []
    for a in arrays:
        flat = a.reshape(-1)
        flat = jnp.pad(flat, (0, (-flat.shape[0]) % 1024))
        parts.append(flat.reshape(-1, 128))
    return jnp.concatenate(parts, axis=0)


def _unpack_rows(packed, shapes):
    out, row = [], 0
    for s in shapes:
        n = int(np.prod(s))
        rows = -(-n // 1024) * 8
        out.append(packed[row:row + rows].reshape(-1)[:n].reshape(s))
        row += rows
    return out


def kernel(x, c, positions, w_ada, b_ada, norm1_g, w_in, ssm_a_re, ssm_a_im, ssm_log_dt, ssm_b_re, ssm_b_im, ssm_c_re, ssm_c_im, ssm_d, w_glu, b_glu, w_a_out, q_norm_g, w_uq, kv_norm_g, w_uk, w_uv, w_b_out, w_out, norm2_g, w_gate, w_up, w_down, final_g, loss_target, m_w_ada, m_b_ada, m_norm1_g, m_w_in, m_ssm_a_re, m_ssm_a_im, m_ssm_log_dt, m_ssm_b_re, m_ssm_b_im, m_ssm_c_re, m_ssm_c_im, m_ssm_d, m_w_glu, m_b_glu, m_w_a_out, m_q_norm_g, m_w_uq, m_kv_norm_g, m_w_uk, m_w_uv, m_w_b_out, m_w_out, m_norm2_g, m_w_gate, m_w_up, m_w_down, m_final_g, v_w_ada, v_b_ada, v_norm1_g, v_w_in, v_ssm_a_re, v_ssm_a_im, v_ssm_log_dt, v_ssm_b_re, v_ssm_b_im, v_ssm_c_re, v_ssm_c_im, v_ssm_d, v_w_glu, v_b_glu, v_w_a_out, v_q_norm_g, v_w_uq, v_kv_norm_g, v_w_uk, v_w_uv, v_w_b_out, v_w_out, v_norm2_g, v_w_gate, v_w_up, v_w_down, v_final_g):
    weights = dict(w_ada=w_ada, b_ada=b_ada, norm1_g=norm1_g, w_in=w_in, ssm_a_re=ssm_a_re, ssm_a_im=ssm_a_im, ssm_log_dt=ssm_log_dt, ssm_b_re=ssm_b_re, ssm_b_im=ssm_b_im, ssm_c_re=ssm_c_re, ssm_c_im=ssm_c_im, ssm_d=ssm_d, w_glu=w_glu, b_glu=b_glu, w_a_out=w_a_out, q_norm_g=q_norm_g, w_uq=w_uq, kv_norm_g=kv_norm_g, w_uk=w_uk, w_uv=w_uv, w_b_out=w_b_out, w_out=w_out, norm2_g=norm2_g, w_gate=w_gate, w_up=w_up, w_down=w_down, final_g=final_g)
    mom_m = dict(w_ada=m_w_ada, b_ada=m_b_ada, norm1_g=m_norm1_g, w_in=m_w_in, ssm_a_re=m_ssm_a_re, ssm_a_im=m_ssm_a_im, ssm_log_dt=m_ssm_log_dt, ssm_b_re=m_ssm_b_re, ssm_b_im=m_ssm_b_im, ssm_c_re=m_ssm_c_re, ssm_c_im=m_ssm_c_im, ssm_d=m_ssm_d, w_glu=m_w_glu, b_glu=m_b_glu, w_a_out=m_w_a_out, q_norm_g=m_q_norm_g, w_uq=m_w_uq, kv_norm_g=m_kv_norm_g, w_uk=m_w_uk, w_uv=m_w_uv, w_b_out=m_w_b_out, w_out=m_w_out, norm2_g=m_norm2_g, w_gate=m_w_gate, w_up=m_w_up, w_down=m_w_down, final_g=m_final_g)
    mom_v = dict(w_ada=v_w_ada, b_ada=v_b_ada, norm1_g=v_norm1_g, w_in=v_w_in, ssm_a_re=v_ssm_a_re, ssm_a_im=v_ssm_a_im, ssm_log_dt=v_ssm_log_dt, ssm_b_re=v_ssm_b_re, ssm_b_im=v_ssm_b_im, ssm_c_re=v_ssm_c_re, ssm_c_im=v_ssm_c_im, ssm_d=v_ssm_d, w_glu=v_w_glu, b_glu=v_b_glu, w_a_out=v_w_a_out, q_norm_g=v_q_norm_g, w_uq=v_w_uq, kv_norm_g=v_kv_norm_g, w_uk=v_w_uk, w_uv=v_w_uv, w_b_out=v_w_b_out, w_out=v_w_out, norm2_g=v_norm2_g, w_gate=v_w_gate, w_up=v_w_up, w_down=v_w_down, final_g=v_final_g)
    names = list(weights)

    depth = w_in.shape[0]
    seq, d = x.shape[1], x.shape[2]
    sw = ssm_d.shape[1]
    groups, n_state, n_chan = ssm_b_re.shape[1:]
    gp = groups * n_state
    ql, kl = q_norm_g.shape[1], kv_norm_g.shape[1]
    nope = w_uk.shape[2] * 4 // N_HEADS
    vdim = w_uv.shape[2] * 4 // N_HEADS
    qk_dim = nope + QK_ROPE
    scale = qk_dim ** -0.5
    tm = _tile(seq, 256, 16)
    tm_ffn = _tile(seq, 128, 16)
    me = 4 * lax.axis_index("x") + 2 * lax.axis_index("y") + lax.axis_index("c")
    chip = 2 * lax.axis_index("x") + lax.axis_index("y")

    xs = x.reshape(seq, d)
    target = loss_target.reshape(seq, d)
    pos_f = positions.astype(F32)
    pos_col = pos_f.reshape(seq, 1)
    pos_row = pos_f.reshape(1, seq)

    (c_all,) = _exchange("gather_c", "gather8", [c])
    c_all = c_all.reshape(8, d)
    ada_cols = w_ada.shape[2]
    mod_part = _mod_fwd(c_all, w_ada.reshape(depth * d, ada_cols), depth)
    (mod_all,) = _exchange("gather_mod", "gather8", [mod_part])
    mod_all = mod_all.reshape(4, 2, depth, 8, ada_cols)[:, 0]
    mod_me = lax.dynamic_index_in_dim(mod_all, me, axis=2, keepdims=False)
    mod = mod_me.transpose(1, 0, 2).reshape(depth, 4 * ada_cols) + b_ada
    mod = mod.reshape(depth, 6, 1, d)

    big = ["w_in", "w_glu", "w_a_out", "w_uq", "w_uk", "w_uv", "w_b_out", "w_out", "w_gate", "w_up", "w_down"]
    row_sharded = {"w_glu", "w_out", "w_down"}
    gathered = _gather_chips_two_level("gather_weights", [weights[n].astype(_MM) for n in big])
    full = {n: (_rows_from_chips(g) if n in row_sharded else _cols_from_chips(g)) for n, g in zip(big, gathered)}
    o1, o2, o3, o4, o5 = sw, sw + ql, sw + ql + kl, sw + ql + kl + QK_ROPE, sw + ql + kl + QK_ROPE + d
    wi = full["w_in"]
    w_u, w_cq, w_ckv, w_ga, w_gb = wi[:, :, :o1], wi[:, :, o1:o2], wi[:, :, o2:o3], wi[:, :, o4:o5], wi[:, :, o5:]
    w_kr = jnp.pad(wi[:, :, o3:o4], ((0, 0), (0, 0), (nope, HEAD_PAD - nope - QK_ROPE)))
    wuq_p = _pad_heads(full["w_uq"], qk_dim)
    wuk_p = _pad_heads(full["w_uk"], nope)
    wuv_p = _pad_heads(full["w_uv"], vdim)
    wb_p = _pad_heads(full["w_b_out"].transpose(0, 2, 1), vdim).transpose(0, 2, 1)

    inv_freq = ROPE_BASE ** (-jnp.arange(0, QK_ROPE, 2, dtype=F32) / QK_ROPE)
    inv_lane = jnp.pad(jnp.concatenate([inv_freq, inv_freq]), (nope, HEAD_PAD - nope - QK_ROPE)).reshape(1, HEAD_PAD)
    rc, rs1, rs2 = _rope_tables(pos_col, inv_lane, nope)
    a_re_col = ssm_a_re.reshape(depth * gp, 1)
    a_im_col = ssm_a_im.reshape(depth * gp, 1)
    ldt_col = jnp.broadcast_to(ssm_log_dt[:, :, None], (depth, groups, n_state)).reshape(depth * gp, 1)
    b_re2, b_im2 = ssm_b_re.reshape(depth * gp, n_chan), ssm_b_im.reshape(depth * gp, n_chan)
    abr, abi, bbr, bbi = _ssm_disc_fwd(a_re_col, a_im_col, ldt_col, b_re2, b_im2)
    abr_rows, abi_rows = abr.reshape(depth, 1, gp), abi.reshape(depth, 1, gp)
    bbr, bbi = bbr.reshape(depth, groups, n_state, n_chan), bbi.reshape(depth, groups, n_state, n_chan)

    saved = []
    xl = xs
    for l in range(depth):
        sh1, sc1, g1, sh2, sc2, g2 = (mod[l, j] for j in range(6))
        n1g, n2g = norm1_g[l].reshape(1, d), norm2_g[l].reshape(1, d)
        w_parts = [w_u[l], w_cq[l], w_ckv[l], w_kr[l], w_ga[l], w_gb[l]]
        hb, u, cq, ckv, kr, ga, gb = _in_fwd(xl, n1g, sc1, sh1, w_parts, tm)
        bre_blk = _block_diag(bbr[l].transpose(0, 2, 1)).astype(_MM)
        bim_blk = _block_diag(bbi[l].transpose(0, 2, 1)).astype(_MM)
        cre_blk = _block_diag(ssm_c_re[l].transpose(0, 2, 1)).astype(_MM)
        cimneg_blk = _block_diag(-ssm_c_im[l].transpose(0, 2, 1)).astype(_MM)
        d_row = ssm_d[l].reshape(1, sw)
        ssm_w = (bre_blk, bim_blk, abr_rows[l], abi_rows[l], cre_blk, cimneg_blk, d_row)
        ypre, hre, him = _ssm_fwd(u, *ssm_w)
        gq, gkv = q_norm_g[l].reshape(1, ql), kv_norm_g[l].reshape(1, kl)
        q, k, v, cqn, ckvn = _mla_prep_fwd(cq, ckv, kr, rc, rs1, rs2, gq, gkv, wuq_p[l], wuk_p[l], wuv_p[l], vdim, tm)
        o, lse = _attn_fwd(q, k, v, pos_col, pos_row, scale, vdim)
        bglu = b_glu[l].reshape(1, sw)
        yg, ya, yb, merged, mo, xmid = _mix_fwd(ypre, o, ga, gb, xl, g1, bglu, full["w_glu"][l], full["w_a_out"][l],
                                                 wb_p[l], full["w_out"][l], tm)
        hb2, fa, fb, dn, xout = _ffn_fwd(xmid, n2g, sc2, sh2, g2, full["w_gate"][l], full["w_up"][l], full["w_down"][l], tm_ffn)
        saved.append(dict(x=xl, hb=hb, u=u, cq=cq, ckv=ckv, ga=ga, gb=gb, ssm_w=ssm_w, ypre=ypre, hre=hre, him=him,
                          q=q, k=k, v=v, cqn=cqn, ckvn=ckvn, o=o, lse=lse, yg=yg, ya=ya, yb=yb, merged=merged, mo=mo,
                          xmid=xmid, hb2=hb2, fa=fa, fb=fb, dn=dn, w_parts=w_parts))
        xl = xout

    dx, loss_acc, g_final = _head(xl, final_g.reshape(1, d), target, tm)
    loss = lax.psum(loss_acc[0, 0], ("x", "y", "c"))

    per_layer = ["w_gate", "w_up", "w_down", "norm2_g", "w_out", "w_a_out", "w_b_out", "w_glu", "b_glu", "w_uq", "w_uk",
                 "w_uv", "q_norm_g", "kv_norm_g", "ssm_d", "ssm_c_re", "ssm_c_im", "w_in", "norm1_g"]
    grads = {n: [None] * depth for n in per_layer}
    dmod = [None] * depth
    for l in reversed(range(depth)):
        s = saved[l]
        sh1, sc1, g1, sh2, sc2, g2 = (mod[l, j] for j in range(6))
        n1g, n2g = norm1_g[l].reshape(1, d), norm2_g[l].reshape(1, d)
        dxm, da, db, fb16, dd, dg2, dsh2, dsc2, dn2 = _ffn_bwd(
            dx, s["xmid"], s["fa"], s["fb"], s["dn"], n2g, sc2, g2, full["w_gate"][l], full["w_up"][l], full["w_down"][l], tm_ffn)
        grads["w_gate"][l] = _mm_tn("dw_gate", s["hb2"], da)
        grads["w_up"][l] = _mm_tn("dw_up", s["hb2"], db)
        grads["w_down"][l] = _mm_tn("dw_down", fb16, dd)
        grads["norm2_g"][l] = dn2.reshape(d)

        bglu = b_glu[l].reshape(1, sw)
        dmo, dya, dyb, dt, ys, dga, dgb, dypre, do, delta, dg1, dbglu = _mix_bwd(
            dxm, s["mo"], s["ya"], s["yb"], s["ga"], s["gb"], s["ypre"], s["o"], g1, bglu, full["w_glu"][l],
            full["w_a_out"][l], wb_p[l], full["w_out"][l], tm)
        grads["w_out"][l] = _mm_tn("dw_out", s["merged"], dmo)
        grads["w_a_out"][l] = _mm_tn("dw_a_out", s["yg"], dya)
        dwb_p = _mm_tn("dw_b_out", s["o"], dyb)
        grads["w_b_out"][l] = _unpad_heads(dwb_p.T, vdim).T
        grads["w_glu"][l] = _mm_tn("dw_glu", ys, dt)
        grads["b_glu"][l] = dbglu.reshape(sw)

        dq, dk, dv = _attn_bwd(s["q"], s["k"], s["v"], do, delta, s["lse"], pos_col, pos_row, scale)
        gq, gkv = q_norm_g[l].reshape(1, ql), kv_norm_g[l].reshape(1, kl)
        dqp, dcq, dckv, dkr, dgq, dgkv = _mla_prep_bwd(dq, dk, dv, s["cq"], s["ckv"], rc, rs1, rs2, gq, gkv,
                                                       wuq_p[l], wuk_p[l], wuv_p[l], nope, tm)
        grads["w_uq"][l] = _unpad_heads(_mm_tn("dw_uq", s["cqn"], dqp), qk_dim)
        grads["w_uk"][l] = _unpad_heads(_mm_tn("dw_uk", s["ckvn"], dk), nope)
        grads["w_uv"][l] = _unpad_heads(_mm_tn("dw_uv", s["ckvn"], dv), vdim)
        grads["q_norm_g"][l] = dgq.reshape(ql)
        grads["kv_norm_g"][l] = dgkv.reshape(kl)

        du, gre, gim, dar, dai, ddskip = _ssm_bwd(dypre, s["u"], s["hre"], s["him"], *s["ssm_w"])
        grads["ssm_d"][l] = ddskip.reshape(sw)
        d_bre = _block_diag_extract(_mm_tn("d_bre", s["u"], gre), groups).transpose(0, 2, 1)
        d_bim = _block_diag_extract(_mm_tn("d_bim", s["u"], gim), groups).transpose(0, 2, 1)
        grads["ssm_c_re"][l] = _block_diag_extract(_mm_tn("d_cre", s["hre"], dypre), groups).transpose(0, 2, 1)
        grads["ssm_c_im"][l] = -_block_diag_extract(_mm_tn("d_cim", s["him"], dypre), groups).transpose(0, 2, 1)
        s["disc_grads"] = (dar.reshape(gp, 1), dai.reshape(gp, 1), d_bre.reshape(gp, n_chan), d_bim.reshape(gp, n_chan))

        dz_parts = [du, dcq, dckv, dkr, dga, dgb]
        dx, dsh1, dsc1, dn1 = _in_bwd(dxm, s["x"], dz_parts, n1g, sc1, s["w_parts"], tm)
        dw_parts = [_mm_tn("dw_in_%d" % j, s["hb"], dz) for j, dz in enumerate(dz_parts)]
        dw_parts[3] = dw_parts[3][:, nope:nope + QK_ROPE]
        grads["w_in"][l] = jnp.concatenate(dw_parts, axis=1)
        grads["norm1_g"][l] = dn1.reshape(d)
        dmod[l] = jnp.concatenate([dsh1, dsc1, dg1, dsh2, dsc2, dg2], axis=1).reshape(6 * d)
    grad_x = dx.reshape(x.shape)

    disc = [jnp.concatenate([saved[l]["disc_grads"][j] for l in range(depth)], axis=0) for j in range(4)]
    da_re, da_im, dldt, db_re, db_im = _ssm_disc_bwd(a_re_col, a_im_col, ldt_col, b_re2, b_im2, *disc)
    stacked = {n: jnp.stack(v) for n, v in grads.items()}
    stacked["ssm_a_re"] = da_re.reshape(ssm_a_re.shape)
    stacked["ssm_a_im"] = da_im.reshape(ssm_a_im.shape)
    stacked["ssm_log_dt"] = _lane_sum(dldt.reshape(depth * groups, n_state)).reshape(ssm_log_dt.shape)
    stacked["ssm_b_re"] = db_re.reshape(ssm_b_re.shape)
    stacked["ssm_b_im"] = db_im.reshape(ssm_b_im.shape)
    stacked["final_g"] = g_final.reshape(d)
    stacked["b_ada"] = jnp.stack(dmod)

    small = [n for n in names if n not in big and n != "w_ada"]
    small_shapes = [weights[n].shape for n in small]
    (small_all,) = _exchange("gather_small", "gather8", [_pack_rows([stacked[n] for n in small])])
    sg, sd, sm, sv = _adamw("adamw_small", [small_all], _pack_rows([weights[n] for n in small]),
                            _pack_rows([mom_m[n] for n in small]), _pack_rows([mom_v[n] for n in small]))
    out_g = dict(zip(small, _unpack_rows(sg, small_shapes)))
    out_d = dict(zip(small, _unpack_rows(sd, small_shapes)))
    out_m = dict(zip(small, _unpack_rows(sm, small_shapes)))
    out_v = dict(zip(small, _unpack_rows(sv, small_shapes)))

    n_dmod = depth * 6 * d
    dmod_all = small_all[:, :n_dmod // 128].reshape(8, depth, 6 * d)
    dmod_cols = lax.dynamic_slice_in_dim(dmod_all, chip * ada_cols, ada_cols, axis=2)
    g_wada = _wada_bwd(c_all, dmod_cols.transpose(1, 0, 2).reshape(depth * 8, ada_cols), depth)
    res = _adamw("adamw_w_ada", [g_wada], w_ada.reshape(depth * d, ada_cols), m_w_ada.reshape(depth * d, ada_cols),
                 v_w_ada.reshape(depth * d, ada_cols))
    out_g["w_ada"], out_d["w_ada"], out_m["w_ada"], out_v["w_ada"] = (r.reshape(w_ada.shape) for r in res)

    core = lax.axis_index("c")
    half = depth // 2
    to_chips = [(_rows_to_chips if n in row_sharded else _cols_to_chips)(stacked[n]).astype(_WIRE) for n in big]
    own_half = [lax.dynamic_slice_in_dim(p, core * half, half, axis=1) for p in to_chips]
    other_half = [lax.dynamic_slice_in_dim(p, (1 - core) * half, half, axis=1) for p in to_chips]
    from_sibling = _exchange("swap_halves", "swap", other_half)
    chip_sum = [_add_pair("pair_" + n, a.reshape(-1, a.shape[-1]), b.reshape(-1, b.shape[-1])).reshape(a.shape)
                for n, a, b in zip(big, own_half, from_sibling)]
    landed = _exchange("scatter_grads", "scatter4", chip_sum)
    finished = [_sum_slots("sum_" + n, r.reshape(4, -1, r.shape[-1])) for n, r in zip(big, landed)]
    sibling = _exchange("swap_partials", "swap", finished)
    for n, mine, theirs in zip(big, finished, sibling):
        shp = weights[n].shape
        as2d = lambda a: a.reshape(-1, shp[-1])
        rows = mine.shape[0]
        g = lax.dynamic_update_slice_in_dim(jnp.zeros((2 * rows, shp[-1]), F32), mine, core * rows, axis=0)
        g = lax.dynamic_update_slice_in_dim(g, theirs, (1 - core) * rows, axis=0)
        res = _adamw("adamw_" + n, [g], as2d(weights[n]), as2d(mom_m[n]), as2d(mom_v[n]))
        out_g[n], out_d[n], out_m[n], out_v[n] = (r.reshape(shp) for r in res)

    return (loss, grad_x, *[out_g[n] for n in names], *[out_d[n] for n in names], *[out_m[n] for n in names],
            *[out_v[n] for n in names])
```

```python
import functools
import math

import numpy as np
import jax
import jax.numpy as jnp
from jax import lax
from jax.experimental import pallas as pl
from jax.experimental.pallas import tpu as pltpu

F32 = jnp.float32
_MM = jnp.bfloat16
_ACT = jnp.bfloat16
_WIRE = jnp.bfloat16

N_HEADS = 8
QK_ROPE = 32
HEAD_PAD = 128
QW = N_HEADS * HEAD_PAD
ROPE_BASE = 10000.0
EPS = 1e-6
DT_MIN = 1e-3
ADAM_LR = 0.001
ADAM_B1 = 0.9
ADAM_B2 = 0.999
ADAM_EPS = 1e-08
ADAM_WD = 0.01
ADAM_STEP = 10
NEG_INF = -1e30
LOG2_E = math.log2(math.e)
ATTN_HEADS_PER_STEP = 4
ATTN_BWD_HEADS_PER_STEP = 2

V7X_VMEM_BYTES = 64 * 1024 * 1024
VMEM_RESERVE_BYTES = 6 * 1024 * 1024
MESH = pl.DeviceIdType.MESH
ANY = pl.BlockSpec(memory_space=pl.ANY)


def _vmem_limit(block_bytes, temp_bytes):
    want = 2 * block_bytes + temp_bytes
    return int(min(V7X_VMEM_BYTES - VMEM_RESERVE_BYTES, max(want, 32 * 1024 * 1024)))


def _nbytes(shape, dtype):
    return int(np.prod(shape)) * jnp.dtype(dtype).itemsize


def _tile(n, target, mult=8):
    t = min(n, target)
    while t >= mult:
        if n % t == 0 and t % mult == 0:
            return t
        t -= 1
    return n


def _dot(a, b):
    return jnp.dot(a.astype(_MM), b.astype(_MM), preferred_element_type=F32)


def _dot_nt(a, b):
    return lax.dot_general(a.astype(_MM), b.astype(_MM), (((1,), (1,)), ((), ())), preferred_element_type=F32)


def _dot_tn(a, b):
    return lax.dot_general(a.astype(_MM), b.astype(_MM), (((0,), (0,)), ((), ())), preferred_element_type=F32)


def _sigmoid(x):
    return jax.nn.sigmoid(x)


_GELU_K = math.sqrt(2.0 / math.pi)


def _gelu(x):
    return x * (0.5 * (1.0 + jnp.tanh(_GELU_K * (x + 0.044715 * (x * x * x)))))


def _gelu_grad(x):
    th = jnp.tanh(_GELU_K * (x + 0.044715 * (x * x * x)))
    return 0.5 * (1.0 + th) + 0.5 * x * (1.0 - th * th) * (_GELU_K * (1.0 + 3.0 * 0.044715 * (x * x)))


def _rows_sum(v):
    return jnp.sum(v, axis=0, keepdims=True)


def _rms_stats(x):
    rstd = lax.rsqrt(jnp.mean(x * x, axis=-1, keepdims=True) + EPS)
    return x * rstd, rstd


def _rms_bwd(dxh, xh, rstd):
    return rstd * (dxh - xh * jnp.mean(dxh * xh, axis=-1, keepdims=True))


def _rowcall(name, body, n_rows, tm, row_ins, full_ins, row_outs, acc_outs=(), temp_cols=0):
    grid = (n_rows // tm,)
    in_specs = [pl.BlockSpec((tm, a.shape[1]), lambda i: (i, 0)) for a in row_ins]
    in_specs += [pl.BlockSpec(a.shape, lambda i: (0, 0), pipeline_mode=pl.Buffered(1)) for a in full_ins]
    out_shape = [jax.ShapeDtypeStruct((n_rows, c), dt) for c, dt in row_outs]
    out_shape += [jax.ShapeDtypeStruct(s, dt) for s, dt in acc_outs]
    out_specs = [pl.BlockSpec((tm, c), lambda i: (i, 0)) for c, _ in row_outs]
    out_specs += [pl.BlockSpec(s, lambda i: (0, 0)) for s, _ in acc_outs]
    blocks = sum(_nbytes((tm, a.shape[1]), a.dtype) for a in row_ins)
    blocks += sum(_nbytes((tm, c), dt) for c, dt in row_outs) + sum(_nbytes(s, dt) for s, dt in acc_outs)
    resident = sum(_nbytes(a.shape, a.dtype) for a in full_ins)
    limit = _vmem_limit(blocks, resident + _nbytes((tm, temp_cols), F32))
    res = pl.pallas_call(
        body, name=name, grid=grid, in_specs=in_specs, out_specs=out_specs, out_shape=out_shape,
        compiler_params=pltpu.CompilerParams(
            dimension_semantics=("arbitrary" if acc_outs else "parallel",), vmem_limit_bytes=limit),
    )(*row_ins, *full_ins)
    return res


def _first_step():
    return pl.program_id(0) == 0


def _acc(ref, val):
    @pl.when(_first_step())
    def _():
        ref[...] = val

    @pl.when(jnp.logical_not(_first_step()))
    def _():
        ref[...] += val


def _mm_tn(name, a, g):
    n_rows, k = a.shape
    n = g.shape[1]
    tk = k if k <= 1024 else _tile(k, 1408, 128)
    tn = n if n <= 1024 else _tile(n, 1408, 128)
    tl = _tile(n_rows, 2048, 16)

    def body(a_ref, g_ref, o_ref):
        @pl.when(pl.program_id(2) == 0)
        def _():
            o_ref[...] = jnp.zeros_like(o_ref)
        o_ref[...] += _dot_tn(a_ref[...], g_ref[...])

    blocks = _nbytes((tl, tk), a.dtype) + _nbytes((tl, tn), g.dtype) + _nbytes((tk, tn), F32)
    return pl.pallas_call(
        body, name=name, grid=(k // tk, n // tn, n_rows // tl),
        in_specs=[pl.BlockSpec((tl, tk), lambda i, j, l: (l, i)), pl.BlockSpec((tl, tn), lambda i, j, l: (l, j))],
        out_specs=pl.BlockSpec((tk, tn), lambda i, j, l: (i, j)),
        out_shape=jax.ShapeDtypeStruct((k, n), F32),
        compiler_params=pltpu.CompilerParams(
            dimension_semantics=("parallel", "parallel", "arbitrary"),
            vmem_limit_bytes=_vmem_limit(blocks, 2 * _nbytes((tl, max(tk, tn)), F32) + _nbytes((tk, tn), F32))),
    )(a, g)


def _place():
    return lax.axis_index("x"), lax.axis_index("y"), lax.axis_index("c")


def _flip(v, bit):
    return 1 - v if bit else v


def _exchange(name, mode, arrays):
    n = len(arrays)
    if mode == "gather8":
        rel = [((k >> 2) & 1, (k >> 1) & 1, k & 1) for k in range(1, 8)]
        out_shape = [jax.ShapeDtypeStruct((8,) + a.shape, a.dtype) for a in arrays]
    elif mode == "scatter4":
        rel = [((k >> 1) & 1, k & 1, 0) for k in range(1, 4)]
        out_shape = [jax.ShapeDtypeStruct(a.shape, a.dtype) for a in arrays]
    else:
        rel = [(0, 0, 1)]
        out_shape = [jax.ShapeDtypeStruct(a.shape, a.dtype) for a in arrays]
    n_rel = len(rel)

    def body(*refs):
        ins, outs = refs[:n], refs[n:2 * n]
        send_sems, recv_sems, local_sems = refs[2 * n:]
        x, y, c = _place()

        def slot(px, py, pc):
            return 4 * px + 2 * py + pc if mode == "gather8" else 2 * px + py

        mine = slot(x, y, c)
        local = []
        if mode != "swap":
            for a in range(n):
                src = ins[a].at[mine] if mode == "scatter4" else ins[a]
                local.append(pltpu.make_async_copy(src, outs[a].at[mine], local_sems.at[a]))
            for cp in local:
                cp.start()

        def remote(r, a):
            px, py, pc = _flip(x, rel[r][0]), _flip(y, rel[r][1]), _flip(c, rel[r][2])
            theirs = slot(px, py, pc)
            if mode == "swap":
                src, dst_there, dst_here = ins[a], outs[a], outs[a]
            elif mode == "scatter4":
                src, dst_there, dst_here = ins[a].at[theirs], outs[a].at[mine], outs[a].at[theirs]
            else:
                src, dst_there, dst_here = ins[a], outs[a].at[mine], outs[a].at[theirs]
            k = r * n + a
            push = pltpu.make_async_remote_copy(src_ref=src, dst_ref=dst_there, send_sem=send_sems.at[k],
                                                recv_sem=recv_sems.at[k], device_id=(px, py, pc), device_id_type=MESH)
            land = pltpu.make_async_remote_copy(src_ref=src, dst_ref=dst_here, send_sem=send_sems.at[k],
                                                recv_sem=recv_sems.at[k], device_id=(px, py, pc), device_id_type=MESH)
            return push, land

        copies = [remote(r, a) for r in range(n_rel) for a in range(n)]
        for push, _ in copies:
            push.start()
        for _, land in copies:
            land.wait_recv()
        for push, _ in copies:
            push.wait_send()
        for cp in local:
            cp.wait()

    return pl.pallas_call(
        body, name=name, in_specs=[ANY] * n, out_specs=[ANY] * n, out_shape=out_shape,
        scratch_shapes=[pltpu.SemaphoreType.DMA((n_rel * n,)), pltpu.SemaphoreType.DMA((n_rel * n,)),
                        pltpu.SemaphoreType.DMA((max(n, 1),))],
    )(*arrays)


def _gather_chips_two_level(name, arrays):
    n = len(arrays)
    rel = [((k >> 1) & 1, k & 1) for k in range(1, 4)]
    halves = [a.shape[0] // 2 for a in arrays]

    def body(*refs):
        ins, outs = refs[:n], refs[n:2 * n]
        ici_send, ici_recv, d2d_send, d2d_recv, local_sems = refs[2 * n:]
        x, y, c = _place()
        mine = 2 * x + y
        local = [pltpu.make_async_copy(ins[a], outs[a].at[mine], local_sems.at[a]) for a in range(n)]
        for cp in local:
            cp.start()

        def ici(r, a):
            px, py = _flip(x, rel[r][0]), _flip(y, rel[r][1])
            theirs = 2 * px + py
            rows = pl.ds(c * halves[a], halves[a])
            k = r * n + a
            push = pltpu.make_async_remote_copy(src_ref=ins[a].at[rows], dst_ref=outs[a].at[mine, rows],
                                                send_sem=ici_send.at[k], recv_sem=ici_recv.at[k],
                                                device_id=(px, py, c), device_id_type=MESH)
            land = pltpu.make_async_remote_copy(src_ref=ins[a].at[rows], dst_ref=outs[a].at[theirs, rows],
                                                send_sem=ici_send.at[k], recv_sem=ici_recv.at[k],
                                                device_id=(px, py, c), device_id_type=MESH)
            there = pl.ds((1 - c) * halves[a], halves[a])
            forward = pltpu.make_async_remote_copy(src_ref=outs[a].at[theirs, rows], dst_ref=outs[a].at[theirs, rows],
                                                   send_sem=d2d_send.at[k], recv_sem=d2d_recv.at[k],
                                                   device_id=(x, y, 1 - c), device_id_type=MESH)
            back = pltpu.make_async_remote_copy(src_ref=outs[a].at[theirs, rows], dst_ref=outs[a].at[theirs, there],
                                                send_sem=d2d_send.at[k], recv_sem=d2d_recv.at[k],
                                                device_id=(x, y, 1 - c), device_id_type=MESH)
            return push, land, forward, back

        copies = [ici(r, a) for r in range(len(rel)) for a in range(n)]
        for push, _, _, _ in copies:
            push.start()
        for _, land, forward, _ in copies:
            land.wait_recv()
            forward.start()
        for _, _, _, back in copies:
            back.wait_recv()
        for push, _, forward, _ in copies:
            push.wait_send()
            forward.wait_send()
        for cp in local:
            cp.wait()

    n_sem = len(rel) * n
    return pl.pallas_call(
        body, name=name, in_specs=[ANY] * n, out_specs=[ANY] * n,
        out_shape=[jax.ShapeDtypeStruct((4,) + a.shape, a.dtype) for a in arrays],
        scratch_shapes=[pltpu.SemaphoreType.DMA((n_sem,))] * 4 + [pltpu.SemaphoreType.DMA((n,))],
    )(*arrays)


def _sum_slots(name, stacked):
    p, rows, cols = stacked.shape
    tr = _tile(rows, 256)

    def body(s_ref, o_ref):
        acc = s_ref[0].astype(F32)
        for j in range(1, p):
            acc = acc + s_ref[j].astype(F32)
        o_ref[...] = acc

    return pl.pallas_call(
        body, name=name, grid=(rows // tr,),
        in_specs=[pl.BlockSpec((p, tr, cols), lambda i: (0, i, 0))],
        out_specs=pl.BlockSpec((tr, cols), lambda i: (i, 0)),
        out_shape=jax.ShapeDtypeStruct((rows, cols), F32),
        compiler_params=pltpu.CompilerParams(dimension_semantics=("parallel",)),
    )(stacked)


def _add_pair(name, a, b):
    rows, cols = a.shape
    tr = _tile(rows, 512, 16)

    def body(a_ref, b_ref, o_ref):
        o_ref[...] = (a_ref[...].astype(F32) + b_ref[...].astype(F32)).astype(o_ref.dtype)

    spec = pl.BlockSpec((tr, cols), lambda i: (i, 0))
    return pl.pallas_call(
        body, name=name, grid=(rows // tr,), in_specs=[spec, spec], out_specs=spec,
        out_shape=jax.ShapeDtypeStruct((rows, cols), a.dtype),
        compiler_params=pltpu.CompilerParams(dimension_semantics=("parallel",)),
    )(a, b)


def _adamw(name, parts, w, m, v):
    rows, cols = w.shape
    tr = _tile(rows, 256)
    n_parts = len(parts)

    def body(*refs):
        part_refs = refs[:n_parts]
        w_ref, m_ref, v_ref, g_out, d_out, m_out, v_out = refs[n_parts:]
        g = None
        for pr in part_refs:
            if len(pr.shape) == 3:
                for j in range(pr.shape[0]):
                    g = pr[j] if g is None else g + pr[j]
            else:
                g = pr[...] if g is None else g + pr[...]
        m_new = ADAM_B1 * m_ref[...] + (1.0 - ADAM_B1) * g
        v_new = ADAM_B2 * v_ref[...] + (1.0 - ADAM_B2) * jnp.square(g)
        m_hat = m_new / (1.0 - ADAM_B1 ** ADAM_STEP)
        v_hat = v_new / (1.0 - ADAM_B2 ** ADAM_STEP)
        g_out[...] = g
        d_out[...] = -ADAM_LR * (m_hat / (jnp.sqrt(v_hat) + ADAM_EPS) + ADAM_WD * w_ref[...])
        m_out[...] = m_new
        v_out[...] = v_new

    spec2 = pl.BlockSpec((tr, cols), lambda i: (i, 0))
    in_specs = [pl.BlockSpec((p.shape[0], tr, cols), lambda i: (0, i, 0)) if p.ndim == 3 else spec2 for p in parts]
    blocks = sum(_nbytes((p.shape[0] if p.ndim == 3 else 1, tr, cols), F32) for p in parts) + 7 * _nbytes((tr, cols), F32)
    return pl.pallas_call(
        body, name=name, grid=(rows // tr,),
        in_specs=in_specs + [spec2] * 3, out_specs=[spec2] * 4,
        out_shape=[jax.ShapeDtypeStruct((rows, cols), F32)] * 4,
        compiler_params=pltpu.CompilerParams(dimension_semantics=("parallel",),
                                             vmem_limit_bytes=_vmem_limit(blocks, 4 * _nbytes((tr, cols), F32))),
    )(*parts, w, m, v)


def _mod_fwd(c_all, w_ada2d, depth):
    nb, d = c_all.shape
    cols = w_ada2d.shape[1]
    tn = _tile(cols, 512, 128)

    def body(c_ref, w_ref, o_ref):
        cv = c_ref[...]
        o_ref[...] = _dot(cv * _sigmoid(cv), w_ref[...])

    return pl.pallas_call(
        body, name="mod_fwd", grid=(depth, cols // tn),
        in_specs=[pl.BlockSpec((nb, d), lambda l, j: (0, 0)), pl.BlockSpec((d, tn), lambda l, j: (l, j))],
        out_specs=pl.BlockSpec((nb, tn), lambda l, j: (l, j)),
        out_shape=jax.ShapeDtypeStruct((depth * nb, cols), F32),
        compiler_params=pltpu.CompilerParams(dimension_semantics=("parallel", "parallel")),
    )(c_all, w_ada2d)


def _wada_bwd(c_all, dmod2d, depth):
    nb, d = c_all.shape
    cols = dmod2d.shape[1]
    tn = _tile(cols, 512, 128)

    def body(c_ref, g_ref, o_ref):
        cv = c_ref[...]
        o_ref[...] = _dot_tn(cv * _sigmoid(cv), g_ref[...])

    return pl.pallas_call(
        body, name="wada_bwd", grid=(depth, cols // tn),
        in_specs=[pl.BlockSpec((nb, d), lambda l, j: (0, 0)), pl.BlockSpec((nb, tn), lambda l, j: (l, j))],
        out_specs=pl.BlockSpec((d, tn), lambda l, j: (l, j)),
        out_shape=jax.ShapeDtypeStruct((depth * d, cols), F32),
        compiler_params=pltpu.CompilerParams(dimension_semantics=("parallel", "parallel")),
    )(c_all, dmod2d)


def _rope_tables(pos_col, inv_freq_lane, nope):
    n_rows = pos_col.shape[0]
    tm = _tile(n_rows, 512)
    half = QK_ROPE // 2

    def body(p_ref, f_ref, c_ref, s1_ref, s2_ref):
        ang = p_ref[...] * f_ref[...]
        lane = lax.broadcasted_iota(jnp.int32, ang.shape, 1)
        first = (lane >= nope) & (lane < nope + half)
        second = (lane >= nope + half) & (lane < nope + 2 * half)
        cos, sin = jnp.cos(ang), jnp.sin(ang)
        c_ref[...] = jnp.where(first | second, cos, 1.0)
        s1_ref[...] = jnp.where(first, -sin, 0.0)
        s2_ref[...] = jnp.where(second, sin, 0.0)

    return _rowcall("rope_tables", body, n_rows, tm, [pos_col], [inv_freq_lane], [(HEAD_PAD, F32)] * 3)


def _rope(q, c, s1, s2):
    w = q.shape[1]
    return q * c + pltpu.roll(q, w - QK_ROPE // 2, axis=1) * s1 + pltpu.roll(q, QK_ROPE // 2, axis=1) * s2


def _rope_adjoint(dr, c, s1, s2):
    w = dr.shape[1]
    return dr * c + pltpu.roll(dr * s1, QK_ROPE // 2, axis=1) + pltpu.roll(dr * s2, w - QK_ROPE // 2, axis=1)


def _ssm_disc(ar, ai, log_dt, br, bi):
    dt = jnp.exp(log_dt)
    mag = jnp.exp(ar * dt)
    abr = mag * jnp.cos(ai * dt)
    abi = mag * jnp.sin(ai * dt)
    den = ar * ar + ai * ai
    nr = abr - 1.0
    ni = abi
    cr = (nr * ar + ni * ai) / den
    ci = (ni * ar - nr * ai) / den
    return abr, abi, cr * br - ci * bi, cr * bi + ci * br


def _ssm_disc_fwd(ar, ai, log_dt, br, bi):
    n_rows, m = br.shape
    tm = _tile(n_rows, 1024)

    def body(ar_ref, ai_ref, dt_ref, br_ref, bi_ref, o1, o2, o3, o4):
        o1[...], o2[...], o3[...], o4[...] = _ssm_disc(ar_ref[...], ai_ref[...], dt_ref[...], br_ref[...], bi_ref[...])

    return _rowcall("ssm_disc_fwd", body, n_rows, tm, [ar, ai, log_dt, br, bi], [],
                    [(1, F32), (1, F32), (m, F32), (m, F32)])


def _ssm_disc_bwd(ar, ai, log_dt, br, bi, g_abr, g_abi, g_bbr, g_bbi):
    n_rows, m = br.shape
    tm = _tile(n_rows, 1024)

    def body(ar_ref, ai_ref, dt_ref, br_ref, bi_ref, g1, g2, g3, g4, o1, o2, o3, o4, o5):
        _, vjp = jax.vjp(_ssm_disc, ar_ref[...], ai_ref[...], dt_ref[...], br_ref[...], bi_ref[...])
        o1[...], o2[...], o3[...], o4[...], o5[...] = vjp((g1[...], g2[...], g3[...], g4[...]))

    return _rowcall("ssm_disc_bwd", body, n_rows, tm, [ar, ai, log_dt, br, bi, g_abr, g_abi, g_bbr, g_bbi], [],
                    [(1, F32), (1, F32), (1, F32), (m, F32), (m, F32)])


def _lane_sum(v2d):
    def body(v_ref, o_ref):
        o_ref[...] = jnp.sum(v_ref[...], axis=1, keepdims=True)
    return pl.pallas_call(body, name="lane_sum", out_shape=jax.ShapeDtypeStruct((v2d.shape[0], 1), F32))(v2d)


def _in_fwd(x, n1g, sc1, sh1, w_parts, tm):
    n_rows = x.shape[0]
    widths = [w.shape[1] for w in w_parts]

    def body(x_ref, g_ref, sc_ref, sh_ref, *rest):
        w_refs, (hb_ref, *z_refs) = rest[:len(w_parts)], rest[len(w_parts):]
        xh, _ = _rms_stats(x_ref[...])
        h = (xh * g_ref[...]) * (1.0 + sc_ref[...]) + sh_ref[...]
        hb = h.astype(_MM)
        hb_ref[...] = hb.astype(_ACT)
        for w_ref, z_ref in zip(w_refs, z_refs):
            z_ref[...] = _dot(hb, w_ref[...])

    return _rowcall("in_fwd", body, n_rows, tm, [x], [n1g, sc1, sh1, *w_parts],
                    [(x.shape[1], _ACT)] + [(w, F32) for w in widths], temp_cols=4 * x.shape[1])


def _ssm_fwd(u, bre_blk, bim_blk, abr_row, abi_row, cre_blk, cimneg_blk, d_row):
    n_rows, sw = u.shape
    gp = abr_row.shape[1]
    t = _tile(n_rows, 256)

    def body(u_ref, bre_ref, bim_ref, ar_ref, ai_ref, cre_ref, cim_ref, d_ref, y_ref, hre_ref, him_ref, cr, ci):
        @pl.when(_first_step())
        def _():
            cr[...] = jnp.zeros_like(cr)
            ci[...] = jnp.zeros_like(ci)

        uv = u_ref[...]
        ub = uv.astype(_MM)
        hre_ref[...] = _dot(ub, bre_ref[...])
        him_ref[...] = _dot(ub, bim_ref[...])
        a_r, a_i = ar_ref[...], ai_ref[...]

        def step(k, carry):
            pr, pi = carry
            row = pl.ds(k, 1)
            hr = a_r * pr - a_i * pi + hre_ref[row, :]
            hi = a_r * pi + a_i * pr + him_ref[row, :]
            hre_ref[row, :] = hr
            him_ref[row, :] = hi
            return hr, hi

        pr, pi = lax.fori_loop(0, t, step, (cr[0:1, :], ci[0:1, :]), unroll=8)
        cr[0:1, :] = pr
        ci[0:1, :] = pi
        y_ref[...] = _dot(hre_ref[...], cre_ref[...]) + _dot(him_ref[...], cim_ref[...]) + d_ref[...] * uv

    row = lambda c: pl.BlockSpec((t, c), lambda i: (i, 0))
    full = lambda a: pl.BlockSpec(a.shape, lambda i: (0, 0), pipeline_mode=pl.Buffered(1))
    blocks = _nbytes((t, sw), F32) * 2 + 2 * _nbytes((t, gp), F32)
    resident = 4 * _nbytes((sw, gp), _MM)
    return pl.pallas_call(
        body, name="ssm_fwd", grid=(n_rows // t,),
        in_specs=[row(sw), full(bre_blk), full(bim_blk), full(abr_row), full(abi_row), full(cre_blk),
                  full(cimneg_blk), full(d_row)],
        out_specs=[row(sw), row(gp), row(gp)],
        out_shape=[jax.ShapeDtypeStruct((n_rows, sw), F32), jax.ShapeDtypeStruct((n_rows, gp), F32),
                   jax.ShapeDtypeStruct((n_rows, gp), F32)],
        scratch_shapes=[pltpu.VMEM((8, gp), F32), pltpu.VMEM((8, gp), F32)],
        compiler_params=pltpu.CompilerParams(dimension_semantics=("arbitrary",),
                                             vmem_limit_bytes=_vmem_limit(blocks, resident + 3 * _nbytes((t, gp), F32))),
    )(u, bre_blk, bim_blk, abr_row, abi_row, cre_blk, cimneg_blk, d_row)


def _mla_prep_fwd(cq, ckv, kr, rc, rs1, rs2, gq, gkv, wuq, wuk, wuv, vdim, tm):
    n_rows = cq.shape[0]

    def body(cq_ref, ckv_ref, kr_ref, c_ref, s1_ref, s2_ref, gq_ref, gkv_ref, wuq_ref, wuk_ref, wuv_ref,
             q_ref, k_ref, v_ref, cqn_ref, ckvn_ref):
        c, s1, s2 = c_ref[...], s1_ref[...], s2_ref[...]
        c8, s18, s28 = (jnp.tile(a, (1, N_HEADS)) for a in (c, s1, s2))
        xh, _ = _rms_stats(cq_ref[...])
        cqn = (xh * gq_ref[...]).astype(_MM)
        cqn_ref[...] = cqn.astype(_ACT)
        q_ref[...] = _rope(_dot(cqn, wuq_ref[...]), c8, s18, s28).astype(_ACT)
        xh, _ = _rms_stats(ckv_ref[...])
        ckvn = (xh * gkv_ref[...]).astype(_MM)
        ckvn_ref[...] = ckvn.astype(_ACT)
        kpe = _rope(kr_ref[...], c, s1, s2)
        k_ref[...] = (_dot(ckvn, wuk_ref[...]) + jnp.tile(kpe, (1, N_HEADS))).astype(_ACT)
        v = _dot(ckvn, wuv_ref[...])
        lane = lax.broadcasted_iota(jnp.int32, v.shape, 1)
        v_ref[...] = jnp.where((lane & (HEAD_PAD - 1)) == vdim, 1.0, v).astype(_ACT)

    return _rowcall("mla_prep_fwd", body, n_rows, tm, [cq, ckv, kr, rc, rs1, rs2], [gq, gkv, wuq, wuk, wuv],
                    [(QW, _ACT), (QW, _ACT), (QW, _ACT), (cq.shape[1], _ACT), (ckv.shape[1], _ACT)], temp_cols=6 * QW)


def _causal_steps(n_blocks, key_major):
    if key_major:
        pairs = [(qi, ki) for ki in range(n_blocks) for qi in range(ki, n_blocks)]
    else:
        pairs = [(qi, ki) for qi in range(n_blocks) for ki in range(qi + 1)]
    return (jnp.asarray(np.array([p[0] for p in pairs], np.int32)), jnp.asarray(np.array([p[1] for p in pairs], np.int32)))


def _attn_fwd(q, k, v, pos_col, pos_row, scale, vdim):
    n_rows = q.shape[0]
    ta = _tile(n_rows, 512, 128)
    nb = n_rows // ta
    hb = ATTN_HEADS_PER_STEP
    wide = hb * HEAD_PAD
    qmap, kmap = _causal_steps(nb, key_major=False)
    c2 = scale * LOG2_E

    def body(qm, km, q_ref, k_ref, v_ref, pq_ref, pk_ref, o_ref, lse_ref, m_sc, acc_sc):
        s_id = pl.program_id(1)
        qi, ki = qm[s_id], km[s_id]

        @pl.when(ki == 0)
        def _():
            m_sc[...] = jnp.full_like(m_sc, NEG_INF)
            acc_sc[...] = jnp.zeros_like(acc_sc)

        def update(on_diagonal):
            if on_diagonal:
                visible = pk_ref[...] <= pq_ref[...]
            for h in range(hb):
                lanes = slice(h * HEAD_PAD, (h + 1) * HEAD_PAD)
                s = _dot_nt(q_ref[:, lanes], k_ref[:, lanes])
                if on_diagonal:
                    s = jnp.where(visible, s, NEG_INF)
                m_prev = m_sc[:, lanes]
                m_new = jnp.maximum(m_prev, jnp.max(s, axis=1, keepdims=True))
                alpha = jnp.exp2((m_prev - m_new) * c2)
                p = jnp.exp2((s - m_new[:, :1]) * c2)
                acc_new = alpha * acc_sc[:, lanes] + _dot(p, v_ref[:, lanes])
                if on_diagonal:
                    l_new = acc_new[:, vdim:vdim + 1]
                    o_ref[:, lanes] = acc_new / l_new
                    lse_ref[:, lanes] = m_new * c2 + jnp.log2(l_new)
                else:
                    acc_sc[:, lanes] = acc_new
                    m_sc[:, lanes] = m_new

        pl.when(ki != qi)(functools.partial(update, False))
        pl.when(ki == qi)(functools.partial(update, True))

    qspec = pl.BlockSpec((ta, wide), lambda h, s, qm, km: (qm[s], h))
    kspec = pl.BlockSpec((ta, wide), lambda h, s, qm, km: (km[s], h))
    grid_spec = pltpu.PrefetchScalarGridSpec(
        num_scalar_prefetch=2, grid=(N_HEADS // hb, int(qmap.shape[0])),
        in_specs=[qspec, kspec, kspec,
                  pl.BlockSpec((ta, 1), lambda h, s, qm, km: (qm[s], 0)),
                  pl.BlockSpec((1, ta), lambda h, s, qm, km: (0, km[s]))],
        out_specs=[qspec, qspec],
        scratch_shapes=[pltpu.VMEM((ta, wide), F32)] * 2)
    return pl.pallas_call(
        body, name="attn_fwd", grid_spec=grid_spec,
        out_shape=[jax.ShapeDtypeStruct((n_rows, QW), F32), jax.ShapeDtypeStruct((n_rows, QW), F32)],
        compiler_params=pltpu.CompilerParams(
            dimension_semantics=("parallel", "arbitrary"),
            vmem_limit_bytes=_vmem_limit(8 * _nbytes((ta, wide), F32), 6 * hb * _nbytes((ta, ta), F32))),
    )(qmap, kmap, q, k, v, pos_col, pos_row)


def _mix_fwd(ypre, o, ga, gb, x, g1, bglu, wglu, wa, wb, wout, tm):
    n_rows, d = x.shape
    sw = ypre.shape[1]

    def body(y_ref, o_ref, ga_ref, gb_ref, x_ref, g1_ref, bglu_ref, wglu_ref, wa_ref, wb_ref, wout_ref,
             yg_ref, ya_ref, yb_ref, mg_ref, mo_ref, xo_ref):
        ys = _gelu(y_ref[...])
        yg = ys * _sigmoid(_dot(ys, wglu_ref[...]) + bglu_ref[...])
        yg_ref[...] = yg.astype(_ACT)
        ya = _dot(yg, wa_ref[...])
        yb = _dot(o_ref[...], wb_ref[...])
        ya_ref[...] = ya
        yb_ref[...] = yb
        merged = _sigmoid(ga_ref[...]) * ya + _sigmoid(gb_ref[...]) * yb
        mg_ref[...] = merged.astype(_ACT)
        mo = _dot(merged, wout_ref[...])
        mo_ref[...] = mo
        xo_ref[...] = x_ref[...] + g1_ref[...] * mo

    return _rowcall("mix_fwd", body, n_rows, tm, [ypre, o, ga, gb, x], [g1, bglu, wglu, wa, wb, wout],
                    [(sw, _ACT), (d, F32), (d, F32), (d, _ACT), (d, F32), (d, F32)], temp_cols=4 * d)


def _ffn_fwd(x, n2g, sc2, sh2, g2, wg, wu, wd, tm):
    n_rows, d = x.shape
    ff = wg.shape[1]

    def body(x_ref, g_ref, sc_ref, sh_ref, g2_ref, wg_ref, wu_ref, wd_ref, hb_ref, a_ref, b_ref, d_ref, xo_ref):
        xv = x_ref[...]
        xh, _ = _rms_stats(xv)
        hb = ((xh * g_ref[...]) * (1.0 + sc_ref[...]) + sh_ref[...]).astype(_MM)
        hb_ref[...] = hb.astype(_ACT)
        a = _dot(hb, wg_ref[...])
        b = _dot(hb, wu_ref[...])
        a_ref[...] = a
        b_ref[...] = b
        dn = _dot((a * _sigmoid(a)) * b, wd_ref[...])
        d_ref[...] = dn
        xo_ref[...] = xv + g2_ref[...] * dn

    return _rowcall("ffn_fwd", body, n_rows, tm, [x], [n2g, sc2, sh2, g2, wg, wu, wd],
                    [(d, _ACT), (ff, F32), (ff, F32), (d, F32), (d, F32)], temp_cols=3 * ff)


def _head(x, fg, target, tm):
    n_rows, d = x.shape

    def body(x_ref, t_ref, g_ref, dx_ref, loss_ref, dg_ref):
        xh, rstd = _rms_stats(x_ref[...])
        err = xh * g_ref[...] - t_ref[...]
        part = jnp.sum(jnp.mean(err * err, axis=-1, keepdims=True), axis=0, keepdims=True) * 0.5
        _acc(loss_ref, jnp.broadcast_to(part, loss_ref.shape))
        dy = err * (1.0 / d)
        _acc(dg_ref, _rows_sum(dy * xh))
        dx_ref[...] = _rms_bwd(dy * g_ref[...], xh, rstd)

    return _rowcall("head", body, n_rows, tm, [x, target], [fg], [(d, F32)], [((1, 128), F32), ((1, d), F32)],
                    temp_cols=4 * d)


def _ffn_bwd(dxo, xmid, a, b, dn, n2g, sc2, g2, wg, wu, wd, tm):
    n_rows, d = dxo.shape
    ff = a.shape[1]

    def act_body(dxo_ref, a_ref, b_ref, dn_ref, g2_ref, wd_ref, da_ref, db_ref, f_ref, dd_ref, dg2_ref):
        dxo_v = dxo_ref[...]
        dd = dxo_v * g2_ref[...]
        dd_ref[...] = dd.astype(_ACT)
        _acc(dg2_ref, _rows_sum(dxo_v * dn_ref[...]))
        df = _dot_nt(dd, wd_ref[...])
        av, bv = a_ref[...], b_ref[...]
        sa = _sigmoid(av)
        si = av * sa
        f_ref[...] = (si * bv).astype(_ACT)
        da_ref[...] = (df * bv * (sa * (1.0 + av * (1.0 - sa)))).astype(_ACT)
        db_ref[...] = (df * si).astype(_ACT)

    da, db, f, dd, dg2 = _rowcall("ffn_bwd_act", act_body, n_rows, 2 * tm, [dxo, a, b, dn], [g2, wd],
                                  [(ff, _ACT), (ff, _ACT), (ff, _ACT), (d, _ACT)], [((1, d), F32)], temp_cols=4 * ff)

    def in_body(dxo_ref, x_ref, da_ref, db_ref, g_ref, sc_ref, wg_ref, wu_ref, dx_ref, dsh_ref, dsc_ref, dn2_ref):
        dh = _dot_nt(da_ref[...], wg_ref[...]) + _dot_nt(db_ref[...], wu_ref[...])
        xh, rstd = _rms_stats(x_ref[...])
        yg = xh * g_ref[...]
        _acc(dsh_ref, _rows_sum(dh))
        _acc(dsc_ref, _rows_sum(dh * yg))
        dy = dh * (1.0 + sc_ref[...])
        _acc(dn2_ref, _rows_sum(dy * xh))
        dx_ref[...] = dxo_ref[...] + _rms_bwd(dy * g_ref[...], xh, rstd)

    dx, dsh, dsc, dn2 = _rowcall("ffn_bwd_in", in_body, n_rows, 2 * tm, [dxo, xmid, da, db], [n2g, sc2, wg, wu],
                                 [(d, F32)], [((1, d), F32)] * 3, temp_cols=5 * d)
    return dx, da, db, f, dd, dg2, dsh, dsc, dn2


def _mix_bwd(dxm, mo, ya, yb, ga, gb, ypre, o, g1, bglu, wglu, wa, wb, wout, tm):
    n_rows, d = dxm.shape
    sw = ypre.shape[1]

    def body(dxm_ref, mo_ref, ya_ref, yb_ref, ga_ref, gb_ref, y_ref, o_ref, g1_ref, bglu_ref, wglu_ref, wa_ref, wb_ref,
             wout_ref, dmo_ref, dya_ref, dyb_ref, dt_ref, ys_ref, dga_ref, dgb_ref, dy_ref, do_ref, delta_ref,
             dg1_ref, dbg_ref):
        dxm_v = dxm_ref[...]
        dmo = dxm_v * g1_ref[...]
        dmo_ref[...] = dmo.astype(_ACT)
        _acc(dg1_ref, _rows_sum(dxm_v * mo_ref[...]))
        dmg = _dot_nt(dmo, wout_ref[...])
        sa, sb = _sigmoid(ga_ref[...]), _sigmoid(gb_ref[...])
        dya, dyb = dmg * sa, dmg * sb
        dya_ref[...] = dya.astype(_ACT)
        dyb_ref[...] = dyb.astype(_ACT)
        dga_ref[...] = (dmg * ya_ref[...] * (sa * (1.0 - sa))).astype(_ACT)
        dgb_ref[...] = (dmg * yb_ref[...] * (sb * (1.0 - sb))).astype(_ACT)
        do = _dot_nt(dyb, wb_ref[...])
        do_ref[...] = do
        prod = do * o_ref[...]
        for h in range(N_HEADS):
            lanes = slice(h * HEAD_PAD, (h + 1) * HEAD_PAD)
            delta_ref[:, lanes] = jnp.broadcast_to(jnp.sum(prod[:, lanes], axis=1, keepdims=True), (prod.shape[0], HEAD_PAD))
        dyg = _dot_nt(dya, wa_ref[...])
        yv = y_ref[...]
        ys = _gelu(yv)
        ys_ref[...] = ys.astype(_ACT)
        sg = _sigmoid(_dot(ys, wglu_ref[...]) + bglu_ref[...])
        dt = dyg * ys * (sg * (1.0 - sg))
        dt_ref[...] = dt.astype(_ACT)
        _acc(dbg_ref, _rows_sum(dt))
        dys = dyg * sg + _dot_nt(dt, wglu_ref[...])
        dy_ref[...] = dys * _gelu_grad(yv)

    return _rowcall("mix_bwd", body, n_rows, tm, [dxm, mo, ya, yb, ga, gb, ypre, o], [g1, bglu, wglu, wa, wb, wout],
                    [(d, _ACT), (d, _ACT), (d, _ACT), (sw, _ACT), (sw, _ACT), (d, _ACT), (d, _ACT), (sw, F32), (QW, F32),
                     (QW, F32)],
                    [((1, d), F32), ((1, sw), F32)], temp_cols=6 * d)


def _attn_bwd(q, k, v, do, delta, lse, pos_col, pos_row, scale):
    n_rows = q.shape[0]
    ta = _tile(n_rows, 512, 128)
    nb = n_rows // ta
    qmap, kmap = _causal_steps(nb, key_major=True)
    hb = ATTN_BWD_HEADS_PER_STEP
    wide = hb * HEAD_PAD
    c2 = scale * LOG2_E

    def body(qm, km, q_ref, k_ref, v_ref, do_ref, delta_ref, lse_ref, pq_ref, pk_ref, dq_ref, dk_ref, dv_ref,
             dk_acc, dv_acc):
        s_id = pl.program_id(1)
        qi, ki = qm[s_id], km[s_id]

        @pl.when(s_id == 0)
        def _():
            dq_ref[...] = jnp.zeros_like(dq_ref)

        @pl.when(qi == ki)
        def _():
            dk_acc[...] = jnp.zeros_like(dk_acc)
            dv_acc[...] = jnp.zeros_like(dv_acc)

        rows = pl.ds(pl.multiple_of(qi * ta, ta), ta)

        def update(on_diagonal):
            if on_diagonal:
                visible = pk_ref[...] <= pq_ref[...]
            for h in range(hb):
                lanes = slice(h * HEAD_PAD, (h + 1) * HEAD_PAD)
                qv, kv, dov = q_ref[:, lanes], k_ref[:, lanes], do_ref[:, lanes]
                e = _dot_nt(qv, kv) * c2 - lse_ref[:, lanes][:, :1]
                if on_diagonal:
                    e = jnp.where(visible, e, NEG_INF)
                p = jnp.exp2(e)
                dp = _dot_nt(dov, v_ref[:, lanes])
                ds = p * (dp - delta_ref[:, lanes][:, :1])
                dv_acc[:, lanes] += _dot_tn(p, dov)
                dk_acc[:, lanes] += _dot_tn(ds, qv)
                dq_ref[rows, lanes] += _dot(ds, kv) * scale

        pl.when(ki != qi)(functools.partial(update, False))
        pl.when(ki == qi)(functools.partial(update, True))

        @pl.when(qi == nb - 1)
        def _():
            dk_ref[...] = dk_acc[...] * scale
            dv_ref[...] = dv_acc[...]

    qspec = pl.BlockSpec((ta, wide), lambda h, s, qm, km: (qm[s], h))
    kspec = pl.BlockSpec((ta, wide), lambda h, s, qm, km: (km[s], h))
    grid_spec = pltpu.PrefetchScalarGridSpec(
        num_scalar_prefetch=2, grid=(N_HEADS // hb, int(qmap.shape[0])),
        in_specs=[qspec, kspec, kspec, qspec, qspec, qspec,
                  pl.BlockSpec((ta, 1), lambda h, s, qm, km: (qm[s], 0)),
                  pl.BlockSpec((1, ta), lambda h, s, qm, km: (0, km[s]))],
        out_specs=[pl.BlockSpec((n_rows, wide), lambda h, s, qm, km: (0, h)), kspec, kspec],
        scratch_shapes=[pltpu.VMEM((ta, wide), F32), pltpu.VMEM((ta, wide), F32)])
    return pl.pallas_call(
        body, name="attn_bwd", grid_spec=grid_spec,
        out_shape=[jax.ShapeDtypeStruct((n_rows, QW), F32)] * 3,
        compiler_params=pltpu.CompilerParams(
            dimension_semantics=("parallel", "arbitrary"),
            vmem_limit_bytes=_vmem_limit(12 * _nbytes((ta, wide), F32) + _nbytes((n_rows, wide), F32),
                                         6 * hb * _nbytes((ta, ta), F32))),
    )(qmap, kmap, q, k, v, do, delta, lse, pos_col, pos_row)


def _mla_prep_bwd(dq, dk, dv, cq, ckv, rc, rs1, rs2, gq, gkv, wuq, wuk, wuv, nope, tm):
    n_rows = cq.shape[0]
    ql, kl = cq.shape[1], ckv.shape[1]

    def body(dq_ref, dk_ref, dv_ref, cq_ref, ckv_ref, c_ref, s1_ref, s2_ref, gq_ref, gkv_ref, wuq_ref, wuk_ref,
             wuv_ref, dqp_ref, dcq_ref, dckv_ref, dkr_ref, dgq_ref, dgkv_ref):
        c, s1, s2 = c_ref[...], s1_ref[...], s2_ref[...]
        c8, s18, s28 = (jnp.tile(a, (1, N_HEADS)) for a in (c, s1, s2))
        dqp = _rope_adjoint(dq_ref[...], c8, s18, s28)
        dqp_ref[...] = dqp.astype(_ACT)
        dcqn = _dot_nt(dqp, wuq_ref[...])
        xh, rstd = _rms_stats(cq_ref[...])
        _acc(dgq_ref, _rows_sum(dcqn * xh))
        dcq_ref[...] = _rms_bwd(dcqn * gq_ref[...], xh, rstd).astype(_ACT)
        dkv = dk_ref[...]
        dkpe = dkv[:, 0:HEAD_PAD]
        for h in range(1, N_HEADS):
            dkpe = dkpe + dkv[:, h * HEAD_PAD:(h + 1) * HEAD_PAD]
        lane = lax.broadcasted_iota(jnp.int32, dkpe.shape, 1)
        dkpe = jnp.where((lane >= nope) & (lane < nope + QK_ROPE), dkpe, 0.0)
        dkr_ref[...] = _rope_adjoint(dkpe, c, s1, s2).astype(_ACT)
        dckvn = _dot_nt(dkv, wuk_ref[...]) + _dot_nt(dv_ref[...], wuv_ref[...])
        xh, rstd = _rms_stats(ckv_ref[...])
        _acc(dgkv_ref, _rows_sum(dckvn * xh))
        dckv_ref[...] = _rms_bwd(dckvn * gkv_ref[...], xh, rstd).astype(_ACT)

    return _rowcall("mla_prep_bwd", body, n_rows, tm, [dq, dk, dv, cq, ckv, rc, rs1, rs2], [gq, gkv, wuq, wuk, wuv],
                    [(QW, _ACT), (ql, _ACT), (kl, _ACT), (HEAD_PAD, _ACT)], [((1, ql), F32), ((1, kl), F32)],
                    temp_cols=6 * QW)


def _ssm_bwd(dy, u, hre, him, bre_blk, bim_blk, abr_row, abi_row, cre_blk, cimneg_blk, d_row):
    n_rows, sw = u.shape
    gp = abr_row.shape[1]
    t = _tile(n_rows, 256)
    n_chunks = n_rows // t

    def body(dy_ref, u_ref, hre_ref, him_ref, hbre_ref, hbim_ref, bre_ref, bim_ref, ar_ref, ai_ref, cre_ref, cim_ref,
             d_ref, du_ref, gre_ref, gim_ref, dar_ref, dai_ref, dd_ref, g_re, g_im, hs_re, hs_im, cr, ci):
        i = pl.program_id(0)

        @pl.when(i == 0)
        def _():
            cr[...] = jnp.zeros_like(cr)
            ci[...] = jnp.zeros_like(ci)

        dyv = dy_ref[...]
        dyb = dyv.astype(_MM)
        g_re[...] = _dot_nt(dyb, cre_ref[...])
        g_im[...] = _dot_nt(dyb, cim_ref[...])
        a_r, a_i = ar_ref[...], ai_ref[...]

        def step(k, carry):
            nr, ni = carry
            row = pl.ds(t - 1 - k, 1)
            gr = g_re[row, :] + a_r * nr + a_i * ni
            gi = g_im[row, :] + a_r * ni - a_i * nr
            g_re[row, :] = gr
            g_im[row, :] = gi
            return gr, gi

        nr, ni = lax.fori_loop(0, t, step, (cr[0:1, :], ci[0:1, :]), unroll=8)
        cr[0:1, :] = nr
        ci[0:1, :] = ni
        gr_all, gi_all = g_re[...], g_im[...]
        gre_ref[...] = gr_all.astype(_ACT)
        gim_ref[...] = gi_all.astype(_ACT)
        du_ref[...] = (_dot_nt(gr_all, bre_ref[...]) + _dot_nt(gi_all, bim_ref[...]) + d_ref[...] * dyv).astype(_ACT)
        _acc(dd_ref, _rows_sum(dyv * u_ref[...]))
        is_first_chunk = i == n_chunks - 1
        hs_re[0:8, :] = jnp.where(is_first_chunk, 0.0, hbre_ref[...])
        hs_im[0:8, :] = jnp.where(is_first_chunk, 0.0, hbim_ref[...])
        hs_re[8:t + 8, :] = hre_ref[...]
        hs_im[8:t + 8, :] = him_ref[...]
        hp_re, hp_im = hs_re[pl.ds(7, t), :], hs_im[pl.ds(7, t), :]
        _acc(dar_ref, _rows_sum(gr_all * hp_re + gi_all * hp_im))
        _acc(dai_ref, _rows_sum(gi_all * hp_re - gr_all * hp_im))

    rev = lambda c: pl.BlockSpec((t, c), lambda i: (n_chunks - 1 - i, 0))
    before = pl.BlockSpec((8, gp), lambda i: (jnp.maximum((n_chunks - 1 - i) * (t // 8) - 1, 0), 0))
    full = lambda a: pl.BlockSpec(a.shape, lambda i: (0, 0), pipeline_mode=pl.Buffered(1))
    acc = lambda c: pl.BlockSpec((1, c), lambda i: (0, 0))
    blocks = 2 * _nbytes((t, sw), F32) + 2 * _nbytes((t, gp), F32) + _nbytes((t, sw), _ACT) + 2 * _nbytes((t, gp), _ACT)
    resident = 4 * _nbytes((sw, gp), _MM) + 4 * _nbytes((t + 8, gp), F32)
    return pl.pallas_call(
        body, name="ssm_bwd", grid=(n_chunks,),
        in_specs=[rev(sw), rev(sw), rev(gp), rev(gp), before, before, full(bre_blk), full(bim_blk), full(abr_row),
                  full(abi_row), full(cre_blk), full(cimneg_blk), full(d_row)],
        out_specs=[rev(sw), rev(gp), rev(gp), acc(gp), acc(gp), acc(sw)],
        out_shape=[jax.ShapeDtypeStruct((n_rows, sw), _ACT), jax.ShapeDtypeStruct((n_rows, gp), _ACT),
                   jax.ShapeDtypeStruct((n_rows, gp), _ACT), jax.ShapeDtypeStruct((1, gp), F32),
                   jax.ShapeDtypeStruct((1, gp), F32), jax.ShapeDtypeStruct((1, sw), F32)],
        scratch_shapes=[pltpu.VMEM((t, gp), F32), pltpu.VMEM((t, gp), F32), pltpu.VMEM((t + 8, gp), F32),
                        pltpu.VMEM((t + 8, gp), F32), pltpu.VMEM((8, gp), F32), pltpu.VMEM((8, gp), F32)],
        compiler_params=pltpu.CompilerParams(dimension_semantics=("arbitrary",),
                                             vmem_limit_bytes=_vmem_limit(blocks, resident + 4 * _nbytes((t, gp), F32))),
    )(dy, u, hre, him, hre, him, bre_blk, bim_blk, abr_row, abi_row, cre_blk, cimneg_blk, d_row)


def _in_bwd(dxm, x, dz_parts, n1g, sc1, w_parts, tm):
    n_rows, d = x.shape
    n = len(dz_parts)

    def body(dxm_ref, x_ref, *rest):
        dz_refs = rest[:n]
        g_ref, sc_ref = rest[n], rest[n + 1]
        w_refs = rest[n + 2:2 * n + 2]
        dx_ref, dsh_ref, dsc_ref, dn1_ref = rest[2 * n + 2:]
        dh = None
        for dz_ref, w_ref in zip(dz_refs, w_refs):
            term = _dot_nt(dz_ref[...], w_ref[...])
            dh = term if dh is None else dh + term
        xh, rstd = _rms_stats(x_ref[...])
        yg = xh * g_ref[...]
        _acc(dsh_ref, _rows_sum(dh))
        _acc(dsc_ref, _rows_sum(dh * yg))
        dy = dh * (1.0 + sc_ref[...])
        _acc(dn1_ref, _rows_sum(dy * xh))
        dx_ref[...] = dxm_ref[...] + _rms_bwd(dy * g_ref[...], xh, rstd)

    return _rowcall("in_bwd", body, n_rows, tm, [dxm, x, *dz_parts], [n1g, sc1, *w_parts],
                    [(d, F32)], [((1, d), F32)] * 3, temp_cols=5 * d)


def _pad_heads(w, per_head):
    lead = w.shape[:-1]
    w = w.reshape(lead + (N_HEADS, per_head))
    w = jnp.pad(w, [(0, 0)] * len(lead) + [(0, 0), (0, HEAD_PAD - per_head)])
    return w.reshape(lead + (QW,))


def _unpad_heads(w, per_head):
    lead = w.shape[:-1]
    return w.reshape(lead + (N_HEADS, HEAD_PAD))[..., :per_head].reshape(lead + (N_HEADS * per_head,))


def _cols_from_chips(g):
    ch, dep, r, cs = g.shape
    return g.transpose(1, 2, 0, 3).reshape(dep, r, ch * cs)


def _rows_from_chips(g):
    ch, dep, rs, c = g.shape
    return g.transpose(1, 0, 2, 3).reshape(dep, ch * rs, c)


def _cols_to_chips(w):
    dep, r, c = w.shape
    return w.reshape(dep, r, 4, c // 4).transpose(2, 0, 1, 3)


def _rows_to_chips(w):
    dep, r, c = w.shape
    return w.reshape(dep, 4, r // 4, c).transpose(1, 0, 2, 3)


def _block_diag(b_gxy):
    g, xx, yy = b_gxy.shape
    eye = jnp.eye(g, dtype=b_gxy.dtype)
    return (b_gxy[:, :, None, :] * eye[:, None, :, None]).reshape(g * xx, g * yy)


def _block_diag_extract(full, g):
    xx, yy = full.shape[0] // g, full.shape[1] // g
    eye = jnp.eye(g, dtype=full.dtype)
    return jnp.sum(full.reshape(g, xx, g, yy) * eye[:, None, :, None], axis=2)


def _pack_rows(arrays):
    parts = []
    for a in arrays:
        flat = a.reshape(-1)
        flat = jnp.pad(flat, (0, (-flat.shape[0]) % 1024))
        parts.append(flat.reshape(-1, 128))
    return jnp.concatenate(parts, axis=0)


def _unpack_rows(packed, shapes):
    out, row = [], 0
    for s in shapes:
        n = int(np.prod(s))
        rows = -(-n // 1024) * 8
        out.append(packed[row:row + rows].reshape(-1)[:n].reshape(s))
        row += rows
    return out


def kernel(x, c, positions, w_ada, b_ada, norm1_g, w_in, ssm_a_re, ssm_a_im, ssm_log_dt, ssm_b_re, ssm_b_im, ssm_c_re, ssm_c_im, ssm_d, w_glu, b_glu, w_a_out, q_norm_g, w_uq, kv_norm_g, w_uk, w_uv, w_b_out, w_out, norm2_g, w_gate, w_up, w_down, final_g, loss_target, m_w_ada, m_b_ada, m_norm1_g, m_w_in, m_ssm_a_re, m_ssm_a_im, m_ssm_log_dt, m_ssm_b_re, m_ssm_b_im, m_ssm_c_re, m_ssm_c_im, m_ssm_d, m_w_glu, m_b_glu, m_w_a_out, m_q_norm_g, m_w_uq, m_kv_norm_g, m_w_uk, m_w_uv, m_w_b_out, m_w_out, m_norm2_g, m_w_gate, m_w_up, m_w_down, m_final_g, v_w_ada, v_b_ada, v_norm1_g, v_w_in, v_ssm_a_re, v_ssm_a_im, v_ssm_log_dt, v_ssm_b_re, v_ssm_b_im, v_ssm_c_re, v_ssm_c_im, v_ssm_d, v_w_glu, v_b_glu, v_w_a_out, v_q_norm_g, v_w_uq, v_kv_norm_g, v_w_uk, v_w_uv, v_w_b_out, v_w_out, v_norm2_g, v_w_gate, v_w_up, v_w_down, v_final_g):
    weights = dict(w_ada=w_ada, b_ada=b_ada, norm1_g=norm1_g, w_in=w_in, ssm_a_re=ssm_a_re, ssm_a_im=ssm_a_im, ssm_log_dt=ssm_log_dt, ssm_b_re=ssm_b_re, ssm_b_im=ssm_b_im, ssm_c_re=ssm_c_re, ssm_c_im=ssm_c_im, ssm_d=ssm_d, w_glu=w_glu, b_glu=b_glu, w_a_out=w_a_out, q_norm_g=q_norm_g, w_uq=w_uq, kv_norm_g=kv_norm_g, w_uk=w_uk, w_uv=w_uv, w_b_out=w_b_out, w_out=w_out, norm2_g=norm2_g, w_gate=w_gate, w_up=w_up, w_down=w_down, final_g=final_g)
    mom_m = dict(w_ada=m_w_ada, b_ada=m_b_ada, norm1_g=m_norm1_g, w_in=m_w_in, ssm_a_re=m_ssm_a_re, ssm_a_im=m_ssm_a_im, ssm_log_dt=m_ssm_log_dt, ssm_b_re=m_ssm_b_re, ssm_b_im=m_ssm_b_im, ssm_c_re=m_ssm_c_re, ssm_c_im=m_ssm_c_im, ssm_d=m_ssm_d, w_glu=m_w_glu, b_glu=m_b_glu, w_a_out=m_w_a_out, q_norm_g=m_q_norm_g, w_uq=m_w_uq, kv_norm_g=m_kv_norm_g, w_uk=m_w_uk, w_uv=m_w_uv, w_b_out=m_w_b_out, w_out=m_w_out, norm2_g=m_norm2_g, w_gate=m_w_gate, w_up=m_w_up, w_down=m_w_down, final_g=m_final_g)
    mom_v = dict(w_ada=v_w_ada, b_ada=v_b_ada, norm1_g=v_norm1_g, w_in=v_w_in, ssm_a_re=v_ssm_a_re, ssm_a_im=v_ssm_a_im, ssm_log_dt=v_ssm_log_dt, ssm_b_re=v_ssm_b_re, ssm_b_im=v_ssm_b_im, ssm_c_re=v_ssm_c_re, ssm_c_im=v_ssm_c_im, ssm_d=v_ssm_d, w_glu=v_w_glu, b_glu=v_b_glu, w_a_out=v_w_a_out, q_norm_g=v_q_norm_g, w_uq=v_w_uq, kv_norm_g=v_kv_norm_g, w_uk=v_w_uk, w_uv=v_w_uv, w_b_out=v_w_b_out, w_out=v_w_out, norm2_g=v_norm2_g, w_gate=v_w_gate, w_up=v_w_up, w_down=v_w_down, final_g=v_final_g)
    names = list(weights)

    depth = w_in.shape[0]
    seq, d = x.shape[1], x.shape[2]
    sw = ssm_d.shape[1]
    groups, n_state, n_chan = ssm_b_re.shape[1:]
    gp = groups * n_state
    ql, kl = q_norm_g.shape[1], kv_norm_g.shape[1]
    nope = w_uk.shape[2] * 4 // N_HEADS
    vdim = w_uv.shape[2] * 4 // N_HEADS
    qk_dim = nope + QK_ROPE
    scale = qk_dim ** -0.5
    tm = _tile(seq, 256, 16)
    tm_ffn = _tile(seq, 128, 16)
    me = 4 * lax.axis_index("x") + 2 * lax.axis_index("y") + lax.axis_index("c")
    chip = 2 * lax.axis_index("x") + lax.axis_index("y")

    xs = x.reshape(seq, d)
    target = loss_target.reshape(seq, d)
    pos_f = positions.astype(F32)
    pos_col = pos_f.reshape(seq, 1)
    pos_row = pos_f.reshape(1, seq)

    (c_all,) = _exchange("gather_c", "gather8", [c])
    c_all = c_all.reshape(8, d)
    ada_cols = w_ada.shape[2]
    mod_part = _mod_fwd(c_all, w_ada.reshape(depth * d, ada_cols), depth)
    (mod_all,) = _exchange("gather_mod", "gather8", [mod_part])
    mod_all = mod_all.reshape(4, 2, depth, 8, ada_cols)[:, 0]
    mod_me = lax.dynamic_index_in_dim(mod_all, me, axis=2, keepdims=False)
    mod = mod_me.transpose(1, 0, 2).reshape(depth, 4 * ada_cols) + b_ada
    mod = mod.reshape(depth, 6, 1, d)

    big = ["w_in", "w_glu", "w_a_out", "w_uq", "w_uk", "w_uv", "w_b_out", "w_out", "w_gate", "w_up", "w_down"]
    row_sharded = {"w_glu", "w_out", "w_down"}
    gathered = _gather_chips_two_level("gather_weights", [weights[n].astype(_MM) for n in big])
    full = {n: (_rows_from_chips(g) if n in row_sharded else _cols_from_chips(g)) for n, g in zip(big, gathered)}
    o1, o2, o3, o4, o5 = sw, sw + ql, sw + ql + kl, sw + ql + kl + QK_ROPE, sw + ql + kl + QK_ROPE + d
    wi = full["w_in"]
    w_u, w_cq, w_ckv, w_ga, w_gb = wi[:, :, :o1], wi[:, :, o1:o2], wi[:, :, o2:o3], wi[:, :, o4:o5], wi[:, :, o5:]
    w_kr = jnp.pad(wi[:, :, o3:o4], ((0, 0), (0, 0), (nope, HEAD_PAD - nope - QK_ROPE)))
    wuq_p = _pad_heads(full["w_uq"], qk_dim)
    wuk_p = _pad_heads(full["w_uk"], nope)
    wuv_p = _pad_heads(full["w_uv"], vdim)
    wb_p = _pad_heads(full["w_b_out"].transpose(0, 2, 1), vdim).transpose(0, 2, 1)

    inv_freq = ROPE_BASE ** (-jnp.arange(0, QK_ROPE, 2, dtype=F32) / QK_ROPE)
    inv_lane = jnp.pad(jnp.concatenate([inv_freq, inv_freq]), (nope, HEAD_PAD - nope - QK_ROPE)).reshape(1, HEAD_PAD)
    rc, rs1, rs2 = _rope_tables(pos_col, inv_lane, nope)
    a_re_col = ssm_a_re.reshape(depth * gp, 1)
    a_im_col = ssm_a_im.reshape(depth * gp, 1)
    ldt_col = jnp.broadcast_to(ssm_log_dt[:, :, None], (depth, groups, n_state)).reshape(depth * gp, 1)
    b_re2, b_im2 = ssm_b_re.reshape(depth * gp, n_chan), ssm_b_im.reshape(depth * gp, n_chan)
    abr, abi, bbr, bbi = _ssm_disc_fwd(a_re_col, a_im_col, ldt_col, b_re2, b_im2)
    abr_rows, abi_rows = abr.reshape(depth, 1, gp), abi.reshape(depth, 1, gp)
    bbr, bbi = bbr.reshape(depth, groups, n_state, n_chan), bbi.reshape(depth, groups, n_state, n_chan)

    saved = []
    xl = xs
    for l in range(depth):
        sh1, sc1, g1, sh2, sc2, g2 = (mod[l, j] for j in range(6))
        n1g, n2g = norm1_g[l].reshape(1, d), norm2_g[l].reshape(1, d)
        w_parts = [w_u[l], w_cq[l], w_ckv[l], w_kr[l], w_ga[l], w_gb[l]]
        hb, u, cq, ckv, kr, ga, gb = _in_fwd(xl, n1g, sc1, sh1, w_parts, tm)
        bre_blk = _block_diag(bbr[l].transpose(0, 2, 1)).astype(_MM)
        bim_blk = _block_diag(bbi[l].transpose(0, 2, 1)).astype(_MM)
        cre_blk = _block_diag(ssm_c_re[l].transpose(0, 2, 1)).astype(_MM)
        cimneg_blk = _block_diag(-ssm_c_im[l].transpose(0, 2, 1)).astype(_MM)
        d_row = ssm_d[l].reshape(1, sw)
        ssm_w = (bre_blk, bim_blk, abr_rows[l], abi_rows[l], cre_blk, cimneg_blk, d_row)
        ypre, hre, him = _ssm_fwd(u, *ssm_w)
        gq, gkv = q_norm_g[l].reshape(1, ql), kv_norm_g[l].reshape(1, kl)
        q, k, v, cqn, ckvn = _mla_prep_fwd(cq, ckv, kr, rc, rs1, rs2, gq, gkv, wuq_p[l], wuk_p[l], wuv_p[l], vdim, tm)
        o, lse = _attn_fwd(q, k, v, pos_col, pos_row, scale, vdim)
        bglu = b_glu[l].reshape(1, sw)
        yg, ya, yb, merged, mo, xmid = _mix_fwd(ypre, o, ga, gb, xl, g1, bglu, full["w_glu"][l], full["w_a_out"][l],
                                                 wb_p[l], full["w_out"][l], tm)
        hb2, fa, fb, dn, xout = _ffn_fwd(xmid, n2g, sc2, sh2, g2, full["w_gate"][l], full["w_up"][l], full["w_down"][l], 2 * tm_ffn)
        saved.append(dict(x=xl, hb=hb, u=u, cq=cq, ckv=ckv, ga=ga, gb=gb, ssm_w=ssm_w, ypre=ypre, hre=hre, him=him,
                          q=q, k=k, v=v, cqn=cqn, ckvn=ckvn, o=o, lse=lse, yg=yg, ya=ya, yb=yb, merged=merged, mo=mo,
                          xmid=xmid, hb2=hb2, fa=fa, fb=fb, dn=dn, w_parts=w_parts))
        xl = xout

    dx, loss_acc, g_final = _head(xl, final_g.reshape(1, d), target, tm)
    loss = lax.psum(loss_acc[0, 0], ("x", "y", "c"))

    per_layer = ["w_gate", "w_up", "w_down", "norm2_g", "w_out", "w_a_out", "w_b_out", "w_glu", "b_glu", "w_uq", "w_uk",
                 "w_uv", "q_norm_g", "kv_norm_g", "ssm_d", "ssm_c_re", "ssm_c_im", "w_in", "norm1_g"]
    grads = {n: [None] * depth for n in per_layer}
    dmod = [None] * depth
    for l in reversed(range(depth)):
        s = saved[l]
        sh1, sc1, g1, sh2, sc2, g2 = (mod[l, j] for j in range(6))
        n1g, n2g = norm1_g[l].reshape(1, d), norm2_g[l].reshape(1, d)
        dxm, da, db, fb16, dd, dg2, dsh2, dsc2, dn2 = _ffn_bwd(
            dx, s["xmid"], s["fa"], s["fb"], s["dn"], n2g, sc2, g2, full["w_gate"][l], full["w_up"][l], full["w_down"][l], tm_ffn)
        grads["w_gate"][l] = _mm_tn("dw_gate", s["hb2"], da)
        grads["w_up"][l] = _mm_tn("dw_up", s["hb2"], db)
        grads["w_down"][l] = _mm_tn("dw_down", fb16, dd)
        grads["norm2_g"][l] = dn2.reshape(d)

        bglu = b_glu[l].reshape(1, sw)
        dmo, dya, dyb, dt, ys, dga, dgb, dypre, do, delta, dg1, dbglu = _mix_bwd(
            dxm, s["mo"], s["ya"], s["yb"], s["ga"], s["gb"], s["ypre"], s["o"], g1, bglu, full["w_glu"][l],
            full["w_a_out"][l], wb_p[l], full["w_out"][l], tm)
        grads["w_out"][l] = _mm_tn("dw_out", s["merged"], dmo)
        grads["w_a_out"][l] = _mm_tn("dw_a_out", s["yg"], dya)
        dwb_p = _mm_tn("dw_b_out", s["o"], dyb)
        grads["w_b_out"][l] = _unpad_heads(dwb_p.T, vdim).T
        grads["w_glu"][l] = _mm_tn("dw_glu", ys, dt)
        grads["b_glu"][l] = dbglu.reshape(sw)

        dq, dk, dv = _attn_bwd(s["q"], s["k"], s["v"], do, delta, s["lse"], pos_col, pos_row, scale)
        gq, gkv = q_norm_g[l].reshape(1, ql), kv_norm_g[l].reshape(1, kl)
        dqp, dcq, dckv, dkr, dgq, dgkv = _mla_prep_bwd(dq, dk, dv, s["cq"], s["ckv"], rc, rs1, rs2, gq, gkv,
                                                       wuq_p[l], wuk_p[l], wuv_p[l], nope, tm)
        grads["w_uq"][l] = _unpad_heads(_mm_tn("dw_uq", s["cqn"], dqp), qk_dim)
        grads["w_uk"][l] = _unpad_heads(_mm_tn("dw_uk", s["ckvn"], dk), nope)
        grads["w_uv"][l] = _unpad_heads(_mm_tn("dw_uv", s["ckvn"], dv), vdim)
        grads["q_norm_g"][l] = dgq.reshape(ql)
        grads["kv_norm_g"][l] = dgkv.reshape(kl)

        du, gre, gim, dar, dai, ddskip = _ssm_bwd(dypre, s["u"], s["hre"], s["him"], *s["ssm_w"])
        grads["ssm_d"][l] = ddskip.reshape(sw)
        d_bre = _block_diag_extract(_mm_tn("d_bre", s["u"], gre), groups).transpose(0, 2, 1)
        d_bim = _block_diag_extract(_mm_tn("d_bim", s["u"], gim), groups).transpose(0, 2, 1)
        grads["ssm_c_re"][l] = _block_diag_extract(_mm_tn("d_cre", s["hre"], dypre), groups).transpose(0, 2, 1)
        grads["ssm_c_im"][l] = -_block_diag_extract(_mm_tn("d_cim", s["him"], dypre), groups).transpose(0, 2, 1)
        s["disc_grads"] = (dar.reshape(gp, 1), dai.reshape(gp, 1), d_bre.reshape(gp, n_chan), d_bim.reshape(gp, n_chan))

        dz_parts = [du, dcq, dckv, dkr, dga, dgb]
        dx, dsh1, dsc1, dn1 = _in_bwd(dxm, s["x"], dz_parts, n1g, sc1, s["w_parts"], tm)
        dw_parts = [_mm_tn("dw_in_%d" % j, s["hb"], dz) for j, dz in enumerate(dz_parts)]
        dw_parts[3] = dw_parts[3][:, nope:nope + QK_ROPE]
        grads["w_in"][l] = jnp.concatenate(dw_parts, axis=1)
        grads["norm1_g"][l] = dn1.reshape(d)
        dmod[l] = jnp.concatenate([dsh1, dsc1, dg1, dsh2, dsc2, dg2], axis=1).reshape(6 * d)
    grad_x = dx.reshape(x.shape)

    disc = [jnp.concatenate([saved[l]["disc_grads"][j] for l in range(depth)], axis=0) for j in range(4)]
    da_re, da_im, dldt, db_re, db_im = _ssm_disc_bwd(a_re_col, a_im_col, ldt_col, b_re2, b_im2, *disc)
    stacked = {n: jnp.stack(v) for n, v in grads.items()}
    stacked["ssm_a_re"] = da_re.reshape(ssm_a_re.shape)
    stacked["ssm_a_im"] = da_im.reshape(ssm_a_im.shape)
    stacked["ssm_log_dt"] = _lane_sum(dldt.reshape(depth * groups, n_state)).reshape(ssm_log_dt.shape)
    stacked["ssm_b_re"] = db_re.reshape(ssm_b_re.shape)
    stacked["ssm_b_im"] = db_im.reshape(ssm_b_im.shape)
    stacked["final_g"] = g_final.reshape(d)
    stacked["b_ada"] = jnp.stack(dmod)

    small = [n for n in names if n not in big and n != "w_ada"]
    small_shapes = [weights[n].shape for n in small]
    (small_all,) = _exchange("gather_small", "gather8", [_pack_rows([stacked[n] for n in small])])
    sg, sd, sm, sv = _adamw("adamw_small", [small_all], _pack_rows([weights[n] for n in small]),
                            _pack_rows([mom_m[n] for n in small]), _pack_rows([mom_v[n] for n in small]))
    out_g = dict(zip(small, _unpack_rows(sg, small_shapes)))
    out_d = dict(zip(small, _unpack_rows(sd, small_shapes)))
    out_m = dict(zip(small, _unpack_rows(sm, small_shapes)))
    out_v = dict(zip(small, _unpack_rows(sv, small_shapes)))

    n_dmod = depth * 6 * d
    dmod_all = small_all[:, :n_dmod // 128].reshape(8, depth, 6 * d)
    dmod_cols = lax.dynamic_slice_in_dim(dmod_all, chip * ada_cols, ada_cols, axis=2)
    g_wada = _wada_bwd(c_all, dmod_cols.transpose(1, 0, 2).reshape(depth * 8, ada_cols), depth)
    res = _adamw("adamw_w_ada", [g_wada], w_ada.reshape(depth * d, ada_cols), m_w_ada.reshape(depth * d, ada_cols),
                 v_w_ada.reshape(depth * d, ada_cols))
    out_g["w_ada"], out_d["w_ada"], out_m["w_ada"], out_v["w_ada"] = (r.reshape(w_ada.shape) for r in res)

    core = lax.axis_index("c")
    half = depth // 2
    to_chips = [(_rows_to_chips if n in row_sharded else _cols_to_chips)(stacked[n]).astype(_WIRE) for n in big]
    own_half = [lax.dynamic_slice_in_dim(p, core * half, half, axis=1) for p in to_chips]
    other_half = [lax.dynamic_slice_in_dim(p, (1 - core) * half, half, axis=1) for p in to_chips]
    from_sibling = _exchange("swap_halves", "swap", other_half)
    chip_sum = [_add_pair("pair_" + n, a.reshape(-1, a.shape[-1]), b.reshape(-1, b.shape[-1])).reshape(a.shape)
                for n, a, b in zip(big, own_half, from_sibling)]
    landed = _exchange("scatter_grads", "scatter4", chip_sum)
    finished = [_sum_slots("sum_" + n, r.reshape(4, -1, r.shape[-1])) for n, r in zip(big, landed)]
    sibling = _exchange("swap_partials", "swap", finished)
    for n, mine, theirs in zip(big, finished, sibling):
        shp = weights[n].shape
        as2d = lambda a: a.reshape(-1, shp[-1])
        rows = mine.shape[0]
        g = lax.dynamic_update_slice_in_dim(jnp.zeros((2 * rows, shp[-1]), F32), mine, core * rows, axis=0)
        g = lax.dynamic_update_slice_in_dim(g, theirs, (1 - core) * rows, axis=0)
        res = _adamw("adamw_" + n, [g], as2d(weights[n]), as2d(mom_m[n]), as2d(mom_v[n]))
        out_g[n], out_d[n], out_m[n], out_v[n] = (r.reshape(shp) for r in res)

    return (loss, grad_x, *[out_g[n] for n in names], *[out_d[n] for n in names], *[out_m[n] for n in names],
            *[out_v[n] for n in names])
```

```python
import functools
import math

import numpy as np
import jax
import jax.numpy as jnp
from jax import lax
from jax.experimental import pallas as pl
from jax.experimental.pallas import tpu as pltpu

F32 = jnp.float32
_MM = jnp.bfloat16
_ACT = jnp.bfloat16
_WIRE = jnp.bfloat16

N_HEADS = 8
QK_ROPE = 32
HEAD_PAD = 128
QW = N_HEADS * HEAD_PAD
ROPE_BASE = 10000.0
EPS = 1e-6
DT_MIN = 1e-3
ADAM_LR = 0.001
ADAM_B1 = 0.9
ADAM_B2 = 0.999
ADAM_EPS = 1e-08
ADAM_WD = 0.01
ADAM_STEP = 10
NEG_INF = -1e30
LOG2_E = math.log2(math.e)
ATTN_HEADS_PER_STEP = 4
ATTN_BWD_HEADS_PER_STEP = 2

V7X_VMEM_BYTES = 64 * 1024 * 1024
VMEM_RESERVE_BYTES = 6 * 1024 * 1024
MESH = pl.DeviceIdType.MESH
ANY = pl.BlockSpec(memory_space=pl.ANY)


def _vmem_limit(block_bytes, temp_bytes):
    want = 2 * block_bytes + temp_bytes
    return int(min(V7X_VMEM_BYTES - VMEM_RESERVE_BYTES, max(want, 32 * 1024 * 1024)))


def _nbytes(shape, dtype):
    return int(np.prod(shape)) * jnp.dtype(dtype).itemsize


def _tile(n, target, mult=8):
    t = min(n, target)
    while t >= mult:
        if n % t == 0 and t % mult == 0:
            return t
        t -= 1
    return n


def _dot(a, b):
    return jnp.dot(a.astype(_MM), b.astype(_MM), preferred_element_type=F32)


def _dot_nt(a, b):
    return lax.dot_general(a.astype(_MM), b.astype(_MM), (((1,), (1,)), ((), ())), preferred_element_type=F32)


def _dot_tn(a, b):
    return lax.dot_general(a.astype(_MM), b.astype(_MM), (((0,), (0,)), ((), ())), preferred_element_type=F32)


def _sigmoid(x):
    return jax.nn.sigmoid(x)


_GELU_K = math.sqrt(2.0 / math.pi)


def _gelu(x):
    return x * (0.5 * (1.0 + jnp.tanh(_GELU_K * (x + 0.044715 * (x * x * x)))))


def _gelu_grad(x):
    th = jnp.tanh(_GELU_K * (x + 0.044715 * (x * x * x)))
    return 0.5 * (1.0 + th) + 0.5 * x * (1.0 - th * th) * (_GELU_K * (1.0 + 3.0 * 0.044715 * (x * x)))


def _rows_sum(v):
    return jnp.sum(v, axis=0, keepdims=True)


def _rms_stats(x):
    rstd = lax.rsqrt(jnp.mean(x * x, axis=-1, keepdims=True) + EPS)
    return x * rstd, rstd


def _rms_bwd(dxh, xh, rstd):
    return rstd * (dxh - xh * jnp.mean(dxh * xh, axis=-1, keepdims=True))


def _rowcall(name, body, n_rows, tm, row_ins, full_ins, row_outs, acc_outs=(), temp_cols=0):
    grid = (n_rows // tm,)
    in_specs = [pl.BlockSpec((tm, a.shape[1]), lambda i: (i, 0)) for a in row_ins]
    in_specs += [pl.BlockSpec(a.shape, lambda i: (0, 0), pipeline_mode=pl.Buffered(1)) for a in full_ins]
    out_shape = [jax.ShapeDtypeStruct((n_rows, c), dt) for c, dt in row_outs]
    out_shape += [jax.ShapeDtypeStruct(s, dt) for s, dt in acc_outs]
    out_specs = [pl.BlockSpec((tm, c), lambda i: (i, 0)) for c, _ in row_outs]
    out_specs += [pl.BlockSpec(s, lambda i: (0, 0)) for s, _ in acc_outs]
    blocks = sum(_nbytes((tm, a.shape[1]), a.dtype) for a in row_ins)
    blocks += sum(_nbytes((tm, c), dt) for c, dt in row_outs) + sum(_nbytes(s, dt) for s, dt in acc_outs)
    resident = sum(_nbytes(a.shape, a.dtype) for a in full_ins)
    limit = _vmem_limit(blocks, resident + _nbytes((tm, temp_cols), F32))
    res = pl.pallas_call(
        body, name=name, grid=grid, in_specs=in_specs, out_specs=out_specs, out_shape=out_shape,
        compiler_params=pltpu.CompilerParams(
            dimension_semantics=("arbitrary" if acc_outs else "parallel",), vmem_limit_bytes=limit),
    )(*row_ins, *full_ins)
    return res


def _first_step():
    return pl.program_id(0) == 0


def _acc(ref, val):
    @pl.when(_first_step())
    def _():
        ref[...] = val

    @pl.when(jnp.logical_not(_first_step()))
    def _():
        ref[...] += val


def _mm_tn(name, a, g):
    n_rows, k = a.shape
    n = g.shape[1]
    tk = k if k <= 1024 else _tile(k, 1408, 128)
    tn = n if n <= 1024 else _tile(n, 1408, 128)
    tl = _tile(n_rows, 2048, 16)

    def body(a_ref, g_ref, o_ref):
        @pl.when(pl.program_id(2) == 0)
        def _():
            o_ref[...] = jnp.zeros_like(o_ref)
        o_ref[...] += _dot_tn(a_ref[...], g_ref[...])

    blocks = _nbytes((tl, tk), a.dtype) + _nbytes((tl, tn), g.dtype) + _nbytes((tk, tn), F32)
    return pl.pallas_call(
        body, name=name, grid=(k // tk, n // tn, n_rows // tl),
        in_specs=[pl.BlockSpec((tl, tk), lambda i, j, l: (l, i)), pl.BlockSpec((tl, tn), lambda i, j, l: (l, j))],
        out_specs=pl.BlockSpec((tk, tn), lambda i, j, l: (i, j)),
        out_shape=jax.ShapeDtypeStruct((k, n), F32),
        compiler_params=pltpu.CompilerParams(
            dimension_semantics=("parallel", "parallel", "arbitrary"),
            vmem_limit_bytes=_vmem_limit(blocks, 2 * _nbytes((tl, max(tk, tn)), F32) + _nbytes((tk, tn), F32))),
    )(a, g)


def _place():
    return lax.axis_index("x"), lax.axis_index("y"), lax.axis_index("c")


def _flip(v, bit):
    return 1 - v if bit else v


def _exchange(name, mode, arrays):
    n = len(arrays)
    if mode == "gather8":
        rel = [((k >> 2) & 1, (k >> 1) & 1, k & 1) for k in range(1, 8)]
        out_shape = [jax.ShapeDtypeStruct((8,) + a.shape, a.dtype) for a in arrays]
    elif mode == "scatter4":
        rel = [((k >> 1) & 1, k & 1, 0) for k in range(1, 4)]
        out_shape = [jax.ShapeDtypeStruct(a.shape, a.dtype) for a in arrays]
    else:
        rel = [(0, 0, 1)]
        out_shape = [jax.ShapeDtypeStruct(a.shape, a.dtype) for a in arrays]
    n_rel = len(rel)

    def body(*refs):
        ins, outs = refs[:n], refs[n:2 * n]
        send_sems, recv_sems, local_sems = refs[2 * n:]
        x, y, c = _place()

        def slot(px, py, pc):
            return 4 * px + 2 * py + pc if mode == "gather8" else 2 * px + py

        mine = slot(x, y, c)
        local = []
        if mode != "swap":
            for a in range(n):
                src = ins[a].at[mine] if mode == "scatter4" else ins[a]
                local.append(pltpu.make_async_copy(src, outs[a].at[mine], local_sems.at[a]))
            for cp in local:
                cp.start()

        def remote(r, a):
            px, py, pc = _flip(x, rel[r][0]), _flip(y, rel[r][1]), _flip(c, rel[r][2])
            theirs = slot(px, py, pc)
            if mode == "swap":
                src, dst_there, dst_here = ins[a], outs[a], outs[a]
            elif mode == "scatter4":
                src, dst_there, dst_here = ins[a].at[theirs], outs[a].at[mine], outs[a].at[theirs]
            else:
                src, dst_there, dst_here = ins[a], outs[a].at[mine], outs[a].at[theirs]
            k = r * n + a
            push = pltpu.make_async_remote_copy(src_ref=src, dst_ref=dst_there, send_sem=send_sems.at[k],
                                                recv_sem=recv_sems.at[k], device_id=(px, py, pc), device_id_type=MESH)
            land = pltpu.make_async_remote_copy(src_ref=src, dst_ref=dst_here, send_sem=send_sems.at[k],
                                                recv_sem=recv_sems.at[k], device_id=(px, py, pc), device_id_type=MESH)
            return push, land

        copies = [remote(r, a) for r in range(n_rel) for a in range(n)]
        for push, _ in copies:
            push.start()
        for _, land in copies:
            land.wait_recv()
        for push, _ in copies:
            push.wait_send()
        for cp in local:
            cp.wait()

    return pl.pallas_call(
        body, name=name, in_specs=[ANY] * n, out_specs=[ANY] * n, out_shape=out_shape,
        scratch_shapes=[pltpu.SemaphoreType.DMA((n_rel * n,)), pltpu.SemaphoreType.DMA((n_rel * n,)),
                        pltpu.SemaphoreType.DMA((max(n, 1),))],
    )(*arrays)


def _gather_chips_two_level(name, arrays):
    n = len(arrays)
    rel = [((k >> 1) & 1, k & 1) for k in range(1, 4)]
    halves = [a.shape[0] // 2 for a in arrays]

    def body(*refs):
        ins, outs = refs[:n], refs[n:2 * n]
        ici_send, ici_recv, d2d_send, d2d_recv, local_sems = refs[2 * n:]
        x, y, c = _place()
        mine = 2 * x + y
        local = [pltpu.make_async_copy(ins[a], outs[a].at[mine], local_sems.at[a]) for a in range(n)]
        for cp in local:
            cp.start()

        def ici(r, a):
            px, py = _flip(x, rel[r][0]), _flip(y, rel[r][1])
            theirs = 2 * px + py
            rows = pl.ds(c * halves[a], halves[a])
            k = r * n + a
            push = pltpu.make_async_remote_copy(src_ref=ins[a].at[rows], dst_ref=outs[a].at[mine, rows],
                                                send_sem=ici_send.at[k], recv_sem=ici_recv.at[k],
                                                device_id=(px, py, c), device_id_type=MESH)
            land = pltpu.make_async_remote_copy(src_ref=ins[a].at[rows], dst_ref=outs[a].at[theirs, rows],
                                                send_sem=ici_send.at[k], recv_sem=ici_recv.at[k],
                                                device_id=(px, py, c), device_id_type=MESH)
            there = pl.ds((1 - c) * halves[a], halves[a])
            forward = pltpu.make_async_remote_copy(src_ref=outs[a].at[theirs, rows], dst_ref=outs[a].at[theirs, rows],
                                                   send_sem=d2d_send.at[k], recv_sem=d2d_recv.at[k],
                                                   device_id=(x, y, 1 - c), device_id_type=MESH)
            back = pltpu.make_async_remote_copy(src_ref=outs[a].at[theirs, rows], dst_ref=outs[a].at[theirs, there],
                                                send_sem=d2d_send.at[k], recv_sem=d2d_recv.at[k],
                                                device_id=(x, y, 1 - c), device_id_type=MESH)
            return push, land, forward, back

        copies = [ici(r, a) for r in range(len(rel)) for a in range(n)]
        for push, _, _, _ in copies:
            push.start()
        for _, land, forward, _ in copies:
            land.wait_recv()
            forward.start()
        for _, _, _, back in copies:
            back.wait_recv()
        for push, _, forward, _ in copies:
            push.wait_send()
            forward.wait_send()
        for cp in local:
            cp.wait()

    n_sem = len(rel) * n
    return pl.pallas_call(
        body, name=name, in_specs=[ANY] * n, out_specs=[ANY] * n,
        out_shape=[jax.ShapeDtypeStruct((4,) + a.shape, a.dtype) for a in arrays],
        scratch_shapes=[pltpu.SemaphoreType.DMA((n_sem,))] * 4 + [pltpu.SemaphoreType.DMA((n,))],
    )(*arrays)


def _sum_slots(name, stacked):
    p, rows, cols = stacked.shape
    tr = _tile(rows, 256)

    def body(s_ref, o_ref):
        acc = s_ref[0].astype(F32)
        for j in range(1, p):
            acc = acc + s_ref[j].astype(F32)
        o_ref[...] = acc

    return pl.pallas_call(
        body, name=name, grid=(rows // tr,),
        in_specs=[pl.BlockSpec((p, tr, cols), lambda i: (0, i, 0))],
        out_specs=pl.BlockSpec((tr, cols), lambda i: (i, 0)),
        out_shape=jax.ShapeDtypeStruct((rows, cols), F32),
        compiler_params=pltpu.CompilerParams(dimension_semantics=("parallel",)),
    )(stacked)


def _add_pair(name, a, b):
    rows, cols = a.shape
    tr = _tile(rows, 512, 16)

    def body(a_ref, b_ref, o_ref):
        o_ref[...] = (a_ref[...].astype(F32) + b_ref[...].astype(F32)).astype(o_ref.dtype)

    spec = pl.BlockSpec((tr, cols), lambda i: (i, 0))
    return pl.pallas_call(
        body, name=name, grid=(rows // tr,), in_specs=[spec, spec], out_specs=spec,
        out_shape=jax.ShapeDtypeStruct((rows, cols), a.dtype),
        compiler_params=pltpu.CompilerParams(dimension_semantics=("parallel",)),
    )(a, b)


def _adamw(name, parts, w, m, v):
    rows, cols = w.shape
    tr = _tile(rows, 256)
    n_parts = len(parts)

    def body(*refs):
        part_refs = refs[:n_parts]
        w_ref, m_ref, v_ref, g_out, d_out, m_out, v_out = refs[n_parts:]
        g = None
        for pr in part_refs:
            if len(pr.shape) == 3:
                for j in range(pr.shape[0]):
                    g = pr[j] if g is None else g + pr[j]
            else:
                g = pr[...] if g is None else g + pr[...]
        m_new = ADAM_B1 * m_ref[...] + (1.0 - ADAM_B1) * g
        v_new = ADAM_B2 * v_ref[...] + (1.0 - ADAM_B2) * jnp.square(g)
        m_hat = m_new / (1.0 - ADAM_B1 ** ADAM_STEP)
        v_hat = v_new / (1.0 - ADAM_B2 ** ADAM_STEP)
        g_out[...] = g
        d_out[...] = -ADAM_LR * (m_hat / (jnp.sqrt(v_hat) + ADAM_EPS) + ADAM_WD * w_ref[...])
        m_out[...] = m_new
        v_out[...] = v_new

    spec2 = pl.BlockSpec((tr, cols), lambda i: (i, 0))
    in_specs = [pl.BlockSpec((p.shape[0], tr, cols), lambda i: (0, i, 0)) if p.ndim == 3 else spec2 for p in parts]
    blocks = sum(_nbytes((p.shape[0] if p.ndim == 3 else 1, tr, cols), F32) for p in parts) + 7 * _nbytes((tr, cols), F32)
    return pl.pallas_call(
        body, name=name, grid=(rows // tr,),
        in_specs=in_specs + [spec2] * 3, out_specs=[spec2] * 4,
        out_shape=[jax.ShapeDtypeStruct((rows, cols), F32)] * 4,
        compiler_params=pltpu.CompilerParams(dimension_semantics=("parallel",),
                                             vmem_limit_bytes=_vmem_limit(blocks, 4 * _nbytes((tr, cols), F32))),
    )(*parts, w, m, v)


def _mod_fwd(c_all, w_ada2d, depth):
    nb, d = c_all.shape
    cols = w_ada2d.shape[1]
    tn = _tile(cols, 512, 128)

    def body(c_ref, w_ref, o_ref):
        cv = c_ref[...]
        o_ref[...] = _dot(cv * _sigmoid(cv), w_ref[...])

    return pl.pallas_call(
        body, name="mod_fwd", grid=(depth, cols // tn),
        in_specs=[pl.BlockSpec((nb, d), lambda l, j: (0, 0)), pl.BlockSpec((d, tn), lambda l, j: (l, j))],
        out_specs=pl.BlockSpec((nb, tn), lambda l, j: (l, j)),
        out_shape=jax.ShapeDtypeStruct((depth * nb, cols), F32),
        compiler_params=pltpu.CompilerParams(dimension_semantics=("parallel", "parallel")),
    )(c_all, w_ada2d)


def _wada_bwd(c_all, dmod2d, depth):
    nb, d = c_all.shape
    cols = dmod2d.shape[1]
    tn = _tile(cols, 512, 128)

    def body(c_ref, g_ref, o_ref):
        cv = c_ref[...]
        o_ref[...] = _dot_tn(cv * _sigmoid(cv), g_ref[...])

    return pl.pallas_call(
        body, name="wada_bwd", grid=(depth, cols // tn),
        in_specs=[pl.BlockSpec((nb, d), lambda l, j: (0, 0)), pl.BlockSpec((nb, tn), lambda l, j: (l, j))],
        out_specs=pl.BlockSpec((d, tn), lambda l, j: (l, j)),
        out_shape=jax.ShapeDtypeStruct((depth * d, cols), F32),
        compiler_params=pltpu.CompilerParams(dimension_semantics=("parallel", "parallel")),
    )(c_all, dmod2d)


def _rope_tables(pos_col, inv_freq_lane, nope):
    n_rows = pos_col.shape[0]
    tm = _tile(n_rows, 512)
    half = QK_ROPE // 2

    def body(p_ref, f_ref, c_ref, s1_ref, s2_ref):
        ang = p_ref[...] * f_ref[...]
        lane = lax.broadcasted_iota(jnp.int32, ang.shape, 1)
        first = (lane >= nope) & (lane < nope + half)
        second = (lane >= nope + half) & (lane < nope + 2 * half)
        cos, sin = jnp.cos(ang), jnp.sin(ang)
        c_ref[...] = jnp.where(first | second, cos, 1.0)
        s1_ref[...] = jnp.where(first, -sin, 0.0)
        s2_ref[...] = jnp.where(second, sin, 0.0)

    return _rowcall("rope_tables", body, n_rows, tm, [pos_col], [inv_freq_lane], [(HEAD_PAD, F32)] * 3)


def _rope(q, c, s1, s2):
    w = q.shape[1]
    return q * c + pltpu.roll(q, w - QK_ROPE // 2, axis=1) * s1 + pltpu.roll(q, QK_ROPE // 2, axis=1) * s2


def _rope_adjoint(dr, c, s1, s2):
    w = dr.shape[1]
    return dr * c + pltpu.roll(dr * s1, QK_ROPE // 2, axis=1) + pltpu.roll(dr * s2, w - QK_ROPE // 2, axis=1)


def _ssm_disc(ar, ai, log_dt, br, bi):
    dt = jnp.exp(log_dt)
    mag = jnp.exp(ar * dt)
    abr = mag * jnp.cos(ai * dt)
    abi = mag * jnp.sin(ai * dt)
    den = ar * ar + ai * ai
    nr = abr - 1.0
    ni = abi
    cr = (nr * ar + ni * ai) / den
    ci = (ni * ar - nr * ai) / den
    return abr, abi, cr * br - ci * bi, cr * bi + ci * br


def _ssm_disc_fwd(ar, ai, log_dt, br, bi):
    n_rows, m = br.shape
    tm = _tile(n_rows, 1024)

    def body(ar_ref, ai_ref, dt_ref, br_ref, bi_ref, o1, o2, o3, o4):
        o1[...], o2[...], o3[...], o4[...] = _ssm_disc(ar_ref[...], ai_ref[...], dt_ref[...], br_ref[...], bi_ref[...])

    return _rowcall("ssm_disc_fwd", body, n_rows, tm, [ar, ai, log_dt, br, bi], [],
                    [(1, F32), (1, F32), (m, F32), (m, F32)])


def _ssm_disc_bwd(ar, ai, log_dt, br, bi, g_abr, g_abi, g_bbr, g_bbi):
    n_rows, m = br.shape
    tm = _tile(n_rows, 1024)

    def body(ar_ref, ai_ref, dt_ref, br_ref, bi_ref, g1, g2, g3, g4, o1, o2, o3, o4, o5):
        _, vjp = jax.vjp(_ssm_disc, ar_ref[...], ai_ref[...], dt_ref[...], br_ref[...], bi_ref[...])
        o1[...], o2[...], o3[...], o4[...], o5[...] = vjp((g1[...], g2[...], g3[...], g4[...]))

    return _rowcall("ssm_disc_bwd", body, n_rows, tm, [ar, ai, log_dt, br, bi, g_abr, g_abi, g_bbr, g_bbi], [],
                    [(1, F32), (1, F32), (1, F32), (m, F32), (m, F32)])


def _lane_sum(v2d):
    def body(v_ref, o_ref):
        o_ref[...] = jnp.sum(v_ref[...], axis=1, keepdims=True)
    return pl.pallas_call(body, name="lane_sum", out_shape=jax.ShapeDtypeStruct((v2d.shape[0], 1), F32))(v2d)


def _in_fwd(x, n1g, sc1, sh1, w_parts, tm):
    n_rows = x.shape[0]
    widths = [w.shape[1] for w in w_parts]

    def body(x_ref, g_ref, sc_ref, sh_ref, *rest):
        w_refs, (hb_ref, *z_refs) = rest[:len(w_parts)], rest[len(w_parts):]
        xh, _ = _rms_stats(x_ref[...])
        h = (xh * g_ref[...]) * (1.0 + sc_ref[...]) + sh_ref[...]
        hb = h.astype(_MM)
        hb_ref[...] = hb.astype(_ACT)
        for w_ref, z_ref in zip(w_refs, z_refs):
            z_ref[...] = _dot(hb, w_ref[...])

    return _rowcall("in_fwd", body, n_rows, tm, [x], [n1g, sc1, sh1, *w_parts],
                    [(x.shape[1], _ACT)] + [(w, F32) for w in widths], temp_cols=4 * x.shape[1])


def _ssm_fwd(u, bre_blk, bim_blk, abr_row, abi_row, cre_blk, cimneg_blk, d_row):
    n_rows, sw = u.shape
    gp = abr_row.shape[1]
    t = _tile(n_rows, 256)

    def body(u_ref, bre_ref, bim_ref, ar_ref, ai_ref, cre_ref, cim_ref, d_ref, y_ref, hre_ref, him_ref, cr, ci):
        @pl.when(_first_step())
        def _():
            cr[...] = jnp.zeros_like(cr)
            ci[...] = jnp.zeros_like(ci)

        uv = u_ref[...]
        ub = uv.astype(_MM)
        hre_ref[...] = _dot(ub, bre_ref[...])
        him_ref[...] = _dot(ub, bim_ref[...])
        a_r, a_i = ar_ref[...], ai_ref[...]

        def step(k, carry):
            pr, pi = carry
            row = pl.ds(k, 1)
            hr = a_r * pr - a_i * pi + hre_ref[row, :]
            hi = a_r * pi + a_i * pr + him_ref[row, :]
            hre_ref[row, :] = hr
            him_ref[row, :] = hi
            return hr, hi

        pr, pi = lax.fori_loop(0, t, step, (cr[0:1, :], ci[0:1, :]), unroll=8)
        cr[0:1, :] = pr
        ci[0:1, :] = pi
        y_ref[...] = _dot(hre_ref[...], cre_ref[...]) + _dot(him_ref[...], cim_ref[...]) + d_ref[...] * uv

    row = lambda c: pl.BlockSpec((t, c), lambda i: (i, 0))
    full = lambda a: pl.BlockSpec(a.shape, lambda i: (0, 0), pipeline_mode=pl.Buffered(1))
    blocks = _nbytes((t, sw), F32) * 2 + 2 * _nbytes((t, gp), F32)
    resident = 4 * _nbytes((sw, gp), _MM)
    return pl.pallas_call(
        body, name="ssm_fwd", grid=(n_rows // t,),
        in_specs=[row(sw), full(bre_blk), full(bim_blk), full(abr_row), full(abi_row), full(cre_blk),
                  full(cimneg_blk), full(d_row)],
        out_specs=[row(sw), row(gp), row(gp)],
        out_shape=[jax.ShapeDtypeStruct((n_rows, sw), F32), jax.ShapeDtypeStruct((n_rows, gp), F32),
                   jax.ShapeDtypeStruct((n_rows, gp), F32)],
        scratch_shapes=[pltpu.VMEM((8, gp), F32), pltpu.VMEM((8, gp), F32)],
        compiler_params=pltpu.CompilerParams(dimension_semantics=("arbitrary",),
                                             vmem_limit_bytes=_vmem_limit(blocks, resident + 3 * _nbytes((t, gp), F32))),
    )(u, bre_blk, bim_blk, abr_row, abi_row, cre_blk, cimneg_blk, d_row)


def _mla_prep_fwd(cq, ckv, kr, rc, rs1, rs2, gq, gkv, wuq, wuk, wuv, vdim, tm):
    n_rows = cq.shape[0]

    def body(cq_ref, ckv_ref, kr_ref, c_ref, s1_ref, s2_ref, gq_ref, gkv_ref, wuq_ref, wuk_ref, wuv_ref,
             q_ref, k_ref, v_ref, cqn_ref, ckvn_ref):
        c, s1, s2 = c_ref[...], s1_ref[...], s2_ref[...]
        c8, s18, s28 = (jnp.tile(a, (1, N_HEADS)) for a in (c, s1, s2))
        xh, _ = _rms_stats(cq_ref[...])
        cqn = (xh * gq_ref[...]).astype(_MM)
        cqn_ref[...] = cqn.astype(_ACT)
        q_ref[...] = _rope(_dot(cqn, wuq_ref[...]), c8, s18, s28).astype(_ACT)
        xh, _ = _rms_stats(ckv_ref[...])
        ckvn = (xh * gkv_ref[...]).astype(_MM)
        ckvn_ref[...] = ckvn.astype(_ACT)
        kpe = _rope(kr_ref[...], c, s1, s2)
        k_ref[...] = (_dot(ckvn, wuk_ref[...]) + jnp.tile(kpe, (1, N_HEADS))).astype(_ACT)
        v = _dot(ckvn, wuv_ref[...])
        lane = lax.broadcasted_iota(jnp.int32, v.shape, 1)
        v_ref[...] = jnp.where((lane & (HEAD_PAD - 1)) == vdim, 1.0, v).astype(_ACT)

    return _rowcall("mla_prep_fwd", body, n_rows, tm, [cq, ckv, kr, rc, rs1, rs2], [gq, gkv, wuq, wuk, wuv],
                    [(QW, _ACT), (QW, _ACT), (QW, _ACT), (cq.shape[1], _ACT), (ckv.shape[1], _ACT)], temp_cols=6 * QW)


def _causal_steps(n_blocks, key_major):
    if key_major:
        pairs = [(qi, ki) for ki in range(n_blocks) for qi in range(ki, n_blocks)]
    else:
        pairs = [(qi, ki) for qi in range(n_blocks) for ki in range(qi + 1)]
    return (jnp.asarray(np.array([p[0] for p in pairs], np.int32)), jnp.asarray(np.array([p[1] for p in pairs], np.int32)))


def _attn_fwd(q, k, v, pos_col, pos_row, scale, vdim):
    n_rows = q.shape[0]
    ta = _tile(n_rows, 512, 128)
    nb = n_rows // ta
    hb = ATTN_HEADS_PER_STEP
    wide = hb * HEAD_PAD
    qmap, kmap = _causal_steps(nb, key_major=False)
    c2 = scale * LOG2_E

    def body(qm, km, q_ref, k_ref, v_ref, pq_ref, pk_ref, o_ref, lse_ref, m_sc, acc_sc):
        s_id = pl.program_id(1)
        qi, ki = qm[s_id], km[s_id]

        @pl.when(ki == 0)
        def _():
            m_sc[...] = jnp.full_like(m_sc, NEG_INF)
            acc_sc[...] = jnp.zeros_like(acc_sc)

        def update(on_diagonal):
            if on_diagonal:
                visible = pk_ref[...] <= pq_ref[...]
            for h in range(hb):
                lanes = slice(h * HEAD_PAD, (h + 1) * HEAD_PAD)
                s = _dot_nt(q_ref[:, lanes], k_ref[:, lanes])
                if on_diagonal:
                    s = jnp.where(visible, s, NEG_INF)
                m_prev = m_sc[:, lanes]
                m_new = jnp.maximum(m_prev, jnp.max(s, axis=1, keepdims=True))
                alpha = jnp.exp2((m_prev - m_new) * c2)
                p = jnp.exp2((s - m_new[:, :1]) * c2)
                acc_new = alpha * acc_sc[:, lanes] + _dot(p, v_ref[:, lanes])
                if on_diagonal:
                    l_new = acc_new[:, vdim:vdim + 1]
                    o_ref[:, lanes] = acc_new / l_new
                    lse_ref[:, lanes] = m_new * c2 + jnp.log2(l_new)
                else:
                    acc_sc[:, lanes] = acc_new
                    m_sc[:, lanes] = m_new

        pl.when(ki != qi)(functools.partial(update, False))
        pl.when(ki == qi)(functools.partial(update, True))

    qspec = pl.BlockSpec((ta, wide), lambda h, s, qm, km: (qm[s], h))
    kspec = pl.BlockSpec((ta, wide), lambda h, s, qm, km: (km[s], h))
    grid_spec = pltpu.PrefetchScalarGridSpec(
        num_scalar_prefetch=2, grid=(N_HEADS // hb, int(qmap.shape[0])),
        in_specs=[qspec, kspec, kspec,
                  pl.BlockSpec((ta, 1), lambda h, s, qm, km: (qm[s], 0)),
                  pl.BlockSpec((1, ta), lambda h, s, qm, km: (0, km[s]))],
        out_specs=[qspec, qspec],
        scratch_shapes=[pltpu.VMEM((ta, wide), F32)] * 2)
    return pl.pallas_call(
        body, name="attn_fwd", grid_spec=grid_spec,
        out_shape=[jax.ShapeDtypeStruct((n_rows, QW), F32), jax.ShapeDtypeStruct((n_rows, QW), F32)],
        compiler_params=pltpu.CompilerParams(
            dimension_semantics=("parallel", "arbitrary"),
            vmem_limit_bytes=_vmem_limit(8 * _nbytes((ta, wide), F32), 6 * hb * _nbytes((ta, ta), F32))),
    )(qmap, kmap, q, k, v, pos_col, pos_row)


def _mix_fwd(ypre, o, ga, gb, x, g1, bglu, wglu, wa, wb, wout, tm):
    n_rows, d = x.shape
    sw = ypre.shape[1]

    def body(y_ref, o_ref, ga_ref, gb_ref, x_ref, g1_ref, bglu_ref, wglu_ref, wa_ref, wb_ref, wout_ref,
             yg_ref, ya_ref, yb_ref, mg_ref, mo_ref, xo_ref):
        ys = _gelu(y_ref[...])
        yg = ys * _sigmoid(_dot(ys, wglu_ref[...]) + bglu_ref[...])
        yg_ref[...] = yg.astype(_ACT)
        ya = _dot(yg, wa_ref[...])
        yb = _dot(o_ref[...], wb_ref[...])
        ya_ref[...] = ya
        yb_ref[...] = yb
        merged = _sigmoid(ga_ref[...]) * ya + _sigmoid(gb_ref[...]) * yb
        mg_ref[...] = merged.astype(_ACT)
        mo = _dot(merged, wout_ref[...])
        mo_ref[...] = mo
        xo_ref[...] = x_ref[...] + g1_ref[...] * mo

    return _rowcall("mix_fwd", body, n_rows, tm, [ypre, o, ga, gb, x], [g1, bglu, wglu, wa, wb, wout],
                    [(sw, _ACT), (d, F32), (d, F32), (d, _ACT), (d, F32), (d, F32)], temp_cols=4 * d)


def _ffn_fwd(x, n2g, sc2, sh2, g2, wg, wu, wd, tm):
    n_rows, d = x.shape
    ff = wg.shape[1]

    def body(x_ref, g_ref, sc_ref, sh_ref, g2_ref, wg_ref, wu_ref, wd_ref, hb_ref, a_ref, b_ref, d_ref, xo_ref):
        xv = x_ref[...]
        xh, _ = _rms_stats(xv)
        hb = ((xh * g_ref[...]) * (1.0 + sc_ref[...]) + sh_ref[...]).astype(_MM)
        hb_ref[...] = hb.astype(_ACT)
        a = _dot(hb, wg_ref[...])
        b = _dot(hb, wu_ref[...])
        a_ref[...] = a
        b_ref[...] = b
        dn = _dot((a * _sigmoid(a)) * b, wd_ref[...])
        d_ref[...] = dn
        xo_ref[...] = xv + g2_ref[...] * dn

    return _rowcall("ffn_fwd", body, n_rows, tm, [x], [n2g, sc2, sh2, g2, wg, wu, wd],
                    [(d, _ACT), (ff, F32), (ff, F32), (d, F32), (d, F32)], temp_cols=3 * ff)


def _head(x, fg, target, tm):
    n_rows, d = x.shape

    def body(x_ref, t_ref, g_ref, dx_ref, loss_ref, dg_ref):
        xh, rstd = _rms_stats(x_ref[...])
        err = xh * g_ref[...] - t_ref[...]
        part = jnp.sum(jnp.mean(err * err, axis=-1, keepdims=True), axis=0, keepdims=True) * 0.5
        _acc(loss_ref, jnp.broadcast_to(part, loss_ref.shape))
        dy = err * (1.0 / d)
        _acc(dg_ref, _rows_sum(dy * xh))
        dx_ref[...] = _rms_bwd(dy * g_ref[...], xh, rstd)

    return _rowcall("head", body, n_rows, tm, [x, target], [fg], [(d, F32)], [((1, 128), F32), ((1, d), F32)],
                    temp_cols=4 * d)


def _ffn_bwd(dxo, xmid, a, b, dn, n2g, sc2, g2, wg, wu, wd, tm):
    n_rows, d = dxo.shape
    ff = a.shape[1]

    def act_body(dxo_ref, a_ref, b_ref, dn_ref, g2_ref, wd_ref, da_ref, db_ref, f_ref, dd_ref, dg2_ref):
        dxo_v = dxo_ref[...]
        dd = dxo_v * g2_ref[...]
        dd_ref[...] = dd.astype(_ACT)
        _acc(dg2_ref, _rows_sum(dxo_v * dn_ref[...]))
        df = _dot_nt(dd, wd_ref[...])
        av, bv = a_ref[...], b_ref[...]
        sa = _sigmoid(av)
        si = av * sa
        f_ref[...] = (si * bv).astype(_ACT)
        da_ref[...] = (df * bv * (sa * (1.0 + av * (1.0 - sa)))).astype(_ACT)
        db_ref[...] = (df * si).astype(_ACT)

    da, db, f, dd, dg2 = _rowcall("ffn_bwd_act", act_body, n_rows, 2 * tm, [dxo, a, b, dn], [g2, wd],
                                  [(ff, _ACT), (ff, _ACT), (ff, _ACT), (d, _ACT)], [((1, d), F32)], temp_cols=4 * ff)

    def in_body(dxo_ref, x_ref, da_ref, db_ref, g_ref, sc_ref, wg_ref, wu_ref, dx_ref, dsh_ref, dsc_ref, dn2_ref):
        dh = _dot_nt(da_ref[...], wg_ref[...]) + _dot_nt(db_ref[...], wu_ref[...])
        xh, rstd = _rms_stats(x_ref[...])
        yg = xh * g_ref[...]
        _acc(dsh_ref, _rows_sum(dh))
        _acc(dsc_ref, _rows_sum(dh * yg))
        dy = dh * (1.0 + sc_ref[...])
        _acc(dn2_ref, _rows_sum(dy * xh))
        dx_ref[...] = dxo_ref[...] + _rms_bwd(dy * g_ref[...], xh, rstd)

    dx, dsh, dsc, dn2 = _rowcall("ffn_bwd_in", in_body, n_rows, 2 * tm, [dxo, xmid, da, db], [n2g, sc2, wg, wu],
                                 [(d, F32)], [((1, d), F32)] * 3, temp_cols=5 * d)
    return dx, da, db, f, dd, dg2, dsh, dsc, dn2


def _mix_bwd(dxm, mo, ya, yb, ga, gb, ypre, o, g1, bglu, wglu, wa, wb, wout, tm):
    n_rows, d = dxm.shape
    sw = ypre.shape[1]

    def body(dxm_ref, mo_ref, ya_ref, yb_ref, ga_ref, gb_ref, y_ref, o_ref, g1_ref, bglu_ref, wglu_ref, wa_ref, wb_ref,
             wout_ref, dmo_ref, dya_ref, dyb_ref, dt_ref, ys_ref, dga_ref, dgb_ref, dy_ref, do_ref, delta_ref,
             dg1_ref, dbg_ref):
        dxm_v = dxm_ref[...]
        dmo = dxm_v * g1_ref[...]
        dmo_ref[...] = dmo.astype(_ACT)
        _acc(dg1_ref, _rows_sum(dxm_v * mo_ref[...]))
        dmg = _dot_nt(dmo, wout_ref[...])
        sa, sb = _sigmoid(ga_ref[...]), _sigmoid(gb_ref[...])
        dya, dyb = dmg * sa, dmg * sb
        dya_ref[...] = dya.astype(_ACT)
        dyb_ref[...] = dyb.astype(_ACT)
        dga_ref[...] = (dmg * ya_ref[...] * (sa * (1.0 - sa))).astype(_ACT)
        dgb_ref[...] = (dmg * yb_ref[...] * (sb * (1.0 - sb))).astype(_ACT)
        do = _dot_nt(dyb, wb_ref[...])
        do_ref[...] = do
        prod = do * o_ref[...]
        for h in range(N_HEADS):
            lanes = slice(h * HEAD_PAD, (h + 1) * HEAD_PAD)
            delta_ref[:, lanes] = jnp.broadcast_to(jnp.sum(prod[:, lanes], axis=1, keepdims=True), (prod.shape[0], HEAD_PAD))
        dyg = _dot_nt(dya, wa_ref[...])
        yv = y_ref[...]
        ys = _gelu(yv)
        ys_ref[...] = ys.astype(_ACT)
        sg = _sigmoid(_dot(ys, wglu_ref[...]) + bglu_ref[...])
        dt = dyg * ys * (sg * (1.0 - sg))
        dt_ref[...] = dt.astype(_ACT)
        _acc(dbg_ref, _rows_sum(dt))
        dys = dyg * sg + _dot_nt(dt, wglu_ref[...])
        dy_ref[...] = dys * _gelu_grad(yv)

    return _rowcall("mix_bwd", body, n_rows, tm, [dxm, mo, ya, yb, ga, gb, ypre, o], [g1, bglu, wglu, wa, wb, wout],
                    [(d, _ACT), (d, _ACT), (d, _ACT), (sw, _ACT), (sw, _ACT), (d, _ACT), (d, _ACT), (sw, F32), (QW, F32),
                     (QW, F32)],
                    [((1, d), F32), ((1, sw), F32)], temp_cols=6 * d)


def _attn_bwd(q, k, v, do, delta, lse, pos_col, pos_row, scale):
    n_rows = q.shape[0]
    ta = _tile(n_rows, 512, 128)
    nb = n_rows // ta
    qmap, kmap = _causal_steps(nb, key_major=True)
    hb = ATTN_BWD_HEADS_PER_STEP
    wide = hb * HEAD_PAD
    c2 = scale * LOG2_E

    def body(qm, km, q_ref, k_ref, v_ref, do_ref, delta_ref, lse_ref, pq_ref, pk_ref, dq_ref, dk_ref, dv_ref,
             dk_acc, dv_acc):
        s_id = pl.program_id(1)
        qi, ki = qm[s_id], km[s_id]

        @pl.when(s_id == 0)
        def _():
            dq_ref[...] = jnp.zeros_like(dq_ref)

        @pl.when(qi == ki)
        def _():
            dk_acc[...] = jnp.zeros_like(dk_acc)
            dv_acc[...] = jnp.zeros_like(dv_acc)

        rows = pl.ds(pl.multiple_of(qi * ta, ta), ta)

        def update(on_diagonal):
            if on_diagonal:
                visible = pk_ref[...] <= pq_ref[...]
            for h in range(hb):
                lanes = slice(h * HEAD_PAD, (h + 1) * HEAD_PAD)
                qv, kv, dov = q_ref[:, lanes], k_ref[:, lanes], do_ref[:, lanes]
                e = _dot_nt(qv, kv) * c2 - lse_ref[:, lanes][:, :1]
                if on_diagonal:
                    e = jnp.where(visible, e, NEG_INF)
                p = jnp.exp2(e)
                dp = _dot_nt(dov, v_ref[:, lanes])
                ds = p * (dp - delta_ref[:, lanes][:, :1])
                dv_acc[:, lanes] += _dot_tn(p, dov)
                dk_acc[:, lanes] += _dot_tn(ds, qv)
                dq_ref[rows, lanes] += _dot(ds, kv) * scale

        pl.when(ki != qi)(functools.partial(update, False))
        pl.when(ki == qi)(functools.partial(update, True))

        @pl.when(qi == nb - 1)
        def _():
            dk_ref[...] = dk_acc[...] * scale
            dv_ref[...] = dv_acc[...]

    qspec = pl.BlockSpec((ta, wide), lambda h, s, qm, km: (qm[s], h))
    kspec = pl.BlockSpec((ta, wide), lambda h, s, qm, km: (km[s], h))
    grid_spec = pltpu.PrefetchScalarGridSpec(
        num_scalar_prefetch=2, grid=(N_HEADS // hb, int(qmap.shape[0])),
        in_specs=[qspec, kspec, kspec, qspec, qspec, qspec,
                  pl.BlockSpec((ta, 1), lambda h, s, qm, km: (qm[s], 0)),
                  pl.BlockSpec((1, ta), lambda h, s, qm, km: (0, km[s]))],
        out_specs=[pl.BlockSpec((n_rows, wide), lambda h, s, qm, km: (0, h)), kspec, kspec],
        scratch_shapes=[pltpu.VMEM((ta, wide), F32), pltpu.VMEM((ta, wide), F32)])
    return pl.pallas_call(
        body, name="attn_bwd", grid_spec=grid_spec,
        out_shape=[jax.ShapeDtypeStruct((n_rows, QW), F32)] * 3,
        compiler_params=pltpu.CompilerParams(
            dimension_semantics=("parallel", "arbitrary"),
            vmem_limit_bytes=_vmem_limit(12 * _nbytes((ta, wide), F32) + _nbytes((n_rows, wide), F32),
                                         6 * hb * _nbytes((ta, ta), F32))),
    )(qmap, kmap, q, k, v, do, delta, lse, pos_col, pos_row)


def _mla_prep_bwd(dq, dk, dv, cq, ckv, rc, rs1, rs2, gq, gkv, wuq, wuk, wuv, nope, tm):
    n_rows = cq.shape[0]
    ql, kl = cq.shape[1], ckv.shape[1]

    def body(dq_ref, dk_ref, dv_ref, cq_ref, ckv_ref, c_ref, s1_ref, s2_ref, gq_ref, gkv_ref, wuq_ref, wuk_ref,
             wuv_ref, dqp_ref, dcq_ref, dckv_ref, dkr_ref, dgq_ref, dgkv_ref):
        c, s1, s2 = c_ref[...], s1_ref[...], s2_ref[...]
        c8, s18, s28 = (jnp.tile(a, (1, N_HEADS)) for a in (c, s1, s2))
        dqp = _rope_adjoint(dq_ref[...], c8, s18, s28)
        dqp_ref[...] = dqp.astype(_ACT)
        dcqn = _dot_nt(dqp, wuq_ref[...])
        xh, rstd = _rms_stats(cq_ref[...])
        _acc(dgq_ref, _rows_sum(dcqn * xh))
        dcq_ref[...] = _rms_bwd(dcqn * gq_ref[...], xh, rstd).astype(_ACT)
        dkv = dk_ref[...]
        dkpe = dkv[:, 0:HEAD_PAD]
        for h in range(1, N_HEADS):
            dkpe = dkpe + dkv[:, h * HEAD_PAD:(h + 1) * HEAD_PAD]
        lane = lax.broadcasted_iota(jnp.int32, dkpe.shape, 1)
        dkpe = jnp.where((lane >= nope) & (lane < nope + QK_ROPE), dkpe, 0.0)
        dkr_ref[...] = _rope_adjoint(dkpe, c, s1, s2).astype(_ACT)
        dckvn = _dot_nt(dkv, wuk_ref[...]) + _dot_nt(dv_ref[...], wuv_ref[...])
        xh, rstd = _rms_stats(ckv_ref[...])
        _acc(dgkv_ref, _rows_sum(dckvn * xh))
        dckv_ref[...] = _rms_bwd(dckvn * gkv_ref[...], xh, rstd).astype(_ACT)

    return _rowcall("mla_prep_bwd", body, n_rows, tm, [dq, dk, dv, cq, ckv, rc, rs1, rs2], [gq, gkv, wuq, wuk, wuv],
                    [(QW, _ACT), (ql, _ACT), (kl, _ACT), (HEAD_PAD, _ACT)], [((1, ql), F32), ((1, kl), F32)],
                    temp_cols=6 * QW)


def _ssm_bwd(dy, u, hre, him, bre_blk, bim_blk, abr_row, abi_row, cre_blk, cimneg_blk, d_row):
    n_rows, sw = u.shape
    gp = abr_row.shape[1]
    t = _tile(n_rows, 256)
    n_chunks = n_rows // t

    def body(dy_ref, u_ref, hre_ref, him_ref, hbre_ref, hbim_ref, bre_ref, bim_ref, ar_ref, ai_ref, cre_ref, cim_ref,
             d_ref, du_ref, gre_ref, gim_ref, dar_ref, dai_ref, dd_ref, g_re, g_im, hs_re, hs_im, cr, ci):
        i = pl.program_id(0)

        @pl.when(i == 0)
        def _():
            cr[...] = jnp.zeros_like(cr)
            ci[...] = jnp.zeros_like(ci)

        dyv = dy_ref[...]
        dyb = dyv.astype(_MM)
        g_re[...] = _dot_nt(dyb, cre_ref[...])
        g_im[...] = _dot_nt(dyb, cim_ref[...])
        a_r, a_i = ar_ref[...], ai_ref[...]

        def step(k, carry):
            nr, ni = carry
            row = pl.ds(t - 1 - k, 1)
            gr = g_re[row, :] + a_r * nr + a_i * ni
            gi = g_im[row, :] + a_r * ni - a_i * nr
            g_re[row, :] = gr
            g_im[row, :] = gi
            return gr, gi

        nr, ni = lax.fori_loop(0, t, step, (cr[0:1, :], ci[0:1, :]), unroll=8)
        cr[0:1, :] = nr
        ci[0:1, :] = ni
        gr_all, gi_all = g_re[...], g_im[...]
        gre_ref[...] = gr_all.astype(_ACT)
        gim_ref[...] = gi_all.astype(_ACT)
        du_ref[...] = (_dot_nt(gr_all, bre_ref[...]) + _dot_nt(gi_all, bim_ref[...]) + d_ref[...] * dyv).astype(_ACT)
        _acc(dd_ref, _rows_sum(dyv * u_ref[...]))
        is_first_chunk = i == n_chunks - 1
        hs_re[0:8, :] = jnp.where(is_first_chunk, 0.0, hbre_ref[...])
        hs_im[0:8, :] = jnp.where(is_first_chunk, 0.0, hbim_ref[...])
        hs_re[8:t + 8, :] = hre_ref[...]
        hs_im[8:t + 8, :] = him_ref[...]
        hp_re, hp_im = hs_re[pl.ds(7, t), :], hs_im[pl.ds(7, t), :]
        _acc(dar_ref, _rows_sum(gr_all * hp_re + gi_all * hp_im))
        _acc(dai_ref, _rows_sum(gi_all * hp_re - gr_all * hp_im))

    rev = lambda c: pl.BlockSpec((t, c), lambda i: (n_chunks - 1 - i, 0))
    before = pl.BlockSpec((8, gp), lambda i: (jnp.maximum((n_chunks - 1 - i) * (t // 8) - 1, 0), 0))
    full = lambda a: pl.BlockSpec(a.shape, lambda i: (0, 0), pipeline_mode=pl.Buffered(1))
    acc = lambda c: pl.BlockSpec((1, c), lambda i: (0, 0))
    blocks = 2 * _nbytes((t, sw), F32) + 2 * _nbytes((t, gp), F32) + _nbytes((t, sw), _ACT) + 2 * _nbytes((t, gp), _ACT)
    resident = 4 * _nbytes((sw, gp), _MM) + 4 * _nbytes((t + 8, gp), F32)
    return pl.pallas_call(
        body, name="ssm_bwd", grid=(n_chunks,),
        in_specs=[rev(sw), rev(sw), rev(gp), rev(gp), before, before, full(bre_blk), full(bim_blk), full(abr_row),
                  full(abi_row), full(cre_blk), full(cimneg_blk), full(d_row)],
        out_specs=[rev(sw), rev(gp), rev(gp), acc(gp), acc(gp), acc(sw)],
        out_shape=[jax.ShapeDtypeStruct((n_rows, sw), _ACT), jax.ShapeDtypeStruct((n_rows, gp), _ACT),
                   jax.ShapeDtypeStruct((n_rows, gp), _ACT), jax.ShapeDtypeStruct((1, gp), F32),
                   jax.ShapeDtypeStruct((1, gp), F32), jax.ShapeDtypeStruct((1, sw), F32)],
        scratch_shapes=[pltpu.VMEM((t, gp), F32), pltpu.VMEM((t, gp), F32), pltpu.VMEM((t + 8, gp), F32),
                        pltpu.VMEM((t + 8, gp), F32), pltpu.VMEM((8, gp), F32), pltpu.VMEM((8, gp), F32)],
        compiler_params=pltpu.CompilerParams(dimension_semantics=("arbitrary",),
                                             vmem_limit_bytes=_vmem_limit(blocks, resident + 4 * _nbytes((t, gp), F32))),
    )(dy, u, hre, him, hre, him, bre_blk, bim_blk, abr_row, abi_row, cre_blk, cimneg_blk, d_row)


def _in_bwd(dxm, x, dz_parts, n1g, sc1, w_parts, tm):
    n_rows, d = x.shape
    n = len(dz_parts)

    def body(dxm_ref, x_ref, *rest):
        dz_refs = rest[:n]
        g_ref, sc_ref = rest[n], rest[n + 1]
        w_refs = rest[n + 2:2 * n + 2]
        dx_ref, dsh_ref, dsc_ref, dn1_ref = rest[2 * n + 2:]
        dh = None
        for dz_ref, w_ref in zip(dz_refs, w_refs):
            term = _dot_nt(dz_ref[...], w_ref[...])
            dh = term if dh is None else dh + term
        xh, rstd = _rms_stats(x_ref[...])
        yg = xh * g_ref[...]
        _acc(dsh_ref, _rows_sum(dh))
        _acc(dsc_ref, _rows_sum(dh * yg))
        dy = dh * (1.0 + sc_ref[...])
        _acc(dn1_ref, _rows_sum(dy * xh))
        dx_ref[...] = dxm_ref[...] + _rms_bwd(dy * g_ref[...], xh, rstd)

    return _rowcall("in_bwd", body, n_rows, tm, [dxm, x, *dz_parts], [n1g, sc1, *w_parts],
                    [(d, F32)], [((1, d), F32)] * 3, temp_cols=5 * d)


def _pad_heads(w, per_head):
    lead = w.shape[:-1]
    w = w.reshape(lead + (N_HEADS, per_head))
    w = jnp.pad(w, [(0, 0)] * len(lead) + [(0, 0), (0, HEAD_PAD - per_head)])
    return w.reshape(lead + (QW,))


def _unpad_heads(w, per_head):
    lead = w.shape[:-1]
    return w.reshape(lead + (N_HEADS, HEAD_PAD))[..., :per_head].reshape(lead + (N_HEADS * per_head,))


def _cols_from_chips(g):
    ch, dep, r, cs = g.shape
    return g.transpose(1, 2, 0, 3).reshape(dep, r, ch * cs)


def _rows_from_chips(g):
    ch, dep, rs, c = g.shape
    return g.transpose(1, 0, 2, 3).reshape(dep, ch * rs, c)


def _cols_to_chips(w):
    dep, r, c = w.shape
    return w.reshape(dep, r, 4, c // 4).transpose(2, 0, 1, 3)


def _rows_to_chips(w):
    dep, r, c = w.shape
    return w.reshape(dep, 4, r // 4, c).transpose(1, 0, 2, 3)


def _block_diag(b_gxy):
    g, xx, yy = b_gxy.shape
    eye = jnp.eye(g, dtype=b_gxy.dtype)
    return (b_gxy[:, :, None, :] * eye[:, None, :, None]).reshape(g * xx, g * yy)


def _block_diag_extract(full, g):
    xx, yy = full.shape[0] // g, full.shape[1] // g
    eye = jnp.eye(g, dtype=full.dtype)
    return jnp.sum(full.reshape(g, xx, g, yy) * eye[:, None, :, None], axis=2)


def _pack_rows(arrays):
    parts = []
    for a in arrays:
        flat = a.reshape(-1)
        flat = jnp.pad(flat, (0, (-flat.shape[0]) % 1024))
        parts.append(flat.reshape(-1, 128))
    return jnp.concatenate(parts, axis=0)


def _unpack_rows(packed, shapes):
    out, row = [], 0
    for s in shapes:
        n = int(np.prod(s))
        rows = -(-n // 1024) * 8
        out.append(packed[row:row + rows].reshape(-1)[:n].reshape(s))
        row += rows
    return out


def kernel(x, c, positions, w_ada, b_ada, norm1_g, w_in, ssm_a_re, ssm_a_im, ssm_log_dt, ssm_b_re, ssm_b_im, ssm_c_re, ssm_c_im, ssm_d, w_glu, b_glu, w_a_out, q_norm_g, w_uq, kv_norm_g, w_uk, w_uv, w_b_out, w_out, norm2_g, w_gate, w_up, w_down, final_g, loss_target, m_w_ada, m_b_ada, m_norm1_g, m_w_in, m_ssm_a_re, m_ssm_a_im, m_ssm_log_dt, m_ssm_b_re, m_ssm_b_im, m_ssm_c_re, m_ssm_c_im, m_ssm_d, m_w_glu, m_b_glu, m_w_a_out, m_q_norm_g, m_w_uq, m_kv_norm_g, m_w_uk, m_w_uv, m_w_b_out, m_w_out, m_norm2_g, m_w_gate, m_w_up, m_w_down, m_final_g, v_w_ada, v_b_ada, v_norm1_g, v_w_in, v_ssm_a_re, v_ssm_a_im, v_ssm_log_dt, v_ssm_b_re, v_ssm_b_im, v_ssm_c_re, v_ssm_c_im, v_ssm_d, v_w_glu, v_b_glu, v_w_a_out, v_q_norm_g, v_w_uq, v_kv_norm_g, v_w_uk, v_w_uv, v_w_b_out, v_w_out, v_norm2_g, v_w_gate, v_w_up, v_w_down, v_final_g):
    weights = dict(w_ada=w_ada, b_ada=b_ada, norm1_g=norm1_g, w_in=w_in, ssm_a_re=ssm_a_re, ssm_a_im=ssm_a_im, ssm_log_dt=ssm_log_dt, ssm_b_re=ssm_b_re, ssm_b_im=ssm_b_im, ssm_c_re=ssm_c_re, ssm_c_im=ssm_c_im, ssm_d=ssm_d, w_glu=w_glu, b_glu=b_glu, w_a_out=w_a_out, q_norm_g=q_norm_g, w_uq=w_uq, kv_norm_g=kv_norm_g, w_uk=w_uk, w_uv=w_uv, w_b_out=w_b_out, w_out=w_out, norm2_g=norm2_g, w_gate=w_gate, w_up=w_up, w_down=w_down, final_g=final_g)
    mom_m = dict(w_ada=m_w_ada, b_ada=m_b_ada, norm1_g=m_norm1_g, w_in=m_w_in, ssm_a_re=m_ssm_a_re, ssm_a_im=m_ssm_a_im, ssm_log_dt=m_ssm_log_dt, ssm_b_re=m_ssm_b_re, ssm_b_im=m_ssm_b_im, ssm_c_re=m_ssm_c_re, ssm_c_im=m_ssm_c_im, ssm_d=m_ssm_d, w_glu=m_w_glu, b_glu=m_b_glu, w_a_out=m_w_a_out, q_norm_g=m_q_norm_g, w_uq=m_w_uq, kv_norm_g=m_kv_norm_g, w_uk=m_w_uk, w_uv=m_w_uv, w_b_out=m_w_b_out, w_out=m_w_out, norm2_g=m_norm2_g, w_gate=m_w_gate, w_up=m_w_up, w_down=m_w_down, final_g=m_final_g)
    mom_v = dict(w_ada=v_w_ada, b_ada=v_b_ada, norm1_g=v_norm1_g, w_in=v_w_in, ssm_a_re=v_ssm_a_re, ssm_a_im=v_ssm_a_im, ssm_log_dt=v_ssm_log_dt, ssm_b_re=v_ssm_b_re, ssm_b_im=v_ssm_b_im, ssm_c_re=v_ssm_c_re, ssm_c_im=v_ssm_c_im, ssm_d=v_ssm_d, w_glu=v_w_glu, b_glu=v_b_glu, w_a_out=v_w_a_out, q_norm_g=v_q_norm_g, w_uq=v_w_uq, kv_norm_g=v_kv_norm_g, w_uk=v_w_uk, w_uv=v_w_uv, w_b_out=v_w_b_out, w_out=v_w_out, norm2_g=v_norm2_g, w_gate=v_w_gate, w_up=v_w_up, w_down=v_w_down, final_g=v_final_g)
    names = list(weights)

    depth = w_in.shape[0]
    seq, d = x.shape[1], x.shape[2]
    sw = ssm_d.shape[1]
    groups, n_state, n_chan = ssm_b_re.shape[1:]
    gp = groups * n_state
    ql, kl = q_norm_g.shape[1], kv_norm_g.shape[1]
    nope = w_uk.shape[2] * 4 // N_HEADS
    vdim = w_uv.shape[2] * 4 // N_HEADS
    qk_dim = nope + QK_ROPE
    scale = qk_dim ** -0.5
    tm = _tile(seq, 256, 16)
    tm_ffn = _tile(seq, 128, 16)
    me = 4 * lax.axis_index("x") + 2 * lax.axis_index("y") + lax.axis_index("c")
    chip = 2 * lax.axis_index("x") + lax.axis_index("y")

    xs = x.reshape(seq, d)
    target = loss_target.reshape(seq, d)
    pos_f = positions.astype(F32)
    pos_col = pos_f.reshape(seq, 1)
    pos_row = pos_f.reshape(1, seq)

    (c_all,) = _exchange("gather_c", "gather8", [c])
    c_all = c_all.reshape(8, d)
    ada_cols = w_ada.shape[2]
    mod_part = _mod_fwd(c_all, w_ada.reshape(depth * d, ada_cols), depth)
    (mod_all,) = _exchange("gather_mod", "gather8", [mod_part])
    mod_all = mod_all.reshape(4, 2, depth, 8, ada_cols)[:, 0]
    mod_me = lax.dynamic_index_in_dim(mod_all, me, axis=2, keepdims=False)
    mod = mod_me.transpose(1, 0, 2).reshape(depth, 4 * ada_cols) + b_ada
    mod = mod.reshape(depth, 6, 1, d)

    big = ["w_in", "w_glu", "w_a_out", "w_uq", "w_uk", "w_uv", "w_b_out", "w_out", "w_gate", "w_up", "w_down"]
    row_sharded = {"w_glu", "w_out", "w_down"}
    gathered = _gather_chips_two_level("gather_weights", [weights[n].astype(_MM) for n in big])
    full = {n: (_rows_from_chips(g) if n in row_sharded else _cols_from_chips(g)) for n, g in zip(big, gathered)}
    o1, o2, o3, o4, o5 = sw, sw + ql, sw + ql + kl, sw + ql + kl + QK_ROPE, sw + ql + kl + QK_ROPE + d
    wi = full["w_in"]
    w_u, w_cq, w_ckv, w_ga, w_gb = wi[:, :, :o1], wi[:, :, o1:o2], wi[:, :, o2:o3], wi[:, :, o4:o5], wi[:, :, o5:]
    w_kr = jnp.pad(wi[:, :, o3:o4], ((0, 0), (0, 0), (nope, HEAD_PAD - nope - QK_ROPE)))
    wuq_p = _pad_heads(full["w_uq"], qk_dim)
    wuk_p = _pad_heads(full["w_uk"], nope)
    wuv_p = _pad_heads(full["w_uv"], vdim)
    wb_p = _pad_heads(full["w_b_out"].transpose(0, 2, 1), vdim).transpose(0, 2, 1)

    inv_freq = ROPE_BASE ** (-jnp.arange(0, QK_ROPE, 2, dtype=F32) / QK_ROPE)
    inv_lane = jnp.pad(jnp.concatenate([inv_freq, inv_freq]), (nope, HEAD_PAD - nope - QK_ROPE)).reshape(1, HEAD_PAD)
    rc, rs1, rs2 = _rope_tables(pos_col, inv_lane, nope)
    a_re_col = ssm_a_re.reshape(depth * gp, 1)
    a_im_col = ssm_a_im.reshape(depth * gp, 1)
    ldt_col = jnp.broadcast_to(ssm_log_dt[:, :, None], (depth, groups, n_state)).reshape(depth * gp, 1)
    b_re2, b_im2 = ssm_b_re.reshape(depth * gp, n_chan), ssm_b_im.reshape(depth * gp, n_chan)
    abr, abi, bbr, bbi = _ssm_disc_fwd(a_re_col, a_im_col, ldt_col, b_re2, b_im2)
    abr_rows, abi_rows = abr.reshape(depth, 1, gp), abi.reshape(depth, 1, gp)
    bbr, bbi = bbr.reshape(depth, groups, n_state, n_chan), bbi.reshape(depth, groups, n_state, n_chan)

    saved = []
    xl = xs
    for l in range(depth):
        sh1, sc1, g1, sh2, sc2, g2 = (mod[l, j] for j in range(6))
        n1g, n2g = norm1_g[l].reshape(1, d), norm2_g[l].reshape(1, d)
        w_parts = [w_u[l], w_cq[l], w_ckv[l], w_kr[l], w_ga[l], w_gb[l]]
        hb, u, cq, ckv, kr, ga, gb = _in_fwd(xl, n1g, sc1, sh1, w_parts, 2 * tm)
        bre_blk = _block_diag(bbr[l].transpose(0, 2, 1)).astype(_MM)
        bim_blk = _block_diag(bbi[l].transpose(0, 2, 1)).astype(_MM)
        cre_blk = _block_diag(ssm_c_re[l].transpose(0, 2, 1)).astype(_MM)
        cimneg_blk = _block_diag(-ssm_c_im[l].transpose(0, 2, 1)).astype(_MM)
        d_row = ssm_d[l].reshape(1, sw)
        ssm_w = (bre_blk, bim_blk, abr_rows[l], abi_rows[l], cre_blk, cimneg_blk, d_row)
        ypre, hre, him = _ssm_fwd(u, *ssm_w)
        gq, gkv = q_norm_g[l].reshape(1, ql), kv_norm_g[l].reshape(1, kl)
        q, k, v, cqn, ckvn = _mla_prep_fwd(cq, ckv, kr, rc, rs1, rs2, gq, gkv, wuq_p[l], wuk_p[l], wuv_p[l], vdim, 2 * tm)
        o, lse = _attn_fwd(q, k, v, pos_col, pos_row, scale, vdim)
        bglu = b_glu[l].reshape(1, sw)
        yg, ya, yb, merged, mo, xmid = _mix_fwd(ypre, o, ga, gb, xl, g1, bglu, full["w_glu"][l], full["w_a_out"][l],
                                                 wb_p[l], full["w_out"][l], tm)
        hb2, fa, fb, dn, xout = _ffn_fwd(xmid, n2g, sc2, sh2, g2, full["w_gate"][l], full["w_up"][l], full["w_down"][l], 2 * tm_ffn)
        saved.append(dict(x=xl, hb=hb, u=u, cq=cq, ckv=ckv, ga=ga, gb=gb, ssm_w=ssm_w, ypre=ypre, hre=hre, him=him,
                          q=q, k=k, v=v, cqn=cqn, ckvn=ckvn, o=o, lse=lse, yg=yg, ya=ya, yb=yb, merged=merged, mo=mo,
                          xmid=xmid, hb2=hb2, fa=fa, fb=fb, dn=dn, w_parts=w_parts))
        xl = xout

    dx, loss_acc, g_final = _head(xl, final_g.reshape(1, d), target, tm)
    loss = lax.psum(loss_acc[0, 0], ("x", "y", "c"))

    per_layer = ["w_gate", "w_up", "w_down", "norm2_g", "w_out", "w_a_out", "w_b_out", "w_glu", "b_glu", "w_uq", "w_uk",
                 "w_uv", "q_norm_g", "kv_norm_g", "ssm_d", "ssm_c_re", "ssm_c_im", "w_in", "norm1_g"]
    grads = {n: [None] * depth for n in per_layer}
    dmod = [None] * depth
    for l in reversed(range(depth)):
        s = saved[l]
        sh1, sc1, g1, sh2, sc2, g2 = (mod[l, j] for j in range(6))
        n1g, n2g = norm1_g[l].reshape(1, d), norm2_g[l].reshape(1, d)
        dxm, da, db, fb16, dd, dg2, dsh2, dsc2, dn2 = _ffn_bwd(
            dx, s["xmid"], s["fa"], s["fb"], s["dn"], n2g, sc2, g2, full["w_gate"][l], full["w_up"][l], full["w_down"][l], tm_ffn)
        grads["w_gate"][l] = _mm_tn("dw_gate", s["hb2"], da)
        grads["w_up"][l] = _mm_tn("dw_up", s["hb2"], db)
        grads["w_down"][l] = _mm_tn("dw_down", fb16, dd)
        grads["norm2_g"][l] = dn2.reshape(d)

        bglu = b_glu[l].reshape(1, sw)
        dmo, dya, dyb, dt, ys, dga, dgb, dypre, do, delta, dg1, dbglu = _mix_bwd(
            dxm, s["mo"], s["ya"], s["yb"], s["ga"], s["gb"], s["ypre"], s["o"], g1, bglu, full["w_glu"][l],
            full["w_a_out"][l], wb_p[l], full["w_out"][l], tm)
        grads["w_out"][l] = _mm_tn("dw_out", s["merged"], dmo)
        grads["w_a_out"][l] = _mm_tn("dw_a_out", s["yg"], dya)
        dwb_p = _mm_tn("dw_b_out", s["o"], dyb)
        grads["w_b_out"][l] = _unpad_heads(dwb_p.T, vdim).T
        grads["w_glu"][l] = _mm_tn("dw_glu", ys, dt)
        grads["b_glu"][l] = dbglu.reshape(sw)

        dq, dk, dv = _attn_bwd(s["q"], s["k"], s["v"], do, delta, s["lse"], pos_col, pos_row, scale)
        gq, gkv = q_norm_g[l].reshape(1, ql), kv_norm_g[l].reshape(1, kl)
        dqp, dcq, dckv, dkr, dgq, dgkv = _mla_prep_bwd(dq, dk, dv, s["cq"], s["ckv"], rc, rs1, rs2, gq, gkv,
                                                       wuq_p[l], wuk_p[l], wuv_p[l], nope, tm)
        grads["w_uq"][l] = _unpad_heads(_mm_tn("dw_uq", s["cqn"], dqp), qk_dim)
        grads["w_uk"][l] = _unpad_heads(_mm_tn("dw_uk", s["ckvn"], dk), nope)
        grads["w_uv"][l] = _unpad_heads(_mm_tn("dw_uv", s["ckvn"], dv), vdim)
        grads["q_norm_g"][l] = dgq.reshape(ql)
        grads["kv_norm_g"][l] = dgkv.reshape(kl)

        du, gre, gim, dar, dai, ddskip = _ssm_bwd(dypre, s["u"], s["hre"], s["him"], *s["ssm_w"])
        grads["ssm_d"][l] = ddskip.reshape(sw)
        d_bre = _block_diag_extract(_mm_tn("d_bre", s["u"], gre), groups).transpose(0, 2, 1)
        d_bim = _block_diag_extract(_mm_tn("d_bim", s["u"], gim), groups).transpose(0, 2, 1)
        grads["ssm_c_re"][l] = _block_diag_extract(_mm_tn("d_cre", s["hre"], dypre), groups).transpose(0, 2, 1)
        grads["ssm_c_im"][l] = -_block_diag_extract(_mm_tn("d_cim", s["him"], dypre), groups).transpose(0, 2, 1)
        s["disc_grads"] = (dar.reshape(gp, 1), dai.reshape(gp, 1), d_bre.reshape(gp, n_chan), d_bim.reshape(gp, n_chan))

        dz_parts = [du, dcq, dckv, dkr, dga, dgb]
        dx, dsh1, dsc1, dn1 = _in_bwd(dxm, s["x"], dz_parts, n1g, sc1, s["w_parts"], 2 * tm)
        dw_parts = [_mm_tn("dw_in_%d" % j, s["hb"], dz) for j, dz in enumerate(dz_parts)]
        dw_parts[3] = dw_parts[3][:, nope:nope + QK_ROPE]
        grads["w_in"][l] = jnp.concatenate(dw_parts, axis=1)
        grads["norm1_g"][l] = dn1.reshape(d)
        dmod[l] = jnp.concatenate([dsh1, dsc1, dg1, dsh2, dsc2, dg2], axis=1).reshape(6 * d)
    grad_x = dx.reshape(x.shape)

    disc = [jnp.concatenate([saved[l]["disc_grads"][j] for l in range(depth)], axis=0) for j in range(4)]
    da_re, da_im, dldt, db_re, db_im = _ssm_disc_bwd(a_re_col, a_im_col, ldt_col, b_re2, b_im2, *disc)
    stacked = {n: jnp.stack(v) for n, v in grads.items()}
    stacked["ssm_a_re"] = da_re.reshape(ssm_a_re.shape)
    stacked["ssm_a_im"] = da_im.reshape(ssm_a_im.shape)
    stacked["ssm_log_dt"] = _lane_sum(dldt.reshape(depth * groups, n_state)).reshape(ssm_log_dt.shape)
    stacked["ssm_b_re"] = db_re.reshape(ssm_b_re.shape)
    stacked["ssm_b_im"] = db_im.reshape(ssm_b_im.shape)
    stacked["final_g"] = g_final.reshape(d)
    stacked["b_ada"] = jnp.stack(dmod)

    small = [n for n in names if n not in big and n != "w_ada"]
    small_shapes = [weights[n].shape for n in small]
    (small_all,) = _exchange("gather_small", "gather8", [_pack_rows([stacked[n] for n in small])])
    sg, sd, sm, sv = _adamw("adamw_small", [small_all], _pack_rows([weights[n] for n in small]),
                            _pack_rows([mom_m[n] for n in small]), _pack_rows([mom_v[n] for n in small]))
    out_g = dict(zip(small, _unpack_rows(sg, small_shapes)))
    out_d = dict(zip(small, _unpack_rows(sd, small_shapes)))
    out_m = dict(zip(small, _unpack_rows(sm, small_shapes)))
    out_v = dict(zip(small, _unpack_rows(sv, small_shapes)))

    n_dmod = depth * 6 * d
    dmod_all = small_all[:, :n_dmod // 128].reshape(8, depth, 6 * d)
    dmod_cols = lax.dynamic_slice_in_dim(dmod_all, chip * ada_cols, ada_cols, axis=2)
    g_wada = _wada_bwd(c_all, dmod_cols.transpose(1, 0, 2).reshape(depth * 8, ada_cols), depth)
    res = _adamw("adamw_w_ada", [g_wada], w_ada.reshape(depth * d, ada_cols), m_w_ada.reshape(depth * d, ada_cols),
                 v_w_ada.reshape(depth * d, ada_cols))
    out_g["w_ada"], out_d["w_ada"], out_m["w_ada"], out_v["w_ada"] = (r.reshape(w_ada.shape) for r in res)

    core = lax.axis_index("c")
    half = depth // 2
    to_chips = [(_rows_to_chips if n in row_sharded else _cols_to_chips)(stacked[n]).astype(_WIRE) for n in big]
    own_half = [lax.dynamic_slice_in_dim(p, core * half, half, axis=1) for p in to_chips]
    other_half = [lax.dynamic_slice_in_dim(p, (1 - core) * half, half, axis=1) for p in to_chips]
    from_sibling = _exchange("swap_halves", "swap", other_half)
    chip_sum = [_add_pair("pair_" + n, a.reshape(-1, a.shape[-1]), b.reshape(-1, b.shape[-1])).reshape(a.shape)
                for n, a, b in zip(big, own_half, from_sibling)]
    landed = _exchange("scatter_grads", "scatter4", chip_sum)
    finished = [_sum_slots("sum_" + n, r.reshape(4, -1, r.shape[-1])) for n, r in zip(big, landed)]
    sibling = _exchange("swap_partials", "swap", finished)
    for n, mine, theirs in zip(big, finished, sibling):
        shp = weights[n].shape
        as2d = lambda a: a.reshape(-1, shp[-1])
        rows = mine.shape[0]
        g = lax.dynamic_update_slice_in_dim(jnp.zeros((2 * rows, shp[-1]), F32), mine, core * rows, axis=0)
        g = lax.dynamic_update_slice_in_dim(g, theirs, (1 - core) * rows, axis=0)
        res = _adamw("adamw_" + n, [g], as2d(weights[n]), as2d(mom_m[n]), as2d(mom_v[n]))
        out_g[n], out_d[n], out_m[n], out_v[n] = (r.reshape(shp) for r in res)

    return (loss, grad_x, *[out_g[n] for n in names], *[out_d[n] for n in names], *[out_m[n] for n in names],
            *[out_v[n] for n in names])
```
